```python
import math
import jax, jax.numpy as jnp
from jax import lax
import numpy as np

D_MODEL = 1024
BATCH = 16
SEQ = 2048
DEPTH = 4

N_MIXERS = 2
RMS_EPS = 1e-6
FFN_HIDDEN = 2816
Q_BLOCK = 128
NEG = -1e30
MLA_HEADS = 16
MLA_Q_LORA = 384
MLA_KV_LORA = 256
MLA_NOPE = 64
MLA_ROPE = 32
MLA_V = 64
ROPE_THETA = 10000.0
NSA_HEADS = 16
NSA_GROUPS = 4
NSA_QK = 64
NSA_V = 64
CMP_BLOCK = 32
CMP_STRIDE = 16
CMP_HIDDEN = 128
SEL_BLOCK = 64
SEL_TOP_N = 16
WINDOW = 512
SEL_Q_CHUNK = 16
REL_BUCKETS = 32
REL_MAX_DIST = 128

kernel_name = "hybrid_mla_nsa_macaron"


def rmsnorm(x, g):
    x32 = x.astype(jnp.float32)
    y = x32 * lax.rsqrt(jnp.mean(x32 * x32, axis=-1, keepdims=True) + RMS_EPS)
    return y.astype(x.dtype) * g


def swiglu(x, w_gate, w_up, w_down):
    return (jax.nn.silu(x @ w_gate) * (x @ w_up)) @ w_down


def rope_angles(S):
    half = MLA_ROPE // 2
    inv = ROPE_THETA ** (-jnp.arange(half, dtype=jnp.float32) * 2.0 / MLA_ROPE)
    ang = jnp.arange(S, dtype=jnp.float32)[:, None] * inv[None, :]
    return jnp.cos(ang), jnp.sin(ang)


def apply_rotary(x, cos, sin):
    half = x.shape[-1] // 2
    cos = cos.astype(x.dtype)
    sin = sin.astype(x.dtype)
    x1, x2 = x[..., :half], x[..., half:]
    return jnp.concatenate([x1 * cos - x2 * sin, x1 * sin + x2 * cos], axis=-1)


def t5_bucket(dist):
    n = jnp.maximum(dist, 0)
    max_exact = REL_BUCKETS // 2
    nf = jnp.maximum(n, 1).astype(jnp.float32)
    large = max_exact + (jnp.log(nf / max_exact) / math.log(REL_MAX_DIST / max_exact)
                         * (REL_BUCKETS - max_exact)).astype(jnp.int32)
    large = jnp.minimum(large, REL_BUCKETS - 1)
    return jnp.where(n < max_exact, n, large)


def mla_mixer(h, w_in, q_norm, kv_norm, w_uq, w_ukv, w_o):
    B, S, _ = h.shape
    H = MLA_HEADS
    proj = h @ w_in
    c_q = proj[..., :MLA_Q_LORA]
    c_kv = proj[..., MLA_Q_LORA:MLA_Q_LORA + MLA_KV_LORA]
    k_rope = proj[..., MLA_Q_LORA + MLA_KV_LORA:]
    q = (rmsnorm(c_q, q_norm) @ w_uq).reshape(B, S, H, MLA_NOPE + MLA_ROPE)
    kv = (rmsnorm(c_kv, kv_norm) @ w_ukv).reshape(B, S, H, MLA_NOPE + MLA_V)
    cos, sin = rope_angles(S)
    q_nope = q[..., :MLA_NOPE]
    q_rope = apply_rotary(q[..., MLA_NOPE:], cos[:, None, :], sin[:, None, :])
    k_rope = apply_rotary(k_rope, cos, sin)
    k_nope = kv[..., :MLA_NOPE]
    v = kv[..., MLA_NOPE:]
    scale = (MLA_NOPE + MLA_ROPE) ** -0.5
    n_blk = S // Q_BLOCK
    qn_b = q_nope.reshape(B, n_blk, Q_BLOCK, H, MLA_NOPE).swapaxes(0, 1)
    qr_b = q_rope.reshape(B, n_blk, Q_BLOCK, H, MLA_ROPE).swapaxes(0, 1)
    starts = jnp.arange(n_blk) * Q_BLOCK
    kpos = jnp.arange(S)

    def block(args):
        qn, qr, s0 = args
        s = (jnp.einsum('bqhd,bkhd->bhqk', qn, k_nope)
             + jnp.einsum('bqhd,bkd->bhqk', qr, k_rope)).astype(jnp.float32) * scale
        qpos = s0 + jnp.arange(Q_BLOCK)
        p = jax.nn.softmax(jnp.where(kpos[None, :] <= qpos[:, None], s, NEG), axis=-1)
        return jnp.einsum('bhqk,bkhd->bqhd', p.astype(v.dtype), v)

    o = lax.map(block, (qn_b, qr_b, starts)).swapaxes(0, 1).reshape(B, S, H * MLA_V)
    return o @ w_o


def compress_blocks(x, blk_idx, pos_emb, w1, w2):
    B = x.shape[0]
    n_cmp, L = blk_idx.shape
    blocks = x[:, blk_idx] + pos_emb[None, None, :, None, :]
    flat = jnp.moveaxis(blocks, 3, 2).reshape(B, n_cmp, x.shape[2], L * x.shape[3])
    return jax.nn.gelu(flat @ w1) @ w2


def selection_overlap(n_cmp, n_sel):
    cs = np.arange(n_cmp) * CMP_STRIDE
    ce = cs + CMP_BLOCK
    ss = np.arange(n_sel) * SEL_BLOCK
    se = ss + SEL_BLOCK
    ov = np.minimum(ce[:, None], se[None, :]) - np.maximum(cs[:, None], ss[None, :])
    return (np.clip(ov, 0, None) / CMP_BLOCK).astype(np.float32)


def selected_attention(q, k, v, sel_idx, tbl, scale):
    B, S, G, R, dk = q.shape
    dv = v.shape[-1]
    n_sel = S // SEL_BLOCK
    n_top = sel_idx.shape[-1]
    kb = k.reshape(B, n_sel, SEL_BLOCK, G, dk).transpose(0, 3, 1, 2, 4)
    vb = v.reshape(B, n_sel, SEL_BLOCK, G, dv).transpose(0, 3, 1, 2, 4)
    C = SEL_Q_CHUNK
    n_ch = S // C
    q_ch = q.reshape(B, n_ch, C, G, R, dk).swapaxes(0, 1)
    idx_ch = sel_idx.reshape(B, G, n_ch, C, n_top).transpose(2, 0, 1, 3, 4)
    starts = jnp.arange(n_ch) * C
    gather = jax.vmap(jax.vmap(lambda blocks, ix: blocks[ix]))
    g_ar = jnp.arange(G)[None, :, None, None]
    K = n_top * SEL_BLOCK

    def chunk(args):
        q_i, ix, s0 = args
        kg = gather(kb, ix).reshape(B, G, C, K, dk)
        vg = gather(vb, ix).reshape(B, G, C, K, dv)
        kpos = (ix[..., None] * SEL_BLOCK + jnp.arange(SEL_BLOCK)).reshape(B, G, C, K)
        qpos = s0 + jnp.arange(C)
        dist = qpos[None, None, :, None] - kpos
        bias = tbl[t5_bucket(dist), g_ar]
        s = jnp.einsum('bcgrd,bgckd->bgcrk', q_i, kg).astype(jnp.float32) * scale \
            + jnp.moveaxis(bias, -1, 3).astype(jnp.float32)
        p = jax.nn.softmax(jnp.where((dist >= 0)[:, :, :, None, :], s, NEG), axis=-1)
        return jnp.einsum('bgcrk,bgckd->bcgrd', p.astype(vg.dtype), vg)

    out = lax.map(chunk, (q_ch, idx_ch, starts))
    return out.swapaxes(0, 1).reshape(B, S, G, R, dv)


def window_attention(q, k, v, tbl, scale):
    B, S, G, R, dk = q.shape
    n_blk = S // Q_BLOCK
    span = WINDOW + Q_BLOCK
    k_pad = jnp.pad(k, ((0, 0), (WINDOW, 0), (0, 0), (0, 0)))
    v_pad = jnp.pad(v, ((0, 0), (WINDOW, 0), (0, 0), (0, 0)))
    q_b = q.reshape(B, n_blk, Q_BLOCK, G, R, dk).swapaxes(0, 1)
    starts = jnp.arange(n_blk) * Q_BLOCK

    def block(args):
        q_i, s0 = args
        k_i = lax.dynamic_slice_in_dim(k_pad, s0, span, axis=1)
        v_i = lax.dynamic_slice_in_dim(v_pad, s0, span, axis=1)
        qpos = s0 + jnp.arange(Q_BLOCK)
        kpos = s0 - WINDOW + jnp.arange(span)
        dist = qpos[:, None] - kpos[None, :]
        mask = (dist >= 0) & (dist < WINDOW) & (kpos >= 0)[None, :]
        bias = tbl[t5_bucket(dist)].transpose(2, 3, 0, 1)
        s = jnp.einsum('bqgrd,bkgd->bgrqk', q_i, k_i).astype(jnp.float32) * scale + bias.astype(jnp.float32)
        p = jax.nn.softmax(jnp.where(mask, s, NEG), axis=-1)
        return jnp.einsum('bgrqk,bkgd->bqgrd', p.astype(v_i.dtype), v_i)

    out = lax.map(block, (q_b, starts))
    return out.swapaxes(0, 1).reshape(B, S, G, R, v.shape[-1])


def nsa_mixer(h, rel_bias, w_in, pos_k, w1_k, w2_k, pos_v, w1_v, w2_v, w_o):
    B, S, _ = h.shape
    H, G = NSA_HEADS, NSA_GROUPS
    R = H // G
    sizes = [H * NSA_QK, G * NSA_QK, G * NSA_V, G * NSA_QK, G * NSA_V, G * NSA_QK, G * NSA_V, H * 3]
    cuts = [int(c) for c in np.cumsum(sizes[:-1])]
    q, k_c, v_c, k_s, v_s, k_w, v_w, g = jnp.split(h @ w_in, cuts, axis=-1)
    q = q.reshape(B, S, G, R, NSA_QK)
    k_c, k_s, k_w = (t.reshape(B, S, G, NSA_QK) for t in (k_c, k_s, k_w))
    v_c, v_s, v_w = (t.reshape(B, S, G, NSA_V) for t in (v_c, v_s, v_w))
    gates = jax.nn.sigmoid(g.astype(jnp.float32)).astype(h.dtype).reshape(B, S, G, R, 3)
    tbl = rel_bias.reshape(REL_BUCKETS, G, R)
    scale = NSA_QK ** -0.5
    pos = jnp.arange(S)

    n_cmp = (S - CMP_BLOCK) // CMP_STRIDE + 1
    blk_idx = jnp.arange(n_cmp)[:, None] * CMP_STRIDE + jnp.arange(CMP_BLOCK)[None, :]
    kc = compress_blocks(k_c, blk_idx, pos_k, w1_k, w2_k)
    vc = compress_blocks(v_c, blk_idx, pos_v, w1_v, w2_v)
    cmp_end = jnp.arange(n_cmp) * CMP_STRIDE + CMP_BLOCK - 1
    dist = pos[:, None] - cmp_end[None, :]
    valid = dist >= 0
    bias = tbl[t5_bucket(dist)].transpose(2, 3, 0, 1)
    s = jnp.einsum('bsgrd,bngd->bgrsn', q, kc).astype(jnp.float32) * scale + bias.astype(jnp.float32)
    p_cmp = jax.nn.softmax(jnp.where(valid, s, NEG), axis=-1) * valid
    o_cmp = jnp.einsum('bgrsn,bngd->bsgrd', p_cmp.astype(vc.dtype), vc)

    n_sel = S // SEL_BLOCK
    overlap = jnp.asarray(selection_overlap(n_cmp, n_sel))
    imp = jnp.einsum('bgrsn,nj->bgsj', p_cmp, overlap)
    blk_t = (pos // SEL_BLOCK)[:, None]
    j = jnp.arange(n_sel)[None, :]
    forced = (j == 0) | (j == blk_t) | (j == blk_t - 1)
    score = jnp.where(forced, 1e6, jnp.where(j <= blk_t, imp, -1e6))
    n_top = min(SEL_TOP_N, n_sel)
    _, sel_idx = lax.top_k(score, n_top)
    o_sel = selected_attention(q, k_s, v_s, sel_idx, tbl, scale)

    o_win = window_attention(q, k_w, v_w, tbl, scale)

    o = gates[..., 0:1] * o_cmp + gates[..., 1:2] * o_sel + gates[..., 2:3] * o_win
    return o.reshape(B, S, H * NSA_V) @ w_o


def setup_inputs(seed: int = 0) -> dict:
    key = jax.random.key(seed)
    ks = iter(jax.random.split(key, 32))
    n_mla = len(range(0, DEPTH, N_MIXERS))
    n_nsa = DEPTH - n_mla

    def dense(shape, fan_in):
        return jax.random.normal(next(ks), shape, jnp.float32) * fan_in ** -0.5

    def gain(shape):
        return 1.0 + 0.01 * jax.random.normal(next(ks), shape, jnp.float32)

    def small(shape, s):
        return s * jax.random.normal(next(ks), shape, jnp.float32)

    mla_in = MLA_Q_LORA + MLA_KV_LORA + MLA_ROPE
    nsa_in = NSA_HEADS * NSA_QK + 3 * NSA_GROUPS * (NSA_QK + NSA_V) + 3 * NSA_HEADS
    return {
        "x": jax.random.normal(next(ks), (BATCH, SEQ, D_MODEL), jnp.float32),
        "ffn_norm_a": gain((DEPTH, D_MODEL)),
        "ffn_a_w_gate": dense((DEPTH, D_MODEL, FFN_HIDDEN), D_MODEL),
        "ffn_a_w_up": dense((DEPTH, D_MODEL, FFN_HIDDEN), D_MODEL),
        "ffn_a_w_down": dense((DEPTH, FFN_HIDDEN, D_MODEL), FFN_HIDDEN),
        "mix_norm": gain((DEPTH, D_MODEL)),
        "ffn_norm_b": gain((DEPTH, D_MODEL)),
        "ffn_b_w_gate": dense((DEPTH, D_MODEL, FFN_HIDDEN), D_MODEL),
        "ffn_b_w_up": dense((DEPTH, D_MODEL, FFN_HIDDEN), D_MODEL),
        "ffn_b_w_down": dense((DEPTH, FFN_HIDDEN, D_MODEL), FFN_HIDDEN),
        "final_norm": gain((D_MODEL,)),
        "rel_bias": small((REL_BUCKETS, NSA_HEADS), 0.5),
        "mla_w_in": dense((n_mla, D_MODEL, mla_in), D_MODEL),
        "mla_q_norm": gain((n_mla, MLA_Q_LORA)),
        "mla_kv_norm": gain((n_mla, MLA_KV_LORA)),
        "mla_w_uq": dense((n_mla, MLA_Q_LORA, MLA_HEADS * (MLA_NOPE + MLA_ROPE)), MLA_Q_LORA),
        "mla_w_ukv": dense((n_mla, MLA_KV_LORA, MLA_HEADS * (MLA_NOPE + MLA_V)), MLA_KV_LORA),
        "mla_w_o": dense((n_mla, MLA_HEADS * MLA_V, D_MODEL), MLA_HEADS * MLA_V),
        "nsa_w_in": dense((n_nsa, D_MODEL, nsa_in), D_MODEL),
        "nsa_cmp_pos_k": small((n_nsa, CMP_BLOCK, NSA_QK), 0.1),
        "nsa_cmp_w1_k": dense((n_nsa, CMP_BLOCK * NSA_QK, CMP_HIDDEN), CMP_BLOCK * NSA_QK),
        "nsa_cmp_w2_k": dense((n_nsa, CMP_HIDDEN, NSA_QK), CMP_HIDDEN),
        "nsa_cmp_pos_v": small((n_nsa, CMP_BLOCK, NSA_V), 0.1),
        "nsa_cmp_w1_v": dense((n_nsa, CMP_BLOCK * NSA_V, CMP_HIDDEN), CMP_BLOCK * NSA_V),
        "nsa_cmp_w2_v": dense((n_nsa, CMP_HIDDEN, NSA_V), CMP_HIDDEN),
        "nsa_w_o": dense((n_nsa, NSA_HEADS * NSA_V, D_MODEL), NSA_HEADS * NSA_V),
    }


def reference(x, ffn_norm_a, ffn_a_w_gate, ffn_a_w_up, ffn_a_w_down, mix_norm, ffn_norm_b,
              ffn_b_w_gate, ffn_b_w_up, ffn_b_w_down, final_norm, rel_bias,
              mla_w_in, mla_q_norm, mla_kv_norm, mla_w_uq, mla_w_ukv, mla_w_o,
              nsa_w_in, nsa_cmp_pos_k, nsa_cmp_w1_k, nsa_cmp_w2_k,
              nsa_cmp_pos_v, nsa_cmp_w1_v, nsa_cmp_w2_v, nsa_w_o):
    h = x
    for i in range(DEPTH):
        h = h + 0.5 * swiglu(rmsnorm(h, ffn_norm_a[i]), ffn_a_w_gate[i], ffn_a_w_up[i], ffn_a_w_down[i])
        m = rmsnorm(h, mix_norm[i])
        j = i // N_MIXERS
        if i % N_MIXERS == 0:
            h = h + mla_mixer(m, mla_w_in[j], mla_q_norm[j], mla_kv_norm[j],
                              mla_w_uq[j], mla_w_ukv[j], mla_w_o[j])
        else:
            h = h + nsa_mixer(m, rel_bias, nsa_w_in[j], nsa_cmp_pos_k[j], nsa_cmp_w1_k[j], nsa_cmp_w2_k[j],
                              nsa_cmp_pos_v[j], nsa_cmp_w1_v[j], nsa_cmp_w2_v[j], nsa_w_o[j])
        h = h + 0.5 * swiglu(rmsnorm(h, ffn_norm_b[i]), ffn_b_w_gate[i], ffn_b_w_up[i], ffn_b_w_down[i])
    return rmsnorm(h, final_norm)
```

```python
import functools
import math

import numpy as np
import jax
import jax.numpy as jnp
from jax import lax
from jax.experimental import pallas as pl
from jax.experimental.pallas import tpu as pltpu

F32 = jnp.float32
BF16 = jnp.bfloat16

D_MODEL = 1024
DEPTH = 4
N_MIXERS = 2
RMS_EPS = 1e-6
FFN_HIDDEN = 2816
NEG = -1e30
MLA_HEADS = 16
MLA_Q_LORA = 384
MLA_KV_LORA = 256
MLA_NOPE = 64
MLA_ROPE = 32
MLA_V = 64
ROPE_THETA = 10000.0
NSA_HEADS = 16
NSA_GROUPS = 4
NSA_R = NSA_HEADS // NSA_GROUPS
NSA_QK = 64
NSA_V = 64
CMP_BLOCK = 32
CMP_STRIDE = 16
CMP_HIDDEN = 128
SEL_BLOCK = 64
SEL_TOP_N = 16
WINDOW = 512
REL_BUCKETS = 32
REL_MAX_DIST = 128

LANE = 128
VMEM_LIMIT = 56 * 1024 * 1024

FFN_TM = 512
FFN_TF = 256
PROJ_TM = 256
ATT_T = 256


def _cparams(sem):
    return pltpu.CompilerParams(dimension_semantics=sem, vmem_limit_bytes=VMEM_LIMIT)


def _rms(x, g):
    ms = jnp.mean(x * x, axis=-1, keepdims=True)
    return x * lax.rsqrt(ms + RMS_EPS) * g


def _const_spec(shape):
    nd = len(shape)
    return pl.BlockSpec(shape, lambda *_: (0,) * nd)


def _ffn_body(*refs, has_proj, has_final):
    it = iter(refs)
    h_ref = next(it)
    if has_proj:
        o_in_ref = next(it)
        wo_ref = next(it)
    g_ref = next(it)
    wg_ref = next(it)
    wu_ref = next(it)
    wd_ref = next(it)
    if has_final:
        gf_ref = next(it)
    out_ref = next(it)
    a_ref = next(it)

    x = h_ref[...]
    if has_proj:
        x = x + jnp.dot(o_in_ref[...], wo_ref[...], preferred_element_type=F32)
    xn = _rms(x, g_ref[...]).astype(BF16)
    for c in range(FFN_HIDDEN // FFN_TF):
        sl = slice(c * FFN_TF, (c + 1) * FFN_TF)
        gt = jnp.dot(xn, wg_ref[:, sl], preferred_element_type=F32)
        up = jnp.dot(xn, wu_ref[:, sl], preferred_element_type=F32)
        a_ref[:, sl] = (gt * jax.nn.sigmoid(gt) * up).astype(BF16)
    y = x + 0.5 * jnp.dot(a_ref[...], wd_ref[...], preferred_element_type=F32)
    if has_final:
        y = _rms(y, gf_ref[...])
    out_ref[...] = y


def _ffn(h, g, wg, wu, wd, proj=None, final_g=None):
    T = h.shape[0]
    tm = FFN_TM
    row = lambda i: (i, 0)
    in_specs = [pl.BlockSpec((tm, D_MODEL), row)]
    args = [h]
    if proj is not None:
        o_in, wo = proj
        in_specs += [pl.BlockSpec((tm, o_in.shape[1]), row), _const_spec(wo.shape)]
        args += [o_in, wo]
    in_specs += [_const_spec((1, D_MODEL)), _const_spec(wg.shape), _const_spec(wu.shape), _const_spec(wd.shape)]
    args += [g.reshape(1, D_MODEL), wg, wu, wd]
    if final_g is not None:
        in_specs.append(_const_spec((1, D_MODEL)))
        args.append(final_g.reshape(1, D_MODEL))
    return pl.pallas_call(
        functools.partial(_ffn_body, has_proj=proj is not None, has_final=final_g is not None),
        grid=(T // tm,),
        in_specs=in_specs,
        out_specs=pl.BlockSpec((tm, D_MODEL), row),
        out_shape=jax.ShapeDtypeStruct((T, D_MODEL), F32),
        scratch_shapes=[pltpu.VMEM((tm, FFN_HIDDEN), BF16)],
        compiler_params=_cparams(("parallel",)),
        name="ffn",
    )(*args)


MLA_CQ0, MLA_CKV0, MLA_KR0, MLA_KRS0, MLA_IN_W = 0, 384, 640, 768, 896
MLA_HCHUNK = 4


def _mla_proj_body(h_ref, g_ref, win_ref, qn_ref, kvn_ref, wuq_ref, wuqs_ref, wuk_ref, wuv_ref,
                   cos_ref, sin_ref, q_out, k_out, v_out):
    xn = _rms(h_ref[...], g_ref[...]).astype(BF16)
    proj = jnp.dot(xn, win_ref[...], preferred_element_type=F32)
    cq = _rms(proj[:, MLA_CQ0:MLA_CKV0], qn_ref[...]).astype(BF16)
    ckv = _rms(proj[:, MLA_CKV0:MLA_KR0], kvn_ref[...]).astype(BF16)
    cos = cos_ref[...]
    sin = sin_ref[...]
    kr = proj[:, MLA_KR0:MLA_KRS0] * cos + proj[:, MLA_KRS0:MLA_IN_W] * sin
    scale = (MLA_NOPE + MLA_ROPE) ** -0.5
    v_out[...] = jnp.dot(ckv, wuv_ref[...], preferred_element_type=F32).astype(BF16)
    cw = MLA_HCHUNK * LANE
    for c in range(MLA_HEADS // MLA_HCHUNK):
        sl = slice(c * cw, (c + 1) * cw)
        q = jnp.dot(cq, wuq_ref[:, sl], preferred_element_type=F32)
        qs = jnp.dot(cq, wuqs_ref[:, sl], preferred_element_type=F32)
        kn = jnp.dot(ckv, wuk_ref[:, sl], preferred_element_type=F32)
        for hh in range(MLA_HCHUNK):
            hs = slice(hh * LANE, (hh + 1) * LANE)
            os_ = slice(c * cw + hh * LANE, c * cw + (hh + 1) * LANE)
            q_out[:, os_] = ((q[:, hs] * cos + qs[:, hs] * sin) * scale).astype(BF16)
            k_out[:, os_] = (kn[:, hs] + kr).astype(BF16)


def _mla_proj(h, g, w, cos128, sin128, S):
    T = h.shape[0]
    tm = PROJ_TM
    ns = S // tm
    row = lambda i: (i, 0)
    pos = lambda i: (i % ns, 0)
    HL = MLA_HEADS * LANE
    return pl.pallas_call(
        _mla_proj_body,
        grid=(T // tm,),
        in_specs=[pl.BlockSpec((tm, D_MODEL), row), _const_spec((1, D_MODEL)),
                  _const_spec(w["w_in"].shape), _const_spec((1, MLA_Q_LORA)), _const_spec((1, MLA_KV_LORA)),
                  _const_spec(w["w_uq"].shape), _const_spec(w["w_uqs"].shape),
                  _const_spec(w["w_uk"].shape), _const_spec(w["w_uv"].shape),
                  pl.BlockSpec((tm, LANE), pos), pl.BlockSpec((tm, LANE), pos)],
        out_specs=[pl.BlockSpec((tm, HL), row), pl.BlockSpec((tm, HL), row),
                   pl.BlockSpec((tm, MLA_HEADS * MLA_V), row)],
        out_shape=[jax.ShapeDtypeStruct((T, HL), BF16), jax.ShapeDtypeStruct((T, HL), BF16),
                   jax.ShapeDtypeStruct((T, MLA_HEADS * MLA_V), BF16)],
        compiler_params=_cparams(("parallel",)),
        name="mla_proj",
    )(h, g.reshape(1, D_MODEL), w["w_in"], w["q_norm"], w["kv_norm"], w["w_uq"], w["w_uqs"],
      w["w_uk"], w["w_uv"], cos128, sin128)


def _online_step(s, kvv, m, l, acc):
    m_new = jnp.maximum(m, jnp.max(s, axis=-1, keepdims=True))
    alpha = jnp.exp(m - m_new)
    p = jnp.exp(s - m_new)
    l = alpha * l + jnp.sum(p, axis=-1, keepdims=True)
    acc = alpha * acc + jnp.dot(p.astype(BF16), kvv, preferred_element_type=F32)
    return m_new, l, acc


_NT = (((1,), (1,)), ((), ()))


def _mla_attn_body(q_ref, k_ref, v_ref, o_ref):
    t = ATT_T
    qi = pl.program_id(2)
    rows = lax.broadcasted_iota(jnp.int32, (t, t), 0)
    cols = lax.broadcasted_iota(jnp.int32, (t, t), 1)
    causal = rows >= cols
    outs = []
    for hh in range(2):
        hs = slice(hh * LANE, (hh + 1) * LANE)
        q = q_ref[:, hs]

        def tile(j, carry, masked, q=q, hs=hs):
            off = pl.multiple_of(j * t, t)
            k = k_ref[pl.ds(off, t), hs]
            v = v_ref[pl.ds(off, t), :]
            s = lax.dot_general(q, k, _NT, preferred_element_type=F32)
            if masked:
                s = jnp.where(causal, s, NEG)
            return _online_step(s, v, *carry)

        init = (jnp.full((t, 1), NEG, F32), jnp.zeros((t, 1), F32), jnp.zeros((t, LANE), F32))
        carry = lax.fori_loop(0, qi, functools.partial(tile, masked=False), init)
        m, l, acc = tile(qi, carry, True)
        outs.append(acc / l)
    lane = lax.broadcasted_iota(jnp.int32, (t, LANE), 1)
    o_ref[...] = jnp.where(lane < MLA_V, outs[0], outs[1]).astype(BF16)


def _mla_attn(q, k, v, B, S):
    t = ATT_T
    nq = S // t
    T = B * S
    return pl.pallas_call(
        _mla_attn_body,
        grid=(B, MLA_HEADS // 2, nq),
        in_specs=[pl.BlockSpec((t, 2 * LANE), lambda b, p, i: (b * nq + i, p)),
                  pl.BlockSpec((S, 2 * LANE), lambda b, p, i: (b, p)),
                  pl.BlockSpec((S, LANE), lambda b, p, i: (b, p))],
        out_specs=pl.BlockSpec((t, LANE), lambda b, p, i: (b * nq + i, p)),
        out_shape=jax.ShapeDtypeStruct((T, MLA_HEADS * MLA_V), BF16),
        compiler_params=_cparams(("parallel", "parallel", "arbitrary")),
        name="mla_attn",
    )(q, k, v)


NSA_QW = NSA_HEADS * LANE
NSA_KVW = NSA_GROUPS * LANE
NSA_IN_W = NSA_QW + 3 * NSA_KVW + LANE


def _nsa_proj_body(h_ref, g_ref, win_ref, q_out, kvc_out, kvs_out, kvw_out, gate_out):
    xn = _rms(h_ref[...], g_ref[...]).astype(BF16)
    scale = NSA_QK ** -0.5
    cw = 4 * LANE
    for c in range(NSA_QW // cw):
        sl = slice(c * cw, (c + 1) * cw)
        q_out[:, sl] = (jnp.dot(xn, win_ref[:, sl], preferred_element_type=F32) * scale).astype(BF16)
    for n, out in enumerate((kvc_out, kvs_out, kvw_out)):
        sl = slice(NSA_QW + n * NSA_KVW, NSA_QW + (n + 1) * NSA_KVW)
        out[...] = jnp.dot(xn, win_ref[:, sl], preferred_element_type=F32).astype(BF16)
    gl = jnp.dot(xn, win_ref[:, NSA_QW + 3 * NSA_KVW:], preferred_element_type=F32)
    gate_out[...] = jax.nn.sigmoid(gl)


def _nsa_proj(h, g, w_in):
    T = h.shape[0]
    tm = PROJ_TM
    row = lambda i: (i, 0)
    return pl.pallas_call(
        _nsa_proj_body,
        grid=(T // tm,),
        in_specs=[pl.BlockSpec((tm, D_MODEL), row), _const_spec((1, D_MODEL)), _const_spec(w_in.shape)],
        out_specs=[pl.BlockSpec((tm, NSA_QW), row), pl.BlockSpec((tm, NSA_KVW), row),
                   pl.BlockSpec((tm, NSA_KVW), row), pl.BlockSpec((tm, NSA_KVW), row),
                   pl.BlockSpec((tm, LANE), row)],
        out_shape=[jax.ShapeDtypeStruct((T, NSA_QW), BF16), jax.ShapeDtypeStruct((T, NSA_KVW), BF16),
                   jax.ShapeDtypeStruct((T, NSA_KVW), BF16), jax.ShapeDtypeStruct((T, NSA_KVW), BF16),
                   jax.ShapeDtypeStruct((T, LANE), F32)],
        compiler_params=_cparams(("parallel",)),
        name="nsa_proj",
    )(h, g.reshape(1, D_MODEL), w_in)


def _compress_body(x_ref, pos_ref, w1_ref, w2_ref, out_ref):
    half = CMP_STRIDE * NSA_QK
    n_chunk = x_ref.shape[1]
    res = []
    for kv in range(2):
        x = x_ref[kv].astype(F32)
        xa = (x + pos_ref[kv, 0:1, :]).astype(BF16)
        xb = (x + pos_ref[kv, 1:2, :]).astype(BF16)
        a = jnp.dot(xa, w1_ref[kv, :half, :], preferred_element_type=F32)
        b = jnp.dot(xb, w1_ref[kv, half:, :], preferred_element_type=F32)
        pre = a + pltpu.roll(b, n_chunk - 1, 0)
        hid = jax.nn.gelu(pre, approximate=True).astype(BF16)
        res.append(jnp.dot(hid, w2_ref[kv], preferred_element_type=F32))
    out_ref[...] = jnp.concatenate(res, axis=-1).astype(BF16)


def _compress(x, pos, w1, w2):
    B, G, _, n_chunk, width = x.shape
    return pl.pallas_call(
        _compress_body,
        grid=(B, G),
        in_specs=[pl.BlockSpec((None, None, 2, n_chunk, width), lambda b, g: (b, g, 0, 0, 0)),
                  _const_spec(pos.shape), _const_spec(w1.shape), _const_spec(w2.shape)],
        out_specs=pl.BlockSpec((None, None, n_chunk, LANE), lambda b, g: (b, g, 0, 0)),
        out_shape=jax.ShapeDtypeStruct((B, G, n_chunk, LANE), BF16),
        compiler_params=_cparams(("parallel", "parallel")),
        name="nsa_compress",
    )(x, pos, w1, w2)


def _nsa_attn_body(q_ref, kvc_ref, kvs_ref, kvw_ref, gate_ref, cb_ref, d_ref, e_ref, ov_ref,
                   o_ref, mask_ref, *, S):
    t = ATT_T
    R = NSA_R
    n_sel = S // SEL_BLOCK
    g = pl.program_id(0)
    qi = pl.program_id(2)
    t0 = qi * t
    qs = jnp.concatenate([q_ref[:, r * LANE:(r + 1) * LANE] for r in range(R)], axis=0)

    kvc = kvc_ref[...]
    n_cp = kvc.shape[0]
    s = lax.dot_general(qs, kvc, _NT, preferred_element_type=F32).reshape(R, t, n_cp)
    tpos = t0 + lax.broadcasted_iota(jnp.int32, (t, n_cp), 0)
    cend = lax.broadcasted_iota(jnp.int32, (t, n_cp), 1) * CMP_STRIDE + (CMP_BLOCK - 1)
    valid = tpos >= cend
    s = jnp.where(valid[None], s + cb_ref[...], NEG)
    e = jnp.exp(s - jnp.max(s, axis=-1, keepdims=True))
    p = jnp.where(valid[None], e / jnp.sum(e, axis=-1, keepdims=True), 0.0)
    o_cmp = jnp.dot(p.reshape(R * t, n_cp).astype(BF16), kvc, preferred_element_type=F32)

    psum = p[0]
    for r in range(1, R):
        psum = psum + p[r]
    imp = jnp.dot(psum, ov_ref[...], preferred_element_type=F32, precision=lax.Precision.HIGHEST)
    jj = lax.broadcasted_iota(jnp.int32, (t, n_sel), 1)
    blk_t = (t0 + lax.broadcasted_iota(jnp.int32, (t, n_sel), 0)) // SEL_BLOCK
    forced = (jj == 0) | (jj == blk_t) | (jj == blk_t - 1)
    score = jnp.where(forced, 1e6, jnp.where(jj <= blk_t, imp, -1e6))
    cnt = jnp.zeros((t, n_sel), jnp.int32)
    for jp in range(n_sel):
        col = score[:, jp:jp + 1]
        beats = (col > score) | ((col == score) & (jj > jp))
        cnt = cnt + beats.astype(jnp.int32)
    sel = (cnt < min(SEL_TOP_N, n_sel)).astype(BF16)
    selx = jnp.dot(sel, e_ref[...], preferred_element_type=F32)
    kpos = lax.broadcasted_iota(jnp.int32, (t, S), 1)
    trow = t0 + lax.broadcasted_iota(jnp.int32, (t, S), 0)
    mask_ref[...] = jnp.where((selx > 0.5) & (kpos <= trow), 0.0, NEG)

    a_i = lax.broadcasted_iota(jnp.int32, (t, t), 0)
    c_i = lax.broadcasted_iota(jnp.int32, (t, t), 1)

    def sweep(kv_ref, lo, windowed):
        def body(j, carry):
            off = pl.multiple_of(j * t, t)
            kv = kv_ref[pl.ds(off, t), :]
            sc = lax.dot_general(qs, kv, _NT, preferred_element_type=F32).reshape(R, t, t)
            delta = qi - j
            bias = d_ref[jnp.minimum(delta, 2)]
            if windowed:
                dist = delta * t + a_i - c_i
                add = jnp.where((dist >= 0) & (dist < WINDOW), 0.0, NEG)
            else:
                add = mask_ref[:, pl.ds(off, t)]
            sc = (sc + bias + add[None]).reshape(R * t, t)
            return _online_step(sc, kv, *carry)

        init = (jnp.full((R * t, 1), NEG, F32), jnp.zeros((R * t, 1), F32), jnp.zeros((R * t, LANE), F32))
        m, l, acc = lax.fori_loop(lo, qi + 1, body, init)
        return acc / l

    o_sel = sweep(kvs_ref, 0, False)
    o_win = sweep(kvw_ref, jnp.maximum(qi - WINDOW // t, 0), True)

    gates = gate_ref[...]
    glane = lax.broadcasted_iota(jnp.int32, (t, LANE), 1)
    lane = lax.broadcasted_iota(jnp.int32, (t, LANE), 1)
    heads = []
    for r in range(R):
        rs = slice(r * t, (r + 1) * t)
        h_idx = g * R + r
        o = jnp.zeros((t, LANE), F32)
        for br, ob in enumerate((o_cmp, o_sel, o_win)):
            gcol = jnp.sum(jnp.where(glane == br * NSA_HEADS + h_idx, gates, 0.0), axis=-1, keepdims=True)
            o = o + gcol * ob[rs]
        heads.append(o)
    pairs = [jnp.where(lane < NSA_V, pltpu.roll(heads[2 * i], NSA_V, 1), heads[2 * i + 1]) for i in range(R // 2)]
    o_ref[...] = jnp.concatenate(pairs, axis=-1).astype(BF16)


def _nsa_attn(q, kvc, kvs, kvw, gates, cbias, dtab, emat, ovl, B, S):
    t = ATT_T
    nq = S // t
    T = B * S
    G, R = NSA_GROUPS, NSA_R
    n_cp = kvc.shape[2]
    return pl.pallas_call(
        functools.partial(_nsa_attn_body, S=S),
        grid=(G, B, nq),
        in_specs=[pl.BlockSpec((t, R * LANE), lambda g, b, i: (b * nq + i, g)),
                  pl.BlockSpec((None, None, n_cp, LANE), lambda g, b, i: (b, g, 0, 0)),
                  pl.BlockSpec((S, LANE), lambda g, b, i: (b, g)),
                  pl.BlockSpec((S, LANE), lambda g, b, i: (b, g)),
                  pl.BlockSpec((t, LANE), lambda g, b, i: (b * nq + i, 0)),
                  pl.BlockSpec((None, R, t, n_cp), lambda g, b, i: (g, 0, i, 0)),
                  pl.BlockSpec((None, 3, R, t, t), lambda g, b, i: (g, 0, 0, 0, 0)),
                  _const_spec(emat.shape), _const_spec(ovl.shape)],
        out_specs=pl.BlockSpec((t, R * NSA_V), lambda g, b, i: (b * nq + i, g)),
        out_shape=jax.ShapeDtypeStruct((T, NSA_HEADS * NSA_V), BF16),
        scratch_shapes=[pltpu.VMEM((t, S), F32)],
        compiler_params=_cparams(("parallel", "parallel", "arbitrary")),
        name="nsa_attn",
    )(q, kvc, kvs, kvw, gates, cbias, dtab, emat, ovl)


def _t5_bucket(dist):
    n = jnp.maximum(dist, 0)
    max_exact = REL_BUCKETS // 2
    nf = jnp.maximum(n, 1).astype(F32)
    large = max_exact + (jnp.log(nf / max_exact) / math.log(REL_MAX_DIST / max_exact)
                         * (REL_BUCKETS - max_exact)).astype(jnp.int32)
    large = jnp.minimum(large, REL_BUCKETS - 1)
    return jnp.where(n < max_exact, n, large)


def _bias_tables(rel_bias, S):
    t = ATT_T
    G, R = NSA_GROUPS, NSA_R
    tblT = rel_bias.T
    a = np.arange(t)[:, None]
    c = np.arange(t)[None, :]
    dists = np.stack([a - c, t + a - c, np.full((t, t), 2 * t)])
    dtab = tblT[:, _t5_bucket(jnp.asarray(dists))]
    dtab = dtab.reshape(G, R, 3, t, t).transpose(0, 2, 1, 3, 4)
    n_cp = S // CMP_STRIDE
    cend = np.arange(n_cp) * CMP_STRIDE + CMP_BLOCK - 1
    cd = np.arange(S)[:, None] - cend[None, :]
    cbias = tblT[:, _t5_bucket(jnp.asarray(cd))].reshape(G, R, S, n_cp)
    return dtab.astype(F32), cbias.astype(F32)


def _selection_tables(S):
    n_cp = S // CMP_STRIDE
    n_cmp = (S - CMP_BLOCK) // CMP_STRIDE + 1
    n_sel = S // SEL_BLOCK
    cs = np.arange(n_cp) * CMP_STRIDE
    ce = cs + CMP_BLOCK
    ss = np.arange(n_sel) * SEL_BLOCK
    se = ss + SEL_BLOCK
    ov = np.minimum(ce[:, None], se[None, :]) - np.maximum(cs[:, None], ss[None, :])
    ov = (np.clip(ov, 0, None) / CMP_BLOCK).astype(np.float32)
    ov[n_cmp:] = 0.0
    emat = (np.arange(S)[None, :] // SEL_BLOCK == np.arange(n_sel)[:, None]).astype(np.float32)
    return jnp.asarray(ov), jnp.asarray(emat, dtype=BF16)


def _rope_tables(S):
    half = MLA_ROPE // 2
    inv = ROPE_THETA ** (-jnp.arange(half, dtype=F32) * 2.0 / MLA_ROPE)
    ang = jnp.arange(S, dtype=F32)[:, None] * inv[None, :]
    cos, sin = jnp.cos(ang), jnp.sin(ang)
    ones = jnp.ones((S, MLA_NOPE), F32)
    pad1 = jnp.ones((S, LANE - MLA_NOPE - MLA_ROPE), F32)
    cos128 = jnp.concatenate([ones, cos, cos, pad1], axis=1)
    sin128 = jnp.concatenate([0 * ones, sin, sin, 0 * pad1], axis=1)
    return cos128, sin128


def _mla_weights(w_in, q_norm, kv_norm, w_uq, w_ukv, w_o):
    H = MLA_HEADS
    half = MLA_ROPE // 2
    pad = LANE - MLA_NOPE - MLA_ROPE
    kr = w_in[:, MLA_Q_LORA + MLA_KV_LORA:]
    kr_sw = jnp.concatenate([-kr[:, half:], kr[:, :half]], axis=1)
    z = lambda n: jnp.zeros((D_MODEL, n), F32)
    w_in_ext = jnp.concatenate([w_in[:, :MLA_Q_LORA + MLA_KV_LORA],
                                z(MLA_NOPE), kr, z(pad), z(MLA_NOPE), kr_sw, z(pad)], axis=1)
    uq = w_uq.reshape(MLA_Q_LORA, H, MLA_NOPE + MLA_ROPE)
    qn, qr = uq[..., :MLA_NOPE], uq[..., MLA_NOPE:]
    qr_sw = jnp.concatenate([-qr[..., half:], qr[..., :half]], axis=-1)
    zq = jnp.zeros((MLA_Q_LORA, H, pad), F32)
    w_uq_p = jnp.concatenate([qn, qr, zq], axis=-1).reshape(MLA_Q_LORA, H * LANE)
    w_uq_s = jnp.concatenate([0 * qn, qr_sw, zq], axis=-1).reshape(MLA_Q_LORA, H * LANE)
    ukv = w_ukv.reshape(MLA_KV_LORA, H, MLA_NOPE + MLA_V)
    zk = jnp.zeros((MLA_KV_LORA, H, LANE - MLA_NOPE), F32)
    w_uk = jnp.concatenate([ukv[..., :MLA_NOPE], zk], axis=-1).reshape(MLA_KV_LORA, H * LANE)
    w_uv = ukv[..., MLA_NOPE:].reshape(MLA_KV_LORA, H * MLA_V)
    return dict(w_in=w_in_ext.astype(BF16), q_norm=q_norm.reshape(1, -1), kv_norm=kv_norm.reshape(1, -1),
                w_uq=w_uq_p.astype(BF16), w_uqs=w_uq_s.astype(BF16), w_uk=w_uk.astype(BF16),
                w_uv=w_uv.astype(BF16), w_o=w_o.astype(BF16))


def _nsa_weights(w_in, pos_k, w1_k, w2_k, pos_v, w1_v, w2_v, w_o):
    H, G = NSA_HEADS, NSA_GROUPS
    gw = G * NSA_QK
    q0 = H * NSA_QK
    wq = w_in[:, :q0].reshape(D_MODEL, H, NSA_QK)
    wq = jnp.concatenate([wq, jnp.zeros_like(wq)], axis=-1).reshape(D_MODEL, H * LANE)
    cols = [wq]
    for n in range(3):
        k = w_in[:, q0 + (2 * n) * gw: q0 + (2 * n + 1) * gw].reshape(D_MODEL, G, NSA_QK)
        v = w_in[:, q0 + (2 * n + 1) * gw: q0 + (2 * n + 2) * gw].reshape(D_MODEL, G, NSA_V)
        cols.append(jnp.concatenate([k, v], axis=-1).reshape(D_MODEL, G * LANE))
    wg = w_in[:, q0 + 6 * gw:].reshape(D_MODEL, H, 3).transpose(0, 2, 1).reshape(D_MODEL, 3 * H)
    cols.append(jnp.concatenate([wg, jnp.zeros((D_MODEL, LANE - 3 * H), F32)], axis=1))
    w_in_ext = jnp.concatenate(cols, axis=1).astype(BF16)
    half = CMP_STRIDE * NSA_QK
    pos = jnp.stack([pos_k.reshape(2, half), pos_v.reshape(2, half)])
    w1 = jnp.stack([w1_k, w1_v]).astype(BF16)
    w2 = jnp.stack([w2_k, w2_v]).astype(BF16)
    return dict(w_in=w_in_ext, pos=pos, w1=w1, w2=w2, w_o=w_o.astype(BF16))


def kernel(x, ffn_norm_a, ffn_a_w_gate, ffn_a_w_up, ffn_a_w_down, mix_norm, ffn_norm_b, ffn_b_w_gate, ffn_b_w_up, ffn_b_w_down, final_norm, rel_bias, mla_w_in, mla_q_norm, mla_kv_norm, mla_w_uq, mla_w_ukv, mla_w_o, nsa_w_in, nsa_cmp_pos_k, nsa_cmp_w1_k, nsa_cmp_w2_k, nsa_cmp_pos_v, nsa_cmp_w1_v, nsa_cmp_w2_v, nsa_w_o):
    B, S, D = x.shape
    assert D == D_MODEL and S % ATT_T == 0 and (B * S) % FFN_TM == 0
    T = B * S
    G = NSA_GROUPS
    n_chunk = S // CMP_STRIDE
    cos128, sin128 = _rope_tables(S)
    dtab, cbias = _bias_tables(rel_bias, S)
    ovl, emat = _selection_tables(S)
    bf = lambda w: w.astype(BF16)

    h = x.reshape(T, D)
    for i in range(DEPTH):
        h = _ffn(h, ffn_norm_a[i], bf(ffn_a_w_gate[i]), bf(ffn_a_w_up[i]), bf(ffn_a_w_down[i]))
        j = i // N_MIXERS
        if i % N_MIXERS == 0:
            w = _mla_weights(mla_w_in[j], mla_q_norm[j], mla_kv_norm[j], mla_w_uq[j], mla_w_ukv[j], mla_w_o[j])
            q, k, v = _mla_proj(h, mix_norm[i], w, cos128, sin128, S)
            o = _mla_attn(q, k, v, B, S)
        else:
            w = _nsa_weights(nsa_w_in[j], nsa_cmp_pos_k[j], nsa_cmp_w1_k[j], nsa_cmp_w2_k[j],
                             nsa_cmp_pos_v[j], nsa_cmp_w1_v[j], nsa_cmp_w2_v[j], nsa_w_o[j])
            q, kvc_in, kvs, kvw, gates = _nsa_proj(h, mix_norm[i], w["w_in"])
            xc = kvc_in.reshape(B, n_chunk, CMP_STRIDE, G, 2, NSA_QK).transpose(0, 3, 4, 1, 2, 5)
            xc = xc.reshape(B, G, 2, n_chunk, CMP_STRIDE * NSA_QK)
            kvc = _compress(xc, w["pos"], w["w1"], w["w2"])
            o = _nsa_attn(q, kvc, kvs, kvw, gates, cbias, dtab, emat, ovl, B, S)
        h = _ffn(h, ffn_norm_b[i], bf(ffn_b_w_gate[i]), bf(ffn_b_w_up[i]), bf(ffn_b_w_down[i]),
                 proj=(o, w["w_o"]), final_g=final_norm if i == DEPTH - 1 else None)
    return h.reshape(B, S, D)
```

```python
import functools
import math

import numpy as np
import jax
import jax.numpy as jnp
from jax import lax
from jax.experimental import pallas as pl
from jax.experimental.pallas import tpu as pltpu

F32 = jnp.float32
BF16 = jnp.bfloat16

D_MODEL = 1024
DEPTH = 4
N_MIXERS = 2
RMS_EPS = 1e-6
FFN_HIDDEN = 2816
NEG = -1e30
MLA_HEADS = 16
MLA_Q_LORA = 384
MLA_KV_LORA = 256
MLA_NOPE = 64
MLA_ROPE = 32
MLA_V = 64
ROPE_THETA = 10000.0
NSA_HEADS = 16
NSA_GROUPS = 4
NSA_R = NSA_HEADS // NSA_GROUPS
NSA_QK = 64
NSA_V = 64
CMP_BLOCK = 32
CMP_STRIDE = 16
CMP_HIDDEN = 128
SEL_BLOCK = 64
SEL_TOP_N = 16
WINDOW = 512
REL_BUCKETS = 32
REL_MAX_DIST = 128

LANE = 128
VMEM_LIMIT = 56 * 1024 * 1024

FFN_TM = 512
FFN_TF = 256
PROJ_TM = 256
ATT_T = 256
NSA_T = 128
CB_CENTER = 64


def _cparams(sem):
    return pltpu.CompilerParams(dimension_semantics=sem, vmem_limit_bytes=VMEM_LIMIT)


def _rms(x, g):
    ms = jnp.mean(x * x, axis=-1, keepdims=True)
    return x * lax.rsqrt(ms + RMS_EPS) * g


def _const_spec(shape):
    nd = len(shape)
    return pl.BlockSpec(shape, lambda *_: (0,) * nd)


def _ffn_body(*refs, has_proj, has_final):
    it = iter(refs)
    h_ref = next(it)
    if has_proj:
        o_in_ref = next(it)
        wo_ref = next(it)
    g_ref = next(it)
    wg_ref = next(it)
    wu_ref = next(it)
    wd_ref = next(it)
    if has_final:
        gf_ref = next(it)
    out_ref = next(it)
    a_ref = next(it)

    x = h_ref[...]
    if has_proj:
        x = x + jnp.dot(o_in_ref[...], wo_ref[...], preferred_element_type=F32)
    xn = _rms(x, g_ref[...]).astype(BF16)
    for c in range(FFN_HIDDEN // FFN_TF):
        sl = slice(c * FFN_TF, (c + 1) * FFN_TF)
        gt = jnp.dot(xn, wg_ref[:, sl], preferred_element_type=F32)
        up = jnp.dot(xn, wu_ref[:, sl], preferred_element_type=F32)
        a_ref[:, sl] = (gt * jax.nn.sigmoid(gt) * up).astype(BF16)
    y = x + 0.5 * jnp.dot(a_ref[...], wd_ref[...], preferred_element_type=F32)
    if has_final:
        y = _rms(y, gf_ref[...])
    out_ref[...] = y


def _ffn(h, g, wg, wu, wd, proj=None, final_g=None):
    T = h.shape[0]
    tm = FFN_TM
    row = lambda i: (i, 0)
    in_specs = [pl.BlockSpec((tm, D_MODEL), row)]
    args = [h]
    if proj is not None:
        o_in, wo = proj
        in_specs += [pl.BlockSpec((tm, o_in.shape[1]), row), _const_spec(wo.shape)]
        args += [o_in, wo]
    in_specs += [_const_spec((1, D_MODEL)), _const_spec(wg.shape), _const_spec(wu.shape), _const_spec(wd.shape)]
    args += [g.reshape(1, D_MODEL), wg, wu, wd]
    if final_g is not None:
        in_specs.append(_const_spec((1, D_MODEL)))
        args.append(final_g.reshape(1, D_MODEL))
    return pl.pallas_call(
        functools.partial(_ffn_body, has_proj=proj is not None, has_final=final_g is not None),
        grid=(T // tm,),
        in_specs=in_specs,
        out_specs=pl.BlockSpec((tm, D_MODEL), row),
        out_shape=jax.ShapeDtypeStruct((T, D_MODEL), F32),
        scratch_shapes=[pltpu.VMEM((tm, FFN_HIDDEN), BF16)],
        compiler_params=_cparams(("parallel",)),
        name="ffn",
    )(*args)


MLA_CQ0, MLA_CKV0, MLA_KR0, MLA_KRS0, MLA_IN_W = 0, 384, 640, 768, 896
MLA_HCHUNK = 4


def _mla_proj_body(h_ref, g_ref, win_ref, qn_ref, kvn_ref, wuq_ref, wuqs_ref, wuk_ref, wuv_ref,
                   cos_ref, sin_ref, q_out, k_out, v_out):
    xn = _rms(h_ref[...], g_ref[...]).astype(BF16)
    proj = jnp.dot(xn, win_ref[...], preferred_element_type=F32)
    cq = _rms(proj[:, MLA_CQ0:MLA_CKV0], qn_ref[...]).astype(BF16)
    ckv = _rms(proj[:, MLA_CKV0:MLA_KR0], kvn_ref[...]).astype(BF16)
    cos = cos_ref[...]
    sin = sin_ref[...]
    kr = proj[:, MLA_KR0:MLA_KRS0] * cos + proj[:, MLA_KRS0:MLA_IN_W] * sin
    scale = (MLA_NOPE + MLA_ROPE) ** -0.5
    v_out[...] = jnp.dot(ckv, wuv_ref[...], preferred_element_type=F32).astype(BF16)
    cw = MLA_HCHUNK * LANE
    for c in range(MLA_HEADS // MLA_HCHUNK):
        sl = slice(c * cw, (c + 1) * cw)
        q = jnp.dot(cq, wuq_ref[:, sl], preferred_element_type=F32)
        qs = jnp.dot(cq, wuqs_ref[:, sl], preferred_element_type=F32)
        kn = jnp.dot(ckv, wuk_ref[:, sl], preferred_element_type=F32)
        for hh in range(MLA_HCHUNK):
            hs = slice(hh * LANE, (hh + 1) * LANE)
            os_ = slice(c * cw + hh * LANE, c * cw + (hh + 1) * LANE)
            q_out[:, os_] = ((q[:, hs] * cos + qs[:, hs] * sin) * scale).astype(BF16)
            k_out[:, os_] = (kn[:, hs] + kr).astype(BF16)


def _mla_proj(h, g, w, cos128, sin128, S):
    T = h.shape[0]
    tm = PROJ_TM
    ns = S // tm
    row = lambda i: (i, 0)
    pos = lambda i: (i % ns, 0)
    HL = MLA_HEADS * LANE
    return pl.pallas_call(
        _mla_proj_body,
        grid=(T // tm,),
        in_specs=[pl.BlockSpec((tm, D_MODEL), row), _const_spec((1, D_MODEL)),
                  _const_spec(w["w_in"].shape), _const_spec((1, MLA_Q_LORA)), _const_spec((1, MLA_KV_LORA)),
                  _const_spec(w["w_uq"].shape), _const_spec(w["w_uqs"].shape),
                  _const_spec(w["w_uk"].shape), _const_spec(w["w_uv"].shape),
                  pl.BlockSpec((tm, LANE), pos), pl.BlockSpec((tm, LANE), pos)],
        out_specs=[pl.BlockSpec((tm, HL), row), pl.BlockSpec((tm, HL), row),
                   pl.BlockSpec((tm, MLA_HEADS * MLA_V), row)],
        out_shape=[jax.ShapeDtypeStruct((T, HL), BF16), jax.ShapeDtypeStruct((T, HL), BF16),
                   jax.ShapeDtypeStruct((T, MLA_HEADS * MLA_V), BF16)],
        compiler_params=_cparams(("parallel",)),
        name="mla_proj",
    )(h, g.reshape(1, D_MODEL), w["w_in"], w["q_norm"], w["kv_norm"], w["w_uq"], w["w_uqs"],
      w["w_uk"], w["w_uv"], cos128, sin128)


def _online_step(s, kvv, m, l, acc):
    m_new = jnp.maximum(m, jnp.max(s, axis=-1, keepdims=True))
    alpha = jnp.exp(m - m_new)
    p = jnp.exp(s - m_new)
    l = alpha * l + jnp.sum(p, axis=-1, keepdims=True)
    acc = alpha * acc + jnp.dot(p.astype(BF16), kvv, preferred_element_type=F32)
    return m_new, l, acc


_NT = (((1,), (1,)), ((), ()))


def _attend(parts):
    m = None
    for s, _ in parts:
        mi = jnp.max(s, axis=-1, keepdims=True)
        m = mi if m is None else jnp.maximum(m, mi)
    l = None
    acc = None
    for s, v in parts:
        p = jnp.exp(s - m)
        li = jnp.sum(p, axis=-1, keepdims=True)
        ai = jnp.dot(p.astype(BF16), v, preferred_element_type=F32)
        l = li if l is None else l + li
        acc = ai if acc is None else acc + ai
    return acc / l


def _mla_attn_body(q_ref, k_ref, v_ref, o_ref, *, nq):
    t = ATT_T
    qi = pl.program_id(2)

    def branch(nt):
        n0 = (nt - 1) * t
        causal = lax.broadcasted_iota(jnp.int32, (t, t), 0) >= lax.broadcasted_iota(jnp.int32, (t, t), 1)
        outs = []
        for hh in range(2):
            hs = slice(hh * LANE, (hh + 1) * LANE)
            q = q_ref[:, hs]
            parts = []
            if nt > 1:
                parts.append((lax.dot_general(q, k_ref[0:n0, hs], _NT, preferred_element_type=F32), v_ref[0:n0, :]))
            sd = lax.dot_general(q, k_ref[n0:n0 + t, hs], _NT, preferred_element_type=F32)
            parts.append((jnp.where(causal, sd, NEG), v_ref[n0:n0 + t, :]))
            outs.append(_attend(parts))
        lane = lax.broadcasted_iota(jnp.int32, (t, LANE), 1)
        o_ref[...] = jnp.where(lane < MLA_V, outs[0], outs[1]).astype(BF16)

    for nt in range(1, nq + 1):
        pl.when(qi == nt - 1)(functools.partial(branch, nt))


def _mla_attn(q, k, v, B, S):
    t = ATT_T
    nq = S // t
    T = B * S
    return pl.pallas_call(
        functools.partial(_mla_attn_body, nq=nq),
        grid=(B, MLA_HEADS // 2, nq),
        in_specs=[pl.BlockSpec((t, 2 * LANE), lambda b, p, i: (b * nq + i, p)),
                  pl.BlockSpec((S, 2 * LANE), lambda b, p, i: (b, p)),
                  pl.BlockSpec((S, LANE), lambda b, p, i: (b, p))],
        out_specs=pl.BlockSpec((t, LANE), lambda b, p, i: (b * nq + i, p)),
        out_shape=jax.ShapeDtypeStruct((T, MLA_HEADS * MLA_V), BF16),
        compiler_params=_cparams(("parallel", "parallel", "arbitrary")),
        name="mla_attn",
    )(q, k, v)


NSA_QW = NSA_HEADS * LANE
NSA_KVW = NSA_GROUPS * LANE
NSA_IN_W = NSA_QW + 3 * NSA_KVW + LANE


def _nsa_proj_body(h_ref, g_ref, win_ref, q_out, kvc_out, kvs_out, kvw_out, gate_out):
    xn = _rms(h_ref[...], g_ref[...]).astype(BF16)
    scale = NSA_QK ** -0.5
    cw = 4 * LANE
    for c in range(NSA_QW // cw):
        sl = slice(c * cw, (c + 1) * cw)
        q_out[:, sl] = (jnp.dot(xn, win_ref[:, sl], preferred_element_type=F32) * scale).astype(BF16)
    for n, out in enumerate((kvc_out, kvs_out, kvw_out)):
        sl = slice(NSA_QW + n * NSA_KVW, NSA_QW + (n + 1) * NSA_KVW)
        out[...] = jnp.dot(xn, win_ref[:, sl], preferred_element_type=F32).astype(BF16)
    gl = jnp.dot(xn, win_ref[:, NSA_QW + 3 * NSA_KVW:], preferred_element_type=F32)
    gate_out[...] = jax.nn.sigmoid(gl)


def _nsa_proj(h, g, w_in):
    T = h.shape[0]
    tm = PROJ_TM
    row = lambda i: (i, 0)
    return pl.pallas_call(
        _nsa_proj_body,
        grid=(T // tm,),
        in_specs=[pl.BlockSpec((tm, D_MODEL), row), _const_spec((1, D_MODEL)), _const_spec(w_in.shape)],
        out_specs=[pl.BlockSpec((tm, NSA_QW), row), pl.BlockSpec((tm, NSA_KVW), row),
                   pl.BlockSpec((tm, NSA_KVW), row), pl.BlockSpec((tm, NSA_KVW), row),
                   pl.BlockSpec((tm, LANE), row)],
        out_shape=[jax.ShapeDtypeStruct((T, NSA_QW), BF16), jax.ShapeDtypeStruct((T, NSA_KVW), BF16),
                   jax.ShapeDtypeStruct((T, NSA_KVW), BF16), jax.ShapeDtypeStruct((T, NSA_KVW), BF16),
                   jax.ShapeDtypeStruct((T, LANE), F32)],
        compiler_params=_cparams(("parallel",)),
        name="nsa_proj",
    )(h, g.reshape(1, D_MODEL), w_in)


def _compress_body(x_ref, pos_ref, w1_ref, w2_ref, out_ref):
    n_chunk = x_ref.shape[0]
    x = x_ref[...].astype(F32)
    xa = (x + pos_ref[0]).astype(BF16)
    xb = (x + pos_ref[1]).astype(BF16)
    a = jnp.dot(xa, w1_ref[0], preferred_element_type=F32)
    b = jnp.dot(xb, w1_ref[1], preferred_element_type=F32)
    pre = a + pltpu.roll(b, n_chunk - 1, 0)
    hid = jax.nn.gelu(pre, approximate=True).astype(BF16)
    out_ref[...] = jnp.dot(hid, w2_ref[...], preferred_element_type=F32).astype(BF16)


def _compress(x, pos, w1, w2):
    B, G, n_chunk, width = x.shape
    return pl.pallas_call(
        _compress_body,
        grid=(B, G),
        in_specs=[pl.BlockSpec((None, None, n_chunk, width), lambda b, g: (b, g, 0, 0)),
                  _const_spec(pos.shape), _const_spec(w1.shape), _const_spec(w2.shape)],
        out_specs=pl.BlockSpec((None, None, n_chunk, LANE), lambda b, g: (b, g, 0, 0)),
        out_shape=jax.ShapeDtypeStruct((B, G, n_chunk, LANE), BF16),
        compiler_params=_cparams(("parallel", "parallel")),
        name="nsa_compress",
    )(x, pos, w1, w2)


def _nsa_attn_body(q_ref, kvc_ref, kvs_ref, kvw_ref, gate_ref, cb_ref, d_ref, e_ref, ov_ref,
                   o_ref, osel_ref, owin_ref, *, S):
    t = NSA_T
    R = NSA_R
    n_sel = S // SEL_BLOCK
    nq = S // t
    g = pl.program_id(0)
    qi = pl.program_id(2)
    t0 = qi * t
    qs = jnp.concatenate([q_ref[:, r * LANE:(r + 1) * LANE] for r in range(R)], axis=0)

    kvc = kvc_ref[...]
    n_cp = kvc.shape[0]
    s = lax.dot_general(qs, kvc, _NT, preferred_element_type=F32).reshape(R, t, n_cp)
    tpos = t0 + lax.broadcasted_iota(jnp.int32, (t, n_cp), 0)
    cend = lax.broadcasted_iota(jnp.int32, (t, n_cp), 1) * CMP_STRIDE + (CMP_BLOCK - 1)
    valid = tpos >= cend
    cb = pltpu.roll(cb_ref[...], (qi * (t // CMP_STRIDE) + CB_CENTER) % n_cp, 2)
    s = jnp.where(valid[None], s + cb, NEG)
    e = jnp.exp(s - jnp.max(s, axis=-1, keepdims=True))
    p = jnp.where(valid[None], e / jnp.sum(e, axis=-1, keepdims=True), 0.0)
    o_cmp = jnp.dot(p.reshape(R * t, n_cp).astype(BF16), kvc, preferred_element_type=F32)

    psum = p[0]
    for r in range(1, R):
        psum = psum + p[r]
    imp = jnp.dot(psum, ov_ref[...], preferred_element_type=F32, precision=lax.Precision.HIGHEST)
    jj = lax.broadcasted_iota(jnp.int32, (t, n_sel), 1)
    blk_t = (t0 + lax.broadcasted_iota(jnp.int32, (t, n_sel), 0)) // SEL_BLOCK
    forced = (jj == 0) | (jj == blk_t) | (jj == blk_t - 1)
    score = jnp.where(forced, 1e6, jnp.where(jj <= blk_t, imp, -1e6))
    cnt = jnp.zeros((t, n_sel), jnp.int32)
    for jp in range(n_sel):
        col = score[:, jp:jp + 1]
        beats = (col > score) | ((col == score) & (jj > jp))
        cnt = cnt + beats.astype(jnp.int32)
    sel = jnp.where(cnt < min(SEL_TOP_N, n_sel), 1.0, 0.0).astype(BF16)

    def scored(kv_main, kv_tail, add_main, add_tail):
        n_tail = kv_tail.shape[0]
        parts = []
        if kv_main is not None:
            sm = lax.dot_general(qs, kv_main, _NT, preferred_element_type=F32).reshape(R, t, kv_main.shape[0])
            parts.append(((sm + add_main[None]).reshape(R * t, kv_main.shape[0]), kv_main))
        st = lax.dot_general(qs, kv_tail, _NT, preferred_element_type=F32).reshape(R, t, n_tail)
        st = st + d_ref[:, :, 2 * t - n_tail:] + add_tail[None]
        parts.append((st.reshape(R * t, n_tail), kv_tail))
        return _attend(parts)

    def sel_arm(nt):
        n = nt * t
        n0 = max(n - 2 * t, 0)
        selx = jnp.dot(sel, e_ref[:, 0:n], preferred_element_type=F32)
        kpos = lax.broadcasted_iota(jnp.int32, (t, n), 1)
        trow = t0 + lax.broadcasted_iota(jnp.int32, (t, n), 0)
        add = jnp.where((selx > 0.5) & (kpos <= trow), 0.0, NEG)
        kv_main = kvs_ref[0:n0, :] if n0 > 0 else None
        osel_ref[...] = scored(kv_main, kvs_ref[n0:n, :], add[:, 0:n0] if n0 > 0 else None, add[:, n0:n])

    def win_arm(kw):
        n = (kw + 1) * t
        n0 = max(n - 2 * t, 0)
        start = pl.multiple_of((qi - kw) * t, t)
        dist = kw * t + lax.broadcasted_iota(jnp.int32, (t, n), 0) - lax.broadcasted_iota(jnp.int32, (t, n), 1)
        add = jnp.where((dist >= 0) & (dist < WINDOW), 0.0, NEG)
        kv_main = kvw_ref[pl.ds(start, n0), :] if n0 > 0 else None
        kv_tail = kvw_ref[pl.ds(start + n0, n - n0), :]
        owin_ref[...] = scored(kv_main, kv_tail, add[:, 0:n0] if n0 > 0 else None, add[:, n0:n])

    for nt in range(1, nq + 1):
        pl.when(qi == nt - 1)(functools.partial(sel_arm, nt))
    kw_max = WINDOW // t
    for kw in range(kw_max + 1):
        cond = (qi == kw) if kw < kw_max else (qi >= kw_max)
        pl.when(cond)(functools.partial(win_arm, kw))
    o_sel = osel_ref[...]
    o_win = owin_ref[...]

    gates = gate_ref[...]
    glane = lax.broadcasted_iota(jnp.int32, (t, LANE), 1)
    lane = lax.broadcasted_iota(jnp.int32, (t, LANE), 1)
    heads = []
    for r in range(R):
        rs = slice(r * t, (r + 1) * t)
        h_idx = g * R + r
        o = jnp.zeros((t, LANE), F32)
        for br, ob in enumerate((o_cmp, o_sel, o_win)):
            gcol = jnp.sum(jnp.where(glane == br * NSA_HEADS + h_idx, gates, 0.0), axis=-1, keepdims=True)
            o = o + gcol * ob[rs]
        heads.append(o)
    pairs = [jnp.where(lane < NSA_V, pltpu.roll(heads[2 * i], NSA_V, 1), heads[2 * i + 1]) for i in range(R // 2)]
    o_ref[...] = jnp.concatenate(pairs, axis=-1).astype(BF16)


def _nsa_attn(q, kvc, kvs, kvw, gates, cbias, dtab, emat, ovl, B, S):
    t = NSA_T
    nq = S // t
    T = B * S
    G, R = NSA_GROUPS, NSA_R
    n_cp = kvc.shape[2]
    return pl.pallas_call(
        functools.partial(_nsa_attn_body, S=S),
        grid=(G, B, nq),
        in_specs=[pl.BlockSpec((t, R * LANE), lambda g, b, i: (b * nq + i, g)),
                  pl.BlockSpec((None, None, n_cp, LANE), lambda g, b, i: (b, g, 0, 0)),
                  pl.BlockSpec((S, LANE), lambda g, b, i: (b, g)),
                  pl.BlockSpec((S, LANE), lambda g, b, i: (b, g)),
                  pl.BlockSpec((t, LANE), lambda g, b, i: (b * nq + i, 0)),
                  pl.BlockSpec((None, R, t, n_cp), lambda g, b, i: (g, 0, 0, 0)),
                  pl.BlockSpec((None, R, t, 2 * t), lambda g, b, i: (g, 0, 0, 0)),
                  _const_spec(emat.shape), _const_spec(ovl.shape)],
        out_specs=pl.BlockSpec((t, R * NSA_V), lambda g, b, i: (b * nq + i, g)),
        out_shape=jax.ShapeDtypeStruct((T, NSA_HEADS * NSA_V), BF16),
        scratch_shapes=[pltpu.VMEM((R * t, LANE), F32), pltpu.VMEM((R * t, LANE), F32)],
        compiler_params=_cparams(("parallel", "parallel", "arbitrary")),
        name="nsa_attn",
    )(q, kvc, kvs, kvw, gates, cbias, dtab, emat, ovl)


def _t5_bucket(dist):
    n = jnp.maximum(dist, 0)
    max_exact = REL_BUCKETS // 2
    nf = jnp.maximum(n, 1).astype(F32)
    large = max_exact + (jnp.log(nf / max_exact) / math.log(REL_MAX_DIST / max_exact)
                         * (REL_BUCKETS - max_exact)).astype(jnp.int32)
    large = jnp.minimum(large, REL_BUCKETS - 1)
    return jnp.where(n < max_exact, n, large)


def _np_bucket(n):
    n = np.maximum(np.asarray(n), 0)
    max_exact = REL_BUCKETS // 2
    large = max_exact + (np.log(np.maximum(n, 1) / max_exact) / math.log(REL_MAX_DIST / max_exact)
                         * (REL_BUCKETS - max_exact)).astype(np.int64)
    return np.where(n < max_exact, n, np.minimum(large, REL_BUCKETS - 1))


def _bias_tables(rel_bias, S):
    t = NSA_T
    G, R = NSA_GROUPS, NSA_R
    n_cp = S // CMP_STRIDE
    a = np.arange(t)[:, None]
    dist_d = a - np.arange(2 * t)[None, :] + t
    dist_c = a - CMP_STRIDE * (np.arange(n_cp)[None, :] - CB_CENTER) - (CMP_BLOCK - 1)
    uncovered = min(t + 1, CMP_STRIDE * (CB_CENTER + 1) - (CMP_BLOCK - 1))
    assert (_np_bucket(np.arange(uncovered, 2 * S)) == REL_BUCKETS - 1).all()
    wrap_from = n_cp - max((S // t - 1) * (t // CMP_STRIDE) - CB_CENTER, 0)
    assert (dist_c[:, wrap_from:] < 0).all() and (dist_c[:, -1] < 0).all()

    def lookup(dist):
        oh = jax.nn.one_hot(_t5_bucket(jnp.asarray(dist)), REL_BUCKETS, dtype=F32)
        oh = oh - jax.nn.one_hot(REL_BUCKETS - 1, REL_BUCKETS, dtype=F32)
        val = jnp.einsum("acb,bh->hac", oh, rel_bias, precision=lax.Precision.HIGHEST)
        return val.reshape(G, R, *dist.shape)

    dtab = lookup(dist_d)
    cbias = jnp.where(jnp.asarray(dist_c >= 0), lookup(dist_c), 0.0)
    return dtab, cbias


def _selection_tables(S):
    n_cp = S // CMP_STRIDE
    n_cmp = (S - CMP_BLOCK) // CMP_STRIDE + 1
    n_sel = S // SEL_BLOCK
    cs = np.arange(n_cp) * CMP_STRIDE
    ce = cs + CMP_BLOCK
    ss = np.arange(n_sel) * SEL_BLOCK
    se = ss + SEL_BLOCK
    ov = np.minimum(ce[:, None], se[None, :]) - np.maximum(cs[:, None], ss[None, :])
    ov = (np.clip(ov, 0, None) / CMP_BLOCK).astype(np.float32)
    ov[n_cmp:] = 0.0
    emat = (np.arange(S)[None, :] // SEL_BLOCK == np.arange(n_sel)[:, None]).astype(np.float32)
    return jnp.asarray(ov), jnp.asarray(emat, dtype=BF16)


def _rope_tables(S):
    half = MLA_ROPE // 2
    inv = ROPE_THETA ** (-jnp.arange(half, dtype=F32) * 2.0 / MLA_ROPE)
    ang = jnp.arange(S, dtype=F32)[:, None] * inv[None, :]
    cos, sin = jnp.cos(ang), jnp.sin(ang)
    ones = jnp.ones((S, MLA_NOPE), F32)
    pad1 = jnp.ones((S, LANE - MLA_NOPE - MLA_ROPE), F32)
    cos128 = jnp.concatenate([ones, cos, cos, pad1], axis=1)
    sin128 = jnp.concatenate([0 * ones, sin, sin, 0 * pad1], axis=1)
    return cos128, sin128


def _mla_weights(w_in, q_norm, kv_norm, w_uq, w_ukv, w_o):
    H = MLA_HEADS
    half = MLA_ROPE // 2
    pad = LANE - MLA_NOPE - MLA_ROPE
    kr = w_in[:, MLA_Q_LORA + MLA_KV_LORA:]
    kr_sw = jnp.concatenate([-kr[:, half:], kr[:, :half]], axis=1)
    z = lambda n: jnp.zeros((D_MODEL, n), F32)
    w_in_ext = jnp.concatenate([w_in[:, :MLA_Q_LORA + MLA_KV_LORA],
                                z(MLA_NOPE), kr, z(pad), z(MLA_NOPE), kr_sw, z(pad)], axis=1)
    uq = w_uq.reshape(MLA_Q_LORA, H, MLA_NOPE + MLA_ROPE)
    qn, qr = uq[..., :MLA_NOPE], uq[..., MLA_NOPE:]
    qr_sw = jnp.concatenate([-qr[..., half:], qr[..., :half]], axis=-1)
    zq = jnp.zeros((MLA_Q_LORA, H, pad), F32)
    w_uq_p = jnp.concatenate([qn, qr, zq], axis=-1).reshape(MLA_Q_LORA, H * LANE)
    w_uq_s = jnp.concatenate([0 * qn, qr_sw, zq], axis=-1).reshape(MLA_Q_LORA, H * LANE)
    ukv = w_ukv.reshape(MLA_KV_LORA, H, MLA_NOPE + MLA_V)
    zk = jnp.zeros((MLA_KV_LORA, H, LANE - MLA_NOPE), F32)
    w_uk = jnp.concatenate([ukv[..., :MLA_NOPE], zk], axis=-1).reshape(MLA_KV_LORA, H * LANE)
    w_uv = ukv[..., MLA_NOPE:].reshape(MLA_KV_LORA, H * MLA_V)
    return dict(w_in=w_in_ext.astype(BF16), q_norm=q_norm.reshape(1, -1), kv_norm=kv_norm.reshape(1, -1),
                w_uq=w_uq_p.astype(BF16), w_uqs=w_uq_s.astype(BF16), w_uk=w_uk.astype(BF16),
                w_uv=w_uv.astype(BF16), w_o=w_o.astype(BF16))


def _nsa_weights(w_in, pos_k, w1_k, w2_k, pos_v, w1_v, w2_v, w_o):
    H, G = NSA_HEADS, NSA_GROUPS
    gw = G * NSA_QK
    q0 = H * NSA_QK
    wq = w_in[:, :q0].reshape(D_MODEL, H, NSA_QK)
    wq = jnp.concatenate([wq, jnp.zeros_like(wq)], axis=-1).reshape(D_MODEL, H * LANE)
    cols = [wq]
    for n in range(3):
        k = w_in[:, q0 + (2 * n) * gw: q0 + (2 * n + 1) * gw].reshape(D_MODEL, G, NSA_QK)
        v = w_in[:, q0 + (2 * n + 1) * gw: q0 + (2 * n + 2) * gw].reshape(D_MODEL, G, NSA_V)
        cols.append(jnp.concatenate([k, v], axis=-1).reshape(D_MODEL, G * LANE))
    wg = w_in[:, q0 + 6 * gw:].reshape(D_MODEL, H, 3).transpose(0, 2, 1).reshape(D_MODEL, 3 * H)
    cols.append(jnp.concatenate([wg, jnp.zeros((D_MODEL, LANE - 3 * H), F32)], axis=1))
    w_in_ext = jnp.concatenate(cols, axis=1).astype(BF16)
    eye = jnp.eye(2, dtype=F32)
    cw = CMP_STRIDE * 2 * NSA_QK
    pos = jnp.stack([pos_k, pos_v]).reshape(2, 2, CMP_STRIDE, NSA_QK)
    pos = pos.transpose(1, 2, 0, 3).reshape(2, 1, cw)
    w1 = jnp.stack([w1_k, w1_v]).reshape(2, 2, CMP_STRIDE, NSA_QK, CMP_HIDDEN)
    w1 = jnp.einsum("khldj,kq->hlkdqj", w1, eye).reshape(2, cw, 2 * CMP_HIDDEN).astype(BF16)
    w2 = jnp.einsum("kjd,kq->kjqd", jnp.stack([w2_k, w2_v]), eye).reshape(2 * CMP_HIDDEN, 2 * NSA_QK).astype(BF16)
    return dict(w_in=w_in_ext, pos=pos, w1=w1, w2=w2, w_o=w_o.astype(BF16))


def kernel(x, ffn_norm_a, ffn_a_w_gate, ffn_a_w_up, ffn_a_w_down, mix_norm, ffn_norm_b, ffn_b_w_gate, ffn_b_w_up, ffn_b_w_down, final_norm, rel_bias, mla_w_in, mla_q_norm, mla_kv_norm, mla_w_uq, mla_w_ukv, mla_w_o, nsa_w_in, nsa_cmp_pos_k, nsa_cmp_w1_k, nsa_cmp_w2_k, nsa_cmp_pos_v, nsa_cmp_w1_v, nsa_cmp_w2_v, nsa_w_o):
    B, S, D = x.shape
    assert D == D_MODEL and S % ATT_T == 0 and S % NSA_T == 0 and (B * S) % FFN_TM == 0
    T = B * S
    G = NSA_GROUPS
    n_chunk = S // CMP_STRIDE
    cos128, sin128 = _rope_tables(S)
    dtab, cbias = _bias_tables(rel_bias, S)
    ovl, emat = _selection_tables(S)
    bf = lambda w: w.astype(BF16)

    h = x.reshape(T, D)
    for i in range(DEPTH):
        h = _ffn(h, ffn_norm_a[i], bf(ffn_a_w_gate[i]), bf(ffn_a_w_up[i]), bf(ffn_a_w_down[i]))
        j = i // N_MIXERS
        if i % N_MIXERS == 0:
            w = _mla_weights(mla_w_in[j], mla_q_norm[j], mla_kv_norm[j], mla_w_uq[j], mla_w_ukv[j], mla_w_o[j])
            q, k, v = _mla_proj(h, mix_norm[i], w, cos128, sin128, S)
            o = _mla_attn(q, k, v, B, S)
        else:
            w = _nsa_weights(nsa_w_in[j], nsa_cmp_pos_k[j], nsa_cmp_w1_k[j], nsa_cmp_w2_k[j],
                             nsa_cmp_pos_v[j], nsa_cmp_w1_v[j], nsa_cmp_w2_v[j], nsa_w_o[j])
            q, kvc_in, kvs, kvw, gates = _nsa_proj(h, mix_norm[i], w["w_in"])
            xc = kvc_in.reshape(B, n_chunk, CMP_STRIDE, G, LANE).transpose(0, 3, 1, 2, 4)
            xc = xc.reshape(B, G, n_chunk, CMP_STRIDE * LANE)
            kvc = _compress(xc, w["pos"], w["w1"], w["w2"])
            o = _nsa_attn(q, kvc, kvs, kvw, gates, cbias, dtab, emat, ovl, B, S)
        h = _ffn(h, ffn_norm_b[i], bf(ffn_b_w_gate[i]), bf(ffn_b_w_up[i]), bf(ffn_b_w_down[i]),
                 proj=(o, w["w_o"]), final_g=final_norm if i == DEPTH - 1 else None)
    return h.reshape(B, S, D)
```

```python
import functools
import math

import numpy as np
import jax
import jax.numpy as jnp
from jax import lax
from jax.experimental import pallas as pl
from jax.experimental.pallas import tpu as pltpu

F32 = jnp.float32
BF16 = jnp.bfloat16

D_MODEL = 1024
DEPTH = 4
N_MIXERS = 2
RMS_EPS = 1e-6
FFN_HIDDEN = 2816
NEG = -1e30
MLA_HEADS = 16
MLA_Q_LORA = 384
MLA_KV_LORA = 256
MLA_NOPE = 64
MLA_ROPE = 32
MLA_V = 64
ROPE_THETA = 10000.0
NSA_HEADS = 16
NSA_GROUPS = 4
NSA_R = NSA_HEADS // NSA_GROUPS
NSA_QK = 64
NSA_V = 64
CMP_BLOCK = 32
CMP_STRIDE = 16
CMP_HIDDEN = 128
SEL_BLOCK = 64
SEL_SHIFT = SEL_BLOCK.bit_length() - 1
assert 1 << SEL_SHIFT == SEL_BLOCK
SEL_TOP_N = 16
WINDOW = 512
REL_BUCKETS = 32
REL_MAX_DIST = 128

LANE = 128
VMEM_LIMIT = 56 * 1024 * 1024

FFN_TM = 512
FFN_TF = 256
PROJ_TM = 256
ATT_T = 256
NSA_T = 128
CB_CENTER = 64


def _cparams(sem):
    return pltpu.CompilerParams(dimension_semantics=sem, vmem_limit_bytes=VMEM_LIMIT)


def _rms(x, g):
    ms = jnp.mean(x * x, axis=-1, keepdims=True)
    return x * lax.rsqrt(ms + RMS_EPS) * g


def _const_spec(shape):
    nd = len(shape)
    return pl.BlockSpec(shape, lambda *_: (0,) * nd)


def _ffn_body(*refs, has_proj, has_final):
    it = iter(refs)
    h_ref = next(it)
    if has_proj:
        o_in_ref = next(it)
        wo_ref = next(it)
    g_ref = next(it)
    wg_ref = next(it)
    wu_ref = next(it)
    wd_ref = next(it)
    if has_final:
        gf_ref = next(it)
    out_ref = next(it)
    a_ref = next(it)

    x = h_ref[...]
    if has_proj:
        x = x + jnp.dot(o_in_ref[...], wo_ref[...], preferred_element_type=F32)
    xn = _rms(x, g_ref[...]).astype(BF16)
    for c in range(FFN_HIDDEN // FFN_TF):
        sl = slice(c * FFN_TF, (c + 1) * FFN_TF)
        gt = jnp.dot(xn, wg_ref[:, sl], preferred_element_type=F32)
        up = jnp.dot(xn, wu_ref[:, sl], preferred_element_type=F32)
        a_ref[:, sl] = (gt * jax.nn.sigmoid(gt) * up).astype(BF16)
    y = x + 0.5 * jnp.dot(a_ref[...], wd_ref[...], preferred_element_type=F32)
    if has_final:
        y = _rms(y, gf_ref[...])
    out_ref[...] = y


def _ffn(h, g, wg, wu, wd, proj=None, final_g=None):
    T = h.shape[0]
    tm = FFN_TM
    row = lambda i: (i, 0)
    in_specs = [pl.BlockSpec((tm, D_MODEL), row)]
    args = [h]
    if proj is not None:
        o_in, wo = proj
        in_specs += [pl.BlockSpec((tm, o_in.shape[1]), row), _const_spec(wo.shape)]
        args += [o_in, wo]
    in_specs += [_const_spec((1, D_MODEL)), _const_spec(wg.shape), _const_spec(wu.shape), _const_spec(wd.shape)]
    args += [g.reshape(1, D_MODEL), wg, wu, wd]
    if final_g is not None:
        in_specs.append(_const_spec((1, D_MODEL)))
        args.append(final_g.reshape(1, D_MODEL))
    return pl.pallas_call(
        functools.partial(_ffn_body, has_proj=proj is not None, has_final=final_g is not None),
        grid=(T // tm,),
        in_specs=in_specs,
        out_specs=pl.BlockSpec((tm, D_MODEL), row),
        out_shape=jax.ShapeDtypeStruct((T, D_MODEL), F32),
        scratch_shapes=[pltpu.VMEM((tm, FFN_HIDDEN), BF16)],
        compiler_params=_cparams(("parallel",)),
        name="ffn",
    )(*args)


MLA_CQ0, MLA_CKV0, MLA_KR0, MLA_KRS0, MLA_IN_W = 0, 384, 640, 768, 896
MLA_HCHUNK = 4


def _mla_proj_body(h_ref, g_ref, win_ref, qn_ref, kvn_ref, wuq_ref, wuqs_ref, wuk_ref, wuv_ref,
                   cos_ref, sin_ref, q_out, k_out, v_out):
    xn = _rms(h_ref[...], g_ref[...]).astype(BF16)
    proj = jnp.dot(xn, win_ref[...], preferred_element_type=F32)
    cq = _rms(proj[:, MLA_CQ0:MLA_CKV0], qn_ref[...]).astype(BF16)
    ckv = _rms(proj[:, MLA_CKV0:MLA_KR0], kvn_ref[...]).astype(BF16)
    cos = cos_ref[...]
    sin = sin_ref[...]
    kr = proj[:, MLA_KR0:MLA_KRS0] * cos + proj[:, MLA_KRS0:MLA_IN_W] * sin
    scale = (MLA_NOPE + MLA_ROPE) ** -0.5 * LOG2E
    v_out[...] = jnp.dot(ckv, wuv_ref[...], preferred_element_type=F32).astype(BF16)
    cw = MLA_HCHUNK * LANE
    for c in range(MLA_HEADS // MLA_HCHUNK):
        sl = slice(c * cw, (c + 1) * cw)
        q = jnp.dot(cq, wuq_ref[:, sl], preferred_element_type=F32)
        qs = jnp.dot(cq, wuqs_ref[:, sl], preferred_element_type=F32)
        kn = jnp.dot(ckv, wuk_ref[:, sl], preferred_element_type=F32)
        for hh in range(MLA_HCHUNK):
            hs = slice(hh * LANE, (hh + 1) * LANE)
            os_ = slice(c * cw + hh * LANE, c * cw + (hh + 1) * LANE)
            q_out[:, os_] = ((q[:, hs] * cos + qs[:, hs] * sin) * scale).astype(BF16)
            k_out[:, os_] = (kn[:, hs] + kr).astype(BF16)


def _mla_proj(h, g, w, cos128, sin128, S):
    T = h.shape[0]
    tm = PROJ_TM
    ns = S // tm
    row = lambda i: (i, 0)
    pos = lambda i: (i % ns, 0)
    HL = MLA_HEADS * LANE
    return pl.pallas_call(
        _mla_proj_body,
        grid=(T // tm,),
        in_specs=[pl.BlockSpec((tm, D_MODEL), row), _const_spec((1, D_MODEL)),
                  _const_spec(w["w_in"].shape), _const_spec((1, MLA_Q_LORA)), _const_spec((1, MLA_KV_LORA)),
                  _const_spec(w["w_uq"].shape), _const_spec(w["w_uqs"].shape),
                  _const_spec(w["w_uk"].shape), _const_spec(w["w_uv"].shape),
                  pl.BlockSpec((tm, LANE), pos), pl.BlockSpec((tm, LANE), pos)],
        out_specs=[pl.BlockSpec((tm, HL), row), pl.BlockSpec((tm, HL), row),
                   pl.BlockSpec((tm, MLA_HEADS * MLA_V), row)],
        out_shape=[jax.ShapeDtypeStruct((T, HL), BF16), jax.ShapeDtypeStruct((T, HL), BF16),
                   jax.ShapeDtypeStruct((T, MLA_HEADS * MLA_V), BF16)],
        compiler_params=_cparams(("parallel",)),
        name="mla_proj",
    )(h, g.reshape(1, D_MODEL), w["w_in"], w["q_norm"], w["kv_norm"], w["w_uq"], w["w_uqs"],
      w["w_uk"], w["w_uv"], cos128, sin128)


_NT = (((1,), (1,)), ((), ()))
SCORE_CHUNK = 2 * LANE


def _chunks(n, lead=0):
    out = [(0, lead)] if lead else []
    return out + [(o, min(SCORE_CHUNK, n - o)) for o in range(lead, n, SCORE_CHUNK)]


def _softmax_pv(score_fn, chunks, values, s_ref, p_ref):
    n = chunks[-1][0] + chunks[-1][1]
    mp = None
    for off, w in chunks:
        s = score_fn(off, w)
        s_ref[:, off:off + w] = s
        for c in range(0, w, LANE):
            part = s[:, c:c + LANE]
            mp = part if mp is None else jnp.maximum(mp, part)
    m = jnp.max(mp, axis=-1, keepdims=True)
    lp = None
    for off, w in chunks:
        p = jnp.exp2(s_ref[:, off:off + w] - m)
        p_ref[:, off:off + w] = p.astype(BF16)
        for c in range(0, w, LANE):
            part = p[:, c:c + LANE]
            lp = part if lp is None else lp + part
    l = jnp.sum(lp, axis=-1, keepdims=True)
    return jnp.dot(p_ref[:, 0:n], values, preferred_element_type=F32) / l


def _mla_attn_body(q_ref, k_ref, v_ref, o_ref, s_ref, p_ref, *, nq):
    t = ATT_T
    qi = pl.program_id(2)

    def branch(nt):
        n = nt * t
        causal = lax.broadcasted_iota(jnp.int32, (t, t), 0) >= lax.broadcasted_iota(jnp.int32, (t, t), 1)
        outs = []
        for hh in range(2):
            hs = slice(hh * LANE, (hh + 1) * LANE)
            q = q_ref[:, hs]

            def score(off, w, q=q, hs=hs):
                s = lax.dot_general(q, k_ref[off:off + w, hs], _NT, preferred_element_type=F32)
                return jnp.where(causal, s, NEG) if off == n - t else s

            outs.append(_softmax_pv(score, _chunks(n), v_ref[0:n, :], s_ref.at[hh], p_ref.at[hh]))
        lane = lax.broadcasted_iota(jnp.int32, (t, LANE), 1)
        o_ref[...] = jnp.where(lane < MLA_V, outs[0], outs[1]).astype(BF16)

    for nt in range(1, nq + 1):
        pl.when(qi == nt - 1)(functools.partial(branch, nt))


def _mla_attn(q, k, v, B, S):
    t = ATT_T
    nq = S // t
    T = B * S
    return pl.pallas_call(
        functools.partial(_mla_attn_body, nq=nq),
        grid=(B, MLA_HEADS // 2, nq),
        in_specs=[pl.BlockSpec((t, 2 * LANE), lambda b, p, i: (b * nq + i, p)),
                  pl.BlockSpec((S, 2 * LANE), lambda b, p, i: (b, p)),
                  pl.BlockSpec((S, LANE), lambda b, p, i: (b, p))],
        out_specs=pl.BlockSpec((t, LANE), lambda b, p, i: (b * nq + i, p)),
        out_shape=jax.ShapeDtypeStruct((T, MLA_HEADS * MLA_V), BF16),
        scratch_shapes=[pltpu.VMEM((2, t, S), F32), pltpu.VMEM((2, t, S), BF16)],
        compiler_params=_cparams(("parallel", "parallel", "arbitrary")),
        name="mla_attn",
    )(q, k, v)


NSA_QW = NSA_HEADS * LANE
NSA_KVW = NSA_GROUPS * LANE
NSA_IN_W = NSA_QW + 3 * NSA_KVW + LANE
GATES_PER_GROUP = 3 * NSA_R
LOG2E = math.log2(math.e)


def _nsa_proj_body(h_ref, g_ref, win_ref, q_out, kvc_out, kvs_out, kvw_out, gate_out):
    xn = _rms(h_ref[...], g_ref[...]).astype(BF16)
    scale = NSA_QK ** -0.5 * LOG2E
    cw = 4 * LANE
    for c in range(NSA_QW // cw):
        sl = slice(c * cw, (c + 1) * cw)
        q_out[:, sl] = (jnp.dot(xn, win_ref[:, sl], preferred_element_type=F32) * scale).astype(BF16)
    for n, out in enumerate((kvc_out, kvs_out, kvw_out)):
        sl = slice(NSA_QW + n * NSA_KVW, NSA_QW + (n + 1) * NSA_KVW)
        out[...] = jnp.dot(xn, win_ref[:, sl], preferred_element_type=F32).astype(BF16)
    gl = jnp.dot(xn, win_ref[:, NSA_QW + 3 * NSA_KVW:], preferred_element_type=F32)
    gates = jax.nn.sigmoid(gl)
    for g in range(NSA_GROUPS):
        gate_out[g] = gates if g == 0 else pltpu.roll(gates, LANE - g * GATES_PER_GROUP, 1)


def _nsa_proj(h, g, w_in):
    T = h.shape[0]
    tm = PROJ_TM
    row = lambda i: (i, 0)
    return pl.pallas_call(
        _nsa_proj_body,
        grid=(T // tm,),
        in_specs=[pl.BlockSpec((tm, D_MODEL), row), _const_spec((1, D_MODEL)), _const_spec(w_in.shape)],
        out_specs=[pl.BlockSpec((tm, NSA_QW), row), pl.BlockSpec((tm, NSA_KVW), row),
                   pl.BlockSpec((tm, NSA_KVW), row), pl.BlockSpec((tm, NSA_KVW), row),
                   pl.BlockSpec((NSA_GROUPS, tm, LANE), lambda i: (0, i, 0))],
        out_shape=[jax.ShapeDtypeStruct((T, NSA_QW), BF16), jax.ShapeDtypeStruct((T, NSA_KVW), BF16),
                   jax.ShapeDtypeStruct((T, NSA_KVW), BF16), jax.ShapeDtypeStruct((T, NSA_KVW), BF16),
                   jax.ShapeDtypeStruct((NSA_GROUPS, T, LANE), F32)],
        compiler_params=_cparams(("parallel",)),
        name="nsa_proj",
    )(h, g.reshape(1, D_MODEL), w_in)


def _compress_body(x_ref, pos_ref, w1_ref, w2_ref, out_ref):
    n_chunk = x_ref.shape[0]
    x = x_ref[...].astype(F32)
    xa = (x + pos_ref[0]).astype(BF16)
    xb = (x + pos_ref[1]).astype(BF16)
    a = jnp.dot(xa, w1_ref[0], preferred_element_type=F32)
    b = jnp.dot(xb, w1_ref[1], preferred_element_type=F32)
    pre = a + pltpu.roll(b, n_chunk - 1, 0)
    hid = jax.nn.gelu(pre, approximate=True).astype(BF16)
    out_ref[...] = jnp.dot(hid, w2_ref[...], preferred_element_type=F32).astype(BF16)


def _compress(x, pos, w1, w2):
    B, G, n_chunk, width = x.shape
    return pl.pallas_call(
        _compress_body,
        grid=(B, G),
        in_specs=[pl.BlockSpec((None, None, n_chunk, width), lambda b, g: (b, g, 0, 0)),
                  _const_spec(pos.shape), _const_spec(w1.shape), _const_spec(w2.shape)],
        out_specs=pl.BlockSpec((None, None, n_chunk, LANE), lambda b, g: (b, g, 0, 0)),
        out_shape=jax.ShapeDtypeStruct((B, G, n_chunk, LANE), BF16),
        compiler_params=_cparams(("parallel", "parallel")),
        name="nsa_compress",
    )(x, pos, w1, w2)


def _nsa_attn_body(*refs, S):
    step = pl.program_id(2)
    for qi in range(S // NSA_T):
        pl.when(step == qi)(functools.partial(_nsa_tile, qi, *refs, S=S))


def _nsa_tile(qi, q_ref, kvc_ref, kvs_ref, kvw_ref, gate_ref, cb_ref, d_ref, e_ref, ovt_ref,
              o_ref, s_ref, p_ref, sw_ref, pw_ref, *, S):
    t = NSA_T
    R = NSA_R
    n_sel = S // SEL_BLOCK
    t0 = qi * t
    qs = jnp.concatenate([q_ref[:, r * LANE:(r + 1) * LANE] for r in range(R)], axis=0)

    kvc = kvc_ref[...]
    n_cp = kvc.shape[0]
    s = lax.dot_general(qs, kvc, _NT, preferred_element_type=F32).reshape(R, t, n_cp)
    tpos = t0 + lax.broadcasted_iota(jnp.int32, (t, n_cp), 0)
    cend = lax.broadcasted_iota(jnp.int32, (t, n_cp), 1) * CMP_STRIDE + (CMP_BLOCK - 1)
    valid = tpos >= cend
    shift = (qi * (t // CMP_STRIDE) + CB_CENTER) % n_cp
    cb = pltpu.roll(cb_ref[...], shift, 2) if shift else cb_ref[...]
    s = jnp.where(valid[None], s + cb, NEG)
    e = jnp.exp2(s - jnp.max(s, axis=-1, keepdims=True))
    p = jnp.where(valid[None], e / jnp.sum(e, axis=-1, keepdims=True), 0.0)
    o_cmp = jnp.dot(p.reshape(R * t, n_cp).astype(BF16), kvc, preferred_element_type=F32)

    psum = p[0]
    for r in range(1, R):
        psum = psum + p[r]
    imp = lax.dot_general(ovt_ref[...], psum, _NT, preferred_element_type=F32,
                          precision=lax.Precision.HIGHEST)
    jj = lax.broadcasted_iota(jnp.int32, (n_sel, t), 0)
    blk_t = (t0 + lax.broadcasted_iota(jnp.int32, (n_sel, t), 1)) >> SEL_SHIFT
    forced = (jj == 0) | (jj == blk_t) | (jj == blk_t - 1)
    score = jnp.where(forced, 1e6, jnp.where(jj <= blk_t, imp, -1e6))
    cnt = jnp.zeros((n_sel, t), jnp.int32)
    for jp in range(n_sel):
        row = score[jp:jp + 1, :]
        beats = (row > score) | ((row == score) & (jj > jp))
        cnt = cnt + beats.astype(jnp.int32)
    sel_t = jnp.where(cnt < min(SEL_TOP_N, n_sel), 1.0, 0.0)
    sel_t = jnp.concatenate([sel_t, jnp.zeros((e_ref.shape[0] - n_sel, t), F32)], axis=0)
    sel = sel_t.T.astype(BF16)

    def scores(kv, add, w, tail):
        sc = lax.dot_general(qs, kv, _NT, preferred_element_type=F32).reshape(R, t, w)
        if add is not None:
            sc = sc + add[None]
        if tail:
            sc = sc + d_ref[:, :, 2 * t - w:]
        return sc.reshape(R * t, w)

    kw = min(qi, WINDOW // t)
    nw = (kw + 1) * t
    w0 = (qi - kw) * t

    def win_score(off, w):
        lo_d, hi_d = kw * t - off - (w - 1), kw * t - off + (t - 1)
        add = None
        if lo_d < 0 or hi_d >= WINDOW:
            dist = (kw * t - off + lax.broadcasted_iota(jnp.int32, (t, w), 0)
                    - lax.broadcasted_iota(jnp.int32, (t, w), 1))
            add = jnp.where((dist >= 0) & (dist < WINDOW), 0.0, NEG)
        return scores(kvw_ref[w0 + off:w0 + off + w, :], add, w, off + w == nw)

    o_win = _softmax_pv(win_score, _chunks(nw, lead=nw % SCORE_CHUNK), kvw_ref[w0:w0 + nw, :], sw_ref, pw_ref)

    n = (qi + 1) * t

    def sel_score(off, w):
        selx = jnp.dot(sel, e_ref[:, off:off + w], preferred_element_type=F32)
        ok = selx > 0.5
        if off + w > n - t:
            kpos = off + lax.broadcasted_iota(jnp.int32, (t, w), 1)
            ok = ok & (kpos <= t0 + lax.broadcasted_iota(jnp.int32, (t, w), 0))
        return scores(kvs_ref[off:off + w, :], jnp.where(ok, 0.0, NEG), w, off + w == n)

    o_sel = _softmax_pv(sel_score, _chunks(n, lead=n % SCORE_CHUNK), kvs_ref[0:n, :], s_ref, p_ref)

    gates = gate_ref[...]
    lane = lax.broadcasted_iota(jnp.int32, (t, LANE), 1)
    heads = []
    for r in range(R):
        rs = slice(r * t, (r + 1) * t)
        o = None
        for br, ob in enumerate((o_cmp, o_sel, o_win)):
            c = br * R + r
            term = gates[:, c:c + 1] * ob[rs]
            o = term if o is None else o + term
        heads.append(o)
    pairs = [jnp.where(lane < NSA_V, pltpu.roll(heads[2 * i], NSA_V, 1), heads[2 * i + 1]) for i in range(R // 2)]
    o_ref[...] = jnp.concatenate(pairs, axis=-1).astype(BF16)


def _nsa_attn(q, kvc, kvs, kvw, gates, cbias, dtab, emat, ovl, B, S):
    t = NSA_T
    nq = S // t
    T = B * S
    G, R = NSA_GROUPS, NSA_R
    n_cp = kvc.shape[2]
    return pl.pallas_call(
        functools.partial(_nsa_attn_body, S=S),
        grid=(G, B, nq),
        in_specs=[pl.BlockSpec((t, R * LANE), lambda g, b, i: (b * nq + i, g)),
                  pl.BlockSpec((None, None, n_cp, LANE), lambda g, b, i: (b, g, 0, 0)),
                  pl.BlockSpec((S, LANE), lambda g, b, i: (b, g)),
                  pl.BlockSpec((S, LANE), lambda g, b, i: (b, g)),
                  pl.BlockSpec((None, t, LANE), lambda g, b, i: (g, b * nq + i, 0)),
                  pl.BlockSpec((None, R, t, n_cp), lambda g, b, i: (g, 0, 0, 0)),
                  pl.BlockSpec((None, R, t, 2 * t), lambda g, b, i: (g, 0, 0, 0)),
                  _const_spec(emat.shape), _const_spec(ovl.shape)],
        out_specs=pl.BlockSpec((t, R * NSA_V), lambda g, b, i: (b * nq + i, g)),
        out_shape=jax.ShapeDtypeStruct((T, NSA_HEADS * NSA_V), BF16),
        scratch_shapes=[pltpu.VMEM((R * t, S), F32), pltpu.VMEM((R * t, S), BF16),
                        pltpu.VMEM((R * t, WINDOW + t), F32), pltpu.VMEM((R * t, WINDOW + t), BF16)],
        compiler_params=_cparams(("parallel", "parallel", "arbitrary")),
        name="nsa_attn",
    )(q, kvc, kvs, kvw, gates, cbias, dtab, emat, ovl)


def _t5_bucket(dist):
    n = jnp.maximum(dist, 0)
    max_exact = REL_BUCKETS // 2
    nf = jnp.maximum(n, 1).astype(F32)
    large = max_exact + (jnp.log(nf / max_exact) / math.log(REL_MAX_DIST / max_exact)
                         * (REL_BUCKETS - max_exact)).astype(jnp.int32)
    large = jnp.minimum(large, REL_BUCKETS - 1)
    return jnp.where(n < max_exact, n, large)


def _np_bucket(n):
    n = np.maximum(np.asarray(n), 0)
    max_exact = REL_BUCKETS // 2
    large = max_exact + (np.log(np.maximum(n, 1) / max_exact) / math.log(REL_MAX_DIST / max_exact)
                         * (REL_BUCKETS - max_exact)).astype(np.int64)
    return np.where(n < max_exact, n, np.minimum(large, REL_BUCKETS - 1))


def _bias_tables(rel_bias, S):
    t = NSA_T
    G, R = NSA_GROUPS, NSA_R
    n_cp = S // CMP_STRIDE
    a = np.arange(t)[:, None]
    dist_d = a - np.arange(2 * t)[None, :] + t
    dist_c = a - CMP_STRIDE * (np.arange(n_cp)[None, :] - CB_CENTER) - (CMP_BLOCK - 1)
    uncovered = min(t + 1, CMP_STRIDE * (CB_CENTER + 1) - (CMP_BLOCK - 1))
    assert (_np_bucket(np.arange(uncovered, 2 * S)) == REL_BUCKETS - 1).all()
    wrap_from = n_cp - max((S // t - 1) * (t // CMP_STRIDE) - CB_CENTER, 0)
    assert (dist_c[:, wrap_from:] < 0).all() and (dist_c[:, -1] < 0).all()

    def lookup(dist):
        oh = jax.nn.one_hot(_t5_bucket(jnp.asarray(dist)), REL_BUCKETS, dtype=F32)
        oh = oh - jax.nn.one_hot(REL_BUCKETS - 1, REL_BUCKETS, dtype=F32)
        val = jnp.einsum("acb,bh->hac", oh, rel_bias, precision=lax.Precision.HIGHEST)
        return val.reshape(G, R, *dist.shape)

    dtab = lookup(dist_d)
    cbias = jnp.where(jnp.asarray(dist_c >= 0), lookup(dist_c), 0.0)
    return dtab * LOG2E, cbias * LOG2E


def _selection_tables(S):
    n_cp = S // CMP_STRIDE
    n_cmp = (S - CMP_BLOCK) // CMP_STRIDE + 1
    n_sel = S // SEL_BLOCK
    cs = np.arange(n_cp) * CMP_STRIDE
    ce = cs + CMP_BLOCK
    ss = np.arange(n_sel) * SEL_BLOCK
    se = ss + SEL_BLOCK
    ov = np.minimum(ce[:, None], se[None, :]) - np.maximum(cs[:, None], ss[None, :])
    ov = (np.clip(ov, 0, None) / CMP_BLOCK).astype(np.float32)
    ov[n_cmp:] = 0.0
    emat = (np.arange(S)[None, :] // SEL_BLOCK == np.arange(max(n_sel, LANE))[:, None]).astype(np.float32)
    return jnp.asarray(ov.T), jnp.asarray(emat, dtype=BF16)


def _rope_tables(S):
    half = MLA_ROPE // 2
    inv = ROPE_THETA ** (-jnp.arange(half, dtype=F32) * 2.0 / MLA_ROPE)
    ang = jnp.arange(S, dtype=F32)[:, None] * inv[None, :]
    cos, sin = jnp.cos(ang), jnp.sin(ang)
    ones = jnp.ones((S, MLA_NOPE), F32)
    pad1 = jnp.ones((S, LANE - MLA_NOPE - MLA_ROPE), F32)
    cos128 = jnp.concatenate([ones, cos, cos, pad1], axis=1)
    sin128 = jnp.concatenate([0 * ones, sin, sin, 0 * pad1], axis=1)
    return cos128, sin128


def _mla_weights(w_in, q_norm, kv_norm, w_uq, w_ukv, w_o):
    H = MLA_HEADS
    half = MLA_ROPE // 2
    pad = LANE - MLA_NOPE - MLA_ROPE
    kr = w_in[:, MLA_Q_LORA + MLA_KV_LORA:]
    kr_sw = jnp.concatenate([-kr[:, half:], kr[:, :half]], axis=1)
    z = lambda n: jnp.zeros((D_MODEL, n), F32)
    w_in_ext = jnp.concatenate([w_in[:, :MLA_Q_LORA + MLA_KV_LORA],
                                z(MLA_NOPE), kr, z(pad), z(MLA_NOPE), kr_sw, z(pad)], axis=1)
    uq = w_uq.reshape(MLA_Q_LORA, H, MLA_NOPE + MLA_ROPE)
    qn, qr = uq[..., :MLA_NOPE], uq[..., MLA_NOPE:]
    qr_sw = jnp.concatenate([-qr[..., half:], qr[..., :half]], axis=-1)
    zq = jnp.zeros((MLA_Q_LORA, H, pad), F32)
    w_uq_p = jnp.concatenate([qn, qr, zq], axis=-1).reshape(MLA_Q_LORA, H * LANE)
    w_uq_s = jnp.concatenate([0 * qn, qr_sw, zq], axis=-1).reshape(MLA_Q_LORA, H * LANE)
    ukv = w_ukv.reshape(MLA_KV_LORA, H, MLA_NOPE + MLA_V)
    zk = jnp.zeros((MLA_KV_LORA, H, LANE - MLA_NOPE), F32)
    w_uk = jnp.concatenate([ukv[..., :MLA_NOPE], zk], axis=-1).reshape(MLA_KV_LORA, H * LANE)
    w_uv = ukv[..., MLA_NOPE:].reshape(MLA_KV_LORA, H * MLA_V)
    return dict(w_in=w_in_ext.astype(BF16), q_norm=q_norm.reshape(1, -1), kv_norm=kv_norm.reshape(1, -1),
                w_uq=w_uq_p.astype(BF16), w_uqs=w_uq_s.astype(BF16), w_uk=w_uk.astype(BF16),
                w_uv=w_uv.astype(BF16), w_o=w_o.astype(BF16))


def _nsa_weights(w_in, pos_k, w1_k, w2_k, pos_v, w1_v, w2_v, w_o):
    H, G = NSA_HEADS, NSA_GROUPS
    gw = G * NSA_QK
    q0 = H * NSA_QK
    wq = w_in[:, :q0].reshape(D_MODEL, H, NSA_QK)
    wq = jnp.concatenate([wq, jnp.zeros_like(wq)], axis=-1).reshape(D_MODEL, H * LANE)
    cols = [wq]
    for n in range(3):
        k = w_in[:, q0 + (2 * n) * gw: q0 + (2 * n + 1) * gw].reshape(D_MODEL, G, NSA_QK)
        v = w_in[:, q0 + (2 * n + 1) * gw: q0 + (2 * n + 2) * gw].reshape(D_MODEL, G, NSA_V)
        cols.append(jnp.concatenate([k, v], axis=-1).reshape(D_MODEL, G * LANE))
    wg = w_in[:, q0 + 6 * gw:].reshape(D_MODEL, G, NSA_R, 3).transpose(0, 1, 3, 2).reshape(D_MODEL, 3 * H)
    cols.append(jnp.concatenate([wg, jnp.zeros((D_MODEL, LANE - 3 * H), F32)], axis=1))
    w_in_ext = jnp.concatenate(cols, axis=1).astype(BF16)
    eye = jnp.eye(2, dtype=F32)
    cw = CMP_STRIDE * 2 * NSA_QK
    pos = jnp.stack([pos_k, pos_v]).reshape(2, 2, CMP_STRIDE, NSA_QK)
    pos = pos.transpose(1, 2, 0, 3).reshape(2, 1, cw)
    w1 = jnp.stack([w1_k, w1_v]).reshape(2, 2, CMP_STRIDE, NSA_QK, CMP_HIDDEN)
    w1 = jnp.einsum("khldj,kq->hlkdqj", w1, eye).reshape(2, cw, 2 * CMP_HIDDEN).astype(BF16)
    w2 = jnp.einsum("kjd,kq->kjqd", jnp.stack([w2_k, w2_v]), eye).reshape(2 * CMP_HIDDEN, 2 * NSA_QK).astype(BF16)
    return dict(w_in=w_in_ext, pos=pos, w1=w1, w2=w2, w_o=w_o.astype(BF16))


def kernel(x, ffn_norm_a, ffn_a_w_gate, ffn_a_w_up, ffn_a_w_down, mix_norm, ffn_norm_b, ffn_b_w_gate, ffn_b_w_up, ffn_b_w_down, final_norm, rel_bias, mla_w_in, mla_q_norm, mla_kv_norm, mla_w_uq, mla_w_ukv, mla_w_o, nsa_w_in, nsa_cmp_pos_k, nsa_cmp_w1_k, nsa_cmp_w2_k, nsa_cmp_pos_v, nsa_cmp_w1_v, nsa_cmp_w2_v, nsa_w_o):
    B, S, D = x.shape
    assert D == D_MODEL and S % ATT_T == 0 and S % NSA_T == 0 and (B * S) % FFN_TM == 0
    T = B * S
    G = NSA_GROUPS
    n_chunk = S // CMP_STRIDE
    cos128, sin128 = _rope_tables(S)
    dtab, cbias = _bias_tables(rel_bias, S)
    ovl, emat = _selection_tables(S)
    bf = lambda w: w.astype(BF16)

    h = x.reshape(T, D)
    for i in range(DEPTH):
        h = _ffn(h, ffn_norm_a[i], bf(ffn_a_w_gate[i]), bf(ffn_a_w_up[i]), bf(ffn_a_w_down[i]))
        j = i // N_MIXERS
        if i % N_MIXERS == 0:
            w = _mla_weights(mla_w_in[j], mla_q_norm[j], mla_kv_norm[j], mla_w_uq[j], mla_w_ukv[j], mla_w_o[j])
            q, k, v = _mla_proj(h, mix_norm[i], w, cos128, sin128, S)
            o = _mla_attn(q, k, v, B, S)
        else:
            w = _nsa_weights(nsa_w_in[j], nsa_cmp_pos_k[j], nsa_cmp_w1_k[j], nsa_cmp_w2_k[j],
                             nsa_cmp_pos_v[j], nsa_cmp_w1_v[j], nsa_cmp_w2_v[j], nsa_w_o[j])
            q, kvc_in, kvs, kvw, gates = _nsa_proj(h, mix_norm[i], w["w_in"])
            xc = kvc_in.reshape(B, n_chunk, CMP_STRIDE, G, LANE).transpose(0, 3, 1, 2, 4)
            xc = xc.reshape(B, G, n_chunk, CMP_STRIDE * LANE)
            kvc = _compress(xc, w["pos"], w["w1"], w["w2"])
            o = _nsa_attn(q, kvc, kvs, kvw, gates, cbias, dtab, emat, ovl, B, S)
        h = _ffn(h, ffn_norm_b[i], bf(ffn_b_w_gate[i]), bf(ffn_b_w_up[i]), bf(ffn_b_w_down[i]),
                 proj=(o, w["w_o"]), final_g=final_norm if i == DEPTH - 1 else None)
    return h.reshape(B, S, D)
```

```python
import functools
import math

import numpy as np
import jax
import jax.numpy as jnp
from jax import lax
from jax.experimental import pallas as pl
from jax.experimental.pallas import tpu as pltpu

F32 = jnp.float32
BF16 = jnp.bfloat16

D_MODEL = 1024
DEPTH = 4
N_MIXERS = 2
RMS_EPS = 1e-6
FFN_HIDDEN = 2816
NEG = -1e30
MLA_HEADS = 16
MLA_Q_LORA = 384
MLA_KV_LORA = 256
MLA_NOPE = 64
MLA_ROPE = 32
MLA_V = 64
ROPE_THETA = 10000.0
NSA_HEADS = 16
NSA_GROUPS = 4
NSA_R = NSA_HEADS // NSA_GROUPS
NSA_QK = 64
NSA_V = 64
CMP_BLOCK = 32
CMP_STRIDE = 16
CMP_HIDDEN = 128
SEL_BLOCK = 64
SEL_SHIFT = SEL_BLOCK.bit_length() - 1
assert 1 << SEL_SHIFT == SEL_BLOCK
SEL_TOP_N = 16
WINDOW = 512
REL_BUCKETS = 32
REL_MAX_DIST = 128

LANE = 128
VMEM_LIMIT = 56 * 1024 * 1024

FFN_TM = 512
FFN_TF = 256
PROJ_TM = 256
ATT_T = 256
NSA_T = 128
CB_CENTER = 64


def _cparams(sem):
    return pltpu.CompilerParams(dimension_semantics=sem, vmem_limit_bytes=VMEM_LIMIT)


def _rms(x, g):
    ms = jnp.mean(x * x, axis=-1, keepdims=True)
    return x * lax.rsqrt(ms + RMS_EPS) * g


def _const_spec(shape):
    nd = len(shape)
    return pl.BlockSpec(shape, lambda *_: (0,) * nd)


def _ffn_body(*refs, has_proj, has_final):
    it = iter(refs)
    h_ref = next(it)
    if has_proj:
        o_in_ref = next(it)
        wo_ref = next(it)
    g_ref = next(it)
    wg_ref = next(it)
    wu_ref = next(it)
    wd_ref = next(it)
    if has_final:
        gf_ref = next(it)
    out_ref = next(it)
    a_ref = next(it)

    x = h_ref[...]
    if has_proj:
        x = x + jnp.dot(o_in_ref[...], wo_ref[...], preferred_element_type=F32)
    xn = _rms(x, g_ref[...]).astype(BF16)
    for c in range(FFN_HIDDEN // FFN_TF):
        sl = slice(c * FFN_TF, (c + 1) * FFN_TF)
        gt = jnp.dot(xn, wg_ref[:, sl], preferred_element_type=F32)
        up = jnp.dot(xn, wu_ref[:, sl], preferred_element_type=F32)
        a_ref[:, sl] = (gt * jax.nn.sigmoid(gt) * up).astype(BF16)
    y = x + 0.5 * jnp.dot(a_ref[...], wd_ref[...], preferred_element_type=F32)
    if has_final:
        y = _rms(y, gf_ref[...])
    out_ref[...] = y


def _ffn(h, g, wg, wu, wd, proj=None, final_g=None):
    T = h.shape[0]
    tm = FFN_TM
    row = lambda i: (i, 0)
    in_specs = [pl.BlockSpec((tm, D_MODEL), row)]
    args = [h]
    if proj is not None:
        o_in, wo = proj
        in_specs += [pl.BlockSpec((tm, o_in.shape[1]), row), _const_spec(wo.shape)]
        args += [o_in, wo]
    in_specs += [_const_spec((1, D_MODEL)), _const_spec(wg.shape), _const_spec(wu.shape), _const_spec(wd.shape)]
    args += [g.reshape(1, D_MODEL), wg, wu, wd]
    if final_g is not None:
        in_specs.append(_const_spec((1, D_MODEL)))
        args.append(final_g.reshape(1, D_MODEL))
    return pl.pallas_call(
        functools.partial(_ffn_body, has_proj=proj is not None, has_final=final_g is not None),
        grid=(T // tm,),
        in_specs=in_specs,
        out_specs=pl.BlockSpec((tm, D_MODEL), row),
        out_shape=jax.ShapeDtypeStruct((T, D_MODEL), F32),
        scratch_shapes=[pltpu.VMEM((tm, FFN_HIDDEN), BF16)],
        compiler_params=_cparams(("parallel",)),
        name="ffn",
    )(*args)


MLA_CQ0, MLA_CKV0, MLA_KR0, MLA_KRS0, MLA_IN_W = 0, 384, 640, 768, 896
MLA_HCHUNK = 4


def _mla_proj_body(h_ref, g_ref, win_ref, qn_ref, kvn_ref, wuqt_ref, wuqst_ref, wuk_ref, wuvt_ref,
                   cos_ref, sin_ref, cost_ref, sint_ref, qt_out, k_out, vt_out):
    xn = _rms(h_ref[...], g_ref[...]).astype(BF16)
    proj = jnp.dot(xn, win_ref[...], preferred_element_type=F32)
    cq = _rms(proj[:, MLA_CQ0:MLA_CKV0], qn_ref[...]).astype(BF16)
    ckv = _rms(proj[:, MLA_CKV0:MLA_KR0], kvn_ref[...]).astype(BF16)
    cos = cos_ref[...]
    sin = sin_ref[...]
    kr = proj[:, MLA_KR0:MLA_KRS0] * cos + proj[:, MLA_KRS0:MLA_IN_W] * sin
    cos_t = cost_ref[...]
    sin_t = sint_ref[...]
    scale = (MLA_NOPE + MLA_ROPE) ** -0.5 * LOG2E
    vt_out[...] = lax.dot_general(wuvt_ref[...], ckv, _NT, preferred_element_type=F32).astype(BF16)
    cw = MLA_HCHUNK * LANE
    for c in range(MLA_HEADS // MLA_HCHUNK):
        sl = slice(c * cw, (c + 1) * cw)
        qt = lax.dot_general(wuqt_ref[sl, :], cq, _NT, preferred_element_type=F32)
        qst = lax.dot_general(wuqst_ref[sl, :], cq, _NT, preferred_element_type=F32)
        kn = jnp.dot(ckv, wuk_ref[:, sl], preferred_element_type=F32)
        for hh in range(MLA_HCHUNK):
            hs = slice(hh * LANE, (hh + 1) * LANE)
            os_ = slice(c * cw + hh * LANE, c * cw + (hh + 1) * LANE)
            qt_out[os_, :] = ((qt[hs] * cos_t + qst[hs] * sin_t) * scale).astype(BF16)
            k_out[:, os_] = (kn[:, hs] + kr).astype(BF16)


def _mla_proj(h, g, w, cos128, sin128, S):
    T = h.shape[0]
    tm = PROJ_TM
    ns = S // tm
    row = lambda i: (i, 0)
    col = lambda i: (0, i)
    pos = lambda i: (i % ns, 0)
    pos_t = lambda i: (0, i % ns)
    HL = MLA_HEADS * LANE
    HV = MLA_HEADS * MLA_V
    return pl.pallas_call(
        _mla_proj_body,
        grid=(T // tm,),
        in_specs=[pl.BlockSpec((tm, D_MODEL), row), _const_spec((1, D_MODEL)),
                  _const_spec(w["w_in"].shape), _const_spec((1, MLA_Q_LORA)), _const_spec((1, MLA_KV_LORA)),
                  _const_spec(w["w_uqt"].shape), _const_spec(w["w_uqst"].shape),
                  _const_spec(w["w_uk"].shape), _const_spec(w["w_uvt"].shape),
                  pl.BlockSpec((tm, LANE), pos), pl.BlockSpec((tm, LANE), pos),
                  pl.BlockSpec((LANE, tm), pos_t), pl.BlockSpec((LANE, tm), pos_t)],
        out_specs=[pl.BlockSpec((HL, tm), col), pl.BlockSpec((tm, HL), row), pl.BlockSpec((HV, tm), col)],
        out_shape=[jax.ShapeDtypeStruct((HL, T), BF16), jax.ShapeDtypeStruct((T, HL), BF16),
                   jax.ShapeDtypeStruct((HV, T), BF16)],
        compiler_params=_cparams(("parallel",)),
        name="mla_proj",
    )(h, g.reshape(1, D_MODEL), w["w_in"], w["q_norm"], w["kv_norm"], w["w_uqt"], w["w_uqst"],
      w["w_uk"], w["w_uvt"], cos128, sin128, cos128.T, sin128.T)


_NT = (((1,), (1,)), ((), ()))
SCORE_CHUNK = 2 * LANE


def _chunks(n, lead=0):
    out = [(0, lead)] if lead else []
    return out + [(o, min(SCORE_CHUNK, n - o)) for o in range(lead, n, SCORE_CHUNK)]


def _softmax_pv(score_fn, chunks, values, s_ref, p_ref):
    n = chunks[-1][0] + chunks[-1][1]
    mp = None
    for off, w in chunks:
        s = score_fn(off, w)
        s_ref[:, off:off + w] = s
        for c in range(0, w, LANE):
            part = s[:, c:c + LANE]
            mp = part if mp is None else jnp.maximum(mp, part)
    m = jnp.max(mp, axis=-1, keepdims=True)
    lp = None
    for off, w in chunks:
        p = jnp.exp2(s_ref[:, off:off + w] - m)
        p_ref[:, off:off + w] = p.astype(BF16)
        for c in range(0, w, LANE):
            part = p[:, c:c + LANE]
            lp = part if lp is None else lp + part
    l = jnp.sum(lp, axis=-1, keepdims=True)
    return jnp.dot(p_ref[:, 0:n], values, preferred_element_type=F32) / l


SUBLANE = 8


def _fold(x, op):
    w, cols = x.shape
    return op(x.reshape(w // SUBLANE, SUBLANE, cols), axis=0)


class _SoftmaxStreamT:
    def __init__(self, score_fn, chunks, values_t, s_ref, p_ref):
        self.score_fn, self.chunks, self.values_t, self.s_ref, self.p_ref = score_fn, chunks, values_t, s_ref, p_ref

    def pass1(self):
        mp = None
        for off, w in self.chunks:
            s = self.score_fn(off, w)
            self.s_ref[off:off + w, :] = s
            part = _fold(s, jnp.max)
            mp = part if mp is None else jnp.maximum(mp, part)
            yield
        self.m = jnp.max(mp, axis=0, keepdims=True)

    def pass2(self):
        lp = None
        for off, w in self.chunks:
            p = jnp.exp2(self.s_ref[off:off + w, :] - self.m)
            self.p_ref[off:off + w, :] = p.astype(BF16)
            part = _fold(p, jnp.sum)
            lp = part if lp is None else lp + part
            yield
        n = self.chunks[-1][0] + self.chunks[-1][1]
        l = jnp.sum(lp, axis=0, keepdims=True)
        self.out = jnp.dot(self.values_t, self.p_ref[0:n, :], preferred_element_type=F32) / l


def _trace_pipelined(streams):
    for _ in streams[0].pass1():
        pass
    for i, st in enumerate(streams):
        gens = [st.pass2()] + ([streams[i + 1].pass1()] if i + 1 < len(streams) else [])
        while gens:
            gens = [g for g in gens if next(g, StopIteration) is not StopIteration]


MLA_HPS = 4


def _mla_attn_body(qt_ref, k_ref, vt_ref, o_ref, *scratch, nq):
    s_refs, p_refs = scratch[:MLA_HPS], scratch[MLA_HPS:]
    t = ATT_T
    qi = pl.program_id(2)

    def branch(nt):
        n = nt * t
        causal = lax.broadcasted_iota(jnp.int32, (t, t), 0) <= lax.broadcasted_iota(jnp.int32, (t, t), 1)
        streams = []
        for hh in range(MLA_HPS):
            hs = slice(hh * LANE, (hh + 1) * LANE)
            qt = qt_ref[hs, :]
            vs = slice(hh // 2 * LANE, (hh // 2 + 1) * LANE)

            def score(off, w, qt=qt, hs=hs):
                s = jnp.dot(k_ref[off:off + w, hs], qt, preferred_element_type=F32)
                return jnp.where(causal, s, NEG) if off == n - t else s

            streams.append(_SoftmaxStreamT(score, _chunks(n), vt_ref[vs, 0:n], s_refs[hh], p_refs[hh]))
        _trace_pipelined(streams)
        outs = [st.out[hh % 2 * MLA_V:(hh % 2 + 1) * MLA_V] for hh, st in enumerate(streams)]
        o_ref[...] = jnp.concatenate(outs, axis=0).T.astype(BF16)

    for nt in range(1, nq + 1):
        pl.when(qi == nt - 1)(functools.partial(branch, nt))


def _mla_attn(qt, k, vt, B, S):
    t = ATT_T
    nq = S // t
    T = B * S
    hps = MLA_HPS
    return pl.pallas_call(
        functools.partial(_mla_attn_body, nq=nq),
        grid=(B, MLA_HEADS // hps, nq),
        in_specs=[pl.BlockSpec((hps * LANE, t), lambda b, p, i: (p, b * nq + i)),
                  pl.BlockSpec((S, hps * LANE), lambda b, p, i: (b, p)),
                  pl.BlockSpec((hps * MLA_V, S), lambda b, p, i: (p, b))],
        out_specs=pl.BlockSpec((t, hps * MLA_V), lambda b, p, i: (b * nq + i, p)),
        out_shape=jax.ShapeDtypeStruct((T, MLA_HEADS * MLA_V), BF16),
        scratch_shapes=[pltpu.VMEM((S, t), F32)] * hps + [pltpu.VMEM((S, t), BF16)] * hps,
        compiler_params=_cparams(("parallel", "parallel", "arbitrary")),
        name="mla_attn",
    )(qt, k, vt)


NSA_QW = NSA_HEADS * LANE
NSA_KVW = NSA_GROUPS * LANE
NSA_IN_W = NSA_QW + 3 * NSA_KVW + LANE
GATES_PER_GROUP = 3 * NSA_R
LOG2E = math.log2(math.e)


def _nsa_proj_body(h_ref, g_ref, win_ref, q_out, kvc_out, kvs_out, kvw_out, gate_out):
    xn = _rms(h_ref[...], g_ref[...]).astype(BF16)
    scale = NSA_QK ** -0.5 * LOG2E
    cw = 4 * LANE
    for c in range(NSA_QW // cw):
        sl = slice(c * cw, (c + 1) * cw)
        q_out[:, sl] = (jnp.dot(xn, win_ref[:, sl], preferred_element_type=F32) * scale).astype(BF16)
    for n, out in enumerate((kvc_out, kvs_out, kvw_out)):
        sl = slice(NSA_QW + n * NSA_KVW, NSA_QW + (n + 1) * NSA_KVW)
        out[...] = jnp.dot(xn, win_ref[:, sl], preferred_element_type=F32).astype(BF16)
    gl = jnp.dot(xn, win_ref[:, NSA_QW + 3 * NSA_KVW:], preferred_element_type=F32)
    gates = jax.nn.sigmoid(gl)
    for g in range(NSA_GROUPS):
        gate_out[g] = gates if g == 0 else pltpu.roll(gates, LANE - g * GATES_PER_GROUP, 1)


def _nsa_proj(h, g, w_in):
    T = h.shape[0]
    tm = PROJ_TM
    row = lambda i: (i, 0)
    return pl.pallas_call(
        _nsa_proj_body,
        grid=(T // tm,),
        in_specs=[pl.BlockSpec((tm, D_MODEL), row), _const_spec((1, D_MODEL)), _const_spec(w_in.shape)],
        out_specs=[pl.BlockSpec((tm, NSA_QW), row), pl.BlockSpec((tm, NSA_KVW), row),
                   pl.BlockSpec((tm, NSA_KVW), row), pl.BlockSpec((tm, NSA_KVW), row),
                   pl.BlockSpec((NSA_GROUPS, tm, LANE), lambda i: (0, i, 0))],
        out_shape=[jax.ShapeDtypeStruct((T, NSA_QW), BF16), jax.ShapeDtypeStruct((T, NSA_KVW), BF16),
                   jax.ShapeDtypeStruct((T, NSA_KVW), BF16), jax.ShapeDtypeStruct((T, NSA_KVW), BF16),
                   jax.ShapeDtypeStruct((NSA_GROUPS, T, LANE), F32)],
        compiler_params=_cparams(("parallel",)),
        name="nsa_proj",
    )(h, g.reshape(1, D_MODEL), w_in)


def _compress_body(x_ref, pos_ref, w1_ref, w2_ref, out_ref):
    n_chunk = x_ref.shape[0]
    x = x_ref[...].astype(F32)
    xa = (x + pos_ref[0]).astype(BF16)
    xb = (x + pos_ref[1]).astype(BF16)
    a = jnp.dot(xa, w1_ref[0], preferred_element_type=F32)
    b = jnp.dot(xb, w1_ref[1], preferred_element_type=F32)
    pre = a + pltpu.roll(b, n_chunk - 1, 0)
    hid = jax.nn.gelu(pre, approximate=True).astype(BF16)
    out_ref[...] = jnp.dot(hid, w2_ref[...], preferred_element_type=F32).astype(BF16)


def _compress(x, pos, w1, w2):
    B, G, n_chunk, width = x.shape
    return pl.pallas_call(
        _compress_body,
        grid=(B, G),
        in_specs=[pl.BlockSpec((None, None, n_chunk, width), lambda b, g: (b, g, 0, 0)),
                  _const_spec(pos.shape), _const_spec(w1.shape), _const_spec(w2.shape)],
        out_specs=pl.BlockSpec((None, None, n_chunk, LANE), lambda b, g: (b, g, 0, 0)),
        out_shape=jax.ShapeDtypeStruct((B, G, n_chunk, LANE), BF16),
        compiler_params=_cparams(("parallel", "parallel")),
        name="nsa_compress",
    )(x, pos, w1, w2)


def _nsa_attn_body(*refs, S):
    step = pl.program_id(2)
    for qi in range(S // NSA_T):
        pl.when(step == qi)(functools.partial(_nsa_tile, qi, *refs, S=S))


def _nsa_tile(qi, q_ref, kvc_ref, kvs_ref, kvw_ref, gate_ref, cb_ref, d_ref, e_ref, ovt_ref,
              o_ref, s_ref, p_ref, sw_ref, pw_ref, *, S):
    t = NSA_T
    R = NSA_R
    n_sel = S // SEL_BLOCK
    t0 = qi * t
    qs = jnp.concatenate([q_ref[:, r * LANE:(r + 1) * LANE] for r in range(R)], axis=0)

    kvc = kvc_ref[...]
    n_cp = kvc.shape[0]
    s = lax.dot_general(qs, kvc, _NT, preferred_element_type=F32).reshape(R, t, n_cp)
    tpos = t0 + lax.broadcasted_iota(jnp.int32, (t, n_cp), 0)
    cend = lax.broadcasted_iota(jnp.int32, (t, n_cp), 1) * CMP_STRIDE + (CMP_BLOCK - 1)
    valid = tpos >= cend
    shift = (qi * (t // CMP_STRIDE) + CB_CENTER) % n_cp
    cb = pltpu.roll(cb_ref[...], shift, 2) if shift else cb_ref[...]
    s = jnp.where(valid[None], s + cb, NEG)
    e = jnp.exp2(s - jnp.max(s, axis=-1, keepdims=True))
    p = jnp.where(valid[None], e / jnp.sum(e, axis=-1, keepdims=True), 0.0)
    o_cmp = jnp.dot(p.reshape(R * t, n_cp).astype(BF16), kvc, preferred_element_type=F32)

    psum = p[0]
    for r in range(1, R):
        psum = psum + p[r]
    imp = lax.dot_general(ovt_ref[...], psum, _NT, preferred_element_type=F32,
                          precision=lax.Precision.HIGHEST)
    jj = lax.broadcasted_iota(jnp.int32, (n_sel, t), 0)
    blk_t = (t0 + lax.broadcasted_iota(jnp.int32, (n_sel, t), 1)) >> SEL_SHIFT
    forced = (jj == 0) | (jj == blk_t) | (jj == blk_t - 1)
    score = jnp.where(forced, 1e6, jnp.where(jj <= blk_t, imp, -1e6))
    cnt = jnp.zeros((n_sel, t), jnp.int32)
    for jp in range(n_sel):
        row = score[jp:jp + 1, :]
        beats = (row > score) | ((row == score) & (jj > jp))
        cnt = cnt + beats.astype(jnp.int32)
    sel_t = jnp.where(cnt < min(SEL_TOP_N, n_sel), 1.0, 0.0)
    sel_t = jnp.concatenate([sel_t, jnp.zeros((e_ref.shape[0] - n_sel, t), F32)], axis=0)
    sel = sel_t.T.astype(BF16)

    def scores(kv, add, w, tail):
        sc = lax.dot_general(qs, kv, _NT, preferred_element_type=F32).reshape(R, t, w)
        if add is not None:
            sc = sc + add[None]
        if tail:
            sc = sc + d_ref[:, :, 2 * t - w:]
        return sc.reshape(R * t, w)

    kw = min(qi, WINDOW // t)
    nw = (kw + 1) * t
    w0 = (qi - kw) * t

    def win_score(off, w):
        lo_d, hi_d = kw * t - off - (w - 1), kw * t - off + (t - 1)
        add = None
        if lo_d < 0 or hi_d >= WINDOW:
            dist = (kw * t - off + lax.broadcasted_iota(jnp.int32, (t, w), 0)
                    - lax.broadcasted_iota(jnp.int32, (t, w), 1))
            add = jnp.where((dist >= 0) & (dist < WINDOW), 0.0, NEG)
        return scores(kvw_ref[w0 + off:w0 + off + w, :], add, w, off + w == nw)

    o_win = _softmax_pv(win_score, _chunks(nw, lead=nw % SCORE_CHUNK), kvw_ref[w0:w0 + nw, :], sw_ref, pw_ref)

    n = (qi + 1) * t

    def sel_score(off, w):
        selx = jnp.dot(sel, e_ref[:, off:off + w], preferred_element_type=F32)
        ok = selx > 0.5
        if off + w > n - t:
            kpos = off + lax.broadcasted_iota(jnp.int32, (t, w), 1)
            ok = ok & (kpos <= t0 + lax.broadcasted_iota(jnp.int32, (t, w), 0))
        return scores(kvs_ref[off:off + w, :], jnp.where(ok, 0.0, NEG), w, off + w == n)

    o_sel = _softmax_pv(sel_score, _chunks(n, lead=n % SCORE_CHUNK), kvs_ref[0:n, :], s_ref, p_ref)

    gates = gate_ref[...]
    lane = lax.broadcasted_iota(jnp.int32, (t, LANE), 1)
    heads = []
    for r in range(R):
        rs = slice(r * t, (r + 1) * t)
        o = None
        for br, ob in enumerate((o_cmp, o_sel, o_win)):
            c = br * R + r
            term = gates[:, c:c + 1] * ob[rs]
            o = term if o is None else o + term
        heads.append(o)
    pairs = [jnp.where(lane < NSA_V, pltpu.roll(heads[2 * i], NSA_V, 1), heads[2 * i + 1]) for i in range(R // 2)]
    o_ref[...] = jnp.concatenate(pairs, axis=-1).astype(BF16)


def _nsa_attn(q, kvc, kvs, kvw, gates, cbias, dtab, emat, ovl, B, S):
    t = NSA_T
    nq = S // t
    T = B * S
    G, R = NSA_GROUPS, NSA_R
    n_cp = kvc.shape[2]
    return pl.pallas_call(
        functools.partial(_nsa_attn_body, S=S),
        grid=(G, B, nq),
        in_specs=[pl.BlockSpec((t, R * LANE), lambda g, b, i: (b * nq + i, g)),
                  pl.BlockSpec((None, None, n_cp, LANE), lambda g, b, i: (b, g, 0, 0)),
                  pl.BlockSpec((S, LANE), lambda g, b, i: (b, g)),
                  pl.BlockSpec((S, LANE), lambda g, b, i: (b, g)),
                  pl.BlockSpec((None, t, LANE), lambda g, b, i: (g, b * nq + i, 0)),
                  pl.BlockSpec((None, R, t, n_cp), lambda g, b, i: (g, 0, 0, 0)),
                  pl.BlockSpec((None, R, t, 2 * t), lambda g, b, i: (g, 0, 0, 0)),
                  _const_spec(emat.shape), _const_spec(ovl.shape)],
        out_specs=pl.BlockSpec((t, R * NSA_V), lambda g, b, i: (b * nq + i, g)),
        out_shape=jax.ShapeDtypeStruct((T, NSA_HEADS * NSA_V), BF16),
        scratch_shapes=[pltpu.VMEM((R * t, S), F32), pltpu.VMEM((R * t, S), BF16),
                        pltpu.VMEM((R * t, WINDOW + t), F32), pltpu.VMEM((R * t, WINDOW + t), BF16)],
        compiler_params=_cparams(("parallel", "parallel", "arbitrary")),
        name="nsa_attn",
    )(q, kvc, kvs, kvw, gates, cbias, dtab, emat, ovl)


def _t5_bucket(dist):
    n = jnp.maximum(dist, 0)
    max_exact = REL_BUCKETS // 2
    nf = jnp.maximum(n, 1).astype(F32)
    large = max_exact + (jnp.log(nf / max_exact) / math.log(REL_MAX_DIST / max_exact)
                         * (REL_BUCKETS - max_exact)).astype(jnp.int32)
    large = jnp.minimum(large, REL_BUCKETS - 1)
    return jnp.where(n < max_exact, n, large)


def _np_bucket(n):
    n = np.maximum(np.asarray(n), 0)
    max_exact = REL_BUCKETS // 2
    large = max_exact + (np.log(np.maximum(n, 1) / max_exact) / math.log(REL_MAX_DIST / max_exact)
                         * (REL_BUCKETS - max_exact)).astype(np.int64)
    return np.where(n < max_exact, n, np.minimum(large, REL_BUCKETS - 1))


def _bias_tables(rel_bias, S):
    t = NSA_T
    G, R = NSA_GROUPS, NSA_R
    n_cp = S // CMP_STRIDE
    a = np.arange(t)[:, None]
    dist_d = a - np.arange(2 * t)[None, :] + t
    dist_c = a - CMP_STRIDE * (np.arange(n_cp)[None, :] - CB_CENTER) - (CMP_BLOCK - 1)
    uncovered = min(t + 1, CMP_STRIDE * (CB_CENTER + 1) - (CMP_BLOCK - 1))
    assert (_np_bucket(np.arange(uncovered, 2 * S)) == REL_BUCKETS - 1).all()
    wrap_from = n_cp - max((S // t - 1) * (t // CMP_STRIDE) - CB_CENTER, 0)
    assert (dist_c[:, wrap_from:] < 0).all() and (dist_c[:, -1] < 0).all()

    def lookup(dist):
        oh = jax.nn.one_hot(_t5_bucket(jnp.asarray(dist)), REL_BUCKETS, dtype=F32)
        oh = oh - jax.nn.one_hot(REL_BUCKETS - 1, REL_BUCKETS, dtype=F32)
        val = jnp.einsum("acb,bh->hac", oh, rel_bias, precision=lax.Precision.HIGHEST)
        return val.reshape(G, R, *dist.shape)

    dtab = lookup(dist_d)
    cbias = jnp.where(jnp.asarray(dist_c >= 0), lookup(dist_c), 0.0)
    return dtab * LOG2E, cbias * LOG2E


def _selection_tables(S):
    n_cp = S // CMP_STRIDE
    n_cmp = (S - CMP_BLOCK) // CMP_STRIDE + 1
    n_sel = S // SEL_BLOCK
    cs = np.arange(n_cp) * CMP_STRIDE
    ce = cs + CMP_BLOCK
    ss = np.arange(n_sel) * SEL_BLOCK
    se = ss + SEL_BLOCK
    ov = np.minimum(ce[:, None], se[None, :]) - np.maximum(cs[:, None], ss[None, :])
    ov = (np.clip(ov, 0, None) / CMP_BLOCK).astype(np.float32)
    ov[n_cmp:] = 0.0
    emat = (np.arange(S)[None, :] // SEL_BLOCK == np.arange(max(n_sel, LANE))[:, None]).astype(np.float32)
    return jnp.asarray(ov.T), jnp.asarray(emat, dtype=BF16)


def _rope_tables(S):
    half = MLA_ROPE // 2
    inv = ROPE_THETA ** (-jnp.arange(half, dtype=F32) * 2.0 / MLA_ROPE)
    ang = jnp.arange(S, dtype=F32)[:, None] * inv[None, :]
    cos, sin = jnp.cos(ang), jnp.sin(ang)
    ones = jnp.ones((S, MLA_NOPE), F32)
    pad1 = jnp.ones((S, LANE - MLA_NOPE - MLA_ROPE), F32)
    cos128 = jnp.concatenate([ones, cos, cos, pad1], axis=1)
    sin128 = jnp.concatenate([0 * ones, sin, sin, 0 * pad1], axis=1)
    return cos128, sin128


def _mla_weights(w_in, q_norm, kv_norm, w_uq, w_ukv, w_o):
    H = MLA_HEADS
    half = MLA_ROPE // 2
    pad = LANE - MLA_NOPE - MLA_ROPE
    kr = w_in[:, MLA_Q_LORA + MLA_KV_LORA:]
    kr_sw = jnp.concatenate([-kr[:, half:], kr[:, :half]], axis=1)
    z = lambda n: jnp.zeros((D_MODEL, n), F32)
    w_in_ext = jnp.concatenate([w_in[:, :MLA_Q_LORA + MLA_KV_LORA],
                                z(MLA_NOPE), kr, z(pad), z(MLA_NOPE), kr_sw, z(pad)], axis=1)
    uq = w_uq.reshape(MLA_Q_LORA, H, MLA_NOPE + MLA_ROPE)
    qn, qr = uq[..., :MLA_NOPE], uq[..., MLA_NOPE:]
    qr_sw = jnp.concatenate([-qr[..., half:], qr[..., :half]], axis=-1)
    zq = jnp.zeros((MLA_Q_LORA, H, pad), F32)
    w_uq_p = jnp.concatenate([qn, qr, zq], axis=-1).reshape(MLA_Q_LORA, H * LANE)
    w_uq_s = jnp.concatenate([0 * qn, qr_sw, zq], axis=-1).reshape(MLA_Q_LORA, H * LANE)
    ukv = w_ukv.reshape(MLA_KV_LORA, H, MLA_NOPE + MLA_V)
    zk = jnp.zeros((MLA_KV_LORA, H, LANE - MLA_NOPE), F32)
    w_uk = jnp.concatenate([ukv[..., :MLA_NOPE], zk], axis=-1).reshape(MLA_KV_LORA, H * LANE)
    w_uv = ukv[..., MLA_NOPE:].reshape(MLA_KV_LORA, H * MLA_V)
    return dict(w_in=w_in_ext.astype(BF16), q_norm=q_norm.reshape(1, -1), kv_norm=kv_norm.reshape(1, -1),
                w_uqt=w_uq_p.T.astype(BF16), w_uqst=w_uq_s.T.astype(BF16), w_uk=w_uk.astype(BF16),
                w_uvt=w_uv.T.astype(BF16), w_o=w_o.astype(BF16))


def _nsa_weights(w_in, pos_k, w1_k, w2_k, pos_v, w1_v, w2_v, w_o):
    H, G = NSA_HEADS, NSA_GROUPS
    gw = G * NSA_QK
    q0 = H * NSA_QK
    wq = w_in[:, :q0].reshape(D_MODEL, H, NSA_QK)
    wq = jnp.concatenate([wq, jnp.zeros_like(wq)], axis=-1).reshape(D_MODEL, H * LANE)
    cols = [wq]
    for n in range(3):
        k = w_in[:, q0 + (2 * n) * gw: q0 + (2 * n + 1) * gw].reshape(D_MODEL, G, NSA_QK)
        v = w_in[:, q0 + (2 * n + 1) * gw: q0 + (2 * n + 2) * gw].reshape(D_MODEL, G, NSA_V)
        cols.append(jnp.concatenate([k, v], axis=-1).reshape(D_MODEL, G * LANE))
    wg = w_in[:, q0 + 6 * gw:].reshape(D_MODEL, G, NSA_R, 3).transpose(0, 1, 3, 2).reshape(D_MODEL, 3 * H)
    cols.append(jnp.concatenate([wg, jnp.zeros((D_MODEL, LANE - 3 * H), F32)], axis=1))
    w_in_ext = jnp.concatenate(cols, axis=1).astype(BF16)
    eye = jnp.eye(2, dtype=F32)
    cw = CMP_STRIDE * 2 * NSA_QK
    pos = jnp.stack([pos_k, pos_v]).reshape(2, 2, CMP_STRIDE, NSA_QK)
    pos = pos.transpose(1, 2, 0, 3).reshape(2, 1, cw)
    w1 = jnp.stack([w1_k, w1_v]).reshape(2, 2, CMP_STRIDE, NSA_QK, CMP_HIDDEN)
    w1 = jnp.einsum("khldj,kq->hlkdqj", w1, eye).reshape(2, cw, 2 * CMP_HIDDEN).astype(BF16)
    w2 = jnp.einsum("kjd,kq->kjqd", jnp.stack([w2_k, w2_v]), eye).reshape(2 * CMP_HIDDEN, 2 * NSA_QK).astype(BF16)
    return dict(w_in=w_in_ext, pos=pos, w1=w1, w2=w2, w_o=w_o.astype(BF16))


def kernel(x, ffn_norm_a, ffn_a_w_gate, ffn_a_w_up, ffn_a_w_down, mix_norm, ffn_norm_b, ffn_b_w_gate, ffn_b_w_up, ffn_b_w_down, final_norm, rel_bias, mla_w_in, mla_q_norm, mla_kv_norm, mla_w_uq, mla_w_ukv, mla_w_o, nsa_w_in, nsa_cmp_pos_k, nsa_cmp_w1_k, nsa_cmp_w2_k, nsa_cmp_pos_v, nsa_cmp_w1_v, nsa_cmp_w2_v, nsa_w_o):
    B, S, D = x.shape
    assert D == D_MODEL and S % ATT_T == 0 and S % NSA_T == 0 and (B * S) % FFN_TM == 0
    T = B * S
    G = NSA_GROUPS
    n_chunk = S // CMP_STRIDE
    cos128, sin128 = _rope_tables(S)
    dtab, cbias = _bias_tables(rel_bias, S)
    ovl, emat = _selection_tables(S)
    bf = lambda w: w.astype(BF16)

    h = x.reshape(T, D)
    for i in range(DEPTH):
        h = _ffn(h, ffn_norm_a[i], bf(ffn_a_w_gate[i]), bf(ffn_a_w_up[i]), bf(ffn_a_w_down[i]))
        j = i // N_MIXERS
        if i % N_MIXERS == 0:
            w = _mla_weights(mla_w_in[j], mla_q_norm[j], mla_kv_norm[j], mla_w_uq[j], mla_w_ukv[j], mla_w_o[j])
            q, k, v = _mla_proj(h, mix_norm[i], w, cos128, sin128, S)
            o = _mla_attn(q, k, v, B, S)
        else:
            w = _nsa_weights(nsa_w_in[j], nsa_cmp_pos_k[j], nsa_cmp_w1_k[j], nsa_cmp_w2_k[j],
                             nsa_cmp_pos_v[j], nsa_cmp_w1_v[j], nsa_cmp_w2_v[j], nsa_w_o[j])
            q, kvc_in, kvs, kvw, gates = _nsa_proj(h, mix_norm[i], w["w_in"])
            xc = kvc_in.reshape(B, n_chunk, CMP_STRIDE, G, LANE).transpose(0, 3, 1, 2, 4)
            xc = xc.reshape(B, G, n_chunk, CMP_STRIDE * LANE)
            kvc = _compress(xc, w["pos"], w["w1"], w["w2"])
            o = _nsa_attn(q, kvc, kvs, kvw, gates, cbias, dtab, emat, ovl, B, S)
        h = _ffn(h, ffn_norm_b[i], bf(ffn_b_w_gate[i]), bf(ffn_b_w_up[i]), bf(ffn_b_w_down[i]),
                 proj=(o, w["w_o"]), final_g=final_norm if i == DEPTH - 1 else None)
    return h.reshape(B, S, D)
```

```python
import functools
import math

import numpy as np
import jax
import jax.numpy as jnp
from jax import lax
from jax.experimental import pallas as pl
from jax.experimental.pallas import tpu as pltpu

F32 = jnp.float32
BF16 = jnp.bfloat16

D_MODEL = 1024
DEPTH = 4
N_MIXERS = 2
RMS_EPS = 1e-6
FFN_HIDDEN = 2816
NEG = -1e30
MLA_HEADS = 16
MLA_Q_LORA = 384
MLA_KV_LORA = 256
MLA_NOPE = 64
MLA_ROPE = 32
MLA_V = 64
ROPE_THETA = 10000.0
NSA_HEADS = 16
NSA_GROUPS = 4
NSA_R = NSA_HEADS // NSA_GROUPS
NSA_QK = 64
NSA_V = 64
CMP_BLOCK = 32
CMP_STRIDE = 16
CMP_HIDDEN = 128
SEL_BLOCK = 64
SEL_SHIFT = SEL_BLOCK.bit_length() - 1
assert 1 << SEL_SHIFT == SEL_BLOCK
SEL_TOP_N = 16
WINDOW = 512
REL_BUCKETS = 32
REL_MAX_DIST = 128

LANE = 128
VMEM_LIMIT = 56 * 1024 * 1024

FFN_TM = 512
FFN_TF = 256
PROJ_TM = 256
ATT_T = 256
NSA_T = 128
CB_CENTER = 64


def _cparams(sem):
    return pltpu.CompilerParams(dimension_semantics=sem, vmem_limit_bytes=VMEM_LIMIT)


def _rms(x, g):
    ms = jnp.mean(x * x, axis=-1, keepdims=True)
    return x * lax.rsqrt(ms + RMS_EPS) * g


def _const_spec(shape):
    nd = len(shape)
    return pl.BlockSpec(shape, lambda *_: (0,) * nd)


def _ffn_body(*refs, has_proj, has_final):
    it = iter(refs)
    h_ref = next(it)
    if has_proj:
        o_in_ref = next(it)
        wo_ref = next(it)
    g_ref = next(it)
    wg_ref = next(it)
    wu_ref = next(it)
    wd_ref = next(it)
    if has_final:
        gf_ref = next(it)
    out_ref = next(it)
    a_ref = next(it)

    x = h_ref[...]
    if has_proj:
        x = x + jnp.dot(o_in_ref[...], wo_ref[...], preferred_element_type=F32)
    xn = _rms(x, g_ref[...]).astype(BF16)
    for c in range(FFN_HIDDEN // FFN_TF):
        sl = slice(c * FFN_TF, (c + 1) * FFN_TF)
        gt = jnp.dot(xn, wg_ref[:, sl], preferred_element_type=F32)
        up = jnp.dot(xn, wu_ref[:, sl], preferred_element_type=F32)
        a_ref[:, sl] = (gt * jax.nn.sigmoid(gt) * up).astype(BF16)
    y = x + 0.5 * jnp.dot(a_ref[...], wd_ref[...], preferred_element_type=F32)
    if has_final:
        y = _rms(y, gf_ref[...])
    out_ref[...] = y


def _ffn(h, g, wg, wu, wd, proj=None, final_g=None):
    T = h.shape[0]
    tm = FFN_TM
    row = lambda i: (i, 0)
    in_specs = [pl.BlockSpec((tm, D_MODEL), row)]
    args = [h]
    if proj is not None:
        o_in, wo = proj
        in_specs += [pl.BlockSpec((tm, o_in.shape[1]), row), _const_spec(wo.shape)]
        args += [o_in, wo]
    in_specs += [_const_spec((1, D_MODEL)), _const_spec(wg.shape), _const_spec(wu.shape), _const_spec(wd.shape)]
    args += [g.reshape(1, D_MODEL), wg, wu, wd]
    if final_g is not None:
        in_specs.append(_const_spec((1, D_MODEL)))
        args.append(final_g.reshape(1, D_MODEL))
    return pl.pallas_call(
        functools.partial(_ffn_body, has_proj=proj is not None, has_final=final_g is not None),
        grid=(T // tm,),
        in_specs=in_specs,
        out_specs=pl.BlockSpec((tm, D_MODEL), row),
        out_shape=jax.ShapeDtypeStruct((T, D_MODEL), F32),
        scratch_shapes=[pltpu.VMEM((tm, FFN_HIDDEN), BF16)],
        compiler_params=_cparams(("parallel",)),
        name="ffn",
    )(*args)


MLA_CQ0, MLA_CKV0, MLA_KR0, MLA_KRS0, MLA_IN_W = 0, 384, 640, 768, 896
MLA_HCHUNK = 4


def _mla_proj_body(h_ref, g_ref, win_ref, qn_ref, kvn_ref, wuqt_ref, wuqst_ref, wuk_ref, wuvt_ref,
                   cos_ref, sin_ref, cost_ref, sint_ref, qt_out, k_out, vt_out):
    xn = _rms(h_ref[...], g_ref[...]).astype(BF16)
    proj = jnp.dot(xn, win_ref[...], preferred_element_type=F32)
    cq = _rms(proj[:, MLA_CQ0:MLA_CKV0], qn_ref[...]).astype(BF16)
    ckv = _rms(proj[:, MLA_CKV0:MLA_KR0], kvn_ref[...]).astype(BF16)
    cos = cos_ref[...]
    sin = sin_ref[...]
    kr = proj[:, MLA_KR0:MLA_KRS0] * cos + proj[:, MLA_KRS0:MLA_IN_W] * sin
    cos_t = cost_ref[...]
    sin_t = sint_ref[...]
    scale = (MLA_NOPE + MLA_ROPE) ** -0.5 * LOG2E
    vt_out[...] = lax.dot_general(wuvt_ref[...], ckv, _NT, preferred_element_type=F32).astype(BF16)
    cw = MLA_HCHUNK * LANE
    for c in range(MLA_HEADS // MLA_HCHUNK):
        sl = slice(c * cw, (c + 1) * cw)
        qt = lax.dot_general(wuqt_ref[sl, :], cq, _NT, preferred_element_type=F32)
        qst = lax.dot_general(wuqst_ref[sl, :], cq, _NT, preferred_element_type=F32)
        kn = jnp.dot(ckv, wuk_ref[:, sl], preferred_element_type=F32)
        for hh in range(MLA_HCHUNK):
            hs = slice(hh * LANE, (hh + 1) * LANE)
            os_ = slice(c * cw + hh * LANE, c * cw + (hh + 1) * LANE)
            qt_out[os_, :] = ((qt[hs] * cos_t + qst[hs] * sin_t) * scale).astype(BF16)
            k_out[:, os_] = (kn[:, hs] + kr).astype(BF16)


def _mla_proj(h, g, w, cos128, sin128, S):
    T = h.shape[0]
    tm = PROJ_TM
    ns = S // tm
    row = lambda i: (i, 0)
    col = lambda i: (0, i)
    pos = lambda i: (i % ns, 0)
    pos_t = lambda i: (0, i % ns)
    HL = MLA_HEADS * LANE
    HV = MLA_HEADS * MLA_V
    return pl.pallas_call(
        _mla_proj_body,
        grid=(T // tm,),
        in_specs=[pl.BlockSpec((tm, D_MODEL), row), _const_spec((1, D_MODEL)),
                  _const_spec(w["w_in"].shape), _const_spec((1, MLA_Q_LORA)), _const_spec((1, MLA_KV_LORA)),
                  _const_spec(w["w_uqt"].shape), _const_spec(w["w_uqst"].shape),
                  _const_spec(w["w_uk"].shape), _const_spec(w["w_uvt"].shape),
                  pl.BlockSpec((tm, LANE), pos), pl.BlockSpec((tm, LANE), pos),
                  pl.BlockSpec((LANE, tm), pos_t), pl.BlockSpec((LANE, tm), pos_t)],
        out_specs=[pl.BlockSpec((HL, tm), col), pl.BlockSpec((tm, HL), row), pl.BlockSpec((HV, tm), col)],
        out_shape=[jax.ShapeDtypeStruct((HL, T), BF16), jax.ShapeDtypeStruct((T, HL), BF16),
                   jax.ShapeDtypeStruct((HV, T), BF16)],
        compiler_params=_cparams(("parallel",)),
        name="mla_proj",
    )(h, g.reshape(1, D_MODEL), w["w_in"], w["q_norm"], w["kv_norm"], w["w_uqt"], w["w_uqst"],
      w["w_uk"], w["w_uvt"], cos128, sin128, cos128.T, sin128.T)


_NT = (((1,), (1,)), ((), ()))
SCORE_CHUNK = 2 * LANE


def _chunks(n, lead=0):
    out = [(0, lead)] if lead else []
    return out + [(o, min(SCORE_CHUNK, n - o)) for o in range(lead, n, SCORE_CHUNK)]


def _softmax_pv(score_fn, chunks, values, s_ref, p_ref):
    n = chunks[-1][0] + chunks[-1][1]
    mp = None
    for off, w in chunks:
        s = score_fn(off, w)
        s_ref[:, off:off + w] = s
        for c in range(0, w, LANE):
            part = s[:, c:c + LANE]
            mp = part if mp is None else jnp.maximum(mp, part)
    m = jnp.max(mp, axis=-1, keepdims=True)
    lp = None
    for off, w in chunks:
        p = jnp.exp2(s_ref[:, off:off + w] - m)
        p_ref[:, off:off + w] = p.astype(BF16)
        for c in range(0, w, LANE):
            part = p[:, c:c + LANE]
            lp = part if lp is None else lp + part
    l = jnp.sum(lp, axis=-1, keepdims=True)
    return jnp.dot(p_ref[:, 0:n], values, preferred_element_type=F32) / l


SUBLANE = 8


def _fold(x, op):
    w, cols = x.shape
    return op(x.reshape(w // SUBLANE, SUBLANE, cols), axis=0)


class _SoftmaxStreamT:
    def __init__(self, score_fn, chunks, values_t, s_ref, p_ref):
        self.score_fn, self.chunks, self.values_t, self.s_ref, self.p_ref = score_fn, chunks, values_t, s_ref, p_ref

    def pass1(self):
        mp = None
        for off, w in self.chunks:
            s = self.score_fn(off, w)
            self.s_ref[off:off + w, :] = s
            part = _fold(s, jnp.max)
            mp = part if mp is None else jnp.maximum(mp, part)
            yield
        self.m = jnp.max(mp, axis=0, keepdims=True)

    def pass2(self):
        lp = None
        for off, w in self.chunks:
            p = jnp.exp2(self.s_ref[off:off + w, :] - self.m)
            self.p_ref[off:off + w, :] = p.astype(BF16)
            part = _fold(p, jnp.sum)
            lp = part if lp is None else lp + part
            yield
        n = self.chunks[-1][0] + self.chunks[-1][1]
        l = jnp.sum(lp, axis=0, keepdims=True)
        self.out = jnp.dot(self.values_t, self.p_ref[0:n, :], preferred_element_type=F32) / l


def _trace_pipelined(streams):
    for _ in streams[0].pass1():
        pass
    for i, st in enumerate(streams):
        gens = [st.pass2()] + ([streams[i + 1].pass1()] if i + 1 < len(streams) else [])
        while gens:
            gens = [g for g in gens if next(g, StopIteration) is not StopIteration]


MLA_HPS = 4


def _mla_attn_body(qt_ref, k_ref, vt_ref, o_ref, *scratch, nq):
    s_refs, p_refs = scratch[:MLA_HPS], scratch[MLA_HPS:]
    t = ATT_T
    qi = pl.program_id(2)

    def branch(nt):
        n = nt * t
        causal = lax.broadcasted_iota(jnp.int32, (t, t), 0) <= lax.broadcasted_iota(jnp.int32, (t, t), 1)
        streams = []
        for hh in range(MLA_HPS):
            hs = slice(hh * LANE, (hh + 1) * LANE)
            qt = qt_ref[hs, :]
            vs = slice(hh // 2 * LANE, (hh // 2 + 1) * LANE)

            def score(off, w, qt=qt, hs=hs):
                s = jnp.dot(k_ref[off:off + w, hs], qt, preferred_element_type=F32)
                return jnp.where(causal, s, NEG) if off == n - t else s

            streams.append(_SoftmaxStreamT(score, _chunks(n), vt_ref[vs, 0:n], s_refs[hh], p_refs[hh]))
        _trace_pipelined(streams)
        outs = [st.out[hh % 2 * MLA_V:(hh % 2 + 1) * MLA_V] for hh, st in enumerate(streams)]
        o_ref[...] = jnp.concatenate(outs, axis=0).T.astype(BF16)

    for nt in range(1, nq + 1):
        pl.when(qi == nt - 1)(functools.partial(branch, nt))


def _mla_attn(qt, k, vt, B, S):
    t = ATT_T
    nq = S // t
    T = B * S
    hps = MLA_HPS
    return pl.pallas_call(
        functools.partial(_mla_attn_body, nq=nq),
        grid=(B, MLA_HEADS // hps, nq),
        in_specs=[pl.BlockSpec((hps * LANE, t), lambda b, p, i: (p, b * nq + i)),
                  pl.BlockSpec((S, hps * LANE), lambda b, p, i: (b, p)),
                  pl.BlockSpec((hps * MLA_V, S), lambda b, p, i: (p, b))],
        out_specs=pl.BlockSpec((t, hps * MLA_V), lambda b, p, i: (b * nq + i, p)),
        out_shape=jax.ShapeDtypeStruct((T, MLA_HEADS * MLA_V), BF16),
        scratch_shapes=[pltpu.VMEM((S, t), F32)] * hps + [pltpu.VMEM((S, t), BF16)] * hps,
        compiler_params=_cparams(("parallel", "parallel", "arbitrary")),
        name="mla_attn",
    )(qt, k, vt)


NSA_QW = NSA_HEADS * LANE
NSA_KVW = NSA_GROUPS * LANE
NSA_IN_W = NSA_QW + 3 * NSA_KVW + LANE
GATES_PER_GROUP = 3 * NSA_R
LOG2E = math.log2(math.e)


def _nsa_proj_body(h_ref, g_ref, wqt_ref, wkv_ref, wkvt_ref, wgt_ref,
                   qt_out, kvc_out, kvs_out, kvw_out, kvst_out, kvwt_out, gate_out):
    xn = _rms(h_ref[...], g_ref[...]).astype(BF16)
    scale = NSA_QK ** -0.5 * LOG2E
    cw = 4 * LANE
    for c in range(NSA_QW // cw):
        sl = slice(c * cw, (c + 1) * cw)
        qt = lax.dot_general(wqt_ref[sl, :], xn, _NT, preferred_element_type=F32)
        qt_out[sl, :] = (qt * scale).astype(BF16)
    for n, out in enumerate((kvc_out, kvs_out, kvw_out)):
        sl = slice(n * NSA_KVW, (n + 1) * NSA_KVW)
        out[...] = jnp.dot(xn, wkv_ref[:, sl], preferred_element_type=F32).astype(BF16)
    for n, out in enumerate((kvst_out, kvwt_out)):
        sl = slice(n * NSA_KVW, (n + 1) * NSA_KVW)
        out[...] = lax.dot_general(wkvt_ref[sl, :], xn, _NT, preferred_element_type=F32).astype(BF16)
    gl = lax.dot_general(wgt_ref[...], xn, _NT, preferred_element_type=F32)
    gate_out[...] = jax.nn.sigmoid(gl)


def _nsa_proj(h, g, w):
    T = h.shape[0]
    tm = PROJ_TM
    row = lambda i: (i, 0)
    col = lambda i: (0, i)
    return pl.pallas_call(
        _nsa_proj_body,
        grid=(T // tm,),
        in_specs=[pl.BlockSpec((tm, D_MODEL), row), _const_spec((1, D_MODEL)), _const_spec(w["w_qt"].shape),
                  _const_spec(w["w_kv"].shape), _const_spec(w["w_kvt"].shape), _const_spec(w["w_gt"].shape)],
        out_specs=[pl.BlockSpec((NSA_QW, tm), col), pl.BlockSpec((tm, NSA_KVW), row),
                   pl.BlockSpec((tm, NSA_KVW), row), pl.BlockSpec((tm, NSA_KVW), row),
                   pl.BlockSpec((NSA_KVW, tm), col), pl.BlockSpec((NSA_KVW, tm), col),
                   pl.BlockSpec((LANE, tm), col)],
        out_shape=[jax.ShapeDtypeStruct((NSA_QW, T), BF16), jax.ShapeDtypeStruct((T, NSA_KVW), BF16),
                   jax.ShapeDtypeStruct((T, NSA_KVW), BF16), jax.ShapeDtypeStruct((T, NSA_KVW), BF16),
                   jax.ShapeDtypeStruct((NSA_KVW, T), BF16), jax.ShapeDtypeStruct((NSA_KVW, T), BF16),
                   jax.ShapeDtypeStruct((LANE, T), F32)],
        compiler_params=_cparams(("parallel",)),
        name="nsa_proj",
    )(h, g.reshape(1, D_MODEL), w["w_qt"], w["w_kv"], w["w_kvt"], w["w_gt"])


def _compress_body(x_ref, pos_ref, w1_ref, w2_ref, out_ref, out_t_ref):
    n_chunk = x_ref.shape[0]
    x = x_ref[...].astype(F32)
    xa = (x + pos_ref[0]).astype(BF16)
    xb = (x + pos_ref[1]).astype(BF16)
    a = jnp.dot(xa, w1_ref[0], preferred_element_type=F32)
    b = jnp.dot(xb, w1_ref[1], preferred_element_type=F32)
    pre = a + pltpu.roll(b, n_chunk - 1, 0)
    hid = jax.nn.gelu(pre, approximate=True).astype(BF16)
    kv = jnp.dot(hid, w2_ref[...], preferred_element_type=F32)
    out_ref[...] = kv.astype(BF16)
    out_t_ref[...] = kv.T.astype(BF16)


def _compress(x, pos, w1, w2):
    B, G, n_chunk, width = x.shape
    return pl.pallas_call(
        _compress_body,
        grid=(B, G),
        in_specs=[pl.BlockSpec((None, None, n_chunk, width), lambda b, g: (b, g, 0, 0)),
                  _const_spec(pos.shape), _const_spec(w1.shape), _const_spec(w2.shape)],
        out_specs=[pl.BlockSpec((None, None, n_chunk, LANE), lambda b, g: (b, g, 0, 0)),
                   pl.BlockSpec((None, None, LANE, n_chunk), lambda b, g: (b, g, 0, 0))],
        out_shape=[jax.ShapeDtypeStruct((B, G, n_chunk, LANE), BF16),
                   jax.ShapeDtypeStruct((B, G, LANE, n_chunk), BF16)],
        compiler_params=_cparams(("parallel", "parallel")),
        name="nsa_compress",
    )(x, pos, w1, w2)


def _nsa_attn_body(*refs, S):
    step = pl.program_id(2)
    for qi in range(S // NSA_T):
        pl.when(step == qi)(functools.partial(_nsa_tile, qi, *refs, S=S))


def _nsa_tile(qi, qt_ref, kvc_ref, kvct_ref, kvs_ref, kvst_ref, kvw_ref, kvwt_ref, gate_ref, cbt_ref, dt_ref,
              et_ref, ovt_ref, o_ref, *scratch, S):
    t = NSA_T
    R = NSA_R
    n_sel = S // SEL_BLOCK
    t0 = qi * t
    s_sel, p_sel, s_win, p_win = scratch[0:2], scratch[2:4], scratch[4:6], scratch[6:8]
    qt = jnp.concatenate([qt_ref[r * LANE:(r + 1) * LANE, :] for r in range(R)], axis=1)
    tile_r = lambda x: jnp.concatenate([x] * R, axis=1)
    half_w = 2 * t
    tile_h = lambda x: jnp.concatenate([x] * (half_w // t), axis=1)

    kvc = kvc_ref[...]
    n_cp = kvc.shape[0]
    sc = jnp.dot(kvc, qt, preferred_element_type=F32)
    cend = lax.broadcasted_iota(jnp.int32, (n_cp, t), 0) * CMP_STRIDE + (CMP_BLOCK - 1)
    valid = tile_r(t0 + lax.broadcasted_iota(jnp.int32, (n_cp, t), 1) >= cend)
    shift = (qi * (t // CMP_STRIDE) + CB_CENTER) % n_cp
    cb = pltpu.roll(cbt_ref[...], shift, 0) if shift else cbt_ref[...]
    sc = jnp.where(valid, sc + cb, NEG)
    e = jnp.exp2(sc - jnp.max(sc, axis=0, keepdims=True))
    p = jnp.where(valid, e / jnp.sum(e, axis=0, keepdims=True), 0.0)
    o_cmp = jnp.dot(kvct_ref[...], p.astype(BF16), preferred_element_type=F32)

    psum = p[:, 0:t]
    for r in range(1, R):
        psum = psum + p[:, r * t:(r + 1) * t]
    imp = jnp.dot(ovt_ref[...], psum, preferred_element_type=F32, precision=lax.Precision.HIGHEST)
    jj = lax.broadcasted_iota(jnp.int32, (n_sel, t), 0)
    blk_t = (t0 + lax.broadcasted_iota(jnp.int32, (n_sel, t), 1)) >> SEL_SHIFT
    forced = (jj == 0) | (jj == blk_t) | (jj == blk_t - 1)
    score = jnp.where(forced, 1e6, jnp.where(jj <= blk_t, imp, -1e6))
    cnt = jnp.zeros((n_sel, t), jnp.int32)
    for jp in range(n_sel):
        row = score[jp:jp + 1, :]
        beats = (row > score) | ((row == score) & (jj > jp))
        cnt = cnt + beats.astype(jnp.int32)
    sel = jnp.where(cnt < min(SEL_TOP_N, n_sel), 1.0, 0.0)
    sel = jnp.concatenate([sel, jnp.zeros((et_ref.shape[1] - n_sel, t), F32)], axis=0).astype(BF16)

    kw = min(qi, WINDOW // t)
    nw = (kw + 1) * t
    w0 = (qi - kw) * t
    n = (qi + 1) * t

    def win_add(off, w):
        lo_d, hi_d = kw * t - off - (w - 1), kw * t - off + (t - 1)
        if lo_d >= 0 and hi_d < WINDOW:
            return None
        dist = (kw * t - off + lax.broadcasted_iota(jnp.int32, (w, t), 1)
                - lax.broadcasted_iota(jnp.int32, (w, t), 0))
        return jnp.where((dist >= 0) & (dist < WINDOW), 0.0, NEG)

    def sel_add(off, w):
        ok = jnp.dot(et_ref[off:off + w, :], sel, preferred_element_type=F32) > 0.5
        if off + w > n - t:
            kpos = off + lax.broadcasted_iota(jnp.int32, (w, t), 0)
            ok = ok & (kpos <= t0 + lax.broadcasted_iota(jnp.int32, (w, t), 1))
        return jnp.where(ok, 0.0, NEG)

    def stream(half, k_ref, vt_ref, k0, nk, add_fn, s_ref, p_ref):
        cs = slice(half * half_w, (half + 1) * half_w)
        qh = qt[:, cs]

        def score(off, w):
            s = jnp.dot(k_ref[k0 + off:k0 + off + w, :], qh, preferred_element_type=F32)
            add = add_fn(off, w)
            if add is not None:
                s = s + tile_h(add)
            if off + w == nk:
                s = s + dt_ref[2 * t - w:, cs]
            return s

        return _SoftmaxStreamT(score, _chunks(nk, lead=nk % SCORE_CHUNK), vt_ref[:, k0:k0 + nk], s_ref, p_ref)

    halves = range(R * t // half_w)
    wins = [stream(h, kvw_ref, kvwt_ref, w0, nw, win_add, s_win[h], p_win[h]) for h in halves]
    sels = [stream(h, kvs_ref, kvst_ref, 0, n, sel_add, s_sel[h], p_sel[h]) for h in halves]
    _trace_pipelined(wins[:1] + sels + wins[1:])

    g = pl.program_id(0)
    heads = []
    for r in range(R):
        h, hc = divmod(r * t, half_w)
        branches = (o_cmp[NSA_QK:, r * t:(r + 1) * t], sels[h].out[NSA_QK:, hc:hc + t], wins[h].out[NSA_QK:, hc:hc + t])
        o = None
        for br, ob in enumerate(branches):
            gate = gate_ref[pl.ds(g * GATES_PER_GROUP + br * R + r, 1), :]
            o = gate * ob if o is None else o + gate * ob
        heads.append(o)
    o_ref[...] = jnp.concatenate(heads, axis=0).T.astype(BF16)


def _nsa_attn(qt, kvc, kvct, kvs, kvst, kvw, kvwt, gates, cbt, dtt, emat_t, ovt, B, S):
    t = NSA_T
    nq = S // t
    T = B * S
    G, R = NSA_GROUPS, NSA_R
    n_cp = kvc.shape[2]
    half_w = 2 * t
    nh = R * t // half_w
    kv_spec = pl.BlockSpec((S, LANE), lambda g, b, i: (b, g))
    kvt_spec = pl.BlockSpec((LANE, S), lambda g, b, i: (g, b))
    return pl.pallas_call(
        functools.partial(_nsa_attn_body, S=S),
        grid=(G, B, nq),
        in_specs=[pl.BlockSpec((R * LANE, t), lambda g, b, i: (g, b * nq + i)),
                  pl.BlockSpec((None, None, n_cp, LANE), lambda g, b, i: (b, g, 0, 0)),
                  pl.BlockSpec((None, None, LANE, n_cp), lambda g, b, i: (b, g, 0, 0)),
                  kv_spec, kvt_spec, kv_spec, kvt_spec,
                  pl.BlockSpec((LANE, t), lambda g, b, i: (0, b * nq + i)),
                  pl.BlockSpec((None, n_cp, R * t), lambda g, b, i: (g, 0, 0)),
                  pl.BlockSpec((None, 2 * t, R * t), lambda g, b, i: (g, 0, 0)),
                  _const_spec(emat_t.shape), _const_spec(ovt.shape)],
        out_specs=pl.BlockSpec((t, R * NSA_V), lambda g, b, i: (b * nq + i, g)),
        out_shape=jax.ShapeDtypeStruct((T, NSA_HEADS * NSA_V), BF16),
        scratch_shapes=([pltpu.VMEM((S, half_w), F32)] * nh + [pltpu.VMEM((S, half_w), BF16)] * nh
                        + [pltpu.VMEM((WINDOW + t, half_w), F32)] * nh
                        + [pltpu.VMEM((WINDOW + t, half_w), BF16)] * nh),
        compiler_params=_cparams(("parallel", "parallel", "arbitrary")),
        name="nsa_attn",
    )(qt, kvc, kvct, kvs, kvst, kvw, kvwt, gates, cbt, dtt, emat_t, ovt)


def _t5_bucket(dist):
    n = jnp.maximum(dist, 0)
    max_exact = REL_BUCKETS // 2
    nf = jnp.maximum(n, 1).astype(F32)
    large = max_exact + (jnp.log(nf / max_exact) / math.log(REL_MAX_DIST / max_exact)
                         * (REL_BUCKETS - max_exact)).astype(jnp.int32)
    large = jnp.minimum(large, REL_BUCKETS - 1)
    return jnp.where(n < max_exact, n, large)


def _np_bucket(n):
    n = np.maximum(np.asarray(n), 0)
    max_exact = REL_BUCKETS // 2
    large = max_exact + (np.log(np.maximum(n, 1) / max_exact) / math.log(REL_MAX_DIST / max_exact)
                         * (REL_BUCKETS - max_exact)).astype(np.int64)
    return np.where(n < max_exact, n, np.minimum(large, REL_BUCKETS - 1))


def _bias_tables(rel_bias, S):
    t = NSA_T
    G, R = NSA_GROUPS, NSA_R
    n_cp = S // CMP_STRIDE
    a = np.arange(t)[:, None]
    dist_d = a - np.arange(2 * t)[None, :] + t
    dist_c = a - CMP_STRIDE * (np.arange(n_cp)[None, :] - CB_CENTER) - (CMP_BLOCK - 1)
    uncovered = min(t + 1, CMP_STRIDE * (CB_CENTER + 1) - (CMP_BLOCK - 1))
    assert (_np_bucket(np.arange(uncovered, 2 * S)) == REL_BUCKETS - 1).all()
    wrap_from = n_cp - max((S // t - 1) * (t // CMP_STRIDE) - CB_CENTER, 0)
    assert (dist_c[:, wrap_from:] < 0).all() and (dist_c[:, -1] < 0).all()

    def lookup(dist):
        oh = jax.nn.one_hot(_t5_bucket(jnp.asarray(dist)), REL_BUCKETS, dtype=F32)
        oh = oh - jax.nn.one_hot(REL_BUCKETS - 1, REL_BUCKETS, dtype=F32)
        val = jnp.einsum("acb,bh->hac", oh, rel_bias, precision=lax.Precision.HIGHEST)
        return val.reshape(G, R, *dist.shape)

    dtab = lookup(dist_d)
    cbias = jnp.where(jnp.asarray(dist_c >= 0), lookup(dist_c), 0.0)
    keys_first = lambda x: x.transpose(0, 3, 1, 2).reshape(G, x.shape[3], R * t)
    return keys_first(dtab) * LOG2E, keys_first(cbias) * LOG2E


def _selection_tables(S):
    n_cp = S // CMP_STRIDE
    n_cmp = (S - CMP_BLOCK) // CMP_STRIDE + 1
    n_sel = S // SEL_BLOCK
    cs = np.arange(n_cp) * CMP_STRIDE
    ce = cs + CMP_BLOCK
    ss = np.arange(n_sel) * SEL_BLOCK
    se = ss + SEL_BLOCK
    ov = np.minimum(ce[:, None], se[None, :]) - np.maximum(cs[:, None], ss[None, :])
    ov = (np.clip(ov, 0, None) / CMP_BLOCK).astype(np.float32)
    ov[n_cmp:] = 0.0
    emat_t = (np.arange(S)[:, None] // SEL_BLOCK == np.arange(max(n_sel, LANE))[None, :]).astype(np.float32)
    return jnp.asarray(ov.T), jnp.asarray(emat_t, dtype=BF16)


def _rope_tables(S):
    half = MLA_ROPE // 2
    inv = ROPE_THETA ** (-jnp.arange(half, dtype=F32) * 2.0 / MLA_ROPE)
    ang = jnp.arange(S, dtype=F32)[:, None] * inv[None, :]
    cos, sin = jnp.cos(ang), jnp.sin(ang)
    ones = jnp.ones((S, MLA_NOPE), F32)
    pad1 = jnp.ones((S, LANE - MLA_NOPE - MLA_ROPE), F32)
    cos128 = jnp.concatenate([ones, cos, cos, pad1], axis=1)
    sin128 = jnp.concatenate([0 * ones, sin, sin, 0 * pad1], axis=1)
    return cos128, sin128


def _mla_weights(w_in, q_norm, kv_norm, w_uq, w_ukv, w_o):
    H = MLA_HEADS
    half = MLA_ROPE // 2
    pad = LANE - MLA_NOPE - MLA_ROPE
    kr = w_in[:, MLA_Q_LORA + MLA_KV_LORA:]
    kr_sw = jnp.concatenate([-kr[:, half:], kr[:, :half]], axis=1)
    z = lambda n: jnp.zeros((D_MODEL, n), F32)
    w_in_ext = jnp.concatenate([w_in[:, :MLA_Q_LORA + MLA_KV_LORA],
                                z(MLA_NOPE), kr, z(pad), z(MLA_NOPE), kr_sw, z(pad)], axis=1)
    uq = w_uq.reshape(MLA_Q_LORA, H, MLA_NOPE + MLA_ROPE)
    qn, qr = uq[..., :MLA_NOPE], uq[..., MLA_NOPE:]
    qr_sw = jnp.concatenate([-qr[..., half:], qr[..., :half]], axis=-1)
    zq = jnp.zeros((MLA_Q_LORA, H, pad), F32)
    w_uq_p = jnp.concatenate([qn, qr, zq], axis=-1).reshape(MLA_Q_LORA, H * LANE)
    w_uq_s = jnp.concatenate([0 * qn, qr_sw, zq], axis=-1).reshape(MLA_Q_LORA, H * LANE)
    ukv = w_ukv.reshape(MLA_KV_LORA, H, MLA_NOPE + MLA_V)
    zk = jnp.zeros((MLA_KV_LORA, H, LANE - MLA_NOPE), F32)
    w_uk = jnp.concatenate([ukv[..., :MLA_NOPE], zk], axis=-1).reshape(MLA_KV_LORA, H * LANE)
    w_uv = ukv[..., MLA_NOPE:].reshape(MLA_KV_LORA, H * MLA_V)
    return dict(w_in=w_in_ext.astype(BF16), q_norm=q_norm.reshape(1, -1), kv_norm=kv_norm.reshape(1, -1),
                w_uqt=w_uq_p.T.astype(BF16), w_uqst=w_uq_s.T.astype(BF16), w_uk=w_uk.astype(BF16),
                w_uvt=w_uv.T.astype(BF16), w_o=w_o.astype(BF16))


def _nsa_weights(w_in, pos_k, w1_k, w2_k, pos_v, w1_v, w2_v, w_o):
    H, G = NSA_HEADS, NSA_GROUPS
    gw = G * NSA_QK
    q0 = H * NSA_QK
    wq = w_in[:, :q0].reshape(D_MODEL, H, NSA_QK)
    wq = jnp.concatenate([wq, jnp.zeros_like(wq)], axis=-1).reshape(D_MODEL, H * LANE)
    cols = []
    for n in range(3):
        k = w_in[:, q0 + (2 * n) * gw: q0 + (2 * n + 1) * gw].reshape(D_MODEL, G, NSA_QK)
        v = w_in[:, q0 + (2 * n + 1) * gw: q0 + (2 * n + 2) * gw].reshape(D_MODEL, G, NSA_V)
        cols.append(jnp.concatenate([k, v], axis=-1).reshape(D_MODEL, G * LANE))
    w_kv = jnp.concatenate(cols, axis=1)
    wg = w_in[:, q0 + 6 * gw:].reshape(D_MODEL, G, NSA_R, 3).transpose(0, 1, 3, 2).reshape(D_MODEL, 3 * H)
    wg = jnp.concatenate([wg, jnp.zeros((D_MODEL, LANE - 3 * H), F32)], axis=1)
    eye = jnp.eye(2, dtype=F32)
    cw = CMP_STRIDE * 2 * NSA_QK
    pos = jnp.stack([pos_k, pos_v]).reshape(2, 2, CMP_STRIDE, NSA_QK)
    pos = pos.transpose(1, 2, 0, 3).reshape(2, 1, cw)
    w1 = jnp.stack([w1_k, w1_v]).reshape(2, 2, CMP_STRIDE, NSA_QK, CMP_HIDDEN)
    w1 = jnp.einsum("khldj,kq->hlkdqj", w1, eye).reshape(2, cw, 2 * CMP_HIDDEN).astype(BF16)
    w2 = jnp.einsum("kjd,kq->kjqd", jnp.stack([w2_k, w2_v]), eye).reshape(2 * CMP_HIDDEN, 2 * NSA_QK).astype(BF16)
    return dict(w_qt=wq.T.astype(BF16), w_kv=w_kv.astype(BF16), w_kvt=w_kv[:, NSA_KVW:].T.astype(BF16),
                w_gt=wg.T.astype(BF16), pos=pos, w1=w1, w2=w2, w_o=w_o.astype(BF16))


def kernel(x, ffn_norm_a, ffn_a_w_gate, ffn_a_w_up, ffn_a_w_down, mix_norm, ffn_norm_b, ffn_b_w_gate, ffn_b_w_up, ffn_b_w_down, final_norm, rel_bias, mla_w_in, mla_q_norm, mla_kv_norm, mla_w_uq, mla_w_ukv, mla_w_o, nsa_w_in, nsa_cmp_pos_k, nsa_cmp_w1_k, nsa_cmp_w2_k, nsa_cmp_pos_v, nsa_cmp_w1_v, nsa_cmp_w2_v, nsa_w_o):
    B, S, D = x.shape
    assert D == D_MODEL and S % ATT_T == 0 and S % NSA_T == 0 and (B * S) % FFN_TM == 0
    T = B * S
    G = NSA_GROUPS
    n_chunk = S // CMP_STRIDE
    cos128, sin128 = _rope_tables(S)
    dtab, cbias = _bias_tables(rel_bias, S)
    ovt, emat_t = _selection_tables(S)
    bf = lambda w: w.astype(BF16)

    h = x.reshape(T, D)
    for i in range(DEPTH):
        h = _ffn(h, ffn_norm_a[i], bf(ffn_a_w_gate[i]), bf(ffn_a_w_up[i]), bf(ffn_a_w_down[i]))
        j = i // N_MIXERS
        if i % N_MIXERS == 0:
            w = _mla_weights(mla_w_in[j], mla_q_norm[j], mla_kv_norm[j], mla_w_uq[j], mla_w_ukv[j], mla_w_o[j])
            q, k, v = _mla_proj(h, mix_norm[i], w, cos128, sin128, S)
            o = _mla_attn(q, k, v, B, S)
        else:
            w = _nsa_weights(nsa_w_in[j], nsa_cmp_pos_k[j], nsa_cmp_w1_k[j], nsa_cmp_w2_k[j],
                             nsa_cmp_pos_v[j], nsa_cmp_w1_v[j], nsa_cmp_w2_v[j], nsa_w_o[j])
            qt, kvc_in, kvs, kvw, kvst, kvwt, gates = _nsa_proj(h, mix_norm[i], w)
            xc = kvc_in.reshape(B, n_chunk, CMP_STRIDE, G, LANE).transpose(0, 3, 1, 2, 4)
            xc = xc.reshape(B, G, n_chunk, CMP_STRIDE * LANE)
            kvc, kvct = _compress(xc, w["pos"], w["w1"], w["w2"])
            o = _nsa_attn(qt, kvc, kvct, kvs, kvst, kvw, kvwt, gates, cbias, dtab, emat_t, ovt, B, S)
        h = _ffn(h, ffn_norm_b[i], bf(ffn_b_w_gate[i]), bf(ffn_b_w_up[i]), bf(ffn_b_w_down[i]),
                 proj=(o, w["w_o"]), final_g=final_norm if i == DEPTH - 1 else None)
    return h.reshape(B, S, D)
```

```python
import functools
import math

import numpy as np
import jax
import jax.numpy as jnp
from jax import lax
from jax.experimental import pallas as pl
from jax.experimental.pallas import tpu as pltpu

F32 = jnp.float32
BF16 = jnp.bfloat16

D_MODEL = 1024
DEPTH = 4
N_MIXERS = 2
RMS_EPS = 1e-6
FFN_HIDDEN = 2816
NEG = -1e30
MLA_HEADS = 16
MLA_Q_LORA = 384
MLA_KV_LORA = 256
MLA_NOPE = 64
MLA_ROPE = 32
MLA_V = 64
ROPE_THETA = 10000.0
NSA_HEADS = 16
NSA_GROUPS = 4
NSA_R = NSA_HEADS // NSA_GROUPS
NSA_QK = 64
NSA_V = 64
CMP_BLOCK = 32
CMP_STRIDE = 16
CMP_HIDDEN = 128
SEL_BLOCK = 64
SEL_SHIFT = SEL_BLOCK.bit_length() - 1
assert 1 << SEL_SHIFT == SEL_BLOCK
SEL_TOP_N = 16
WINDOW = 512
REL_BUCKETS = 32
REL_MAX_DIST = 128

LANE = 128
VMEM_LIMIT = 56 * 1024 * 1024

FFN_TM = 512
FFN_TF = 256
PROJ_TM = 256
ATT_T = 256
NSA_T = 128
CB_CENTER = 64


def _cparams(sem):
    return pltpu.CompilerParams(dimension_semantics=sem, vmem_limit_bytes=VMEM_LIMIT)


def _rms(x, g):
    ms = jnp.mean(x * x, axis=-1, keepdims=True)
    return x * lax.rsqrt(ms + RMS_EPS) * g


def _const_spec(shape):
    nd = len(shape)
    return pl.BlockSpec(shape, lambda *_: (0,) * nd)


def _ffn_body(*refs, has_proj, has_final):
    it = iter(refs)
    h_ref = next(it)
    if has_proj:
        o_in_ref = next(it)
        wo_ref = next(it)
    g_ref = next(it)
    wg_ref = next(it)
    wu_ref = next(it)
    wd_ref = next(it)
    if has_final:
        gf_ref = next(it)
    out_ref = next(it)
    a_ref = next(it)

    x = h_ref[...]
    if has_proj:
        x = x + jnp.dot(o_in_ref[...], wo_ref[...], preferred_element_type=F32)
    xn = _rms(x, g_ref[...]).astype(BF16)
    for c in range(FFN_HIDDEN // FFN_TF):
        sl = slice(c * FFN_TF, (c + 1) * FFN_TF)
        gt = jnp.dot(xn, wg_ref[:, sl], preferred_element_type=F32)
        up = jnp.dot(xn, wu_ref[:, sl], preferred_element_type=F32)
        a_ref[:, sl] = (gt * jax.nn.sigmoid(gt) * up).astype(BF16)
    y = x + 0.5 * jnp.dot(a_ref[...], wd_ref[...], preferred_element_type=F32)
    if has_final:
        y = _rms(y, gf_ref[...])
    out_ref[...] = y


def _ffn(h, g, wg, wu, wd, proj=None, final_g=None):
    T = h.shape[0]
    tm = FFN_TM
    row = lambda i: (i, 0)
    in_specs = [pl.BlockSpec((tm, D_MODEL), row)]
    args = [h]
    if proj is not None:
        o_in, wo = proj
        in_specs += [pl.BlockSpec((tm, o_in.shape[1]), row), _const_spec(wo.shape)]
        args += [o_in, wo]
    in_specs += [_const_spec((1, D_MODEL)), _const_spec(wg.shape), _const_spec(wu.shape), _const_spec(wd.shape)]
    args += [g.reshape(1, D_MODEL), wg, wu, wd]
    if final_g is not None:
        in_specs.append(_const_spec((1, D_MODEL)))
        args.append(final_g.reshape(1, D_MODEL))
    return pl.pallas_call(
        functools.partial(_ffn_body, has_proj=proj is not None, has_final=final_g is not None),
        grid=(T // tm,),
        in_specs=in_specs,
        out_specs=pl.BlockSpec((tm, D_MODEL), row),
        out_shape=jax.ShapeDtypeStruct((T, D_MODEL), F32),
        scratch_shapes=[pltpu.VMEM((tm, FFN_HIDDEN), BF16)],
        compiler_params=_cparams(("parallel",)),
        name="ffn",
    )(*args)


MLA_CQ0, MLA_CKV0, MLA_KR0, MLA_KRS0, MLA_IN_W = 0, 384, 640, 768, 896
MLA_HCHUNK = 4


def _mla_proj_body(h_ref, g_ref, win_ref, qn_ref, kvn_ref, wuqt_ref, wuqst_ref, wuk_ref, wuvt_ref,
                   cos_ref, sin_ref, cost_ref, sint_ref, qt_out, k_out, vt_out):
    xn = _rms(h_ref[...], g_ref[...]).astype(BF16)
    proj = jnp.dot(xn, win_ref[...], preferred_element_type=F32)
    cq = _rms(proj[:, MLA_CQ0:MLA_CKV0], qn_ref[...]).astype(BF16)
    ckv = _rms(proj[:, MLA_CKV0:MLA_KR0], kvn_ref[...]).astype(BF16)
    cos = cos_ref[...]
    sin = sin_ref[...]
    kr = proj[:, MLA_KR0:MLA_KRS0] * cos + proj[:, MLA_KRS0:MLA_IN_W] * sin
    cos_t = cost_ref[...]
    sin_t = sint_ref[...]
    scale = (MLA_NOPE + MLA_ROPE) ** -0.5 * LOG2E
    vt_out[...] = lax.dot_general(wuvt_ref[...], ckv, _NT, preferred_element_type=F32).astype(BF16)
    cw = MLA_HCHUNK * LANE
    for c in range(MLA_HEADS // MLA_HCHUNK):
        sl = slice(c * cw, (c + 1) * cw)
        qt = lax.dot_general(wuqt_ref[sl, :], cq, _NT, preferred_element_type=F32)
        qst = lax.dot_general(wuqst_ref[sl, :], cq, _NT, preferred_element_type=F32)
        kn = jnp.dot(ckv, wuk_ref[:, sl], preferred_element_type=F32)
        for hh in range(MLA_HCHUNK):
            hs = slice(hh * LANE, (hh + 1) * LANE)
            os_ = slice(c * cw + hh * LANE, c * cw + (hh + 1) * LANE)
            qt_out[os_, :] = ((qt[hs] * cos_t + qst[hs] * sin_t) * scale).astype(BF16)
            k_out[:, os_] = (kn[:, hs] + kr).astype(BF16)


def _mla_proj(h, g, w, cos128, sin128, S):
    T = h.shape[0]
    tm = PROJ_TM
    ns = S // tm
    row = lambda i: (i, 0)
    col = lambda i: (0, i)
    pos = lambda i: (i % ns, 0)
    pos_t = lambda i: (0, i % ns)
    HL = MLA_HEADS * LANE
    HV = MLA_HEADS * MLA_V
    return pl.pallas_call(
        _mla_proj_body,
        grid=(T // tm,),
        in_specs=[pl.BlockSpec((tm, D_MODEL), row), _const_spec((1, D_MODEL)),
                  _const_spec(w["w_in"].shape), _const_spec((1, MLA_Q_LORA)), _const_spec((1, MLA_KV_LORA)),
                  _const_spec(w["w_uqt"].shape), _const_spec(w["w_uqst"].shape),
                  _const_spec(w["w_uk"].shape), _const_spec(w["w_uvt"].shape),
                  pl.BlockSpec((tm, LANE), pos), pl.BlockSpec((tm, LANE), pos),
                  pl.BlockSpec((LANE, tm), pos_t), pl.BlockSpec((LANE, tm), pos_t)],
        out_specs=[pl.BlockSpec((HL, tm), col), pl.BlockSpec((tm, HL), row), pl.BlockSpec((HV, tm), col)],
        out_shape=[jax.ShapeDtypeStruct((HL, T), BF16), jax.ShapeDtypeStruct((T, HL), BF16),
                   jax.ShapeDtypeStruct((HV, T), BF16)],
        compiler_params=_cparams(("parallel",)),
        name="mla_proj",
    )(h, g.reshape(1, D_MODEL), w["w_in"], w["q_norm"], w["kv_norm"], w["w_uqt"], w["w_uqst"],
      w["w_uk"], w["w_uvt"], cos128, sin128, cos128.T, sin128.T)


_NT = (((1,), (1,)), ((), ()))
SCORE_CHUNK = 2 * LANE


def _chunks(n, lead=0):
    out = [(0, lead)] if lead else []
    return out + [(o, min(SCORE_CHUNK, n - o)) for o in range(lead, n, SCORE_CHUNK)]


SUBLANE = 8


def _fold(x, op):
    w, cols = x.shape
    return op(x.reshape(w // SUBLANE, SUBLANE, cols), axis=0)


class _SoftmaxStreamT:
    def __init__(self, score_fn, chunks, values_fn, s_ref):
        self.score_fn, self.chunks, self.values_fn, self.s_ref = score_fn, chunks, values_fn, s_ref

    def pass1(self):
        mp = None
        for off, w in self.chunks:
            s = self.score_fn(off, w)
            self.s_ref[off:off + w, :] = s
            part = _fold(s, jnp.max)
            mp = part if mp is None else jnp.maximum(mp, part)
            yield
        self.m = jnp.max(mp, axis=0, keepdims=True)

    def pass2(self):
        lp = acc = None
        for off, w in self.chunks:
            p = jnp.exp2(self.s_ref[off:off + w, :] - self.m)
            part = _fold(p, jnp.sum)
            lp = part if lp is None else lp + part
            pv = jnp.dot(self.values_fn(off, w), p.astype(BF16), preferred_element_type=F32)
            acc = pv if acc is None else acc + pv
            yield
        self.out = acc / jnp.sum(lp, axis=0, keepdims=True)


def _trace_pipelined(streams, side=()):
    step = lambda gens: [g for g in gens if next(g, StopIteration) is not StopIteration]
    side = list(side)
    for _ in streams[0].pass1():
        side = step(side)
    for i, st in enumerate(streams):
        gens = [st.pass2()] + ([streams[i + 1].pass1()] if i + 1 < len(streams) else [])
        while gens:
            gens = step(gens)
            side = step(side)
    while side:
        side = step(side)


MLA_HPS = 4


def _mla_attn_body(qt_ref, k_ref, vt_ref, o_ref, *s_refs, nq):
    t = ATT_T
    qi = pl.program_id(2)

    def branch(nt):
        n = nt * t
        causal = lax.broadcasted_iota(jnp.int32, (t, t), 0) <= lax.broadcasted_iota(jnp.int32, (t, t), 1)
        streams = []
        for hh in range(MLA_HPS):
            hs = slice(hh * LANE, (hh + 1) * LANE)
            qt = qt_ref[hs, :]
            vs = slice(hh // 2 * LANE, (hh // 2 + 1) * LANE)

            def score(off, w, qt=qt, hs=hs):
                s = jnp.dot(k_ref[off:off + w, hs], qt, preferred_element_type=F32)
                return jnp.where(causal, s, NEG) if off == n - t else s

            values = lambda off, w, vs=vs: vt_ref[vs, off:off + w]
            streams.append(_SoftmaxStreamT(score, _chunks(n), values, s_refs[hh]))
        _trace_pipelined(streams)
        outs = [st.out[hh % 2 * MLA_V:(hh % 2 + 1) * MLA_V] for hh, st in enumerate(streams)]
        o_ref[...] = jnp.concatenate(outs, axis=0).T.astype(BF16)

    for nt in range(1, nq + 1):
        pl.when(qi == nt - 1)(functools.partial(branch, nt))


def _mla_attn(qt, k, vt, B, S):
    t = ATT_T
    nq = S // t
    T = B * S
    hps = MLA_HPS
    return pl.pallas_call(
        functools.partial(_mla_attn_body, nq=nq),
        grid=(B, MLA_HEADS // hps, nq),
        in_specs=[pl.BlockSpec((hps * LANE, t), lambda b, p, i: (p, b * nq + i)),
                  pl.BlockSpec((S, hps * LANE), lambda b, p, i: (b, p)),
                  pl.BlockSpec((hps * MLA_V, S), lambda b, p, i: (p, b))],
        out_specs=pl.BlockSpec((t, hps * MLA_V), lambda b, p, i: (b * nq + i, p)),
        out_shape=jax.ShapeDtypeStruct((T, MLA_HEADS * MLA_V), BF16),
        scratch_shapes=[pltpu.VMEM((S, t), F32)] * hps,
        compiler_params=_cparams(("parallel", "parallel", "arbitrary")),
        name="mla_attn",
    )(qt, k, vt)


NSA_QW = NSA_HEADS * LANE
NSA_KVW = NSA_GROUPS * LANE
NSA_IN_W = NSA_QW + 3 * NSA_KVW + LANE
GATES_PER_GROUP = 3 * NSA_R
RANK_STEPS_PER_STAGE = 8
LOG2E = math.log2(math.e)


def _nsa_proj_body(h_ref, g_ref, wqt_ref, wkv_ref, wkvt_ref, wgt_ref,
                   qt_out, kvc_out, kvs_out, kvw_out, kvst_out, kvwt_out, gate_out):
    xn = _rms(h_ref[...], g_ref[...]).astype(BF16)
    scale = NSA_QK ** -0.5 * LOG2E
    cw = 4 * LANE
    for c in range(NSA_QW // cw):
        sl = slice(c * cw, (c + 1) * cw)
        qt = lax.dot_general(wqt_ref[sl, :], xn, _NT, preferred_element_type=F32)
        qt_out[sl, :] = (qt * scale).astype(BF16)
    for n, out in enumerate((kvc_out, kvs_out, kvw_out)):
        sl = slice(n * NSA_KVW, (n + 1) * NSA_KVW)
        out[...] = jnp.dot(xn, wkv_ref[:, sl], preferred_element_type=F32).astype(BF16)
    for n, out in enumerate((kvst_out, kvwt_out)):
        sl = slice(n * NSA_KVW, (n + 1) * NSA_KVW)
        out[...] = lax.dot_general(wkvt_ref[sl, :], xn, _NT, preferred_element_type=F32).astype(BF16)
    gl = lax.dot_general(wgt_ref[...], xn, _NT, preferred_element_type=F32)
    gate_out[...] = jax.nn.sigmoid(gl)


def _nsa_proj(h, g, w):
    T = h.shape[0]
    tm = PROJ_TM
    row = lambda i: (i, 0)
    col = lambda i: (0, i)
    return pl.pallas_call(
        _nsa_proj_body,
        grid=(T // tm,),
        in_specs=[pl.BlockSpec((tm, D_MODEL), row), _const_spec((1, D_MODEL)), _const_spec(w["w_qt"].shape),
                  _const_spec(w["w_kv"].shape), _const_spec(w["w_kvt"].shape), _const_spec(w["w_gt"].shape)],
        out_specs=[pl.BlockSpec((NSA_QW, tm), col), pl.BlockSpec((tm, NSA_KVW), row),
                   pl.BlockSpec((tm, NSA_KVW), row), pl.BlockSpec((tm, NSA_KVW), row),
                   pl.BlockSpec((NSA_KVW, tm), col), pl.BlockSpec((NSA_KVW, tm), col),
                   pl.BlockSpec((LANE, tm), col)],
        out_shape=[jax.ShapeDtypeStruct((NSA_QW, T), BF16), jax.ShapeDtypeStruct((T, NSA_KVW), BF16),
                   jax.ShapeDtypeStruct((T, NSA_KVW), BF16), jax.ShapeDtypeStruct((T, NSA_KVW), BF16),
                   jax.ShapeDtypeStruct((NSA_KVW, T), BF16), jax.ShapeDtypeStruct((NSA_KVW, T), BF16),
                   jax.ShapeDtypeStruct((LANE, T), F32)],
        compiler_params=_cparams(("parallel",)),
        name="nsa_proj",
    )(h, g.reshape(1, D_MODEL), w["w_qt"], w["w_kv"], w["w_kvt"], w["w_gt"])


def _compress_body(x_ref, pos_ref, w1_ref, w2_ref, out_ref, out_t_ref):
    n_chunk = x_ref.shape[0]
    x = x_ref[...].astype(F32)
    xa = (x + pos_ref[0]).astype(BF16)
    xb = (x + pos_ref[1]).astype(BF16)
    a = jnp.dot(xa, w1_ref[0], preferred_element_type=F32)
    b = jnp.dot(xb, w1_ref[1], preferred_element_type=F32)
    pre = a + pltpu.roll(b, n_chunk - 1, 0)
    hid = jax.nn.gelu(pre, approximate=True).astype(BF16)
    kv = jnp.dot(hid, w2_ref[...], preferred_element_type=F32)
    out_ref[...] = kv.astype(BF16)
    out_t_ref[...] = kv.T.astype(BF16)


def _compress(x, pos, w1, w2):
    B, G, n_chunk, width = x.shape
    return pl.pallas_call(
        _compress_body,
        grid=(B, G),
        in_specs=[pl.BlockSpec((None, None, n_chunk, width), lambda b, g: (b, g, 0, 0)),
                  _const_spec(pos.shape), _const_spec(w1.shape), _const_spec(w2.shape)],
        out_specs=[pl.BlockSpec((None, None, n_chunk, LANE), lambda b, g: (b, g, 0, 0)),
                   pl.BlockSpec((None, None, LANE, n_chunk), lambda b, g: (b, g, 0, 0))],
        out_shape=[jax.ShapeDtypeStruct((B, G, n_chunk, LANE), BF16),
                   jax.ShapeDtypeStruct((B, G, LANE, n_chunk), BF16)],
        compiler_params=_cparams(("parallel", "parallel")),
        name="nsa_compress",
    )(x, pos, w1, w2)


def _nsa_attn_body(*refs, S):
    step = pl.program_id(2)
    for qi in range(S // NSA_T):
        pl.when(step == qi)(functools.partial(_nsa_tile, qi, *refs, S=S))


def _nsa_tile(qi, qt_ref, qtn_ref, kvc_ref, kvct_ref, kvs_ref, kvst_ref, kvw_ref, kvwt_ref, gate_ref, cbt_ref,
              dt_ref, et_ref, ovt_ref, o_ref, *scratch, S):
    t = NSA_T
    R = NSA_R
    n_sel = S // SEL_BLOCK
    t0 = qi * t
    s_sel, s_win = scratch[0:2], scratch[2:4]
    ocmp_scr, sel_scr = scratch[4:6]
    heads_on_lanes = lambda ref: jnp.concatenate([ref[r * LANE:(r + 1) * LANE, :] for r in range(R)], axis=1)
    qt = heads_on_lanes(qt_ref)
    tile_r = lambda x: jnp.concatenate([x] * R, axis=1)
    half_w = 2 * t
    tile_h = lambda x: jnp.concatenate([x] * (half_w // t), axis=1)

    def compressed_and_selection(q_tile, tile, out):
        kvc = kvc_ref[...]
        n_cp = kvc.shape[0]
        sc = jnp.dot(kvc, q_tile, preferred_element_type=F32)
        yield
        cend = lax.broadcasted_iota(jnp.int32, (n_cp, t), 0) * CMP_STRIDE + (CMP_BLOCK - 1)
        valid = tile_r(tile * t + lax.broadcasted_iota(jnp.int32, (n_cp, t), 1) >= cend)
        shift = (tile * (t // CMP_STRIDE) + CB_CENTER) % n_cp
        cb = pltpu.roll(cbt_ref[...], shift, 0) if shift else cbt_ref[...]
        sc = jnp.where(valid, sc + cb, NEG)
        e = jnp.exp2(sc - jnp.max(sc, axis=0, keepdims=True))
        yield
        p = jnp.where(valid, e / jnp.sum(e, axis=0, keepdims=True), 0.0)
        out["o_cmp"] = jnp.dot(kvct_ref[...], p.astype(BF16), preferred_element_type=F32)
        yield

        psum = p[:, 0:t]
        for r in range(1, R):
            psum = psum + p[:, r * t:(r + 1) * t]
        imp = jnp.dot(ovt_ref[...], psum, preferred_element_type=F32, precision=lax.Precision.HIGHEST)
        yield
        jj = lax.broadcasted_iota(jnp.int32, (n_sel, t), 0)
        blk_t = (tile * t + lax.broadcasted_iota(jnp.int32, (n_sel, t), 1)) >> SEL_SHIFT
        forced = (jj == 0) | (jj == blk_t) | (jj == blk_t - 1)
        score = jnp.where(forced, 1e6, jnp.where(jj <= blk_t, imp, -1e6))
        cnt = jnp.zeros((n_sel, t), jnp.int32)
        for jp in range(n_sel):
            row = score[jp:jp + 1, :]
            beats = (row > score) | ((row == score) & (jj > jp))
            cnt = cnt + beats.astype(jnp.int32)
            if jp % RANK_STEPS_PER_STAGE == RANK_STEPS_PER_STAGE - 1:
                yield
        chosen = jnp.where(cnt < min(SEL_TOP_N, n_sel), 1.0, 0.0)
        chosen = jnp.concatenate([chosen, jnp.zeros((et_ref.shape[1] - n_sel, t), F32)], axis=0)
        out["sel"] = chosen.astype(BF16)

    if qi == 0:
        cur = {}
        for _ in compressed_and_selection(qt, 0, cur):
            pass
        o_cmp, sel = cur["o_cmp"], cur["sel"]
    else:
        o_cmp, sel = ocmp_scr[...], sel_scr[...]
    nxt, side = {}, []
    if (qi + 1) * t < S:
        side = [compressed_and_selection(heads_on_lanes(qtn_ref), qi + 1, nxt)]

    kw = min(qi, WINDOW // t)
    nw = (kw + 1) * t
    w0 = (qi - kw) * t
    n = (qi + 1) * t

    def win_add(off, w):
        lo_d, hi_d = kw * t - off - (w - 1), kw * t - off + (t - 1)
        if lo_d >= 0 and hi_d < WINDOW:
            return None
        dist = (kw * t - off + lax.broadcasted_iota(jnp.int32, (w, t), 1)
                - lax.broadcasted_iota(jnp.int32, (w, t), 0))
        return jnp.where((dist >= 0) & (dist < WINDOW), 0.0, NEG)

    def sel_add(off, w):
        ok = jnp.dot(et_ref[off:off + w, :], sel, preferred_element_type=F32) > 0.5
        if off + w > n - t:
            kpos = off + lax.broadcasted_iota(jnp.int32, (w, t), 0)
            ok = ok & (kpos <= t0 + lax.broadcasted_iota(jnp.int32, (w, t), 1))
        return jnp.where(ok, 0.0, NEG)

    def stream(half, k_ref, vt_ref, k0, nk, add_fn, s_ref):
        cs = slice(half * half_w, (half + 1) * half_w)
        qh = qt[:, cs]

        def score(off, w):
            s = jnp.dot(k_ref[k0 + off:k0 + off + w, :], qh, preferred_element_type=F32)
            add = add_fn(off, w)
            if add is not None:
                s = s + tile_h(add)
            if off + w == nk:
                s = s + dt_ref[2 * t - w:, cs]
            return s

        values = lambda off, w: vt_ref[:, k0 + off:k0 + off + w]
        return _SoftmaxStreamT(score, _chunks(nk, lead=nk % SCORE_CHUNK), values, s_ref)

    halves = range(R * t // half_w)
    wins = [stream(h, kvw_ref, kvwt_ref, w0, nw, win_add, s_win[h]) for h in halves]
    sels = [stream(h, kvs_ref, kvst_ref, 0, n, sel_add, s_sel[h]) for h in halves]
    _trace_pipelined(wins[:1] + sels + wins[1:], side)

    g = pl.program_id(0)
    heads = []
    for r in range(R):
        h, hc = divmod(r * t, half_w)
        branches = (o_cmp[NSA_QK:, r * t:(r + 1) * t], sels[h].out[NSA_QK:, hc:hc + t], wins[h].out[NSA_QK:, hc:hc + t])
        o = None
        for br, ob in enumerate(branches):
            gate = gate_ref[pl.ds(g * GATES_PER_GROUP + br * R + r, 1), :]
            o = gate * ob if o is None else o + gate * ob
        heads.append(o)
    o_ref[...] = jnp.concatenate(heads, axis=0).T.astype(BF16)
    if nxt:
        ocmp_scr[...] = nxt["o_cmp"]
        sel_scr[...] = nxt["sel"]


def _nsa_attn(qt, kvc, kvct, kvs, kvst, kvw, kvwt, gates, cbt, dtt, emat_t, ovt, B, S):
    t = NSA_T
    nq = S // t
    T = B * S
    G, R = NSA_GROUPS, NSA_R
    n_cp = kvc.shape[2]
    half_w = 2 * t
    nh = R * t // half_w
    kv_spec = pl.BlockSpec((S, LANE), lambda g, b, i: (b, g))
    kvt_spec = pl.BlockSpec((LANE, S), lambda g, b, i: (g, b))
    return pl.pallas_call(
        functools.partial(_nsa_attn_body, S=S),
        grid=(G, B, nq),
        in_specs=[pl.BlockSpec((R * LANE, t), lambda g, b, i: (g, b * nq + i)),
                  pl.BlockSpec((R * LANE, t), lambda g, b, i: (g, b * nq + jnp.minimum(i + 1, nq - 1))),
                  pl.BlockSpec((None, None, n_cp, LANE), lambda g, b, i: (b, g, 0, 0)),
                  pl.BlockSpec((None, None, LANE, n_cp), lambda g, b, i: (b, g, 0, 0)),
                  kv_spec, kvt_spec, kv_spec, kvt_spec,
                  pl.BlockSpec((LANE, t), lambda g, b, i: (0, b * nq + i)),
                  pl.BlockSpec((None, n_cp, R * t), lambda g, b, i: (g, 0, 0)),
                  pl.BlockSpec((None, 2 * t, R * t), lambda g, b, i: (g, 0, 0)),
                  _const_spec(emat_t.shape), _const_spec(ovt.shape)],
        out_specs=pl.BlockSpec((t, R * NSA_V), lambda g, b, i: (b * nq + i, g)),
        out_shape=jax.ShapeDtypeStruct((T, NSA_HEADS * NSA_V), BF16),
        scratch_shapes=([pltpu.VMEM((S, half_w), F32)] * nh + [pltpu.VMEM((WINDOW + t, half_w), F32)] * nh
                        + [pltpu.VMEM((LANE, R * t), F32), pltpu.VMEM((emat_t.shape[1], t), BF16)]),
        compiler_params=_cparams(("arbitrary", "arbitrary", "arbitrary")),
        name="nsa_attn",
    )(qt, qt, kvc, kvct, kvs, kvst, kvw, kvwt, gates, cbt, dtt, emat_t, ovt)


def _t5_bucket(dist):
    n = jnp.maximum(dist, 0)
    max_exact = REL_BUCKETS // 2
    nf = jnp.maximum(n, 1).astype(F32)
    large = max_exact + (jnp.log(nf / max_exact) / math.log(REL_MAX_DIST / max_exact)
                         * (REL_BUCKETS - max_exact)).astype(jnp.int32)
    large = jnp.minimum(large, REL_BUCKETS - 1)
    return jnp.where(n < max_exact, n, large)


def _np_bucket(n):
    n = np.maximum(np.asarray(n), 0)
    max_exact = REL_BUCKETS // 2
    large = max_exact + (np.log(np.maximum(n, 1) / max_exact) / math.log(REL_MAX_DIST / max_exact)
                         * (REL_BUCKETS - max_exact)).astype(np.int64)
    return np.where(n < max_exact, n, np.minimum(large, REL_BUCKETS - 1))


def _bias_tables(rel_bias, S):
    t = NSA_T
    G, R = NSA_GROUPS, NSA_R
    n_cp = S // CMP_STRIDE
    a = np.arange(t)[:, None]
    dist_d = a - np.arange(2 * t)[None, :] + t
    dist_c = a - CMP_STRIDE * (np.arange(n_cp)[None, :] - CB_CENTER) - (CMP_BLOCK - 1)
    uncovered = min(t + 1, CMP_STRIDE * (CB_CENTER + 1) - (CMP_BLOCK - 1))
    assert (_np_bucket(np.arange(uncovered, 2 * S)) == REL_BUCKETS - 1).all()
    wrap_from = n_cp - max((S // t - 1) * (t // CMP_STRIDE) - CB_CENTER, 0)
    assert (dist_c[:, wrap_from:] < 0).all() and (dist_c[:, -1] < 0).all()

    def lookup(dist):
        oh = jax.nn.one_hot(_t5_bucket(jnp.asarray(dist)), REL_BUCKETS, dtype=F32)
        oh = oh - jax.nn.one_hot(REL_BUCKETS - 1, REL_BUCKETS, dtype=F32)
        val = jnp.einsum("acb,bh->hac", oh, rel_bias, precision=lax.Precision.HIGHEST)
        return val.reshape(G, R, *dist.shape)

    dtab = lookup(dist_d)
    cbias = jnp.where(jnp.asarray(dist_c >= 0), lookup(dist_c), 0.0)
    keys_first = lambda x: x.transpose(0, 3, 1, 2).reshape(G, x.shape[3], R * t)
    return keys_first(dtab) * LOG2E, keys_first(cbias) * LOG2E


def _selection_tables(S):
    n_cp = S // CMP_STRIDE
    n_cmp = (S - CMP_BLOCK) // CMP_STRIDE + 1
    n_sel = S // SEL_BLOCK
    cs = np.arange(n_cp) * CMP_STRIDE
    ce = cs + CMP_BLOCK
    ss = np.arange(n_sel) * SEL_BLOCK
    se = ss + SEL_BLOCK
    ov = np.minimum(ce[:, None], se[None, :]) - np.maximum(cs[:, None], ss[None, :])
    ov = (np.clip(ov, 0, None) / CMP_BLOCK).astype(np.float32)
    ov[n_cmp:] = 0.0
    emat_t = (np.arange(S)[:, None] // SEL_BLOCK == np.arange(max(n_sel, LANE))[None, :]).astype(np.float32)
    return jnp.asarray(ov.T), jnp.asarray(emat_t, dtype=BF16)


def _rope_tables(S):
    half = MLA_ROPE // 2
    inv = ROPE_THETA ** (-jnp.arange(half, dtype=F32) * 2.0 / MLA_ROPE)
    ang = jnp.arange(S, dtype=F32)[:, None] * inv[None, :]
    cos, sin = jnp.cos(ang), jnp.sin(ang)
    ones = jnp.ones((S, MLA_NOPE), F32)
    pad1 = jnp.ones((S, LANE - MLA_NOPE - MLA_ROPE), F32)
    cos128 = jnp.concatenate([ones, cos, cos, pad1], axis=1)
    sin128 = jnp.concatenate([0 * ones, sin, sin, 0 * pad1], axis=1)
    return cos128, sin128


def _mla_weights(w_in, q_norm, kv_norm, w_uq, w_ukv, w_o):
    H = MLA_HEADS
    half = MLA_ROPE // 2
    pad = LANE - MLA_NOPE - MLA_ROPE
    kr = w_in[:, MLA_Q_LORA + MLA_KV_LORA:]
    kr_sw = jnp.concatenate([-kr[:, half:], kr[:, :half]], axis=1)
    z = lambda n: jnp.zeros((D_MODEL, n), F32)
    w_in_ext = jnp.concatenate([w_in[:, :MLA_Q_LORA + MLA_KV_LORA],
                                z(MLA_NOPE), kr, z(pad), z(MLA_NOPE), kr_sw, z(pad)], axis=1)
    uq = w_uq.reshape(MLA_Q_LORA, H, MLA_NOPE + MLA_ROPE)
    qn, qr = uq[..., :MLA_NOPE], uq[..., MLA_NOPE:]
    qr_sw = jnp.concatenate([-qr[..., half:], qr[..., :half]], axis=-1)
    zq = jnp.zeros((MLA_Q_LORA, H, pad), F32)
    w_uq_p = jnp.concatenate([qn, qr, zq], axis=-1).reshape(MLA_Q_LORA, H * LANE)
    w_uq_s = jnp.concatenate([0 * qn, qr_sw, zq], axis=-1).reshape(MLA_Q_LORA, H * LANE)
    ukv = w_ukv.reshape(MLA_KV_LORA, H, MLA_NOPE + MLA_V)
    zk = jnp.zeros((MLA_KV_LORA, H, LANE - MLA_NOPE), F32)
    w_uk = jnp.concatenate([ukv[..., :MLA_NOPE], zk], axis=-1).reshape(MLA_KV_LORA, H * LANE)
    w_uv = ukv[..., MLA_NOPE:].reshape(MLA_KV_LORA, H * MLA_V)
    return dict(w_in=w_in_ext.astype(BF16), q_norm=q_norm.reshape(1, -1), kv_norm=kv_norm.reshape(1, -1),
                w_uqt=w_uq_p.T.astype(BF16), w_uqst=w_uq_s.T.astype(BF16), w_uk=w_uk.astype(BF16),
                w_uvt=w_uv.T.astype(BF16), w_o=w_o.astype(BF16))


def _nsa_weights(w_in, pos_k, w1_k, w2_k, pos_v, w1_v, w2_v, w_o):
    H, G = NSA_HEADS, NSA_GROUPS
    gw = G * NSA_QK
    q0 = H * NSA_QK
    wq = w_in[:, :q0].reshape(D_MODEL, H, NSA_QK)
    wq = jnp.concatenate([wq, jnp.zeros_like(wq)], axis=-1).reshape(D_MODEL, H * LANE)
    cols = []
    for n in range(3):
        k = w_in[:, q0 + (2 * n) * gw: q0 + (2 * n + 1) * gw].reshape(D_MODEL, G, NSA_QK)
        v = w_in[:, q0 + (2 * n + 1) * gw: q0 + (2 * n + 2) * gw].reshape(D_MODEL, G, NSA_V)
        cols.append(jnp.concatenate([k, v], axis=-1).reshape(D_MODEL, G * LANE))
    w_kv = jnp.concatenate(cols, axis=1)
    wg = w_in[:, q0 + 6 * gw:].reshape(D_MODEL, G, NSA_R, 3).transpose(0, 1, 3, 2).reshape(D_MODEL, 3 * H)
    wg = jnp.concatenate([wg, jnp.zeros((D_MODEL, LANE - 3 * H), F32)], axis=1)
    eye = jnp.eye(2, dtype=F32)
    cw = CMP_STRIDE * 2 * NSA_QK
    pos = jnp.stack([pos_k, pos_v]).reshape(2, 2, CMP_STRIDE, NSA_QK)
    pos = pos.transpose(1, 2, 0, 3).reshape(2, 1, cw)
    w1 = jnp.stack([w1_k, w1_v]).reshape(2, 2, CMP_STRIDE, NSA_QK, CMP_HIDDEN)
    w1 = jnp.einsum("khldj,kq->hlkdqj", w1, eye).reshape(2, cw, 2 * CMP_HIDDEN).astype(BF16)
    w2 = jnp.einsum("kjd,kq->kjqd", jnp.stack([w2_k, w2_v]), eye).reshape(2 * CMP_HIDDEN, 2 * NSA_QK).astype(BF16)
    return dict(w_qt=wq.T.astype(BF16), w_kv=w_kv.astype(BF16), w_kvt=w_kv[:, NSA_KVW:].T.astype(BF16),
                w_gt=wg.T.astype(BF16), pos=pos, w1=w1, w2=w2, w_o=w_o.astype(BF16))


def kernel(x, ffn_norm_a, ffn_a_w_gate, ffn_a_w_up, ffn_a_w_down, mix_norm, ffn_norm_b, ffn_b_w_gate, ffn_b_w_up, ffn_b_w_down, final_norm, rel_bias, mla_w_in, mla_q_norm, mla_kv_norm, mla_w_uq, mla_w_ukv, mla_w_o, nsa_w_in, nsa_cmp_pos_k, nsa_cmp_w1_k, nsa_cmp_w2_k, nsa_cmp_pos_v, nsa_cmp_w1_v, nsa_cmp_w2_v, nsa_w_o):
    B, S, D = x.shape
    assert D == D_MODEL and S % ATT_T == 0 and S % NSA_T == 0 and (B * S) % FFN_TM == 0
    T = B * S
    G = NSA_GROUPS
    n_chunk = S // CMP_STRIDE
    cos128, sin128 = _rope_tables(S)
    dtab, cbias = _bias_tables(rel_bias, S)
    ovt, emat_t = _selection_tables(S)
    bf = lambda w: w.astype(BF16)

    h = x.reshape(T, D)
    for i in range(DEPTH):
        h = _ffn(h, ffn_norm_a[i], bf(ffn_a_w_gate[i]), bf(ffn_a_w_up[i]), bf(ffn_a_w_down[i]))
        j = i // N_MIXERS
        if i % N_MIXERS == 0:
            w = _mla_weights(mla_w_in[j], mla_q_norm[j], mla_kv_norm[j], mla_w_uq[j], mla_w_ukv[j], mla_w_o[j])
            q, k, v = _mla_proj(h, mix_norm[i], w, cos128, sin128, S)
            o = _mla_attn(q, k, v, B, S)
        else:
            w = _nsa_weights(nsa_w_in[j], nsa_cmp_pos_k[j], nsa_cmp_w1_k[j], nsa_cmp_w2_k[j],
                             nsa_cmp_pos_v[j], nsa_cmp_w1_v[j], nsa_cmp_w2_v[j], nsa_w_o[j])
            qt, kvc_in, kvs, kvw, kvst, kvwt, gates = _nsa_proj(h, mix_norm[i], w)
            xc = kvc_in.reshape(B, n_chunk, CMP_STRIDE, G, LANE).transpose(0, 3, 1, 2, 4)
            xc = xc.reshape(B, G, n_chunk, CMP_STRIDE * LANE)
            kvc, kvct = _compress(xc, w["pos"], w["w1"], w["w2"])
            o = _nsa_attn(qt, kvc, kvct, kvs, kvst, kvw, kvwt, gates, cbias, dtab, emat_t, ovt, B, S)
        h = _ffn(h, ffn_norm_b[i], bf(ffn_b_w_gate[i]), bf(ffn_b_w_up[i]), bf(ffn_b_w_down[i]),
                 proj=(o, w["w_o"]), final_g=final_norm if i == DEPTH - 1 else None)
    return h.reshape(B, S, D)
```

```python
import functools
import math

import numpy as np
import jax
import jax.numpy as jnp
from jax import lax
from jax.experimental import pallas as pl
from jax.experimental.pallas import tpu as pltpu

F32 = jnp.float32
BF16 = jnp.bfloat16

D_MODEL = 1024
DEPTH = 4
N_MIXERS = 2
RMS_EPS = 1e-6
FFN_HIDDEN = 2816
NEG = -1e30
MLA_HEADS = 16
MLA_Q_LORA = 384
MLA_KV_LORA = 256
MLA_NOPE = 64
MLA_ROPE = 32
MLA_V = 64
ROPE_THETA = 10000.0
NSA_HEADS = 16
NSA_GROUPS = 4
NSA_R = NSA_HEADS // NSA_GROUPS
NSA_QK = 64
NSA_V = 64
CMP_BLOCK = 32
CMP_STRIDE = 16
CMP_HIDDEN = 128
SEL_BLOCK = 64
SEL_SHIFT = SEL_BLOCK.bit_length() - 1
assert 1 << SEL_SHIFT == SEL_BLOCK
SEL_TOP_N = 16
WINDOW = 512
REL_BUCKETS = 32
REL_MAX_DIST = 128

LANE = 128
VMEM_LIMIT = 56 * 1024 * 1024

FFN_TM = 512
FFN_TF = 256
PROJ_TM = 256
ATT_T = 256
NSA_T = 128
CB_CENTER = 64


def _cparams(sem):
    return pltpu.CompilerParams(dimension_semantics=sem, vmem_limit_bytes=VMEM_LIMIT)


def _rms(x, g):
    ms = jnp.mean(x * x, axis=-1, keepdims=True)
    return x * lax.rsqrt(ms + RMS_EPS) * g


def _const_spec(shape):
    nd = len(shape)
    return pl.BlockSpec(shape, lambda *_: (0,) * nd)


def _wdot(a, w):
    return lax.dot_general(a, w, (((1,), (0,)), ((), ())), preferred_element_type=F32)


def _ffn_body(*refs, has_proj, has_final):
    it = iter(refs)
    h_ref = next(it)
    if has_proj:
        o_in_ref = next(it)
        wo_ref = next(it)
    g_ref = next(it)
    wg_ref = next(it)
    wu_ref = next(it)
    wd_ref = next(it)
    if has_final:
        gf_ref = next(it)
    out_ref = next(it)
    a_ref = next(it)

    x = h_ref[...]
    if has_proj:
        x = x + _wdot(o_in_ref[...], wo_ref[...])
    xn = _rms(x, g_ref[...]).astype(BF16)
    for c in range(FFN_HIDDEN // FFN_TF):
        sl = slice(c * FFN_TF, (c + 1) * FFN_TF)
        gt = _wdot(xn, wg_ref[:, sl])
        up = _wdot(xn, wu_ref[:, sl])
        a_ref[:, sl] = (gt * jax.nn.sigmoid(gt) * up).astype(BF16)
    y = x + 0.5 * _wdot(a_ref[...], wd_ref[...])
    if has_final:
        y = _rms(y, gf_ref[...])
    out_ref[...] = y


def _layer_spec(stacked, layer):
    return pl.BlockSpec((None,) + stacked.shape[1:], lambda i: (layer, 0, 0), pipeline_mode=pl.Buffered(1))


def _ffn(h, g, wg, wu, wd, layer, proj=None, final_g=None):
    T = h.shape[0]
    tm = FFN_TM
    row = lambda i: (i, 0)
    in_specs = [pl.BlockSpec((tm, D_MODEL), row)]
    args = [h]
    if proj is not None:
        o_in, wo, wo_idx = proj
        in_specs += [pl.BlockSpec((tm, o_in.shape[1]), row), _layer_spec(wo, wo_idx)]
        args += [o_in, wo]
    in_specs += [_const_spec((1, D_MODEL)), _layer_spec(wg, layer), _layer_spec(wu, layer), _layer_spec(wd, layer)]
    args += [g.reshape(1, D_MODEL), wg, wu, wd]
    if final_g is not None:
        in_specs.append(_const_spec((1, D_MODEL)))
        args.append(final_g.reshape(1, D_MODEL))
    return pl.pallas_call(
        functools.partial(_ffn_body, has_proj=proj is not None, has_final=final_g is not None),
        grid=(T // tm,),
        in_specs=in_specs,
        out_specs=pl.BlockSpec((tm, D_MODEL), row),
        out_shape=jax.ShapeDtypeStruct((T, D_MODEL), F32),
        scratch_shapes=[pltpu.VMEM((tm, FFN_HIDDEN), BF16)],
        compiler_params=_cparams(("parallel",)),
        name="ffn",
    )(*args)


MLA_CQ0, MLA_CKV0, MLA_KR0, MLA_KRS0, MLA_IN_W = 0, 384, 640, 768, 896
MLA_HCHUNK = 4


def _mla_proj_body(h_ref, g_ref, win_ref, qn_ref, kvn_ref, wuqt_ref, wuqst_ref, wuk_ref, wuvt_ref,
                   cos_ref, sin_ref, cost_ref, sint_ref, qt_out, k_out, vt_out):
    xn = _rms(h_ref[...], g_ref[...]).astype(BF16)
    proj = jnp.dot(xn, win_ref[...], preferred_element_type=F32)
    cq = _rms(proj[:, MLA_CQ0:MLA_CKV0], qn_ref[...]).astype(BF16)
    ckv = _rms(proj[:, MLA_CKV0:MLA_KR0], kvn_ref[...]).astype(BF16)
    cos = cos_ref[...]
    sin = sin_ref[...]
    kr = proj[:, MLA_KR0:MLA_KRS0] * cos + proj[:, MLA_KRS0:MLA_IN_W] * sin
    cos_t = cost_ref[...]
    sin_t = sint_ref[...]
    scale = (MLA_NOPE + MLA_ROPE) ** -0.5 * LOG2E
    vt_out[...] = lax.dot_general(wuvt_ref[...], ckv, _NT, preferred_element_type=F32).astype(BF16)
    cw = MLA_HCHUNK * LANE
    for c in range(MLA_HEADS // MLA_HCHUNK):
        sl = slice(c * cw, (c + 1) * cw)
        qt = lax.dot_general(wuqt_ref[sl, :], cq, _NT, preferred_element_type=F32)
        qst = lax.dot_general(wuqst_ref[sl, :], cq, _NT, preferred_element_type=F32)
        kn = jnp.dot(ckv, wuk_ref[:, sl], preferred_element_type=F32)
        for hh in range(MLA_HCHUNK):
            hs = slice(hh * LANE, (hh + 1) * LANE)
            os_ = slice(c * cw + hh * LANE, c * cw + (hh + 1) * LANE)
            qt_out[os_, :] = ((qt[hs] * cos_t + qst[hs] * sin_t) * scale).astype(BF16)
            k_out[:, os_] = (kn[:, hs] + kr).astype(BF16)


def _mla_proj(h, g, w, cos128, sin128, S):
    T = h.shape[0]
    tm = PROJ_TM
    ns = S // tm
    row = lambda i: (i, 0)
    col = lambda i: (0, i)
    pos = lambda i: (i % ns, 0)
    pos_t = lambda i: (0, i % ns)
    HL = MLA_HEADS * LANE
    HV = MLA_HEADS * MLA_V
    return pl.pallas_call(
        _mla_proj_body,
        grid=(T // tm,),
        in_specs=[pl.BlockSpec((tm, D_MODEL), row), _const_spec((1, D_MODEL)),
                  _const_spec(w["w_in"].shape), _const_spec((1, MLA_Q_LORA)), _const_spec((1, MLA_KV_LORA)),
                  _const_spec(w["w_uqt"].shape), _const_spec(w["w_uqst"].shape),
                  _const_spec(w["w_uk"].shape), _const_spec(w["w_uvt"].shape),
                  pl.BlockSpec((tm, LANE), pos), pl.BlockSpec((tm, LANE), pos),
                  pl.BlockSpec((LANE, tm), pos_t), pl.BlockSpec((LANE, tm), pos_t)],
        out_specs=[pl.BlockSpec((HL, tm), col), pl.BlockSpec((tm, HL), row), pl.BlockSpec((HV, tm), col)],
        out_shape=[jax.ShapeDtypeStruct((HL, T), BF16), jax.ShapeDtypeStruct((T, HL), BF16),
                   jax.ShapeDtypeStruct((HV, T), BF16)],
        compiler_params=_cparams(("parallel",)),
        name="mla_proj",
    )(h, g.reshape(1, D_MODEL), w["w_in"], w["q_norm"], w["kv_norm"], w["w_uqt"], w["w_uqst"],
      w["w_uk"], w["w_uvt"], cos128, sin128, cos128.T, sin128.T)


_NT = (((1,), (1,)), ((), ()))
SCORE_CHUNK = 2 * LANE


def _chunks(n, lead=0):
    out = [(0, lead)] if lead else []
    return out + [(o, min(SCORE_CHUNK, n - o)) for o in range(lead, n, SCORE_CHUNK)]


SUBLANE = 8


def _fold(x, op):
    w, cols = x.shape
    return op(x.reshape(w // SUBLANE, SUBLANE, cols), axis=0)


class _SoftmaxStreamT:
    def __init__(self, score_fn, chunks, values_fn, s_ref):
        self.score_fn, self.chunks, self.values_fn, self.s_ref = score_fn, chunks, values_fn, s_ref

    def pass1(self):
        mp = None
        for off, w in self.chunks:
            s = self.score_fn(off, w)
            self.s_ref[off:off + w, :] = s
            part = _fold(s, jnp.max)
            mp = part if mp is None else jnp.maximum(mp, part)
            yield
        self.m = jnp.max(mp, axis=0, keepdims=True)

    def pass2(self):
        lp = acc = None
        for off, w in self.chunks:
            p = jnp.exp2(self.s_ref[off:off + w, :] - self.m)
            part = _fold(p, jnp.sum)
            lp = part if lp is None else lp + part
            pv = jnp.dot(self.values_fn(off, w), p.astype(BF16), preferred_element_type=F32)
            acc = pv if acc is None else acc + pv
            yield
        self.out = acc / jnp.sum(lp, axis=0, keepdims=True)


def _trace_pipelined(streams, side=()):
    step = lambda gens: [g for g in gens if next(g, StopIteration) is not StopIteration]
    side = list(side)
    for _ in streams[0].pass1():
        side = step(side)
    for i, st in enumerate(streams):
        gens = [st.pass2()] + ([streams[i + 1].pass1()] if i + 1 < len(streams) else [])
        while gens:
            gens = step(gens)
            side = step(side)
    while side:
        side = step(side)


MLA_HPS = 4


def _mla_attn_body(qt_ref, k_ref, vt_ref, o_ref, *s_refs, nq):
    t = ATT_T
    qi = pl.program_id(2)

    def branch(nt):
        n = nt * t
        causal = lax.broadcasted_iota(jnp.int32, (t, t), 0) <= lax.broadcasted_iota(jnp.int32, (t, t), 1)
        streams = []
        for hh in range(MLA_HPS):
            hs = slice(hh * LANE, (hh + 1) * LANE)
            qt = qt_ref[hs, :]
            vs = slice(hh // 2 * LANE, (hh // 2 + 1) * LANE)

            def score(off, w, qt=qt, hs=hs):
                s = jnp.dot(k_ref[off:off + w, hs], qt, preferred_element_type=F32)
                return jnp.where(causal, s, NEG) if off == n - t else s

            values = lambda off, w, vs=vs: vt_ref[vs, off:off + w]
            streams.append(_SoftmaxStreamT(score, _chunks(n), values, s_refs[hh]))
        _trace_pipelined(streams)
        outs = [st.out[hh % 2 * MLA_V:(hh % 2 + 1) * MLA_V] for hh, st in enumerate(streams)]
        o_ref[...] = jnp.concatenate(outs, axis=0).T.astype(BF16)

    for nt in range(1, nq + 1):
        pl.when(qi == nt - 1)(functools.partial(branch, nt))


def _mla_attn(qt, k, vt, B, S):
    t = ATT_T
    nq = S // t
    T = B * S
    hps = MLA_HPS
    return pl.pallas_call(
        functools.partial(_mla_attn_body, nq=nq),
        grid=(B, MLA_HEADS // hps, nq),
        in_specs=[pl.BlockSpec((hps * LANE, t), lambda b, p, i: (p, b * nq + i)),
                  pl.BlockSpec((S, hps * LANE), lambda b, p, i: (b, p)),
                  pl.BlockSpec((hps * MLA_V, S), lambda b, p, i: (p, b))],
        out_specs=pl.BlockSpec((t, hps * MLA_V), lambda b, p, i: (b * nq + i, p)),
        out_shape=jax.ShapeDtypeStruct((T, MLA_HEADS * MLA_V), BF16),
        scratch_shapes=[pltpu.VMEM((S, t), F32)] * hps,
        compiler_params=_cparams(("parallel", "parallel", "arbitrary")),
        name="mla_attn",
    )(qt, k, vt)


NSA_QW = NSA_HEADS * LANE
NSA_KVW = NSA_GROUPS * LANE
NSA_IN_W = NSA_QW + 3 * NSA_KVW + LANE
GATES_PER_GROUP = 3 * NSA_R
RANK_STEPS_PER_STAGE = 8
LOG2E = math.log2(math.e)


def _nsa_proj_body(h_ref, g_ref, wqt_ref, wkv_ref, wkvt_ref, wgt_ref,
                   qt_out, kvc_out, kvs_out, kvw_out, kvst_out, kvwt_out, gate_out):
    xn = _rms(h_ref[...], g_ref[...]).astype(BF16)
    scale = NSA_QK ** -0.5 * LOG2E
    cw = 4 * LANE
    for c in range(NSA_QW // cw):
        sl = slice(c * cw, (c + 1) * cw)
        qt = lax.dot_general(wqt_ref[sl, :], xn, _NT, preferred_element_type=F32)
        qt_out[sl, :] = (qt * scale).astype(BF16)
    for n, out in enumerate((kvc_out, kvs_out, kvw_out)):
        sl = slice(n * NSA_KVW, (n + 1) * NSA_KVW)
        out[...] = jnp.dot(xn, wkv_ref[:, sl], preferred_element_type=F32).astype(BF16)
    for n, out in enumerate((kvst_out, kvwt_out)):
        sl = slice(n * NSA_KVW, (n + 1) * NSA_KVW)
        out[...] = lax.dot_general(wkvt_ref[sl, :], xn, _NT, preferred_element_type=F32).astype(BF16)
    gl = lax.dot_general(wgt_ref[...], xn, _NT, preferred_element_type=F32)
    gate_out[...] = jax.nn.sigmoid(gl)


def _nsa_proj(h, g, w):
    T = h.shape[0]
    tm = PROJ_TM
    row = lambda i: (i, 0)
    col = lambda i: (0, i)
    return pl.pallas_call(
        _nsa_proj_body,
        grid=(T // tm,),
        in_specs=[pl.BlockSpec((tm, D_MODEL), row), _const_spec((1, D_MODEL)), _const_spec(w["w_qt"].shape),
                  _const_spec(w["w_kv"].shape), _const_spec(w["w_kvt"].shape), _const_spec(w["w_gt"].shape)],
        out_specs=[pl.BlockSpec((NSA_QW, tm), col), pl.BlockSpec((tm, NSA_KVW), row),
                   pl.BlockSpec((tm, NSA_KVW), row), pl.BlockSpec((tm, NSA_KVW), row),
                   pl.BlockSpec((NSA_KVW, tm), col), pl.BlockSpec((NSA_KVW, tm), col),
                   pl.BlockSpec((LANE, tm), col)],
        out_shape=[jax.ShapeDtypeStruct((NSA_QW, T), BF16), jax.ShapeDtypeStruct((T, NSA_KVW), BF16),
                   jax.ShapeDtypeStruct((T, NSA_KVW), BF16), jax.ShapeDtypeStruct((T, NSA_KVW), BF16),
                   jax.ShapeDtypeStruct((NSA_KVW, T), BF16), jax.ShapeDtypeStruct((NSA_KVW, T), BF16),
                   jax.ShapeDtypeStruct((LANE, T), F32)],
        compiler_params=_cparams(("parallel",)),
        name="nsa_proj",
    )(h, g.reshape(1, D_MODEL), w["w_qt"], w["w_kv"], w["w_kvt"], w["w_gt"])


def _compress_body(x_ref, pos_ref, w1_ref, w2_ref, out_ref, out_t_ref):
    n_chunk = x_ref.shape[0]
    x = x_ref[...].astype(F32)
    xa = (x + pos_ref[0]).astype(BF16)
    xb = (x + pos_ref[1]).astype(BF16)
    a = jnp.dot(xa, w1_ref[0], preferred_element_type=F32)
    b = jnp.dot(xb, w1_ref[1], preferred_element_type=F32)
    pre = a + pltpu.roll(b, n_chunk - 1, 0)
    hid = jax.nn.gelu(pre, approximate=True).astype(BF16)
    kv = jnp.dot(hid, w2_ref[...], preferred_element_type=F32)
    out_ref[...] = kv.astype(BF16)
    out_t_ref[...] = kv.T.astype(BF16)


def _compress(x, pos, w1, w2):
    B, G, n_chunk, width = x.shape
    return pl.pallas_call(
        _compress_body,
        grid=(B, G),
        in_specs=[pl.BlockSpec((None, None, n_chunk, width), lambda b, g: (b, g, 0, 0)),
                  _const_spec(pos.shape), _const_spec(w1.shape), _const_spec(w2.shape)],
        out_specs=[pl.BlockSpec((None, None, n_chunk, LANE), lambda b, g: (b, g, 0, 0)),
                   pl.BlockSpec((None, None, LANE, n_chunk), lambda b, g: (b, g, 0, 0))],
        out_shape=[jax.ShapeDtypeStruct((B, G, n_chunk, LANE), BF16),
                   jax.ShapeDtypeStruct((B, G, LANE, n_chunk), BF16)],
        compiler_params=_cparams(("parallel", "parallel")),
        name="nsa_compress",
    )(x, pos, w1, w2)


def _nsa_attn_body(*refs, S):
    step = pl.program_id(2)
    for qi in range(S // NSA_T):
        pl.when(step == qi)(functools.partial(_nsa_tile, qi, *refs, S=S))


def _nsa_tile(qi, qt_ref, qtn_ref, kvc_ref, kvct_ref, kvs_ref, kvst_ref, kvw_ref, kvwt_ref, gate_ref, cbt_ref,
              dt_ref, et_ref, ovt_ref, o_ref, *scratch, S):
    t = NSA_T
    R = NSA_R
    n_sel = S // SEL_BLOCK
    t0 = qi * t
    s_sel, s_win = scratch[0:2], scratch[2:4]
    ocmp_scr, sel_scr = scratch[4:6]
    heads_on_lanes = lambda ref: jnp.concatenate([ref[r * LANE:(r + 1) * LANE, :] for r in range(R)], axis=1)
    qt = heads_on_lanes(qt_ref)
    tile_r = lambda x: jnp.concatenate([x] * R, axis=1)
    half_w = 2 * t
    tile_h = lambda x: jnp.concatenate([x] * (half_w // t), axis=1)

    def compressed_and_selection(q_tile, tile, out):
        kvc = kvc_ref[...]
        n_cp = kvc.shape[0]
        sc = jnp.dot(kvc, q_tile, preferred_element_type=F32)
        yield
        cend = lax.broadcasted_iota(jnp.int32, (n_cp, t), 0) * CMP_STRIDE + (CMP_BLOCK - 1)
        valid = tile_r(tile * t + lax.broadcasted_iota(jnp.int32, (n_cp, t), 1) >= cend)
        shift = (tile * (t // CMP_STRIDE) + CB_CENTER) % n_cp
        cb = pltpu.roll(cbt_ref[...], shift, 0) if shift else cbt_ref[...]
        sc = jnp.where(valid, sc + cb, NEG)
        e = jnp.exp2(sc - jnp.max(sc, axis=0, keepdims=True))
        yield
        p = jnp.where(valid, e / jnp.sum(e, axis=0, keepdims=True), 0.0)
        out["o_cmp"] = jnp.dot(kvct_ref[...], p.astype(BF16), preferred_element_type=F32)
        yield

        psum = p[:, 0:t]
        for r in range(1, R):
            psum = psum + p[:, r * t:(r + 1) * t]
        imp = jnp.dot(ovt_ref[...], psum, preferred_element_type=F32, precision=lax.Precision.HIGHEST)
        yield
        jj = lax.broadcasted_iota(jnp.int32, (n_sel, t), 0)
        blk_t = (tile * t + lax.broadcasted_iota(jnp.int32, (n_sel, t), 1)) >> SEL_SHIFT
        forced = (jj == 0) | (jj == blk_t) | (jj == blk_t - 1)
        score = jnp.where(forced, 1e6, jnp.where(jj <= blk_t, imp, -1e6))
        cnt = jnp.zeros((n_sel, t), jnp.int32)
        for jp in range(n_sel):
            row = score[jp:jp + 1, :]
            beats = (row > score) | ((row == score) & (jj > jp))
            cnt = cnt + beats.astype(jnp.int32)
            if jp % RANK_STEPS_PER_STAGE == RANK_STEPS_PER_STAGE - 1:
                yield
        chosen = jnp.where(cnt < min(SEL_TOP_N, n_sel), 1.0, 0.0)
        chosen = jnp.concatenate([chosen, jnp.zeros((et_ref.shape[1] - n_sel, t), F32)], axis=0)
        out["sel"] = chosen.astype(BF16)

    if qi == 0:
        cur = {}
        for _ in compressed_and_selection(qt, 0, cur):
            pass
        o_cmp, sel = cur["o_cmp"], cur["sel"]
    else:
        o_cmp, sel = ocmp_scr[...], sel_scr[...]
    nxt, side = {}, []
    if (qi + 1) * t < S:
        side = [compressed_and_selection(heads_on_lanes(qtn_ref), qi + 1, nxt)]

    kw = min(qi, WINDOW // t)
    nw = (kw + 1) * t
    w0 = (qi - kw) * t
    n = (qi + 1) * t

    def win_add(off, w):
        lo_d, hi_d = kw * t - off - (w - 1), kw * t - off + (t - 1)
        if lo_d >= 0 and hi_d < WINDOW:
            return None
        dist = (kw * t - off + lax.broadcasted_iota(jnp.int32, (w, t), 1)
                - lax.broadcasted_iota(jnp.int32, (w, t), 0))
        return jnp.where((dist >= 0) & (dist < WINDOW), 0.0, NEG)

    def sel_add(off, w):
        ok = jnp.dot(et_ref[off:off + w, :], sel, preferred_element_type=F32) > 0.5
        if off + w > n - t:
            kpos = off + lax.broadcasted_iota(jnp.int32, (w, t), 0)
            ok = ok & (kpos <= t0 + lax.broadcasted_iota(jnp.int32, (w, t), 1))
        return jnp.where(ok, 0.0, NEG)

    def stream(half, k_ref, vt_ref, k0, nk, add_fn, s_ref):
        cs = slice(half * half_w, (half + 1) * half_w)
        qh = qt[:, cs]

        def score(off, w):
            s = jnp.dot(k_ref[k0 + off:k0 + off + w, :], qh, preferred_element_type=F32)
            add = add_fn(off, w)
            if add is not None:
                s = s + tile_h(add)
            if off + w == nk:
                s = s + dt_ref[2 * t - w:, cs]
            return s

        values = lambda off, w: vt_ref[:, k0 + off:k0 + off + w]
        return _SoftmaxStreamT(score, _chunks(nk, lead=nk % SCORE_CHUNK), values, s_ref)

    halves = range(R * t // half_w)
    wins = [stream(h, kvw_ref, kvwt_ref, w0, nw, win_add, s_win[h]) for h in halves]
    sels = [stream(h, kvs_ref, kvst_ref, 0, n, sel_add, s_sel[h]) for h in halves]
    _trace_pipelined(wins[:1] + sels + wins[1:], side)

    g = pl.program_id(0)
    heads = []
    for r in range(R):
        h, hc = divmod(r * t, half_w)
        branches = (o_cmp[NSA_QK:, r * t:(r + 1) * t], sels[h].out[NSA_QK:, hc:hc + t], wins[h].out[NSA_QK:, hc:hc + t])
        o = None
        for br, ob in enumerate(branches):
            gate = gate_ref[pl.ds(g * GATES_PER_GROUP + br * R + r, 1), :]
            o = gate * ob if o is None else o + gate * ob
        heads.append(o)
    o_ref[...] = jnp.concatenate(heads, axis=0).T.astype(BF16)
    if nxt:
        ocmp_scr[...] = nxt["o_cmp"]
        sel_scr[...] = nxt["sel"]


def _nsa_attn(qt, kvc, kvct, kvs, kvst, kvw, kvwt, gates, cbt, dtt, emat_t, ovt, B, S):
    t = NSA_T
    nq = S // t
    T = B * S
    G, R = NSA_GROUPS, NSA_R
    n_cp = kvc.shape[2]
    half_w = 2 * t
    nh = R * t // half_w
    kv_spec = pl.BlockSpec((S, LANE), lambda g, b, i: (b, g))
    kvt_spec = pl.BlockSpec((LANE, S), lambda g, b, i: (g, b))
    return pl.pallas_call(
        functools.partial(_nsa_attn_body, S=S),
        grid=(G, B, nq),
        in_specs=[pl.BlockSpec((R * LANE, t), lambda g, b, i: (g, b * nq + i)),
                  pl.BlockSpec((R * LANE, t), lambda g, b, i: (g, b * nq + jnp.minimum(i + 1, nq - 1))),
                  pl.BlockSpec((None, None, n_cp, LANE), lambda g, b, i: (b, g, 0, 0)),
                  pl.BlockSpec((None, None, LANE, n_cp), lambda g, b, i: (b, g, 0, 0)),
                  kv_spec, kvt_spec, kv_spec, kvt_spec,
                  pl.BlockSpec((LANE, t), lambda g, b, i: (0, b * nq + i)),
                  pl.BlockSpec((None, n_cp, R * t), lambda g, b, i: (g, 0, 0)),
                  pl.BlockSpec((None, 2 * t, R * t), lambda g, b, i: (g, 0, 0)),
                  _const_spec(emat_t.shape), _const_spec(ovt.shape)],
        out_specs=pl.BlockSpec((t, R * NSA_V), lambda g, b, i: (b * nq + i, g)),
        out_shape=jax.ShapeDtypeStruct((T, NSA_HEADS * NSA_V), BF16),
        scratch_shapes=([pltpu.VMEM((S, half_w), F32)] * nh + [pltpu.VMEM((WINDOW + t, half_w), F32)] * nh
                        + [pltpu.VMEM((LANE, R * t), F32), pltpu.VMEM((emat_t.shape[1], t), BF16)]),
        compiler_params=_cparams(("arbitrary", "arbitrary", "arbitrary")),
        name="nsa_attn",
    )(qt, qt, kvc, kvct, kvs, kvst, kvw, kvwt, gates, cbt, dtt, emat_t, ovt)


def _t5_bucket(dist):
    n = jnp.maximum(dist, 0)
    max_exact = REL_BUCKETS // 2
    nf = jnp.maximum(n, 1).astype(F32)
    large = max_exact + (jnp.log(nf / max_exact) / math.log(REL_MAX_DIST / max_exact)
                         * (REL_BUCKETS - max_exact)).astype(jnp.int32)
    large = jnp.minimum(large, REL_BUCKETS - 1)
    return jnp.where(n < max_exact, n, large)


def _np_bucket(n):
    n = np.maximum(np.asarray(n), 0)
    max_exact = REL_BUCKETS // 2
    large = max_exact + (np.log(np.maximum(n, 1) / max_exact) / math.log(REL_MAX_DIST / max_exact)
                         * (REL_BUCKETS - max_exact)).astype(np.int64)
    return np.where(n < max_exact, n, np.minimum(large, REL_BUCKETS - 1))


def _bias_tables(rel_bias, S):
    t = NSA_T
    G, R = NSA_GROUPS, NSA_R
    n_cp = S // CMP_STRIDE
    a = np.arange(t)[:, None]
    dist_d = a - np.arange(2 * t)[None, :] + t
    dist_c = a - CMP_STRIDE * (np.arange(n_cp)[None, :] - CB_CENTER) - (CMP_BLOCK - 1)
    uncovered = min(t + 1, CMP_STRIDE * (CB_CENTER + 1) - (CMP_BLOCK - 1))
    assert (_np_bucket(np.arange(uncovered, 2 * S)) == REL_BUCKETS - 1).all()
    wrap_from = n_cp - max((S // t - 1) * (t // CMP_STRIDE) - CB_CENTER, 0)
    assert (dist_c[:, wrap_from:] < 0).all() and (dist_c[:, -1] < 0).all()

    def lookup(dist):
        oh = jax.nn.one_hot(_t5_bucket(jnp.asarray(dist)), REL_BUCKETS, dtype=F32)
        oh = oh - jax.nn.one_hot(REL_BUCKETS - 1, REL_BUCKETS, dtype=F32)
        val = jnp.einsum("acb,bh->hac", oh, rel_bias, precision=lax.Precision.HIGHEST)
        return val.reshape(G, R, *dist.shape)

    dtab = lookup(dist_d)
    cbias = jnp.where(jnp.asarray(dist_c >= 0), lookup(dist_c), 0.0)
    keys_first = lambda x: x.transpose(0, 3, 1, 2).reshape(G, x.shape[3], R * t)
    return keys_first(dtab) * LOG2E, keys_first(cbias) * LOG2E


def _selection_tables(S):
    n_cp = S // CMP_STRIDE
    n_cmp = (S - CMP_BLOCK) // CMP_STRIDE + 1
    n_sel = S // SEL_BLOCK
    cs = np.arange(n_cp) * CMP_STRIDE
    ce = cs + CMP_BLOCK
    ss = np.arange(n_sel) * SEL_BLOCK
    se = ss + SEL_BLOCK
    ov = np.minimum(ce[:, None], se[None, :]) - np.maximum(cs[:, None], ss[None, :])
    ov = (np.clip(ov, 0, None) / CMP_BLOCK).astype(np.float32)
    ov[n_cmp:] = 0.0
    emat_t = (np.arange(S)[:, None] // SEL_BLOCK == np.arange(max(n_sel, LANE))[None, :]).astype(np.float32)
    return jnp.asarray(ov.T), jnp.asarray(emat_t, dtype=BF16)


def _rope_tables(S):
    half = MLA_ROPE // 2
    inv = ROPE_THETA ** (-jnp.arange(half, dtype=F32) * 2.0 / MLA_ROPE)
    ang = jnp.arange(S, dtype=F32)[:, None] * inv[None, :]
    cos, sin = jnp.cos(ang), jnp.sin(ang)
    ones = jnp.ones((S, MLA_NOPE), F32)
    pad1 = jnp.ones((S, LANE - MLA_NOPE - MLA_ROPE), F32)
    cos128 = jnp.concatenate([ones, cos, cos, pad1], axis=1)
    sin128 = jnp.concatenate([0 * ones, sin, sin, 0 * pad1], axis=1)
    return cos128, sin128


def _mla_weights(w_in, q_norm, kv_norm, w_uq, w_ukv, w_o):
    H = MLA_HEADS
    half = MLA_ROPE // 2
    pad = LANE - MLA_NOPE - MLA_ROPE
    kr = w_in[:, MLA_Q_LORA + MLA_KV_LORA:]
    kr_sw = jnp.concatenate([-kr[:, half:], kr[:, :half]], axis=1)
    z = lambda n: jnp.zeros((D_MODEL, n), F32)
    w_in_ext = jnp.concatenate([w_in[:, :MLA_Q_LORA + MLA_KV_LORA],
                                z(MLA_NOPE), kr, z(pad), z(MLA_NOPE), kr_sw, z(pad)], axis=1)
    uq = w_uq.reshape(MLA_Q_LORA, H, MLA_NOPE + MLA_ROPE)
    qn, qr = uq[..., :MLA_NOPE], uq[..., MLA_NOPE:]
    qr_sw = jnp.concatenate([-qr[..., half:], qr[..., :half]], axis=-1)
    zq = jnp.zeros((MLA_Q_LORA, H, pad), F32)
    w_uq_p = jnp.concatenate([qn, qr, zq], axis=-1).reshape(MLA_Q_LORA, H * LANE)
    w_uq_s = jnp.concatenate([0 * qn, qr_sw, zq], axis=-1).reshape(MLA_Q_LORA, H * LANE)
    ukv = w_ukv.reshape(MLA_KV_LORA, H, MLA_NOPE + MLA_V)
    zk = jnp.zeros((MLA_KV_LORA, H, LANE - MLA_NOPE), F32)
    w_uk = jnp.concatenate([ukv[..., :MLA_NOPE], zk], axis=-1).reshape(MLA_KV_LORA, H * LANE)
    w_uv = ukv[..., MLA_NOPE:].reshape(MLA_KV_LORA, H * MLA_V)
    return dict(w_in=w_in_ext.astype(BF16), q_norm=q_norm.reshape(1, -1), kv_norm=kv_norm.reshape(1, -1),
                w_uqt=w_uq_p.T.astype(BF16), w_uqst=w_uq_s.T.astype(BF16), w_uk=w_uk.astype(BF16),
                w_uvt=w_uv.T.astype(BF16), w_o=w_o.astype(BF16))


def _nsa_weights(w_in, pos_k, w1_k, w2_k, pos_v, w1_v, w2_v, w_o):
    H, G = NSA_HEADS, NSA_GROUPS
    gw = G * NSA_QK
    q0 = H * NSA_QK
    wq = w_in[:, :q0].reshape(D_MODEL, H, NSA_QK)
    wq = jnp.concatenate([wq, jnp.zeros_like(wq)], axis=-1).reshape(D_MODEL, H * LANE)
    cols = []
    for n in range(3):
        k = w_in[:, q0 + (2 * n) * gw: q0 + (2 * n + 1) * gw].reshape(D_MODEL, G, NSA_QK)
        v = w_in[:, q0 + (2 * n + 1) * gw: q0 + (2 * n + 2) * gw].reshape(D_MODEL, G, NSA_V)
        cols.append(jnp.concatenate([k, v], axis=-1).reshape(D_MODEL, G * LANE))
    w_kv = jnp.concatenate(cols, axis=1)
    wg = w_in[:, q0 + 6 * gw:].reshape(D_MODEL, G, NSA_R, 3).transpose(0, 1, 3, 2).reshape(D_MODEL, 3 * H)
    wg = jnp.concatenate([wg, jnp.zeros((D_MODEL, LANE - 3 * H), F32)], axis=1)
    eye = jnp.eye(2, dtype=F32)
    cw = CMP_STRIDE * 2 * NSA_QK
    pos = jnp.stack([pos_k, pos_v]).reshape(2, 2, CMP_STRIDE, NSA_QK)
    pos = pos.transpose(1, 2, 0, 3).reshape(2, 1, cw)
    w1 = jnp.stack([w1_k, w1_v]).reshape(2, 2, CMP_STRIDE, NSA_QK, CMP_HIDDEN)
    w1 = jnp.einsum("khldj,kq->hlkdqj", w1, eye).reshape(2, cw, 2 * CMP_HIDDEN).astype(BF16)
    w2 = jnp.einsum("kjd,kq->kjqd", jnp.stack([w2_k, w2_v]), eye).reshape(2 * CMP_HIDDEN, 2 * NSA_QK).astype(BF16)
    return dict(w_qt=wq.T.astype(BF16), w_kv=w_kv.astype(BF16), w_kvt=w_kv[:, NSA_KVW:].T.astype(BF16),
                w_gt=wg.T.astype(BF16), pos=pos, w1=w1, w2=w2, w_o=w_o.astype(BF16))


def kernel(x, ffn_norm_a, ffn_a_w_gate, ffn_a_w_up, ffn_a_w_down, mix_norm, ffn_norm_b, ffn_b_w_gate, ffn_b_w_up, ffn_b_w_down, final_norm, rel_bias, mla_w_in, mla_q_norm, mla_kv_norm, mla_w_uq, mla_w_ukv, mla_w_o, nsa_w_in, nsa_cmp_pos_k, nsa_cmp_w1_k, nsa_cmp_w2_k, nsa_cmp_pos_v, nsa_cmp_w1_v, nsa_cmp_w2_v, nsa_w_o):
    B, S, D = x.shape
    assert D == D_MODEL and S % ATT_T == 0 and S % NSA_T == 0 and (B * S) % FFN_TM == 0
    T = B * S
    G = NSA_GROUPS
    n_chunk = S // CMP_STRIDE
    cos128, sin128 = _rope_tables(S)
    dtab, cbias = _bias_tables(rel_bias, S)
    ovt, emat_t = _selection_tables(S)

    h = x.reshape(T, D)
    for i in range(DEPTH):
        h = _ffn(h, ffn_norm_a[i], ffn_a_w_gate, ffn_a_w_up, ffn_a_w_down, i)
        j = i // N_MIXERS
        if i % N_MIXERS == 0:
            w = _mla_weights(mla_w_in[j], mla_q_norm[j], mla_kv_norm[j], mla_w_uq[j], mla_w_ukv[j], mla_w_o[j])
            q, k, v = _mla_proj(h, mix_norm[i], w, cos128, sin128, S)
            o = _mla_attn(q, k, v, B, S)
        else:
            w = _nsa_weights(nsa_w_in[j], nsa_cmp_pos_k[j], nsa_cmp_w1_k[j], nsa_cmp_w2_k[j],
                             nsa_cmp_pos_v[j], nsa_cmp_w1_v[j], nsa_cmp_w2_v[j], nsa_w_o[j])
            qt, kvc_in, kvs, kvw, kvst, kvwt, gates = _nsa_proj(h, mix_norm[i], w)
            xc = kvc_in.reshape(B, n_chunk, CMP_STRIDE, G, LANE).transpose(0, 3, 1, 2, 4)
            xc = xc.reshape(B, G, n_chunk, CMP_STRIDE * LANE)
            kvc, kvct = _compress(xc, w["pos"], w["w1"], w["w2"])
            o = _nsa_attn(qt, kvc, kvct, kvs, kvst, kvw, kvwt, gates, cbias, dtab, emat_t, ovt, B, S)
        w_o = mla_w_o if i % N_MIXERS == 0 else nsa_w_o
        h = _ffn(h, ffn_norm_b[i], ffn_b_w_gate, ffn_b_w_up, ffn_b_w_down, i,
                 proj=(o, w_o, j), final_g=final_norm if i == DEPTH - 1 else None)
    return h.reshape(B, S, D)
```

```python
import functools
import math

import numpy as np
import jax
import jax.numpy as jnp
from jax import lax
from jax.experimental import pallas as pl
from jax.experimental.pallas import tpu as pltpu

F32 = jnp.float32
BF16 = jnp.bfloat16

D_MODEL = 1024
DEPTH = 4
N_MIXERS = 2
RMS_EPS = 1e-6
FFN_HIDDEN = 2816
NEG = -1e30
MLA_HEADS = 16
MLA_Q_LORA = 384
MLA_KV_LORA = 256
MLA_NOPE = 64
MLA_ROPE = 32
MLA_V = 64
ROPE_THETA = 10000.0
NSA_HEADS = 16
NSA_GROUPS = 4
NSA_R = NSA_HEADS // NSA_GROUPS
NSA_QK = 64
NSA_V = 64
CMP_BLOCK = 32
CMP_STRIDE = 16
CMP_HIDDEN = 128
SEL_BLOCK = 64
SEL_SHIFT = SEL_BLOCK.bit_length() - 1
assert 1 << SEL_SHIFT == SEL_BLOCK
SEL_TOP_N = 16
WINDOW = 512
REL_BUCKETS = 32
REL_MAX_DIST = 128

LANE = 128
VMEM_LIMIT = 56 * 1024 * 1024

FFN_TM = 512
FFN_TF = 256
PROJ_TM = 256
ATT_T = 256
NSA_T = 128
CB_CENTER = 64


def _cparams(sem):
    return pltpu.CompilerParams(dimension_semantics=sem, vmem_limit_bytes=VMEM_LIMIT)


def _rms(x, g):
    ms = jnp.mean(x * x, axis=-1, keepdims=True)
    return x * lax.rsqrt(ms + RMS_EPS) * g


def _const_spec(shape):
    nd = len(shape)
    return pl.BlockSpec(shape, lambda *_: (0,) * nd)


def _wdot(a, w):
    return lax.dot_general(a, w, (((1,), (0,)), ((), ())), preferred_element_type=F32)


def _ffn_body(*refs, has_proj, has_final):
    it = iter(refs)
    h_ref = next(it)
    if has_proj:
        o_in_ref = next(it)
        wo_ref = next(it)
    g_ref = next(it)
    wg_ref = next(it)
    wu_ref = next(it)
    wd_ref = next(it)
    if has_final:
        gf_ref = next(it)
    out_ref = next(it)
    a_ref = next(it)

    x = h_ref[...]
    if has_proj:
        x = x + _wdot(o_in_ref[...], wo_ref[...])
    xn = _rms(x, g_ref[...]).astype(BF16)
    for c in range(FFN_HIDDEN // FFN_TF):
        sl = slice(c * FFN_TF, (c + 1) * FFN_TF)
        gt = _wdot(xn, wg_ref[:, sl])
        up = _wdot(xn, wu_ref[:, sl])
        a_ref[:, sl] = (gt * jax.nn.sigmoid(gt) * up).astype(BF16)
    y = x + 0.5 * _wdot(a_ref[...], wd_ref[...])
    if has_final:
        y = _rms(y, gf_ref[...])
    out_ref[...] = y


def _layer_spec(stacked, layer):
    return pl.BlockSpec((None,) + stacked.shape[1:], lambda i: (layer, 0, 0), pipeline_mode=pl.Buffered(1))


def _ffn(h, g, wg, wu, wd, layer, proj=None, final_g=None):
    T = h.shape[0]
    tm = FFN_TM
    row = lambda i: (i, 0)
    in_specs = [pl.BlockSpec((tm, D_MODEL), row)]
    args = [h]
    if proj is not None:
        o_in, wo, wo_idx = proj
        in_specs += [pl.BlockSpec((tm, o_in.shape[1]), row), _layer_spec(wo, wo_idx)]
        args += [o_in, wo]
    in_specs += [_const_spec((1, D_MODEL)), _layer_spec(wg, layer), _layer_spec(wu, layer), _layer_spec(wd, layer)]
    args += [g.reshape(1, D_MODEL), wg, wu, wd]
    if final_g is not None:
        in_specs.append(_const_spec((1, D_MODEL)))
        args.append(final_g.reshape(1, D_MODEL))
    return pl.pallas_call(
        functools.partial(_ffn_body, has_proj=proj is not None, has_final=final_g is not None),
        grid=(T // tm,),
        in_specs=in_specs,
        out_specs=pl.BlockSpec((tm, D_MODEL), row),
        out_shape=jax.ShapeDtypeStruct((T, D_MODEL), F32),
        scratch_shapes=[pltpu.VMEM((tm, FFN_HIDDEN), BF16)],
        compiler_params=_cparams(("parallel",)),
        name="ffn",
    )(*args)


MLA_CQ0, MLA_CKV0, MLA_KR0, MLA_KRS0, MLA_IN_W = 0, 384, 640, 768, 896
MLA_HCHUNK = 4


def _with_ones_rows(vt):
    row = lax.broadcasted_iota(jnp.int32, vt.shape, 0)
    return jnp.where(row % LANE == 0, 1.0, vt)


def _mla_proj_body(h_ref, g_ref, win_ref, qn_ref, kvn_ref, wuqt_ref, wuqst_ref, wuk_ref, wuvt_ref,
                   cos_ref, sin_ref, cost_ref, sint_ref, qt_out, k_out, vt_out):
    xn = _rms(h_ref[...], g_ref[...]).astype(BF16)
    proj = jnp.dot(xn, win_ref[...], preferred_element_type=F32)
    cq = _rms(proj[:, MLA_CQ0:MLA_CKV0], qn_ref[...]).astype(BF16)
    ckv = _rms(proj[:, MLA_CKV0:MLA_KR0], kvn_ref[...]).astype(BF16)
    cos = cos_ref[...]
    sin = sin_ref[...]
    kr = proj[:, MLA_KR0:MLA_KRS0] * cos + proj[:, MLA_KRS0:MLA_IN_W] * sin
    cos_t = cost_ref[...]
    sin_t = sint_ref[...]
    scale = (MLA_NOPE + MLA_ROPE) ** -0.5 * LOG2E
    vt = lax.dot_general(wuvt_ref[...], ckv, _NT, preferred_element_type=F32)
    vt_out[...] = _with_ones_rows(vt).astype(BF16)
    cw = MLA_HCHUNK * LANE
    for c in range(MLA_HEADS // MLA_HCHUNK):
        sl = slice(c * cw, (c + 1) * cw)
        qt = lax.dot_general(wuqt_ref[sl, :], cq, _NT, preferred_element_type=F32)
        qst = lax.dot_general(wuqst_ref[sl, :], cq, _NT, preferred_element_type=F32)
        kn = jnp.dot(ckv, wuk_ref[:, sl], preferred_element_type=F32)
        for hh in range(MLA_HCHUNK):
            hs = slice(hh * LANE, (hh + 1) * LANE)
            os_ = slice(c * cw + hh * LANE, c * cw + (hh + 1) * LANE)
            qt_out[os_, :] = ((qt[hs] * cos_t + qst[hs] * sin_t) * scale).astype(BF16)
            k_out[:, os_] = (kn[:, hs] + kr).astype(BF16)


def _mla_proj(h, g, w, cos128, sin128, S):
    T = h.shape[0]
    tm = PROJ_TM
    ns = S // tm
    row = lambda i: (i, 0)
    col = lambda i: (0, i)
    pos = lambda i: (i % ns, 0)
    pos_t = lambda i: (0, i % ns)
    HL = MLA_HEADS * LANE
    HV = MLA_HEADS * LANE
    return pl.pallas_call(
        _mla_proj_body,
        grid=(T // tm,),
        in_specs=[pl.BlockSpec((tm, D_MODEL), row), _const_spec((1, D_MODEL)),
                  _const_spec(w["w_in"].shape), _const_spec((1, MLA_Q_LORA)), _const_spec((1, MLA_KV_LORA)),
                  _const_spec(w["w_uqt"].shape), _const_spec(w["w_uqst"].shape),
                  _const_spec(w["w_uk"].shape), _const_spec(w["w_uvt"].shape),
                  pl.BlockSpec((tm, LANE), pos), pl.BlockSpec((tm, LANE), pos),
                  pl.BlockSpec((LANE, tm), pos_t), pl.BlockSpec((LANE, tm), pos_t)],
        out_specs=[pl.BlockSpec((HL, tm), col), pl.BlockSpec((tm, HL), row), pl.BlockSpec((HV, tm), col)],
        out_shape=[jax.ShapeDtypeStruct((HL, T), BF16), jax.ShapeDtypeStruct((T, HL), BF16),
                   jax.ShapeDtypeStruct((HV, T), BF16)],
        compiler_params=_cparams(("parallel",)),
        name="mla_proj",
    )(h, g.reshape(1, D_MODEL), w["w_in"], w["q_norm"], w["kv_norm"], w["w_uqt"], w["w_uqst"],
      w["w_uk"], w["w_uvt"], cos128, sin128, cos128.T, sin128.T)


_NT = (((1,), (1,)), ((), ()))
SCORE_CHUNK = 2 * LANE


def _chunks(n, lead=0):
    out = [(0, lead)] if lead else []
    return out + [(o, min(SCORE_CHUNK, n - o)) for o in range(lead, n, SCORE_CHUNK)]


SUBLANE = 8


def _fold(x, op):
    w, cols = x.shape
    return op(x.reshape(w // SUBLANE, SUBLANE, cols), axis=0)


class _SoftmaxStreamT:
    def __init__(self, score_fn, chunks, values_fn, s_ref):
        self.score_fn, self.chunks, self.values_fn, self.s_ref = score_fn, chunks, values_fn, s_ref

    def pass1(self):
        mp = None
        for off, w in self.chunks:
            s = self.score_fn(off, w)
            self.s_ref[off:off + w, :] = s
            part = _fold(s, jnp.max)
            mp = part if mp is None else jnp.maximum(mp, part)
            yield
        self.m = jnp.max(mp, axis=0, keepdims=True)

    def pass2(self):
        acc = None
        for off, w in self.chunks:
            p = jnp.exp2(self.s_ref[off:off + w, :] - self.m)
            pv = jnp.dot(self.values_fn(off, w), p.astype(BF16), preferred_element_type=F32)
            acc = pv if acc is None else acc + pv
            yield
        self.out = acc / acc[0:1, :]


def _trace_pipelined(streams, side=()):
    step = lambda gens: [g for g in gens if next(g, StopIteration) is not StopIteration]
    side = list(side)
    for _ in streams[0].pass1():
        side = step(side)
    for i, st in enumerate(streams):
        gens = [st.pass2()] + ([streams[i + 1].pass1()] if i + 1 < len(streams) else [])
        while gens:
            gens = step(gens)
            side = step(side)
    while side:
        side = step(side)


MLA_HPS = 4


def _mla_attn_body(qt_ref, k_ref, vt_ref, o_ref, *s_refs, nq):
    t = ATT_T
    qi = pl.program_id(2)

    def branch(nt):
        n = nt * t
        causal = lax.broadcasted_iota(jnp.int32, (t, t), 0) <= lax.broadcasted_iota(jnp.int32, (t, t), 1)
        streams = []
        for hh in range(MLA_HPS):
            hs = slice(hh * LANE, (hh + 1) * LANE)
            qt = qt_ref[hs, :]

            def score(off, w, qt=qt, hs=hs):
                s = jnp.dot(k_ref[off:off + w, hs], qt, preferred_element_type=F32)
                return jnp.where(causal, s, NEG) if off == n - t else s

            values = lambda off, w, hs=hs: vt_ref[hs, off:off + w]
            streams.append(_SoftmaxStreamT(score, _chunks(n), values, s_refs[hh]))
        _trace_pipelined(streams)
        outs = [st.out[LANE - MLA_V:] for st in streams]
        o_ref[...] = jnp.concatenate(outs, axis=0).T.astype(BF16)

    for nt in range(1, nq + 1):
        pl.when(qi == nt - 1)(functools.partial(branch, nt))


def _mla_attn(qt, k, vt, B, S):
    t = ATT_T
    nq = S // t
    T = B * S
    hps = MLA_HPS
    return pl.pallas_call(
        functools.partial(_mla_attn_body, nq=nq),
        grid=(B, MLA_HEADS // hps, nq),
        in_specs=[pl.BlockSpec((hps * LANE, t), lambda b, p, i: (p, b * nq + i)),
                  pl.BlockSpec((S, hps * LANE), lambda b, p, i: (b, p)),
                  pl.BlockSpec((hps * LANE, S), lambda b, p, i: (p, b))],
        out_specs=pl.BlockSpec((t, hps * MLA_V), lambda b, p, i: (b * nq + i, p)),
        out_shape=jax.ShapeDtypeStruct((T, MLA_HEADS * MLA_V), BF16),
        scratch_shapes=[pltpu.VMEM((S, t), F32)] * hps,
        compiler_params=_cparams(("parallel", "parallel", "arbitrary")),
        name="mla_attn",
    )(qt, k, vt)


NSA_QW = NSA_HEADS * LANE
NSA_KVW = NSA_GROUPS * LANE
NSA_IN_W = NSA_QW + 3 * NSA_KVW + LANE
GATES_PER_GROUP = 3 * NSA_R
RANK_STEPS_PER_STAGE = 8
LOG2E = math.log2(math.e)


def _nsa_proj_body(h_ref, g_ref, wqt_ref, wkv_ref, wkvt_ref, wgt_ref, blk_ref,
                   qt_out, kvc_out, kvs_out, kvw_out, kvst_out, kvwt_out, gate_out):
    xn = _rms(h_ref[...], g_ref[...]).astype(BF16)
    scale = NSA_QK ** -0.5 * LOG2E
    cw = 4 * LANE
    for c in range(NSA_QW // cw):
        sl = slice(c * cw, (c + 1) * cw)
        qt = lax.dot_general(wqt_ref[sl, :], xn, _NT, preferred_element_type=F32)
        qt_out[sl, :] = (qt * scale).astype(BF16)
    for n, out in enumerate((kvc_out, kvs_out, kvw_out)):
        sl = slice(n * NSA_KVW, (n + 1) * NSA_KVW)
        kv = jnp.dot(xn, wkv_ref[:, sl], preferred_element_type=F32)
        if out is kvs_out:
            kv = kv + jnp.concatenate([blk_ref[...]] * NSA_GROUPS, axis=1)
        out[...] = kv.astype(BF16)
    for n, out in enumerate((kvst_out, kvwt_out)):
        sl = slice(n * NSA_KVW, (n + 1) * NSA_KVW)
        vt = lax.dot_general(wkvt_ref[sl, :], xn, _NT, preferred_element_type=F32)
        out[...] = _with_ones_rows(vt).astype(BF16)
    gl = lax.dot_general(wgt_ref[...], xn, _NT, preferred_element_type=F32)
    gate_out[...] = jax.nn.sigmoid(gl)


def _nsa_proj(h, g, w, blk_onehot):
    T = h.shape[0]
    tm = PROJ_TM
    ns = blk_onehot.shape[0] // tm
    row = lambda i: (i, 0)
    col = lambda i: (0, i)
    return pl.pallas_call(
        _nsa_proj_body,
        grid=(T // tm,),
        in_specs=[pl.BlockSpec((tm, D_MODEL), row), _const_spec((1, D_MODEL)), _const_spec(w["w_qt"].shape),
                  _const_spec(w["w_kv"].shape), _const_spec(w["w_kvt"].shape), _const_spec(w["w_gt"].shape),
                  pl.BlockSpec((tm, LANE), lambda i: (i % ns, 0))],
        out_specs=[pl.BlockSpec((NSA_QW, tm), col), pl.BlockSpec((tm, NSA_KVW), row),
                   pl.BlockSpec((tm, NSA_KVW), row), pl.BlockSpec((tm, NSA_KVW), row),
                   pl.BlockSpec((NSA_KVW, tm), col), pl.BlockSpec((NSA_KVW, tm), col),
                   pl.BlockSpec((LANE, tm), col)],
        out_shape=[jax.ShapeDtypeStruct((NSA_QW, T), BF16), jax.ShapeDtypeStruct((T, NSA_KVW), BF16),
                   jax.ShapeDtypeStruct((T, NSA_KVW), BF16), jax.ShapeDtypeStruct((T, NSA_KVW), BF16),
                   jax.ShapeDtypeStruct((NSA_KVW, T), BF16), jax.ShapeDtypeStruct((NSA_KVW, T), BF16),
                   jax.ShapeDtypeStruct((LANE, T), F32)],
        compiler_params=_cparams(("parallel",)),
        name="nsa_proj",
    )(h, g.reshape(1, D_MODEL), w["w_qt"], w["w_kv"], w["w_kvt"], w["w_gt"], blk_onehot)


def _compress_body(x_ref, pos_ref, w1_ref, w2_ref, out_ref, out_t_ref):
    n_chunk = x_ref.shape[0]
    x = x_ref[...].astype(F32)
    xa = (x + pos_ref[0]).astype(BF16)
    xb = (x + pos_ref[1]).astype(BF16)
    a = jnp.dot(xa, w1_ref[0], preferred_element_type=F32)
    b = jnp.dot(xb, w1_ref[1], preferred_element_type=F32)
    pre = a + pltpu.roll(b, n_chunk - 1, 0)
    hid = jax.nn.gelu(pre, approximate=True).astype(BF16)
    kv = jnp.dot(hid, w2_ref[...], preferred_element_type=F32)
    out_ref[...] = kv.astype(BF16)
    out_t_ref[...] = kv.T.astype(BF16)


def _compress(x, pos, w1, w2):
    B, G, n_chunk, width = x.shape
    return pl.pallas_call(
        _compress_body,
        grid=(B, G),
        in_specs=[pl.BlockSpec((None, None, n_chunk, width), lambda b, g: (b, g, 0, 0)),
                  _const_spec(pos.shape), _const_spec(w1.shape), _const_spec(w2.shape)],
        out_specs=[pl.BlockSpec((None, None, n_chunk, LANE), lambda b, g: (b, g, 0, 0)),
                   pl.BlockSpec((None, None, LANE, n_chunk), lambda b, g: (b, g, 0, 0))],
        out_shape=[jax.ShapeDtypeStruct((B, G, n_chunk, LANE), BF16),
                   jax.ShapeDtypeStruct((B, G, LANE, n_chunk), BF16)],
        compiler_params=_cparams(("parallel", "parallel")),
        name="nsa_compress",
    )(x, pos, w1, w2)


def _nsa_attn_body(*refs, S):
    step = pl.program_id(2)
    for qi in range(S // NSA_T):
        pl.when(step == qi)(functools.partial(_nsa_tile, qi, *refs, S=S))


def _nsa_tile(qi, qt_ref, qtn_ref, kvc_ref, kvct_ref, kvs_ref, kvst_ref, kvw_ref, kvwt_ref, gate_ref, cbt_ref,
              dt_ref, ovt_ref, o_ref, *scratch, S):
    t = NSA_T
    R = NSA_R
    n_sel = S // SEL_BLOCK
    t0 = qi * t
    s_sel, s_win = scratch[0:2], scratch[2:4]
    ocmp_scr, sel_scr = scratch[4:6]
    heads_on_lanes = lambda ref: jnp.concatenate([ref[r * LANE:(r + 1) * LANE, :] for r in range(R)], axis=1)
    qt = heads_on_lanes(qt_ref)
    tile_r = lambda x: jnp.concatenate([x] * R, axis=1)
    half_w = 2 * t
    tile_h = lambda x: jnp.concatenate([x] * (half_w // t), axis=1)

    def compressed_and_selection(q_tile, tile, out):
        kvc = kvc_ref[...]
        n_cp = kvc.shape[0]
        sc = jnp.dot(kvc, q_tile, preferred_element_type=F32)
        yield
        cend = lax.broadcasted_iota(jnp.int32, (n_cp, t), 0) * CMP_STRIDE + (CMP_BLOCK - 1)
        valid = tile_r(tile * t + lax.broadcasted_iota(jnp.int32, (n_cp, t), 1) >= cend)
        shift = (tile * (t // CMP_STRIDE) + CB_CENTER) % n_cp
        cb = pltpu.roll(cbt_ref[...], shift, 0) if shift else cbt_ref[...]
        sc = jnp.where(valid, sc + cb, NEG)
        e = jnp.exp2(sc - jnp.max(sc, axis=0, keepdims=True))
        yield
        p = jnp.where(valid, e / jnp.sum(e, axis=0, keepdims=True), 0.0)
        out["o_cmp"] = jnp.dot(kvct_ref[...], p.astype(BF16), preferred_element_type=F32)
        yield

        psum = p[:, 0:t]
        for r in range(1, R):
            psum = psum + p[:, r * t:(r + 1) * t]
        imp = jnp.dot(ovt_ref[...], psum, preferred_element_type=F32, precision=lax.Precision.HIGHEST)
        yield
        jj = lax.broadcasted_iota(jnp.int32, (n_sel, t), 0)
        blk_t = (tile * t + lax.broadcasted_iota(jnp.int32, (n_sel, t), 1)) >> SEL_SHIFT
        forced = (jj == 0) | (jj == blk_t) | (jj == blk_t - 1)
        score = jnp.where(forced, 1e6, jnp.where(jj <= blk_t, imp, -1e6))
        cnt = jnp.zeros((n_sel, t), jnp.int32)
        for jp in range(n_sel):
            row = score[jp:jp + 1, :]
            beats = (row > score) | ((row == score) & (jj > jp))
            cnt = cnt + beats.astype(jnp.int32)
            if jp % RANK_STEPS_PER_STAGE == RANK_STEPS_PER_STAGE - 1:
                yield
        out["sel"] = jnp.where(cnt < min(SEL_TOP_N, n_sel), 0.0, NEG).astype(BF16)

    if qi == 0:
        cur = {}
        for _ in compressed_and_selection(qt, 0, cur):
            pass
        o_cmp, sel = cur["o_cmp"], cur["sel"]
    else:
        o_cmp, sel = ocmp_scr[...], sel_scr[...]
    nxt, side = {}, []
    if (qi + 1) * t < S:
        side = [compressed_and_selection(heads_on_lanes(qtn_ref), qi + 1, nxt)]

    kw = min(qi, WINDOW // t)
    nw = (kw + 1) * t
    w0 = (qi - kw) * t
    n = (qi + 1) * t

    def win_add(off, w):
        lo_d, hi_d = kw * t - off - (w - 1), kw * t - off + (t - 1)
        if lo_d >= 0 and hi_d < WINDOW:
            return None
        dist = (kw * t - off + lax.broadcasted_iota(jnp.int32, (w, t), 1)
                - lax.broadcasted_iota(jnp.int32, (w, t), 0))
        return jnp.where((dist >= 0) & (dist < WINDOW), 0.0, NEG)

    q_sel = jnp.concatenate([qt[0:NSA_QK], tile_r(sel), jnp.zeros((LANE - NSA_QK - n_sel, R * t), BF16)], axis=0)

    def sel_add(off, w):
        if off + w <= n - t:
            return None
        kpos = off + lax.broadcasted_iota(jnp.int32, (w, t), 0)
        return jnp.where(kpos <= t0 + lax.broadcasted_iota(jnp.int32, (w, t), 1), 0.0, NEG)

    def stream(half, q_all, k_ref, vt_ref, k0, nk, add_fn, s_ref):
        cs = slice(half * half_w, (half + 1) * half_w)
        qh = q_all[:, cs]

        def score(off, w):
            s = jnp.dot(k_ref[k0 + off:k0 + off + w, :], qh, preferred_element_type=F32)
            add = add_fn(off, w)
            if add is not None:
                s = s + tile_h(add)
            if off + w == nk:
                s = s + dt_ref[2 * t - w:, cs]
            return s

        values = lambda off, w: vt_ref[:, k0 + off:k0 + off + w]
        return _SoftmaxStreamT(score, _chunks(nk, lead=nk % SCORE_CHUNK), values, s_ref)

    halves = range(R * t // half_w)
    wins = [stream(h, qt, kvw_ref, kvwt_ref, w0, nw, win_add, s_win[h]) for h in halves]
    sels = [stream(h, q_sel, kvs_ref, kvst_ref, 0, n, sel_add, s_sel[h]) for h in halves]
    _trace_pipelined(wins[:1] + sels + wins[1:], side)

    g = pl.program_id(0)
    heads = []
    for r in range(R):
        h, hc = divmod(r * t, half_w)
        branches = (o_cmp[NSA_QK:, r * t:(r + 1) * t], sels[h].out[NSA_QK:, hc:hc + t], wins[h].out[NSA_QK:, hc:hc + t])
        o = None
        for br, ob in enumerate(branches):
            gate = gate_ref[pl.ds(g * GATES_PER_GROUP + br * R + r, 1), :]
            o = gate * ob if o is None else o + gate * ob
        heads.append(o)
    o_ref[...] = jnp.concatenate(heads, axis=0).T.astype(BF16)
    if nxt:
        ocmp_scr[...] = nxt["o_cmp"]
        sel_scr[...] = nxt["sel"]


def _nsa_attn(qt, kvc, kvct, kvs, kvst, kvw, kvwt, gates, cbt, dtt, ovt, B, S):
    t = NSA_T
    nq = S // t
    T = B * S
    G, R = NSA_GROUPS, NSA_R
    n_cp = kvc.shape[2]
    half_w = 2 * t
    nh = R * t // half_w
    kv_spec = pl.BlockSpec((S, LANE), lambda g, b, i: (b, g))
    kvt_spec = pl.BlockSpec((LANE, S), lambda g, b, i: (g, b))
    return pl.pallas_call(
        functools.partial(_nsa_attn_body, S=S),
        grid=(G, B, nq),
        in_specs=[pl.BlockSpec((R * LANE, t), lambda g, b, i: (g, b * nq + i)),
                  pl.BlockSpec((R * LANE, t), lambda g, b, i: (g, b * nq + jnp.minimum(i + 1, nq - 1))),
                  pl.BlockSpec((None, None, n_cp, LANE), lambda g, b, i: (b, g, 0, 0)),
                  pl.BlockSpec((None, None, LANE, n_cp), lambda g, b, i: (b, g, 0, 0)),
                  kv_spec, kvt_spec, kv_spec, kvt_spec,
                  pl.BlockSpec((LANE, t), lambda g, b, i: (0, b * nq + i)),
                  pl.BlockSpec((None, n_cp, R * t), lambda g, b, i: (g, 0, 0)),
                  pl.BlockSpec((None, 2 * t, R * t), lambda g, b, i: (g, 0, 0)),
                  _const_spec(ovt.shape)],
        out_specs=pl.BlockSpec((t, R * NSA_V), lambda g, b, i: (b * nq + i, g)),
        out_shape=jax.ShapeDtypeStruct((T, NSA_HEADS * NSA_V), BF16),
        scratch_shapes=([pltpu.VMEM((S, half_w), F32)] * nh + [pltpu.VMEM((WINDOW + t, half_w), F32)] * nh
                        + [pltpu.VMEM((LANE, R * t), F32), pltpu.VMEM((S // SEL_BLOCK, t), BF16)]),
        compiler_params=_cparams(("arbitrary", "arbitrary", "arbitrary")),
        name="nsa_attn",
    )(qt, qt, kvc, kvct, kvs, kvst, kvw, kvwt, gates, cbt, dtt, ovt)


def _t5_bucket(dist):
    n = jnp.maximum(dist, 0)
    max_exact = REL_BUCKETS // 2
    nf = jnp.maximum(n, 1).astype(F32)
    large = max_exact + (jnp.log(nf / max_exact) / math.log(REL_MAX_DIST / max_exact)
                         * (REL_BUCKETS - max_exact)).astype(jnp.int32)
    large = jnp.minimum(large, REL_BUCKETS - 1)
    return jnp.where(n < max_exact, n, large)


def _np_bucket(n):
    n = np.maximum(np.asarray(n), 0)
    max_exact = REL_BUCKETS // 2
    large = max_exact + (np.log(np.maximum(n, 1) / max_exact) / math.log(REL_MAX_DIST / max_exact)
                         * (REL_BUCKETS - max_exact)).astype(np.int64)
    return np.where(n < max_exact, n, np.minimum(large, REL_BUCKETS - 1))


def _bias_tables(rel_bias, S):
    t = NSA_T
    G, R = NSA_GROUPS, NSA_R
    n_cp = S // CMP_STRIDE
    a = np.arange(t)[:, None]
    dist_d = a - np.arange(2 * t)[None, :] + t
    dist_c = a - CMP_STRIDE * (np.arange(n_cp)[None, :] - CB_CENTER) - (CMP_BLOCK - 1)
    uncovered = min(t + 1, CMP_STRIDE * (CB_CENTER + 1) - (CMP_BLOCK - 1))
    assert (_np_bucket(np.arange(uncovered, 2 * S)) == REL_BUCKETS - 1).all()
    wrap_from = n_cp - max((S // t - 1) * (t // CMP_STRIDE) - CB_CENTER, 0)
    assert (dist_c[:, wrap_from:] < 0).all() and (dist_c[:, -1] < 0).all()

    def lookup(dist):
        oh = jax.nn.one_hot(_t5_bucket(jnp.asarray(dist)), REL_BUCKETS, dtype=F32)
        oh = oh - jax.nn.one_hot(REL_BUCKETS - 1, REL_BUCKETS, dtype=F32)
        val = jnp.einsum("acb,bh->hac", oh, rel_bias, precision=lax.Precision.HIGHEST)
        return val.reshape(G, R, *dist.shape)

    dtab = lookup(dist_d)
    cbias = jnp.where(jnp.asarray(dist_c >= 0), lookup(dist_c), 0.0)
    keys_first = lambda x: x.transpose(0, 3, 1, 2).reshape(G, x.shape[3], R * t)
    return keys_first(dtab) * LOG2E, keys_first(cbias) * LOG2E


def _selection_tables(S):
    n_cp = S // CMP_STRIDE
    n_cmp = (S - CMP_BLOCK) // CMP_STRIDE + 1
    n_sel = S // SEL_BLOCK
    cs = np.arange(n_cp) * CMP_STRIDE
    ce = cs + CMP_BLOCK
    ss = np.arange(n_sel) * SEL_BLOCK
    se = ss + SEL_BLOCK
    ov = np.minimum(ce[:, None], se[None, :]) - np.maximum(cs[:, None], ss[None, :])
    ov = (np.clip(ov, 0, None) / CMP_BLOCK).astype(np.float32)
    ov[n_cmp:] = 0.0
    assert n_sel <= LANE - NSA_QK
    blk_onehot = (np.arange(S)[:, None] // SEL_BLOCK == np.arange(LANE)[None, :] - NSA_QK).astype(np.float32)
    return jnp.asarray(ov.T), jnp.asarray(blk_onehot)


def _rope_tables(S):
    half = MLA_ROPE // 2
    inv = ROPE_THETA ** (-jnp.arange(half, dtype=F32) * 2.0 / MLA_ROPE)
    ang = jnp.arange(S, dtype=F32)[:, None] * inv[None, :]
    cos, sin = jnp.cos(ang), jnp.sin(ang)
    ones = jnp.ones((S, MLA_NOPE), F32)
    pad1 = jnp.ones((S, LANE - MLA_NOPE - MLA_ROPE), F32)
    cos128 = jnp.concatenate([ones, cos, cos, pad1], axis=1)
    sin128 = jnp.concatenate([0 * ones, sin, sin, 0 * pad1], axis=1)
    return cos128, sin128


def _mla_weights(w_in, q_norm, kv_norm, w_uq, w_ukv, w_o):
    H = MLA_HEADS
    half = MLA_ROPE // 2
    pad = LANE - MLA_NOPE - MLA_ROPE
    kr = w_in[:, MLA_Q_LORA + MLA_KV_LORA:]
    kr_sw = jnp.concatenate([-kr[:, half:], kr[:, :half]], axis=1)
    z = lambda n: jnp.zeros((D_MODEL, n), F32)
    w_in_ext = jnp.concatenate([w_in[:, :MLA_Q_LORA + MLA_KV_LORA],
                                z(MLA_NOPE), kr, z(pad), z(MLA_NOPE), kr_sw, z(pad)], axis=1)
    uq = w_uq.reshape(MLA_Q_LORA, H, MLA_NOPE + MLA_ROPE)
    qn, qr = uq[..., :MLA_NOPE], uq[..., MLA_NOPE:]
    qr_sw = jnp.concatenate([-qr[..., half:], qr[..., :half]], axis=-1)
    zq = jnp.zeros((MLA_Q_LORA, H, pad), F32)
    w_uq_p = jnp.concatenate([qn, qr, zq], axis=-1).reshape(MLA_Q_LORA, H * LANE)
    w_uq_s = jnp.concatenate([0 * qn, qr_sw, zq], axis=-1).reshape(MLA_Q_LORA, H * LANE)
    ukv = w_ukv.reshape(MLA_KV_LORA, H, MLA_NOPE + MLA_V)
    zk = jnp.zeros((MLA_KV_LORA, H, LANE - MLA_NOPE), F32)
    w_uk = jnp.concatenate([ukv[..., :MLA_NOPE], zk], axis=-1).reshape(MLA_KV_LORA, H * LANE)
    zv = jnp.zeros((MLA_KV_LORA, H, LANE - MLA_V), F32)
    w_uv = jnp.concatenate([zv, ukv[..., MLA_NOPE:]], axis=-1).reshape(MLA_KV_LORA, H * LANE)
    return dict(w_in=w_in_ext.astype(BF16), q_norm=q_norm.reshape(1, -1), kv_norm=kv_norm.reshape(1, -1),
                w_uqt=w_uq_p.T.astype(BF16), w_uqst=w_uq_s.T.astype(BF16), w_uk=w_uk.astype(BF16),
                w_uvt=w_uv.T.astype(BF16), w_o=w_o.astype(BF16))


def _nsa_weights(w_in, pos_k, w1_k, w2_k, pos_v, w1_v, w2_v, w_o):
    H, G = NSA_HEADS, NSA_GROUPS
    gw = G * NSA_QK
    q0 = H * NSA_QK
    wq = w_in[:, :q0].reshape(D_MODEL, H, NSA_QK)
    wq = jnp.concatenate([wq, jnp.zeros_like(wq)], axis=-1).reshape(D_MODEL, H * LANE)
    cols, cols_t = [], []
    for n in range(3):
        k = w_in[:, q0 + (2 * n) * gw: q0 + (2 * n + 1) * gw].reshape(D_MODEL, G, NSA_QK)
        v = w_in[:, q0 + (2 * n + 1) * gw: q0 + (2 * n + 2) * gw].reshape(D_MODEL, G, NSA_V)
        right = v if n == 0 else jnp.zeros_like(v)
        cols.append(jnp.concatenate([k, right], axis=-1).reshape(D_MODEL, G * LANE))
        if n > 0:
            cols_t.append(jnp.concatenate([jnp.zeros_like(k), v], axis=-1).reshape(D_MODEL, G * LANE))
    w_kv = jnp.concatenate(cols, axis=1)
    w_vt = jnp.concatenate(cols_t, axis=1).T
    wg = w_in[:, q0 + 6 * gw:].reshape(D_MODEL, G, NSA_R, 3).transpose(0, 1, 3, 2).reshape(D_MODEL, 3 * H)
    wg = jnp.concatenate([wg, jnp.zeros((D_MODEL, LANE - 3 * H), F32)], axis=1)
    eye = jnp.eye(2, dtype=F32)
    cw = CMP_STRIDE * 2 * NSA_QK
    pos = jnp.stack([pos_k, pos_v]).reshape(2, 2, CMP_STRIDE, NSA_QK)
    pos = pos.transpose(1, 2, 0, 3).reshape(2, 1, cw)
    w1 = jnp.stack([w1_k, w1_v]).reshape(2, 2, CMP_STRIDE, NSA_QK, CMP_HIDDEN)
    w1 = jnp.einsum("khldj,kq->hlkdqj", w1, eye).reshape(2, cw, 2 * CMP_HIDDEN).astype(BF16)
    w2 = jnp.einsum("kjd,kq->kjqd", jnp.stack([w2_k, w2_v]), eye).reshape(2 * CMP_HIDDEN, 2 * NSA_QK).astype(BF16)
    return dict(w_qt=wq.T.astype(BF16), w_kv=w_kv.astype(BF16), w_kvt=w_vt.astype(BF16),
                w_gt=wg.T.astype(BF16), pos=pos, w1=w1, w2=w2, w_o=w_o.astype(BF16))


def kernel(x, ffn_norm_a, ffn_a_w_gate, ffn_a_w_up, ffn_a_w_down, mix_norm, ffn_norm_b, ffn_b_w_gate, ffn_b_w_up, ffn_b_w_down, final_norm, rel_bias, mla_w_in, mla_q_norm, mla_kv_norm, mla_w_uq, mla_w_ukv, mla_w_o, nsa_w_in, nsa_cmp_pos_k, nsa_cmp_w1_k, nsa_cmp_w2_k, nsa_cmp_pos_v, nsa_cmp_w1_v, nsa_cmp_w2_v, nsa_w_o):
    B, S, D = x.shape
    assert D == D_MODEL and S % ATT_T == 0 and S % NSA_T == 0 and (B * S) % FFN_TM == 0
    T = B * S
    G = NSA_GROUPS
    n_chunk = S // CMP_STRIDE
    cos128, sin128 = _rope_tables(S)
    dtab, cbias = _bias_tables(rel_bias, S)
    ovt, blk_onehot = _selection_tables(S)

    h = x.reshape(T, D)
    for i in range(DEPTH):
        h = _ffn(h, ffn_norm_a[i], ffn_a_w_gate, ffn_a_w_up, ffn_a_w_down, i)
        j = i // N_MIXERS
        if i % N_MIXERS == 0:
            w = _mla_weights(mla_w_in[j], mla_q_norm[j], mla_kv_norm[j], mla_w_uq[j], mla_w_ukv[j], mla_w_o[j])
            q, k, v = _mla_proj(h, mix_norm[i], w, cos128, sin128, S)
            o = _mla_attn(q, k, v, B, S)
        else:
            w = _nsa_weights(nsa_w_in[j], nsa_cmp_pos_k[j], nsa_cmp_w1_k[j], nsa_cmp_w2_k[j],
                             nsa_cmp_pos_v[j], nsa_cmp_w1_v[j], nsa_cmp_w2_v[j], nsa_w_o[j])
            qt, kvc_in, kvs, kvw, kvst, kvwt, gates = _nsa_proj(h, mix_norm[i], w, blk_onehot)
            xc = kvc_in.reshape(B, n_chunk, CMP_STRIDE, G, LANE).transpose(0, 3, 1, 2, 4)
            xc = xc.reshape(B, G, n_chunk, CMP_STRIDE * LANE)
            kvc, kvct = _compress(xc, w["pos"], w["w1"], w["w2"])
            o = _nsa_attn(qt, kvc, kvct, kvs, kvst, kvw, kvwt, gates, cbias, dtab, ovt, B, S)
        w_o = mla_w_o if i % N_MIXERS == 0 else nsa_w_o
        h = _ffn(h, ffn_norm_b[i], ffn_b_w_gate, ffn_b_w_up, ffn_b_w_down, i,
                 proj=(o, w_o, j), final_g=final_norm if i == DEPTH - 1 else None)
    return h.reshape(B, S, D)
```

```python
import functools
import math

import numpy as np
import jax
import jax.numpy as jnp
from jax import lax
from jax.experimental import pallas as pl
from jax.experimental.pallas import tpu as pltpu

F32 = jnp.float32
BF16 = jnp.bfloat16

D_MODEL = 1024
DEPTH = 4
N_MIXERS = 2
RMS_EPS = 1e-6
FFN_HIDDEN = 2816
NEG = -1e30
MLA_HEADS = 16
MLA_Q_LORA = 384
MLA_KV_LORA = 256
MLA_NOPE = 64
MLA_ROPE = 32
MLA_V = 64
ROPE_THETA = 10000.0
NSA_HEADS = 16
NSA_GROUPS = 4
NSA_R = NSA_HEADS // NSA_GROUPS
NSA_QK = 64
NSA_V = 64
CMP_BLOCK = 32
CMP_STRIDE = 16
CMP_HIDDEN = 128
SEL_BLOCK = 64
SEL_SHIFT = SEL_BLOCK.bit_length() - 1
assert 1 << SEL_SHIFT == SEL_BLOCK
SEL_TOP_N = 16
WINDOW = 512
REL_BUCKETS = 32
REL_MAX_DIST = 128

LANE = 128
VMEM_LIMIT = 56 * 1024 * 1024

FFN_TM = 512
FFN_TF = 256
PROJ_TM = 256
ATT_T = 256
NSA_T = 256
CB_CENTER = 64


def _cparams(sem):
    return pltpu.CompilerParams(dimension_semantics=sem, vmem_limit_bytes=VMEM_LIMIT)


def _rms(x, g):
    ms = jnp.mean(x * x, axis=-1, keepdims=True)
    return x * lax.rsqrt(ms + RMS_EPS) * g


def _const_spec(shape):
    nd = len(shape)
    return pl.BlockSpec(shape, lambda *_: (0,) * nd)


def _wdot(a, w):
    return lax.dot_general(a, w, (((1,), (0,)), ((), ())), preferred_element_type=F32)


def _ffn_body(*refs, has_proj, has_final):
    it = iter(refs)
    h_ref = next(it)
    if has_proj:
        o_in_ref = next(it)
        wo_ref = next(it)
    g_ref = next(it)
    wg_ref = next(it)
    wu_ref = next(it)
    wd_ref = next(it)
    if has_final:
        gf_ref = next(it)
    out_ref = next(it)
    a_ref = next(it)

    x = h_ref[...]
    if has_proj:
        x = x + _wdot(o_in_ref[...], wo_ref[...])
    xn = _rms(x, g_ref[...]).astype(BF16)
    for c in range(FFN_HIDDEN // FFN_TF):
        sl = slice(c * FFN_TF, (c + 1) * FFN_TF)
        gt = _wdot(xn, wg_ref[:, sl])
        up = _wdot(xn, wu_ref[:, sl])
        a_ref[:, sl] = (gt * jax.nn.sigmoid(gt) * up).astype(BF16)
    y = x + 0.5 * _wdot(a_ref[...], wd_ref[...])
    if has_final:
        y = _rms(y, gf_ref[...])
    out_ref[...] = y


def _layer_spec(stacked, layer):
    return pl.BlockSpec((None,) + stacked.shape[1:], lambda i: (layer, 0, 0), pipeline_mode=pl.Buffered(1))


def _ffn(h, g, wg, wu, wd, layer, proj=None, final_g=None):
    T = h.shape[0]
    tm = FFN_TM
    row = lambda i: (i, 0)
    in_specs = [pl.BlockSpec((tm, D_MODEL), row)]
    args = [h]
    if proj is not None:
        o_in, wo, wo_idx = proj
        in_specs += [pl.BlockSpec((tm, o_in.shape[1]), row), _layer_spec(wo, wo_idx)]
        args += [o_in, wo]
    in_specs += [_const_spec((1, D_MODEL)), _layer_spec(wg, layer), _layer_spec(wu, layer), _layer_spec(wd, layer)]
    args += [g.reshape(1, D_MODEL), wg, wu, wd]
    if final_g is not None:
        in_specs.append(_const_spec((1, D_MODEL)))
        args.append(final_g.reshape(1, D_MODEL))
    return pl.pallas_call(
        functools.partial(_ffn_body, has_proj=proj is not None, has_final=final_g is not None),
        grid=(T // tm,),
        in_specs=in_specs,
        out_specs=pl.BlockSpec((tm, D_MODEL), row),
        out_shape=jax.ShapeDtypeStruct((T, D_MODEL), F32),
        scratch_shapes=[pltpu.VMEM((tm, FFN_HIDDEN), BF16)],
        compiler_params=_cparams(("parallel",)),
        name="ffn",
    )(*args)


MLA_CQ0, MLA_CKV0, MLA_KR0, MLA_KRS0, MLA_IN_W = 0, 384, 640, 768, 896
MLA_HCHUNK = 4


def _with_ones_rows(vt):
    row = lax.broadcasted_iota(jnp.int32, vt.shape, 0)
    return jnp.where(row % LANE == 0, 1.0, vt)


def _mla_proj_body(h_ref, g_ref, win_ref, qn_ref, kvn_ref, wuqt_ref, wuqst_ref, wuk_ref, wuvt_ref,
                   cos_ref, sin_ref, cost_ref, sint_ref, qt_out, k_out, vt_out):
    xn = _rms(h_ref[...], g_ref[...]).astype(BF16)
    proj = jnp.dot(xn, win_ref[...], preferred_element_type=F32)
    cq = _rms(proj[:, MLA_CQ0:MLA_CKV0], qn_ref[...]).astype(BF16)
    ckv = _rms(proj[:, MLA_CKV0:MLA_KR0], kvn_ref[...]).astype(BF16)
    cos = cos_ref[...]
    sin = sin_ref[...]
    kr = proj[:, MLA_KR0:MLA_KRS0] * cos + proj[:, MLA_KRS0:MLA_IN_W] * sin
    cos_t = cost_ref[...]
    sin_t = sint_ref[...]
    scale = (MLA_NOPE + MLA_ROPE) ** -0.5 * LOG2E
    vt = lax.dot_general(wuvt_ref[...], ckv, _NT, preferred_element_type=F32)
    vt_out[...] = _with_ones_rows(vt).astype(BF16)
    cw = MLA_HCHUNK * LANE
    for c in range(MLA_HEADS // MLA_HCHUNK):
        sl = slice(c * cw, (c + 1) * cw)
        qt = lax.dot_general(wuqt_ref[sl, :], cq, _NT, preferred_element_type=F32)
        qst = lax.dot_general(wuqst_ref[sl, :], cq, _NT, preferred_element_type=F32)
        kn = jnp.dot(ckv, wuk_ref[:, sl], preferred_element_type=F32)
        for hh in range(MLA_HCHUNK):
            hs = slice(hh * LANE, (hh + 1) * LANE)
            os_ = slice(c * cw + hh * LANE, c * cw + (hh + 1) * LANE)
            qt_out[os_, :] = ((qt[hs] * cos_t + qst[hs] * sin_t) * scale).astype(BF16)
            k_out[:, os_] = (kn[:, hs] + kr).astype(BF16)


def _mla_proj(h, g, w, cos128, sin128, S):
    T = h.shape[0]
    tm = PROJ_TM
    ns = S // tm
    row = lambda i: (i, 0)
    col = lambda i: (0, i)
    pos = lambda i: (i % ns, 0)
    pos_t = lambda i: (0, i % ns)
    HL = MLA_HEADS * LANE
    HV = MLA_HEADS * LANE
    return pl.pallas_call(
        _mla_proj_body,
        grid=(T // tm,),
        in_specs=[pl.BlockSpec((tm, D_MODEL), row), _const_spec((1, D_MODEL)),
                  _const_spec(w["w_in"].shape), _const_spec((1, MLA_Q_LORA)), _const_spec((1, MLA_KV_LORA)),
                  _const_spec(w["w_uqt"].shape), _const_spec(w["w_uqst"].shape),
                  _const_spec(w["w_uk"].shape), _const_spec(w["w_uvt"].shape),
                  pl.BlockSpec((tm, LANE), pos), pl.BlockSpec((tm, LANE), pos),
                  pl.BlockSpec((LANE, tm), pos_t), pl.BlockSpec((LANE, tm), pos_t)],
        out_specs=[pl.BlockSpec((HL, tm), col), pl.BlockSpec((tm, HL), row), pl.BlockSpec((HV, tm), col)],
        out_shape=[jax.ShapeDtypeStruct((HL, T), BF16), jax.ShapeDtypeStruct((T, HL), BF16),
                   jax.ShapeDtypeStruct((HV, T), BF16)],
        compiler_params=_cparams(("parallel",)),
        name="mla_proj",
    )(h, g.reshape(1, D_MODEL), w["w_in"], w["q_norm"], w["kv_norm"], w["w_uqt"], w["w_uqst"],
      w["w_uk"], w["w_uvt"], cos128, sin128, cos128.T, sin128.T)


_NT = (((1,), (1,)), ((), ()))
SCORE_CHUNK = 2 * LANE


def _chunks(n, lead=0):
    out = [(0, lead)] if lead else []
    return out + [(o, min(SCORE_CHUNK, n - o)) for o in range(lead, n, SCORE_CHUNK)]


SUBLANE = 8


def _fold(x, op):
    w, cols = x.shape
    return op(x.reshape(w // SUBLANE, SUBLANE, cols), axis=0)


class _SoftmaxStreamT:
    def __init__(self, score_fn, chunks, values_fn, s_ref):
        self.score_fn, self.chunks, self.values_fn, self.s_ref = score_fn, chunks, values_fn, s_ref

    def pass1(self):
        mp = None
        for off, w in self.chunks:
            s = self.score_fn(off, w)
            self.s_ref[off:off + w, :] = s
            part = _fold(s, jnp.max)
            mp = part if mp is None else jnp.maximum(mp, part)
            yield
        self.m = jnp.max(mp, axis=0, keepdims=True)

    def pass2(self):
        acc = None
        for off, w in self.chunks:
            p = jnp.exp2(self.s_ref[off:off + w, :] - self.m)
            pv = jnp.dot(self.values_fn(off, w), p.astype(BF16), preferred_element_type=F32)
            acc = pv if acc is None else acc + pv
            yield
        self.out = acc / acc[0:1, :]


def _trace_pipelined(streams):
    for _ in streams[0].pass1():
        pass
    for i, st in enumerate(streams):
        gens = [st.pass2()] + ([streams[i + 1].pass1()] if i + 1 < len(streams) else [])
        while gens:
            gens = [g for g in gens if next(g, StopIteration) is not StopIteration]


MLA_HPS = 4


def _mla_attn_body(qt_ref, k_ref, vt_ref, o_ref, *s_refs, nq):
    t = ATT_T
    qi = pl.program_id(2)

    def branch(nt):
        n = nt * t
        causal = lax.broadcasted_iota(jnp.int32, (t, t), 0) <= lax.broadcasted_iota(jnp.int32, (t, t), 1)
        streams = []
        for hh in range(MLA_HPS):
            hs = slice(hh * LANE, (hh + 1) * LANE)
            qt = qt_ref[hs, :]

            def score(off, w, qt=qt, hs=hs):
                s = jnp.dot(k_ref[off:off + w, hs], qt, preferred_element_type=F32)
                return jnp.where(causal, s, NEG) if off == n - t else s

            values = lambda off, w, hs=hs: vt_ref[hs, off:off + w]
            streams.append(_SoftmaxStreamT(score, _chunks(n), values, s_refs[hh]))
        _trace_pipelined(streams)
        outs = [st.out[LANE - MLA_V:] for st in streams]
        o_ref[...] = jnp.concatenate(outs, axis=0).T.astype(BF16)

    for nt in range(1, nq + 1):
        pl.when(qi == nt - 1)(functools.partial(branch, nt))


def _mla_attn(qt, k, vt, B, S):
    t = ATT_T
    nq = S // t
    T = B * S
    hps = MLA_HPS
    return pl.pallas_call(
        functools.partial(_mla_attn_body, nq=nq),
        grid=(B, MLA_HEADS // hps, nq),
        in_specs=[pl.BlockSpec((hps * LANE, t), lambda b, p, i: (p, b * nq + i)),
                  pl.BlockSpec((S, hps * LANE), lambda b, p, i: (b, p)),
                  pl.BlockSpec((hps * LANE, S), lambda b, p, i: (p, b))],
        out_specs=pl.BlockSpec((t, hps * MLA_V), lambda b, p, i: (b * nq + i, p)),
        out_shape=jax.ShapeDtypeStruct((T, MLA_HEADS * MLA_V), BF16),
        scratch_shapes=[pltpu.VMEM((S, t), F32)] * hps,
        compiler_params=_cparams(("parallel", "parallel", "arbitrary")),
        name="mla_attn",
    )(qt, k, vt)


NSA_QW = NSA_HEADS * LANE
NSA_KVW = NSA_GROUPS * LANE
NSA_IN_W = NSA_QW + 3 * NSA_KVW + LANE
GATES_PER_GROUP = 3 * NSA_R
RANK_STEPS_PER_STAGE = 8
LOG2E = math.log2(math.e)


def _nsa_proj_body(h_ref, g_ref, wqt_ref, wkv_ref, wkvt_ref, wgt_ref, blk_ref,
                   qt_out, kvc_out, kvs_out, kvw_out, kvst_out, kvwt_out, gate_out):
    xn = _rms(h_ref[...], g_ref[...]).astype(BF16)
    scale = NSA_QK ** -0.5 * LOG2E
    cw = 4 * LANE
    for c in range(NSA_QW // cw):
        sl = slice(c * cw, (c + 1) * cw)
        qt = lax.dot_general(wqt_ref[sl, :], xn, _NT, preferred_element_type=F32)
        qt_out[sl, :] = (qt * scale).astype(BF16)
    for n, out in enumerate((kvc_out, kvs_out, kvw_out)):
        sl = slice(n * NSA_KVW, (n + 1) * NSA_KVW)
        kv = jnp.dot(xn, wkv_ref[:, sl], preferred_element_type=F32)
        if out is kvs_out:
            kv = kv + jnp.concatenate([blk_ref[...]] * NSA_GROUPS, axis=1)
        out[...] = kv.astype(BF16)
    for n, out in enumerate((kvst_out, kvwt_out)):
        sl = slice(n * NSA_KVW, (n + 1) * NSA_KVW)
        vt = lax.dot_general(wkvt_ref[sl, :], xn, _NT, preferred_element_type=F32)
        out[...] = _with_ones_rows(vt).astype(BF16)
    gl = lax.dot_general(wgt_ref[...], xn, _NT, preferred_element_type=F32)
    gate_out[...] = jax.nn.sigmoid(gl)


def _nsa_proj(h, g, w, blk_onehot):
    T = h.shape[0]
    tm = PROJ_TM
    ns = blk_onehot.shape[0] // tm
    row = lambda i: (i, 0)
    col = lambda i: (0, i)
    return pl.pallas_call(
        _nsa_proj_body,
        grid=(T // tm,),
        in_specs=[pl.BlockSpec((tm, D_MODEL), row), _const_spec((1, D_MODEL)), _const_spec(w["w_qt"].shape),
                  _const_spec(w["w_kv"].shape), _const_spec(w["w_kvt"].shape), _const_spec(w["w_gt"].shape),
                  pl.BlockSpec((tm, LANE), lambda i: (i % ns, 0))],
        out_specs=[pl.BlockSpec((NSA_QW, tm), col), pl.BlockSpec((tm, NSA_KVW), row),
                   pl.BlockSpec((tm, NSA_KVW), row), pl.BlockSpec((tm, NSA_KVW), row),
                   pl.BlockSpec((NSA_KVW, tm), col), pl.BlockSpec((NSA_KVW, tm), col),
                   pl.BlockSpec((LANE, tm), col)],
        out_shape=[jax.ShapeDtypeStruct((NSA_QW, T), BF16), jax.ShapeDtypeStruct((T, NSA_KVW), BF16),
                   jax.ShapeDtypeStruct((T, NSA_KVW), BF16), jax.ShapeDtypeStruct((T, NSA_KVW), BF16),
                   jax.ShapeDtypeStruct((NSA_KVW, T), BF16), jax.ShapeDtypeStruct((NSA_KVW, T), BF16),
                   jax.ShapeDtypeStruct((LANE, T), F32)],
        compiler_params=_cparams(("parallel",)),
        name="nsa_proj",
    )(h, g.reshape(1, D_MODEL), w["w_qt"], w["w_kv"], w["w_kvt"], w["w_gt"], blk_onehot)


def _compress_body(x_ref, pos_ref, w1_ref, w2_ref, out_ref, out_t_ref):
    n_chunk = x_ref.shape[0]
    x = x_ref[...].astype(F32)
    xa = (x + pos_ref[0]).astype(BF16)
    xb = (x + pos_ref[1]).astype(BF16)
    a = jnp.dot(xa, w1_ref[0], preferred_element_type=F32)
    b = jnp.dot(xb, w1_ref[1], preferred_element_type=F32)
    pre = a + pltpu.roll(b, n_chunk - 1, 0)
    hid = jax.nn.gelu(pre, approximate=True).astype(BF16)
    kv = jnp.dot(hid, w2_ref[...], preferred_element_type=F32)
    out_ref[...] = kv.astype(BF16)
    out_t_ref[...] = kv.T.astype(BF16)


def _compress(x, pos, w1, w2):
    B, G, n_chunk, width = x.shape
    return pl.pallas_call(
        _compress_body,
        grid=(B, G),
        in_specs=[pl.BlockSpec((None, None, n_chunk, width), lambda b, g: (b, g, 0, 0)),
                  _const_spec(pos.shape), _const_spec(w1.shape), _const_spec(w2.shape)],
        out_specs=[pl.BlockSpec((None, None, n_chunk, LANE), lambda b, g: (b, g, 0, 0)),
                   pl.BlockSpec((None, None, LANE, n_chunk), lambda b, g: (b, g, 0, 0))],
        out_shape=[jax.ShapeDtypeStruct((B, G, n_chunk, LANE), BF16),
                   jax.ShapeDtypeStruct((B, G, LANE, n_chunk), BF16)],
        compiler_params=_cparams(("parallel", "parallel")),
        name="nsa_compress",
    )(x, pos, w1, w2)


def _nsa_attn_body(*refs, S):
    step = pl.program_id(2)
    for qi in range(S // NSA_T):
        pl.when(step == qi)(functools.partial(_nsa_tile, qi, *refs, S=S))


def _nsa_tile(qi, qt_ref, qtn_ref, kvc_ref, kvct_ref, kvs_ref, kvst_ref, kvw_ref, kvwt_ref, gate_ref, cbt_ref,
              dt_ref, ovt_ref, o_ref, *scratch, S):
    t = NSA_T
    R = NSA_R
    n_sel = S // SEL_BLOCK
    t0 = qi * t
    s_sel, s_win = scratch[0:2], scratch[2:4]
    ocmp_scr, sel_scr = scratch[4:6]
    heads_on_lanes = lambda ref: jnp.concatenate([ref[r * LANE:(r + 1) * LANE, :] for r in range(R)], axis=1)
    qt = heads_on_lanes(qt_ref)
    tile_r = lambda x: jnp.concatenate([x] * R, axis=1)
    half_w = 2 * t
    tile_h = lambda x: jnp.concatenate([x] * (half_w // t), axis=1)

    def compressed_and_selection(q_tile, tile, out):
        kvc = kvc_ref[...]
        n_cp = kvc.shape[0]
        sc = jnp.dot(kvc, q_tile, preferred_element_type=F32)
        yield
        cend = lax.broadcasted_iota(jnp.int32, (n_cp, t), 0) * CMP_STRIDE + (CMP_BLOCK - 1)
        valid = tile_r(tile * t + lax.broadcasted_iota(jnp.int32, (n_cp, t), 1) >= cend)
        shift = (tile * (t // CMP_STRIDE) + CB_CENTER) % n_cp
        cb = pltpu.roll(cbt_ref[...], shift, 0) if shift else cbt_ref[...]
        sc = jnp.where(valid, sc + cb, NEG)
        e = jnp.exp2(sc - jnp.max(sc, axis=0, keepdims=True))
        yield
        p = jnp.where(valid, e / jnp.sum(e, axis=0, keepdims=True), 0.0)
        out["o_cmp"] = jnp.dot(kvct_ref[...], p.astype(BF16), preferred_element_type=F32)
        yield

        psum = p[:, 0:t]
        for r in range(1, R):
            psum = psum + p[:, r * t:(r + 1) * t]
        imp = jnp.dot(ovt_ref[...], psum, preferred_element_type=F32, precision=lax.Precision.HIGHEST)
        yield
        jj = lax.broadcasted_iota(jnp.int32, (n_sel, t), 0)
        blk_t = (tile * t + lax.broadcasted_iota(jnp.int32, (n_sel, t), 1)) >> SEL_SHIFT
        forced = (jj == 0) | (jj == blk_t) | (jj == blk_t - 1)
        score = jnp.where(forced, 1e6, jnp.where(jj <= blk_t, imp, -1e6))
        cnt = jnp.zeros((n_sel, t), jnp.int32)
        for jp in range(n_sel):
            row = score[jp:jp + 1, :]
            beats = (row > score) | ((row == score) & (jj > jp))
            cnt = cnt + beats.astype(jnp.int32)
            if jp % RANK_STEPS_PER_STAGE == RANK_STEPS_PER_STAGE - 1:
                yield
        out["sel"] = jnp.where(cnt < min(SEL_TOP_N, n_sel), 0.0, NEG).astype(BF16)

    if qi == 0:
        cur = {}
        for _ in compressed_and_selection(qt, 0, cur):
            pass
        o_cmp, sel = cur["o_cmp"], cur["sel"]
    else:
        o_cmp, sel = ocmp_scr[...], sel_scr[...]
    nxt, side = {}, []
    if (qi + 1) * t < S:
        side = [compressed_and_selection(heads_on_lanes(qtn_ref), qi + 1, nxt)]

    kw = min(qi, WINDOW // t)
    nw = (kw + 1) * t
    w0 = (qi - kw) * t
    n = (qi + 1) * t

    def win_add(off, w):
        lo_d, hi_d = kw * t - off - (w - 1), kw * t - off + (t - 1)
        if lo_d >= 0 and hi_d < WINDOW:
            return None
        dist = (kw * t - off + lax.broadcasted_iota(jnp.int32, (w, t), 1)
                - lax.broadcasted_iota(jnp.int32, (w, t), 0))
        return jnp.where((dist >= 0) & (dist < WINDOW), 0.0, NEG)

    q_sel = jnp.concatenate([qt[0:NSA_QK], tile_r(sel), jnp.zeros((LANE - NSA_QK - n_sel, R * t), BF16)], axis=0)

    def sel_add(off, w):
        if off + w <= n - t:
            return None
        kpos = off + lax.broadcasted_iota(jnp.int32, (w, t), 0)
        return jnp.where(kpos <= t0 + lax.broadcasted_iota(jnp.int32, (w, t), 1), 0.0, NEG)

    def stream(half, q_all, k_ref, vt_ref, k0, nk, add_fn, s_ref):
        cs = slice(half * half_w, (half + 1) * half_w)
        qh = q_all[:, cs]

        def score(off, w):
            s = jnp.dot(k_ref[k0 + off:k0 + off + w, :], qh, preferred_element_type=F32)
            add = add_fn(off, w)
            if add is not None:
                s = s + tile_h(add)
            d0 = off - (nk - 2 * t)
            if d0 + w > 0:
                assert d0 >= 0
                s = s + dt_ref[d0:d0 + w, cs]
            return s

        values = lambda off, w: vt_ref[:, k0 + off:k0 + off + w]
        return _SoftmaxStreamT(score, _chunks(nk, lead=nk % SCORE_CHUNK), values, s_ref)

    halves = range(R * t // half_w)
    wins = [stream(h, qt, kvw_ref, kvwt_ref, w0, nw, win_add, s_win[h]) for h in halves]
    sels = [stream(h, q_sel, kvs_ref, kvst_ref, 0, n, sel_add, s_sel[h]) for h in halves]
    _trace_pipelined(wins[:1] + sels + wins[1:])
    for gen in side:
        for _ in gen:
            pass

    g = pl.program_id(0)
    heads = []
    for r in range(R):
        h, hc = divmod(r * t, half_w)
        branches = (o_cmp[NSA_QK:, r * t:(r + 1) * t], sels[h].out[NSA_QK:, hc:hc + t], wins[h].out[NSA_QK:, hc:hc + t])
        o = None
        for br, ob in enumerate(branches):
            gate = gate_ref[pl.ds(g * GATES_PER_GROUP + br * R + r, 1), :]
            o = gate * ob if o is None else o + gate * ob
        heads.append(o)
    o_ref[...] = jnp.concatenate(heads, axis=0).T.astype(BF16)
    if nxt:
        ocmp_scr[...] = nxt["o_cmp"]
        sel_scr[...] = nxt["sel"]


def _nsa_attn(qt, kvc, kvct, kvs, kvst, kvw, kvwt, gates, cbt, dtt, ovt, B, S):
    t = NSA_T
    nq = S // t
    T = B * S
    G, R = NSA_GROUPS, NSA_R
    n_cp = kvc.shape[2]
    half_w = 2 * t
    nh = R * t // half_w
    kv_spec = pl.BlockSpec((S, LANE), lambda g, b, i: (b, g))
    kvt_spec = pl.BlockSpec((LANE, S), lambda g, b, i: (g, b))
    return pl.pallas_call(
        functools.partial(_nsa_attn_body, S=S),
        grid=(G, B, nq),
        in_specs=[pl.BlockSpec((R * LANE, t), lambda g, b, i: (g, b * nq + i)),
                  pl.BlockSpec((R * LANE, t), lambda g, b, i: (g, b * nq + jnp.minimum(i + 1, nq - 1))),
                  pl.BlockSpec((None, None, n_cp, LANE), lambda g, b, i: (b, g, 0, 0)),
                  pl.BlockSpec((None, None, LANE, n_cp), lambda g, b, i: (b, g, 0, 0)),
                  kv_spec, kvt_spec, kv_spec, kvt_spec,
                  pl.BlockSpec((LANE, t), lambda g, b, i: (0, b * nq + i)),
                  pl.BlockSpec((None, n_cp, R * t), lambda g, b, i: (g, 0, 0)),
                  pl.BlockSpec((None, 2 * t, R * t), lambda g, b, i: (g, 0, 0)),
                  _const_spec(ovt.shape)],
        out_specs=pl.BlockSpec((t, R * NSA_V), lambda g, b, i: (b * nq + i, g)),
        out_shape=jax.ShapeDtypeStruct((T, NSA_HEADS * NSA_V), BF16),
        scratch_shapes=([pltpu.VMEM((S, half_w), F32)] * nh + [pltpu.VMEM((WINDOW + t, half_w), F32)] * nh
                        + [pltpu.VMEM((LANE, R * t), F32), pltpu.VMEM((S // SEL_BLOCK, t), BF16)]),
        compiler_params=_cparams(("arbitrary", "arbitrary", "arbitrary")),
        name="nsa_attn",
    )(qt, qt, kvc, kvct, kvs, kvst, kvw, kvwt, gates, cbt, dtt, ovt)


def _t5_bucket(dist):
    n = jnp.maximum(dist, 0)
    max_exact = REL_BUCKETS // 2
    nf = jnp.maximum(n, 1).astype(F32)
    large = max_exact + (jnp.log(nf / max_exact) / math.log(REL_MAX_DIST / max_exact)
                         * (REL_BUCKETS - max_exact)).astype(jnp.int32)
    large = jnp.minimum(large, REL_BUCKETS - 1)
    return jnp.where(n < max_exact, n, large)


def _np_bucket(n):
    n = np.maximum(np.asarray(n), 0)
    max_exact = REL_BUCKETS // 2
    large = max_exact + (np.log(np.maximum(n, 1) / max_exact) / math.log(REL_MAX_DIST / max_exact)
                         * (REL_BUCKETS - max_exact)).astype(np.int64)
    return np.where(n < max_exact, n, np.minimum(large, REL_BUCKETS - 1))


def _bias_tables(rel_bias, S):
    t = NSA_T
    G, R = NSA_GROUPS, NSA_R
    n_cp = S // CMP_STRIDE
    a = np.arange(t)[:, None]
    dist_d = a - np.arange(2 * t)[None, :] + t
    dist_c = a - CMP_STRIDE * (np.arange(n_cp)[None, :] - CB_CENTER) - (CMP_BLOCK - 1)
    uncovered = min(t + 1, CMP_STRIDE * (CB_CENTER + 1) - (CMP_BLOCK - 1))
    assert (_np_bucket(np.arange(uncovered, 2 * S)) == REL_BUCKETS - 1).all()
    wrap_from = n_cp - max((S // t - 1) * (t // CMP_STRIDE) - CB_CENTER, 0)
    assert (dist_c[:, wrap_from:] < 0).all() and (dist_c[:, -1] < 0).all()

    def lookup(dist):
        oh = jax.nn.one_hot(_t5_bucket(jnp.asarray(dist)), REL_BUCKETS, dtype=F32)
        oh = oh - jax.nn.one_hot(REL_BUCKETS - 1, REL_BUCKETS, dtype=F32)
        val = jnp.einsum("acb,bh->hac", oh, rel_bias, precision=lax.Precision.HIGHEST)
        return val.reshape(G, R, *dist.shape)

    dtab = lookup(dist_d)
    cbias = jnp.where(jnp.asarray(dist_c >= 0), lookup(dist_c), 0.0)
    keys_first = lambda x: x.transpose(0, 3, 1, 2).reshape(G, x.shape[3], R * t)
    return keys_first(dtab) * LOG2E, keys_first(cbias) * LOG2E


def _selection_tables(S):
    n_cp = S // CMP_STRIDE
    n_cmp = (S - CMP_BLOCK) // CMP_STRIDE + 1
    n_sel = S // SEL_BLOCK
    cs = np.arange(n_cp) * CMP_STRIDE
    ce = cs + CMP_BLOCK
    ss = np.arange(n_sel) * SEL_BLOCK
    se = ss + SEL_BLOCK
    ov = np.minimum(ce[:, None], se[None, :]) - np.maximum(cs[:, None], ss[None, :])
    ov = (np.clip(ov, 0, None) / CMP_BLOCK).astype(np.float32)
    ov[n_cmp:] = 0.0
    assert n_sel <= LANE - NSA_QK
    blk_onehot = (np.arange(S)[:, None] // SEL_BLOCK == np.arange(LANE)[None, :] - NSA_QK).astype(np.float32)
    return jnp.asarray(ov.T), jnp.asarray(blk_onehot)


def _rope_tables(S):
    half = MLA_ROPE // 2
    inv = ROPE_THETA ** (-jnp.arange(half, dtype=F32) * 2.0 / MLA_ROPE)
    ang = jnp.arange(S, dtype=F32)[:, None] * inv[None, :]
    cos, sin = jnp.cos(ang), jnp.sin(ang)
    ones = jnp.ones((S, MLA_NOPE), F32)
    pad1 = jnp.ones((S, LANE - MLA_NOPE - MLA_ROPE), F32)
    cos128 = jnp.concatenate([ones, cos, cos, pad1], axis=1)
    sin128 = jnp.concatenate([0 * ones, sin, sin, 0 * pad1], axis=1)
    return cos128, sin128


def _mla_weights(w_in, q_norm, kv_norm, w_uq, w_ukv, w_o):
    H = MLA_HEADS
    half = MLA_ROPE // 2
    pad = LANE - MLA_NOPE - MLA_ROPE
    kr = w_in[:, MLA_Q_LORA + MLA_KV_LORA:]
    kr_sw = jnp.concatenate([-kr[:, half:], kr[:, :half]], axis=1)
    z = lambda n: jnp.zeros((D_MODEL, n), F32)
    w_in_ext = jnp.concatenate([w_in[:, :MLA_Q_LORA + MLA_KV_LORA],
                                z(MLA_NOPE), kr, z(pad), z(MLA_NOPE), kr_sw, z(pad)], axis=1)
    uq = w_uq.reshape(MLA_Q_LORA, H, MLA_NOPE + MLA_ROPE)
    qn, qr = uq[..., :MLA_NOPE], uq[..., MLA_NOPE:]
    qr_sw = jnp.concatenate([-qr[..., half:], qr[..., :half]], axis=-1)
    zq = jnp.zeros((MLA_Q_LORA, H, pad), F32)
    w_uq_p = jnp.concatenate([qn, qr, zq], axis=-1).reshape(MLA_Q_LORA, H * LANE)
    w_uq_s = jnp.concatenate([0 * qn, qr_sw, zq], axis=-1).reshape(MLA_Q_LORA, H * LANE)
    ukv = w_ukv.reshape(MLA_KV_LORA, H, MLA_NOPE + MLA_V)
    zk = jnp.zeros((MLA_KV_LORA, H, LANE - MLA_NOPE), F32)
    w_uk = jnp.concatenate([ukv[..., :MLA_NOPE], zk], axis=-1).reshape(MLA_KV_LORA, H * LANE)
    zv = jnp.zeros((MLA_KV_LORA, H, LANE - MLA_V), F32)
    w_uv = jnp.concatenate([zv, ukv[..., MLA_NOPE:]], axis=-1).reshape(MLA_KV_LORA, H * LANE)
    return dict(w_in=w_in_ext.astype(BF16), q_norm=q_norm.reshape(1, -1), kv_norm=kv_norm.reshape(1, -1),
                w_uqt=w_uq_p.T.astype(BF16), w_uqst=w_uq_s.T.astype(BF16), w_uk=w_uk.astype(BF16),
                w_uvt=w_uv.T.astype(BF16), w_o=w_o.astype(BF16))


def _nsa_weights(w_in, pos_k, w1_k, w2_k, pos_v, w1_v, w2_v, w_o):
    H, G = NSA_HEADS, NSA_GROUPS
    gw = G * NSA_QK
    q0 = H * NSA_QK
    wq = w_in[:, :q0].reshape(D_MODEL, H, NSA_QK)
    wq = jnp.concatenate([wq, jnp.zeros_like(wq)], axis=-1).reshape(D_MODEL, H * LANE)
    cols, cols_t = [], []
    for n in range(3):
        k = w_in[:, q0 + (2 * n) * gw: q0 + (2 * n + 1) * gw].reshape(D_MODEL, G, NSA_QK)
        v = w_in[:, q0 + (2 * n + 1) * gw: q0 + (2 * n + 2) * gw].reshape(D_MODEL, G, NSA_V)
        right = v if n == 0 else jnp.zeros_like(v)
        cols.append(jnp.concatenate([k, right], axis=-1).reshape(D_MODEL, G * LANE))
        if n > 0:
            cols_t.append(jnp.concatenate([jnp.zeros_like(k), v], axis=-1).reshape(D_MODEL, G * LANE))
    w_kv = jnp.concatenate(cols, axis=1)
    w_vt = jnp.concatenate(cols_t, axis=1).T
    wg = w_in[:, q0 + 6 * gw:].reshape(D_MODEL, G, NSA_R, 3).transpose(0, 1, 3, 2).reshape(D_MODEL, 3 * H)
    wg = jnp.concatenate([wg, jnp.zeros((D_MODEL, LANE - 3 * H), F32)], axis=1)
    eye = jnp.eye(2, dtype=F32)
    cw = CMP_STRIDE * 2 * NSA_QK
    pos = jnp.stack([pos_k, pos_v]).reshape(2, 2, CMP_STRIDE, NSA_QK)
    pos = pos.transpose(1, 2, 0, 3).reshape(2, 1, cw)
    w1 = jnp.stack([w1_k, w1_v]).reshape(2, 2, CMP_STRIDE, NSA_QK, CMP_HIDDEN)
    w1 = jnp.einsum("khldj,kq->hlkdqj", w1, eye).reshape(2, cw, 2 * CMP_HIDDEN).astype(BF16)
    w2 = jnp.einsum("kjd,kq->kjqd", jnp.stack([w2_k, w2_v]), eye).reshape(2 * CMP_HIDDEN, 2 * NSA_QK).astype(BF16)
    return dict(w_qt=wq.T.astype(BF16), w_kv=w_kv.astype(BF16), w_kvt=w_vt.astype(BF16),
                w_gt=wg.T.astype(BF16), pos=pos, w1=w1, w2=w2, w_o=w_o.astype(BF16))


def kernel(x, ffn_norm_a, ffn_a_w_gate, ffn_a_w_up, ffn_a_w_down, mix_norm, ffn_norm_b, ffn_b_w_gate, ffn_b_w_up, ffn_b_w_down, final_norm, rel_bias, mla_w_in, mla_q_norm, mla_kv_norm, mla_w_uq, mla_w_ukv, mla_w_o, nsa_w_in, nsa_cmp_pos_k, nsa_cmp_w1_k, nsa_cmp_w2_k, nsa_cmp_pos_v, nsa_cmp_w1_v, nsa_cmp_w2_v, nsa_w_o):
    B, S, D = x.shape
    assert D == D_MODEL and S % ATT_T == 0 and S % NSA_T == 0 and (B * S) % FFN_TM == 0
    T = B * S
    G = NSA_GROUPS
    n_chunk = S // CMP_STRIDE
    cos128, sin128 = _rope_tables(S)
    dtab, cbias = _bias_tables(rel_bias, S)
    ovt, blk_onehot = _selection_tables(S)

    h = x.reshape(T, D)
    for i in range(DEPTH):
        h = _ffn(h, ffn_norm_a[i], ffn_a_w_gate, ffn_a_w_up, ffn_a_w_down, i)
        j = i // N_MIXERS
        if i % N_MIXERS == 0:
            w = _mla_weights(mla_w_in[j], mla_q_norm[j], mla_kv_norm[j], mla_w_uq[j], mla_w_ukv[j], mla_w_o[j])
            q, k, v = _mla_proj(h, mix_norm[i], w, cos128, sin128, S)
            o = _mla_attn(q, k, v, B, S)
        else:
            w = _nsa_weights(nsa_w_in[j], nsa_cmp_pos_k[j], nsa_cmp_w1_k[j], nsa_cmp_w2_k[j],
                             nsa_cmp_pos_v[j], nsa_cmp_w1_v[j], nsa_cmp_w2_v[j], nsa_w_o[j])
            qt, kvc_in, kvs, kvw, kvst, kvwt, gates = _nsa_proj(h, mix_norm[i], w, blk_onehot)
            xc = kvc_in.reshape(B, n_chunk, CMP_STRIDE, G, LANE).transpose(0, 3, 1, 2, 4)
            xc = xc.reshape(B, G, n_chunk, CMP_STRIDE * LANE)
            kvc, kvct = _compress(xc, w["pos"], w["w1"], w["w2"])
            o = _nsa_attn(qt, kvc, kvct, kvs, kvst, kvw, kvwt, gates, cbias, dtab, ovt, B, S)
        w_o = mla_w_o if i % N_MIXERS == 0 else nsa_w_o
        h = _ffn(h, ffn_norm_b[i], ffn_b_w_gate, ffn_b_w_up, ffn_b_w_down, i,
                 proj=(o, w_o, j), final_g=final_norm if i == DEPTH - 1 else None)
    return h.reshape(B, S, D)
```

```python
import functools
import math

import numpy as np
import jax
import jax.numpy as jnp
from jax import lax
from jax.experimental import pallas as pl
from jax.experimental.pallas import tpu as pltpu

F32 = jnp.float32
BF16 = jnp.bfloat16

D_MODEL = 1024
DEPTH = 4
N_MIXERS = 2
RMS_EPS = 1e-6
FFN_HIDDEN = 2816
NEG = -1e30
MLA_HEADS = 16
MLA_Q_LORA = 384
MLA_KV_LORA = 256
MLA_NOPE = 64
MLA_ROPE = 32
MLA_V = 64
ROPE_THETA = 10000.0
NSA_HEADS = 16
NSA_GROUPS = 4
NSA_R = NSA_HEADS // NSA_GROUPS
NSA_QK = 64
NSA_V = 64
CMP_BLOCK = 32
CMP_STRIDE = 16
CMP_HIDDEN = 128
SEL_BLOCK = 64
SEL_SHIFT = SEL_BLOCK.bit_length() - 1
assert 1 << SEL_SHIFT == SEL_BLOCK
SEL_TOP_N = 16
WINDOW = 512
REL_BUCKETS = 32
REL_MAX_DIST = 128

LANE = 128
VMEM_LIMIT = 56 * 1024 * 1024

FFN_TM = 512
FFN_TF = 256
PROJ_TM = 256
ATT_T = 256
NSA_T = 256
CB_CENTER = 64


def _cparams(sem):
    return pltpu.CompilerParams(dimension_semantics=sem, vmem_limit_bytes=VMEM_LIMIT)


def _rms(x, g):
    ms = jnp.mean(x * x, axis=-1, keepdims=True)
    return x * lax.rsqrt(ms + RMS_EPS) * g


def _const_spec(shape):
    nd = len(shape)
    return pl.BlockSpec(shape, lambda *_: (0,) * nd)


def _wdot(a, w):
    return lax.dot_general(a, w, (((1,), (0,)), ((), ())), preferred_element_type=F32)


def _ffn_body(*refs, has_proj, has_final):
    it = iter(refs)
    h_ref = next(it)
    if has_proj:
        o_in_ref = next(it)
        wo_ref = next(it)
    g_ref = next(it)
    wg_ref = next(it)
    wu_ref = next(it)
    wd_ref = next(it)
    if has_final:
        gf_ref = next(it)
    out_ref = next(it)
    a_ref = next(it)

    x = h_ref[...]
    if has_proj:
        x = x + _wdot(o_in_ref[...], wo_ref[...])
    xn = _rms(x, g_ref[...]).astype(BF16)
    for c in range(FFN_HIDDEN // FFN_TF):
        sl = slice(c * FFN_TF, (c + 1) * FFN_TF)
        gt = _wdot(xn, wg_ref[:, sl])
        up = _wdot(xn, wu_ref[:, sl])
        a_ref[:, sl] = (gt * jax.nn.sigmoid(gt) * up).astype(BF16)
    y = x + 0.5 * _wdot(a_ref[...], wd_ref[...])
    if has_final:
        y = _rms(y, gf_ref[...])
    out_ref[...] = y


def _layer_spec(stacked, layer):
    return pl.BlockSpec((None,) + stacked.shape[1:], lambda i: (layer, 0, 0), pipeline_mode=pl.Buffered(1))


def _ffn(h, g, wg, wu, wd, layer, proj=None, final_g=None):
    T = h.shape[0]
    tm = FFN_TM
    row = lambda i: (i, 0)
    in_specs = [pl.BlockSpec((tm, D_MODEL), row)]
    args = [h]
    if proj is not None:
        o_in, wo, wo_idx = proj
        in_specs += [pl.BlockSpec((tm, o_in.shape[1]), row), _layer_spec(wo, wo_idx)]
        args += [o_in, wo]
    in_specs += [_const_spec((1, D_MODEL)), _layer_spec(wg, layer), _layer_spec(wu, layer), _layer_spec(wd, layer)]
    args += [g.reshape(1, D_MODEL), wg, wu, wd]
    if final_g is not None:
        in_specs.append(_const_spec((1, D_MODEL)))
        args.append(final_g.reshape(1, D_MODEL))
    return pl.pallas_call(
        functools.partial(_ffn_body, has_proj=proj is not None, has_final=final_g is not None),
        grid=(T // tm,),
        in_specs=in_specs,
        out_specs=pl.BlockSpec((tm, D_MODEL), row),
        out_shape=jax.ShapeDtypeStruct((T, D_MODEL), F32),
        scratch_shapes=[pltpu.VMEM((tm, FFN_HIDDEN), BF16)],
        compiler_params=_cparams(("parallel",)),
        name="ffn",
    )(*args)


MLA_CQ0, MLA_CKV0, MLA_KR0, MLA_KRS0, MLA_IN_W = 0, 384, 640, 768, 896
MLA_HCHUNK = 4


def _with_ones_rows(vt):
    row = lax.broadcasted_iota(jnp.int32, vt.shape, 0)
    return jnp.where(row % LANE == 0, 1.0, vt)


def _mla_proj_body(h_ref, g_ref, win_ref, qn_ref, kvn_ref, wuqt_ref, wuk_ref, wuvt_ref,
                   cos_ref, sin_ref, cost_ref, sinat_ref, sinbt_ref, qt_out, k_out, vt_out):
    xn = _rms(h_ref[...], g_ref[...]).astype(BF16)
    proj = jnp.dot(xn, win_ref[...], preferred_element_type=F32)
    cq = _rms(proj[:, MLA_CQ0:MLA_CKV0], qn_ref[...]).astype(BF16)
    ckv = _rms(proj[:, MLA_CKV0:MLA_KR0], kvn_ref[...]).astype(BF16)
    cos = cos_ref[...]
    sin = sin_ref[...]
    kr = proj[:, MLA_KR0:MLA_KRS0] * cos + proj[:, MLA_KRS0:MLA_IN_W] * sin
    cos_t = cost_ref[...]
    sina_t = sinat_ref[...]
    sinb_t = sinbt_ref[...]
    half = MLA_ROPE // 2
    scale = (MLA_NOPE + MLA_ROPE) ** -0.5 * LOG2E
    vt = lax.dot_general(wuvt_ref[...], ckv, _NT, preferred_element_type=F32)
    vt_out[...] = _with_ones_rows(vt).astype(BF16)
    cw = MLA_HCHUNK * LANE
    for c in range(MLA_HEADS // MLA_HCHUNK):
        sl = slice(c * cw, (c + 1) * cw)
        qt = lax.dot_general(wuqt_ref[sl, :], cq, _NT, preferred_element_type=F32)
        kn = jnp.dot(ckv, wuk_ref[:, sl], preferred_element_type=F32)
        for hh in range(MLA_HCHUNK):
            hs = slice(hh * LANE, (hh + 1) * LANE)
            os_ = slice(c * cw + hh * LANE, c * cw + (hh + 1) * LANE)
            qh = qt[hs]
            roped = (qh * cos_t + pltpu.roll(qh, LANE - half, 0) * sina_t + pltpu.roll(qh, half, 0) * sinb_t)
            qt_out[os_, :] = (roped * scale).astype(BF16)
            k_out[:, os_] = (kn[:, hs] + kr).astype(BF16)


def _mla_proj(h, g, w, cos128, sin128, S):
    T = h.shape[0]
    tm = PROJ_TM
    ns = S // tm
    row = lambda i: (i, 0)
    col = lambda i: (0, i)
    pos = lambda i: (i % ns, 0)
    pos_t = lambda i: (0, i % ns)
    HL = MLA_HEADS * LANE
    HV = MLA_HEADS * LANE
    lane = np.arange(LANE)[None, :]
    x1 = (lane >= MLA_NOPE) & (lane < MLA_NOPE + MLA_ROPE // 2)
    x2 = (lane >= MLA_NOPE + MLA_ROPE // 2) & (lane < MLA_NOPE + MLA_ROPE)
    sign_a, sign_b = -x1.astype(np.float32), x2.astype(np.float32)
    return pl.pallas_call(
        _mla_proj_body,
        grid=(T // tm,),
        in_specs=[pl.BlockSpec((tm, D_MODEL), row), _const_spec((1, D_MODEL)),
                  _const_spec(w["w_in"].shape), _const_spec((1, MLA_Q_LORA)), _const_spec((1, MLA_KV_LORA)),
                  _const_spec(w["w_uqt"].shape),
                  _const_spec(w["w_uk"].shape), _const_spec(w["w_uvt"].shape),
                  pl.BlockSpec((tm, LANE), pos), pl.BlockSpec((tm, LANE), pos),
                  pl.BlockSpec((LANE, tm), pos_t), pl.BlockSpec((LANE, tm), pos_t), pl.BlockSpec((LANE, tm), pos_t)],
        out_specs=[pl.BlockSpec((HL, tm), col), pl.BlockSpec((tm, HL), row), pl.BlockSpec((HV, tm), col)],
        out_shape=[jax.ShapeDtypeStruct((HL, T), BF16), jax.ShapeDtypeStruct((T, HL), BF16),
                   jax.ShapeDtypeStruct((HV, T), BF16)],
        compiler_params=_cparams(("parallel",)),
        name="mla_proj",
    )(h, g.reshape(1, D_MODEL), w["w_in"], w["q_norm"], w["kv_norm"], w["w_uqt"],
      w["w_uk"], w["w_uvt"], cos128, sin128, cos128.T, (sin128 * sign_a).T, (sin128 * sign_b).T)


_NT = (((1,), (1,)), ((), ()))
SCORE_CHUNK = 2 * LANE


def _chunks(n, lead=0):
    out = [(0, lead)] if lead else []
    return out + [(o, min(SCORE_CHUNK, n - o)) for o in range(lead, n, SCORE_CHUNK)]


SUBLANE = 8


def _fold(x, op):
    w, cols = x.shape
    return op(x.reshape(w // SUBLANE, SUBLANE, cols), axis=0)


class _SoftmaxStreamT:
    def __init__(self, score_fn, chunks, values_fn, s_ref):
        self.score_fn, self.chunks, self.values_fn, self.s_ref = score_fn, chunks, values_fn, s_ref

    def pass1(self):
        mp = None
        for off, w in self.chunks:
            s = self.score_fn(off, w)
            self.s_ref[off:off + w, :] = s
            part = _fold(s, jnp.max)
            mp = part if mp is None else jnp.maximum(mp, part)
            yield
        self.m = jnp.max(mp, axis=0, keepdims=True)

    def pass2(self):
        acc = None
        for off, w in self.chunks:
            p = jnp.exp2(self.s_ref[off:off + w, :] - self.m)
            pv = jnp.dot(self.values_fn(off, w), p.astype(BF16), preferred_element_type=F32)
            acc = pv if acc is None else acc + pv
            yield
        self.out = acc / acc[0:1, :]


def _trace_pipelined(streams):
    for _ in streams[0].pass1():
        pass
    for i, st in enumerate(streams):
        gens = [st.pass2()] + ([streams[i + 1].pass1()] if i + 1 < len(streams) else [])
        while gens:
            gens = [g for g in gens if next(g, StopIteration) is not StopIteration]


MLA_HPS = 8


def _mla_attn_body(qt_ref, k_ref, vt_ref, o_ref, *s_refs, nq):
    t = ATT_T
    qi = pl.program_id(2)

    def branch(nt):
        n = nt * t
        causal = lax.broadcasted_iota(jnp.int32, (t, t), 0) <= lax.broadcasted_iota(jnp.int32, (t, t), 1)
        streams = []
        for hh in range(MLA_HPS):
            hs = slice(hh * LANE, (hh + 1) * LANE)
            qt = qt_ref[hs, :]

            def score(off, w, qt=qt, hs=hs):
                s = jnp.dot(k_ref[off:off + w, hs], qt, preferred_element_type=F32)
                return jnp.where(causal, s, NEG) if off == n - t else s

            values = lambda off, w, hs=hs: vt_ref[hs, off:off + w]
            streams.append(_SoftmaxStreamT(score, _chunks(n), values, s_refs[hh]))
        _trace_pipelined(streams)
        outs = [st.out[LANE - MLA_V:] for st in streams]
        o_ref[...] = jnp.concatenate(outs, axis=0).T.astype(BF16)

    for nt in range(1, nq + 1):
        pl.when(qi == nt - 1)(functools.partial(branch, nt))


def _mla_attn(qt, k, vt, B, S):
    t = ATT_T
    nq = S // t
    T = B * S
    hps = MLA_HPS
    return pl.pallas_call(
        functools.partial(_mla_attn_body, nq=nq),
        grid=(B, MLA_HEADS // hps, nq),
        in_specs=[pl.BlockSpec((hps * LANE, t), lambda b, p, i: (p, b * nq + i)),
                  pl.BlockSpec((S, hps * LANE), lambda b, p, i: (b, p)),
                  pl.BlockSpec((hps * LANE, S), lambda b, p, i: (p, b))],
        out_specs=pl.BlockSpec((t, hps * MLA_V), lambda b, p, i: (b * nq + i, p)),
        out_shape=jax.ShapeDtypeStruct((T, MLA_HEADS * MLA_V), BF16),
        scratch_shapes=[pltpu.VMEM((S, t), F32)] * hps,
        compiler_params=_cparams(("parallel", "parallel", "arbitrary")),
        name="mla_attn",
    )(qt, k, vt)


NSA_QW = NSA_HEADS * LANE
NSA_KVW = NSA_GROUPS * LANE
NSA_IN_W = NSA_QW + 3 * NSA_KVW + LANE
GATES_PER_GROUP = 3 * NSA_R
RANK_STEPS_PER_STAGE = 8
LOG2E = math.log2(math.e)


def _nsa_proj_body(h_ref, g_ref, wqt_ref, wkv_ref, wkvt_ref, wgt_ref, blk_ref,
                   qt_out, kvc_out, kvs_out, kvw_out, kvst_out, kvwt_out, gate_out):
    xn = _rms(h_ref[...], g_ref[...]).astype(BF16)
    scale = NSA_QK ** -0.5 * LOG2E
    cw = 4 * LANE
    for c in range(NSA_QW // cw):
        sl = slice(c * cw, (c + 1) * cw)
        qt = lax.dot_general(wqt_ref[sl, :], xn, _NT, preferred_element_type=F32)
        qt_out[sl, :] = (qt * scale).astype(BF16)
    for n, out in enumerate((kvc_out, kvs_out, kvw_out)):
        sl = slice(n * NSA_KVW, (n + 1) * NSA_KVW)
        kv = jnp.dot(xn, wkv_ref[:, sl], preferred_element_type=F32)
        if out is kvs_out:
            kv = kv + jnp.concatenate([blk_ref[...]] * NSA_GROUPS, axis=1)
        out[...] = kv.astype(BF16)
    for n, out in enumerate((kvst_out, kvwt_out)):
        sl = slice(n * NSA_KVW, (n + 1) * NSA_KVW)
        vt = lax.dot_general(wkvt_ref[sl, :], xn, _NT, preferred_element_type=F32)
        out[...] = _with_ones_rows(vt).astype(BF16)
    gl = lax.dot_general(wgt_ref[...], xn, _NT, preferred_element_type=F32)
    gate_out[...] = jax.nn.sigmoid(gl)


def _nsa_proj(h, g, w, blk_onehot):
    T = h.shape[0]
    tm = PROJ_TM
    ns = blk_onehot.shape[0] // tm
    row = lambda i: (i, 0)
    col = lambda i: (0, i)
    return pl.pallas_call(
        _nsa_proj_body,
        grid=(T // tm,),
        in_specs=[pl.BlockSpec((tm, D_MODEL), row), _const_spec((1, D_MODEL)), _const_spec(w["w_qt"].shape),
                  _const_spec(w["w_kv"].shape), _const_spec(w["w_kvt"].shape), _const_spec(w["w_gt"].shape),
                  pl.BlockSpec((tm, LANE), lambda i: (i % ns, 0))],
        out_specs=[pl.BlockSpec((NSA_QW, tm), col), pl.BlockSpec((tm, NSA_KVW), row),
                   pl.BlockSpec((tm, NSA_KVW), row), pl.BlockSpec((tm, NSA_KVW), row),
                   pl.BlockSpec((NSA_KVW, tm), col), pl.BlockSpec((NSA_KVW, tm), col),
                   pl.BlockSpec((LANE, tm), col)],
        out_shape=[jax.ShapeDtypeStruct((NSA_QW, T), BF16), jax.ShapeDtypeStruct((T, NSA_KVW), BF16),
                   jax.ShapeDtypeStruct((T, NSA_KVW), BF16), jax.ShapeDtypeStruct((T, NSA_KVW), BF16),
                   jax.ShapeDtypeStruct((NSA_KVW, T), BF16), jax.ShapeDtypeStruct((NSA_KVW, T), BF16),
                   jax.ShapeDtypeStruct((LANE, T), F32)],
        compiler_params=_cparams(("parallel",)),
        name="nsa_proj",
    )(h, g.reshape(1, D_MODEL), w["w_qt"], w["w_kv"], w["w_kvt"], w["w_gt"], blk_onehot)


def _compress_body(x_ref, pos_ref, w1_ref, w2_ref, out_ref, out_t_ref):
    n_chunk = x_ref.shape[0]
    x = x_ref[...].astype(F32)
    xa = (x + pos_ref[0]).astype(BF16)
    xb = (x + pos_ref[1]).astype(BF16)
    a = jnp.dot(xa, w1_ref[0], preferred_element_type=F32)
    b = jnp.dot(xb, w1_ref[1], preferred_element_type=F32)
    pre = a + pltpu.roll(b, n_chunk - 1, 0)
    hid = jax.nn.gelu(pre, approximate=True).astype(BF16)
    kv = jnp.dot(hid, w2_ref[...], preferred_element_type=F32)
    out_ref[...] = kv.astype(BF16)
    out_t_ref[...] = kv.T.astype(BF16)


def _compress(x, pos, w1, w2):
    B, G, n_chunk, width = x.shape
    return pl.pallas_call(
        _compress_body,
        grid=(B, G),
        in_specs=[pl.BlockSpec((None, None, n_chunk, width), lambda b, g: (b, g, 0, 0)),
                  _const_spec(pos.shape), _const_spec(w1.shape), _const_spec(w2.shape)],
        out_specs=[pl.BlockSpec((None, None, n_chunk, LANE), lambda b, g: (b, g, 0, 0)),
                   pl.BlockSpec((None, None, LANE, n_chunk), lambda b, g: (b, g, 0, 0))],
        out_shape=[jax.ShapeDtypeStruct((B, G, n_chunk, LANE), BF16),
                   jax.ShapeDtypeStruct((B, G, LANE, n_chunk), BF16)],
        compiler_params=_cparams(("parallel", "parallel")),
        name="nsa_compress",
    )(x, pos, w1, w2)


def _nsa_attn_body(*refs, S):
    step = pl.program_id(2)
    for qi in range(S // NSA_T):
        pl.when(step == qi)(functools.partial(_nsa_tile, qi, *refs, S=S))


def _nsa_tile(qi, qt_ref, qtn_ref, kvc_ref, kvct_ref, kvs_ref, kvst_ref, kvw_ref, kvwt_ref, gate_ref, cbt_ref,
              dt_ref, ovt_ref, o_ref, *scratch, S):
    t = NSA_T
    R = NSA_R
    n_sel = S // SEL_BLOCK
    t0 = qi * t
    s_sel, s_win = scratch[0:2], scratch[2:4]
    ocmp_scr, sel_scr = scratch[4:6]
    heads_on_lanes = lambda ref: jnp.concatenate([ref[r * LANE:(r + 1) * LANE, :] for r in range(R)], axis=1)
    qt = heads_on_lanes(qt_ref)
    tile_r = lambda x: jnp.concatenate([x] * R, axis=1)
    half_w = 2 * t
    tile_h = lambda x: jnp.concatenate([x] * (half_w // t), axis=1)

    def compressed_and_selection(q_tile, tile, out):
        kvc = kvc_ref[...]
        n_cp = kvc.shape[0]
        sc = jnp.dot(kvc, q_tile, preferred_element_type=F32)
        yield
        cend = lax.broadcasted_iota(jnp.int32, (n_cp, t), 0) * CMP_STRIDE + (CMP_BLOCK - 1)
        valid = tile_r(tile * t + lax.broadcasted_iota(jnp.int32, (n_cp, t), 1) >= cend)
        shift = (tile * (t // CMP_STRIDE) + CB_CENTER) % n_cp
        cb = pltpu.roll(cbt_ref[...], shift, 0) if shift else cbt_ref[...]
        sc = jnp.where(valid, sc + cb, NEG)
        e = jnp.exp2(sc - jnp.max(sc, axis=0, keepdims=True))
        yield
        p = jnp.where(valid, e / jnp.sum(e, axis=0, keepdims=True), 0.0)
        out["o_cmp"] = jnp.dot(kvct_ref[...], p.astype(BF16), preferred_element_type=F32)
        yield

        psum = p[:, 0:t]
        for r in range(1, R):
            psum = psum + p[:, r * t:(r + 1) * t]
        imp = jnp.dot(ovt_ref[...], psum, preferred_element_type=F32, precision=lax.Precision.HIGHEST)
        yield
        jj = lax.broadcasted_iota(jnp.int32, (n_sel, t), 0)
        blk_t = (tile * t + lax.broadcasted_iota(jnp.int32, (n_sel, t), 1)) >> SEL_SHIFT
        forced = (jj == 0) | (jj == blk_t) | (jj == blk_t - 1)
        score = jnp.where(forced, 1e6, jnp.where(jj <= blk_t, imp, -1e6))
        cnt = jnp.zeros((n_sel, t), jnp.int32)
        for jp in range(n_sel):
            row = score[jp:jp + 1, :]
            beats = (row > score) | ((row == score) & (jj > jp))
            cnt = cnt + beats.astype(jnp.int32)
            if jp % RANK_STEPS_PER_STAGE == RANK_STEPS_PER_STAGE - 1:
                yield
        out["sel"] = jnp.where(cnt < min(SEL_TOP_N, n_sel), 0.0, NEG).astype(BF16)

    if qi == 0:
        cur = {}
        for _ in compressed_and_selection(qt, 0, cur):
            pass
        o_cmp, sel = cur["o_cmp"], cur["sel"]
    else:
        o_cmp, sel = ocmp_scr[...], sel_scr[...]
    nxt, side = {}, []
    if (qi + 1) * t < S:
        side = [compressed_and_selection(heads_on_lanes(qtn_ref), qi + 1, nxt)]

    kw = min(qi, WINDOW // t)
    nw = (kw + 1) * t
    w0 = (qi - kw) * t
    n = (qi + 1) * t

    def win_add(off, w):
        lo_d, hi_d = kw * t - off - (w - 1), kw * t - off + (t - 1)
        if lo_d >= 0 and hi_d < WINDOW:
            return None
        dist = (kw * t - off + lax.broadcasted_iota(jnp.int32, (w, t), 1)
                - lax.broadcasted_iota(jnp.int32, (w, t), 0))
        return jnp.where((dist >= 0) & (dist < WINDOW), 0.0, NEG)

    q_sel = jnp.concatenate([qt[0:NSA_QK], tile_r(sel), jnp.zeros((LANE - NSA_QK - n_sel, R * t), BF16)], axis=0)

    def sel_add(off, w):
        if off + w <= n - t:
            return None
        kpos = off + lax.broadcasted_iota(jnp.int32, (w, t), 0)
        return jnp.where(kpos <= t0 + lax.broadcasted_iota(jnp.int32, (w, t), 1), 0.0, NEG)

    def stream(half, q_all, k_ref, vt_ref, k0, nk, add_fn, s_ref):
        cs = slice(half * half_w, (half + 1) * half_w)
        qh = q_all[:, cs]

        def score(off, w):
            s = jnp.dot(k_ref[k0 + off:k0 + off + w, :], qh, preferred_element_type=F32)
            add = add_fn(off, w)
            if add is not None:
                s = s + tile_h(add)
            d0 = off - (nk - 2 * t)
            if d0 + w > 0:
                assert d0 >= 0
                s = s + dt_ref[d0:d0 + w, cs]
            return s

        values = lambda off, w: vt_ref[:, k0 + off:k0 + off + w]
        return _SoftmaxStreamT(score, _chunks(nk, lead=nk % SCORE_CHUNK), values, s_ref)

    halves = range(R * t // half_w)
    wins = [stream(h, qt, kvw_ref, kvwt_ref, w0, nw, win_add, s_win[h]) for h in halves]
    sels = [stream(h, q_sel, kvs_ref, kvst_ref, 0, n, sel_add, s_sel[h]) for h in halves]
    _trace_pipelined(wins[:1] + sels + wins[1:])
    for gen in side:
        for _ in gen:
            pass

    g = pl.program_id(0)
    heads = []
    for r in range(R):
        h, hc = divmod(r * t, half_w)
        branches = (o_cmp[NSA_QK:, r * t:(r + 1) * t], sels[h].out[NSA_QK:, hc:hc + t], wins[h].out[NSA_QK:, hc:hc + t])
        o = None
        for br, ob in enumerate(branches):
            gate = gate_ref[pl.ds(g * GATES_PER_GROUP + br * R + r, 1), :]
            o = gate * ob if o is None else o + gate * ob
        heads.append(o)
    o_ref[...] = jnp.concatenate(heads, axis=0).T.astype(BF16)
    if nxt:
        ocmp_scr[...] = nxt["o_cmp"]
        sel_scr[...] = nxt["sel"]


def _nsa_attn(qt, kvc, kvct, kvs, kvst, kvw, kvwt, gates, cbt, dtt, ovt, B, S):
    t = NSA_T
    nq = S // t
    T = B * S
    G, R = NSA_GROUPS, NSA_R
    n_cp = kvc.shape[2]
    half_w = 2 * t
    nh = R * t // half_w
    kv_spec = pl.BlockSpec((S, LANE), lambda g, b, i: (b, g))
    kvt_spec = pl.BlockSpec((LANE, S), lambda g, b, i: (g, b))
    return pl.pallas_call(
        functools.partial(_nsa_attn_body, S=S),
        grid=(G, B, nq),
        in_specs=[pl.BlockSpec((R * LANE, t), lambda g, b, i: (g, b * nq + i)),
                  pl.BlockSpec((R * LANE, t), lambda g, b, i: (g, b * nq + jnp.minimum(i + 1, nq - 1))),
                  pl.BlockSpec((None, None, n_cp, LANE), lambda g, b, i: (b, g, 0, 0)),
                  pl.BlockSpec((None, None, LANE, n_cp), lambda g, b, i: (b, g, 0, 0)),
                  kv_spec, kvt_spec, kv_spec, kvt_spec,
                  pl.BlockSpec((LANE, t), lambda g, b, i: (0, b * nq + i)),
                  pl.BlockSpec((None, n_cp, R * t), lambda g, b, i: (g, 0, 0)),
                  pl.BlockSpec((None, 2 * t, R * t), lambda g, b, i: (g, 0, 0)),
                  _const_spec(ovt.shape)],
        out_specs=pl.BlockSpec((t, R * NSA_V), lambda g, b, i: (b * nq + i, g)),
        out_shape=jax.ShapeDtypeStruct((T, NSA_HEADS * NSA_V), BF16),
        scratch_shapes=([pltpu.VMEM((S, half_w), F32)] * nh + [pltpu.VMEM((WINDOW + t, half_w), F32)] * nh
                        + [pltpu.VMEM((LANE, R * t), F32), pltpu.VMEM((S // SEL_BLOCK, t), BF16)]),
        compiler_params=_cparams(("arbitrary", "arbitrary", "arbitrary")),
        name="nsa_attn",
    )(qt, qt, kvc, kvct, kvs, kvst, kvw, kvwt, gates, cbt, dtt, ovt)


def _t5_bucket(dist):
    n = jnp.maximum(dist, 0)
    max_exact = REL_BUCKETS // 2
    nf = jnp.maximum(n, 1).astype(F32)
    large = max_exact + (jnp.log(nf / max_exact) / math.log(REL_MAX_DIST / max_exact)
                         * (REL_BUCKETS - max_exact)).astype(jnp.int32)
    large = jnp.minimum(large, REL_BUCKETS - 1)
    return jnp.where(n < max_exact, n, large)


def _np_bucket(n):
    n = np.maximum(np.asarray(n), 0)
    max_exact = REL_BUCKETS // 2
    large = max_exact + (np.log(np.maximum(n, 1) / max_exact) / math.log(REL_MAX_DIST / max_exact)
                         * (REL_BUCKETS - max_exact)).astype(np.int64)
    return np.where(n < max_exact, n, np.minimum(large, REL_BUCKETS - 1))


def _bias_tables(rel_bias, S):
    t = NSA_T
    G, R = NSA_GROUPS, NSA_R
    n_cp = S // CMP_STRIDE
    a = np.arange(t)[:, None]
    dist_d = a - np.arange(2 * t)[None, :] + t
    dist_c = a - CMP_STRIDE * (np.arange(n_cp)[None, :] - CB_CENTER) - (CMP_BLOCK - 1)
    uncovered = min(t + 1, CMP_STRIDE * (CB_CENTER + 1) - (CMP_BLOCK - 1))
    assert (_np_bucket(np.arange(uncovered, 2 * S)) == REL_BUCKETS - 1).all()
    wrap_from = n_cp - max((S // t - 1) * (t // CMP_STRIDE) - CB_CENTER, 0)
    assert (dist_c[:, wrap_from:] < 0).all() and (dist_c[:, -1] < 0).all()

    def lookup(dist):
        oh = jax.nn.one_hot(_t5_bucket(jnp.asarray(dist)), REL_BUCKETS, dtype=F32)
        oh = oh - jax.nn.one_hot(REL_BUCKETS - 1, REL_BUCKETS, dtype=F32)
        val = jnp.einsum("acb,bh->hac", oh, rel_bias, precision=lax.Precision.HIGHEST)
        return val.reshape(G, R, *dist.shape)

    dtab = lookup(dist_d)
    cbias = jnp.where(jnp.asarray(dist_c >= 0), lookup(dist_c), 0.0)
    keys_first = lambda x: x.transpose(0, 3, 1, 2).reshape(G, x.shape[3], R * t)
    return keys_first(dtab) * LOG2E, keys_first(cbias) * LOG2E


def _selection_tables(S):
    n_cp = S // CMP_STRIDE
    n_cmp = (S - CMP_BLOCK) // CMP_STRIDE + 1
    n_sel = S // SEL_BLOCK
    cs = np.arange(n_cp) * CMP_STRIDE
    ce = cs + CMP_BLOCK
    ss = np.arange(n_sel) * SEL_BLOCK
    se = ss + SEL_BLOCK
    ov = np.minimum(ce[:, None], se[None, :]) - np.maximum(cs[:, None], ss[None, :])
    ov = (np.clip(ov, 0, None) / CMP_BLOCK).astype(np.float32)
    ov[n_cmp:] = 0.0
    assert n_sel <= LANE - NSA_QK
    blk_onehot = (np.arange(S)[:, None] // SEL_BLOCK == np.arange(LANE)[None, :] - NSA_QK).astype(np.float32)
    return jnp.asarray(ov.T), jnp.asarray(blk_onehot)


def _rope_tables(S):
    half = MLA_ROPE // 2
    inv = ROPE_THETA ** (-jnp.arange(half, dtype=F32) * 2.0 / MLA_ROPE)
    ang = jnp.arange(S, dtype=F32)[:, None] * inv[None, :]
    cos, sin = jnp.cos(ang), jnp.sin(ang)
    ones = jnp.ones((S, MLA_NOPE), F32)
    pad1 = jnp.ones((S, LANE - MLA_NOPE - MLA_ROPE), F32)
    cos128 = jnp.concatenate([ones, cos, cos, pad1], axis=1)
    sin128 = jnp.concatenate([0 * ones, sin, sin, 0 * pad1], axis=1)
    return cos128, sin128


def _mla_weights(w_in, q_norm, kv_norm, w_uq, w_ukv, w_o):
    H = MLA_HEADS
    half = MLA_ROPE // 2
    pad = LANE - MLA_NOPE - MLA_ROPE
    kr = w_in[:, MLA_Q_LORA + MLA_KV_LORA:]
    kr_sw = jnp.concatenate([-kr[:, half:], kr[:, :half]], axis=1)
    z = lambda n: jnp.zeros((D_MODEL, n), F32)
    w_in_ext = jnp.concatenate([w_in[:, :MLA_Q_LORA + MLA_KV_LORA],
                                z(MLA_NOPE), kr, z(pad), z(MLA_NOPE), kr_sw, z(pad)], axis=1)
    uq = w_uq.reshape(MLA_Q_LORA, H, MLA_NOPE + MLA_ROPE)
    qn, qr = uq[..., :MLA_NOPE], uq[..., MLA_NOPE:]
    zq = jnp.zeros((MLA_Q_LORA, H, pad), F32)
    w_uq_p = jnp.concatenate([qn, qr, zq], axis=-1).reshape(MLA_Q_LORA, H * LANE)
    ukv = w_ukv.reshape(MLA_KV_LORA, H, MLA_NOPE + MLA_V)
    zk = jnp.zeros((MLA_KV_LORA, H, LANE - MLA_NOPE), F32)
    w_uk = jnp.concatenate([ukv[..., :MLA_NOPE], zk], axis=-1).reshape(MLA_KV_LORA, H * LANE)
    zv = jnp.zeros((MLA_KV_LORA, H, LANE - MLA_V), F32)
    w_uv = jnp.concatenate([zv, ukv[..., MLA_NOPE:]], axis=-1).reshape(MLA_KV_LORA, H * LANE)
    return dict(w_in=w_in_ext.astype(BF16), q_norm=q_norm.reshape(1, -1), kv_norm=kv_norm.reshape(1, -1),
                w_uqt=w_uq_p.T.astype(BF16), w_uk=w_uk.astype(BF16),
                w_uvt=w_uv.T.astype(BF16), w_o=w_o.astype(BF16))


def _nsa_weights(w_in, pos_k, w1_k, w2_k, pos_v, w1_v, w2_v, w_o):
    H, G = NSA_HEADS, NSA_GROUPS
    gw = G * NSA_QK
    q0 = H * NSA_QK
    wq = w_in[:, :q0].reshape(D_MODEL, H, NSA_QK)
    wq = jnp.concatenate([wq, jnp.zeros_like(wq)], axis=-1).reshape(D_MODEL, H * LANE)
    cols, cols_t = [], []
    for n in range(3):
        k = w_in[:, q0 + (2 * n) * gw: q0 + (2 * n + 1) * gw].reshape(D_MODEL, G, NSA_QK)
        v = w_in[:, q0 + (2 * n + 1) * gw: q0 + (2 * n + 2) * gw].reshape(D_MODEL, G, NSA_V)
        right = v if n == 0 else jnp.zeros_like(v)
        cols.append(jnp.concatenate([k, right], axis=-1).reshape(D_MODEL, G * LANE))
        if n > 0:
            cols_t.append(jnp.concatenate([jnp.zeros_like(k), v], axis=-1).reshape(D_MODEL, G * LANE))
    w_kv = jnp.concatenate(cols, axis=1)
    w_vt = jnp.concatenate(cols_t, axis=1).T
    wg = w_in[:, q0 + 6 * gw:].reshape(D_MODEL, G, NSA_R, 3).transpose(0, 1, 3, 2).reshape(D_MODEL, 3 * H)
    wg = jnp.concatenate([wg, jnp.zeros((D_MODEL, LANE - 3 * H), F32)], axis=1)
    eye = jnp.eye(2, dtype=F32)
    cw = CMP_STRIDE * 2 * NSA_QK
    pos = jnp.stack([pos_k, pos_v]).reshape(2, 2, CMP_STRIDE, NSA_QK)
    pos = pos.transpose(1, 2, 0, 3).reshape(2, 1, cw)
    w1 = jnp.stack([w1_k, w1_v]).reshape(2, 2, CMP_STRIDE, NSA_QK, CMP_HIDDEN)
    w1 = jnp.einsum("khldj,kq->hlkdqj", w1, eye).reshape(2, cw, 2 * CMP_HIDDEN).astype(BF16)
    w2 = jnp.einsum("kjd,kq->kjqd", jnp.stack([w2_k, w2_v]), eye).reshape(2 * CMP_HIDDEN, 2 * NSA_QK).astype(BF16)
    return dict(w_qt=wq.T.astype(BF16), w_kv=w_kv.astype(BF16), w_kvt=w_vt.astype(BF16),
                w_gt=wg.T.astype(BF16), pos=pos, w1=w1, w2=w2, w_o=w_o.astype(BF16))


def kernel(x, ffn_norm_a, ffn_a_w_gate, ffn_a_w_up, ffn_a_w_down, mix_norm, ffn_norm_b, ffn_b_w_gate, ffn_b_w_up, ffn_b_w_down, final_norm, rel_bias, mla_w_in, mla_q_norm, mla_kv_norm, mla_w_uq, mla_w_ukv, mla_w_o, nsa_w_in, nsa_cmp_pos_k, nsa_cmp_w1_k, nsa_cmp_w2_k, nsa_cmp_pos_v, nsa_cmp_w1_v, nsa_cmp_w2_v, nsa_w_o):
    B, S, D = x.shape
    assert D == D_MODEL and S % ATT_T == 0 and S % NSA_T == 0 and (B * S) % FFN_TM == 0
    T = B * S
    G = NSA_GROUPS
    n_chunk = S // CMP_STRIDE
    cos128, sin128 = _rope_tables(S)
    dtab, cbias = _bias_tables(rel_bias, S)
    ovt, blk_onehot = _selection_tables(S)

    h = x.reshape(T, D)
    for i in range(DEPTH):
        h = _ffn(h, ffn_norm_a[i], ffn_a_w_gate, ffn_a_w_up, ffn_a_w_down, i)
        j = i // N_MIXERS
        if i % N_MIXERS == 0:
            w = _mla_weights(mla_w_in[j], mla_q_norm[j], mla_kv_norm[j], mla_w_uq[j], mla_w_ukv[j], mla_w_o[j])
            q, k, v = _mla_proj(h, mix_norm[i], w, cos128, sin128, S)
            o = _mla_attn(q, k, v, B, S)
        else:
            w = _nsa_weights(nsa_w_in[j], nsa_cmp_pos_k[j], nsa_cmp_w1_k[j], nsa_cmp_w2_k[j],
                             nsa_cmp_pos_v[j], nsa_cmp_w1_v[j], nsa_cmp_w2_v[j], nsa_w_o[j])
            qt, kvc_in, kvs, kvw, kvst, kvwt, gates = _nsa_proj(h, mix_norm[i], w, blk_onehot)
            xc = kvc_in.reshape(B, n_chunk, CMP_STRIDE, G, LANE).transpose(0, 3, 1, 2, 4)
            xc = xc.reshape(B, G, n_chunk, CMP_STRIDE * LANE)
            kvc, kvct = _compress(xc, w["pos"], w["w1"], w["w2"])
            o = _nsa_attn(qt, kvc, kvct, kvs, kvst, kvw, kvwt, gates, cbias, dtab, ovt, B, S)
        w_o = mla_w_o if i % N_MIXERS == 0 else nsa_w_o
        h = _ffn(h, ffn_norm_b[i], ffn_b_w_gate, ffn_b_w_up, ffn_b_w_down, i,
                 proj=(o, w_o, j), final_g=final_norm if i == DEPTH - 1 else None)
    return h.reshape(B, S, D)
```

```python
import functools
import math

import numpy as np
import jax
import jax.numpy as jnp
from jax import lax
from jax.experimental import pallas as pl
from jax.experimental.pallas import tpu as pltpu

F32 = jnp.float32
BF16 = jnp.bfloat16

D_MODEL = 1024
DEPTH = 4
N_MIXERS = 2
RMS_EPS = 1e-6
FFN_HIDDEN = 2816
NEG = -1e30
MLA_HEADS = 16
MLA_Q_LORA = 384
MLA_KV_LORA = 256
MLA_NOPE = 64
MLA_ROPE = 32
MLA_V = 64
ROPE_THETA = 10000.0
NSA_HEADS = 16
NSA_GROUPS = 4
NSA_R = NSA_HEADS // NSA_GROUPS
NSA_QK = 64
NSA_V = 64
CMP_BLOCK = 32
CMP_STRIDE = 16
CMP_HIDDEN = 128
SEL_BLOCK = 64
SEL_SHIFT = SEL_BLOCK.bit_length() - 1
assert 1 << SEL_SHIFT == SEL_BLOCK
SEL_TOP_N = 16
WINDOW = 512
REL_BUCKETS = 32
REL_MAX_DIST = 128

LANE = 128
VMEM_LIMIT = 56 * 1024 * 1024

FFN_TM = 512
FFN_TF = 256
PROJ_TM = 256
ATT_T = 256
NSA_T = 256
CB_CENTER = 64


def _cparams(sem):
    return pltpu.CompilerParams(dimension_semantics=sem, vmem_limit_bytes=VMEM_LIMIT)


def _rms(x, g):
    ms = jnp.mean(x * x, axis=-1, keepdims=True)
    return x * lax.rsqrt(ms + RMS_EPS) * g


def _const_spec(shape):
    nd = len(shape)
    return pl.BlockSpec(shape, lambda *_: (0,) * nd)


def _wdot(a, w):
    return lax.dot_general(a, w, (((1,), (0,)), ((), ())), preferred_element_type=F32)


def _ffn_body(*refs, has_proj, has_final):
    it = iter(refs)
    h_ref = next(it)
    if has_proj:
        o_in_ref = next(it)
        wo_ref = next(it)
    g_ref = next(it)
    wg_ref = next(it)
    wu_ref = next(it)
    wd_ref = next(it)
    if has_final:
        gf_ref = next(it)
    out_ref = next(it)
    a_ref = next(it)

    x = h_ref[...]
    if has_proj:
        x = x + _wdot(o_in_ref[...], wo_ref[...])
    xn = _rms(x, g_ref[...]).astype(BF16)
    for c in range(FFN_HIDDEN // FFN_TF):
        sl = slice(c * FFN_TF, (c + 1) * FFN_TF)
        gt = _wdot(xn, wg_ref[:, sl])
        up = _wdot(xn, wu_ref[:, sl])
        a_ref[:, sl] = (gt * jax.nn.sigmoid(gt) * up).astype(BF16)
    y = x + 0.5 * _wdot(a_ref[...], wd_ref[...])
    if has_final:
        y = _rms(y, gf_ref[...])
    out_ref[...] = y


def _layer_spec(stacked, layer):
    return pl.BlockSpec((None,) + stacked.shape[1:], lambda i: (layer, 0, 0), pipeline_mode=pl.Buffered(1))


def _ffn(h, g, wg, wu, wd, layer, proj=None, final_g=None):
    T = h.shape[0]
    tm = FFN_TM
    row = lambda i: (i, 0)
    in_specs = [pl.BlockSpec((tm, D_MODEL), row)]
    args = [h]
    if proj is not None:
        o_in, wo, wo_idx = proj
        in_specs += [pl.BlockSpec((tm, o_in.shape[1]), row), _layer_spec(wo, wo_idx)]
        args += [o_in, wo]
    in_specs += [_const_spec((1, D_MODEL)), _layer_spec(wg, layer), _layer_spec(wu, layer), _layer_spec(wd, layer)]
    args += [g.reshape(1, D_MODEL), wg, wu, wd]
    if final_g is not None:
        in_specs.append(_const_spec((1, D_MODEL)))
        args.append(final_g.reshape(1, D_MODEL))
    return pl.pallas_call(
        functools.partial(_ffn_body, has_proj=proj is not None, has_final=final_g is not None),
        grid=(T // tm,),
        in_specs=in_specs,
        out_specs=pl.BlockSpec((tm, D_MODEL), row),
        out_shape=jax.ShapeDtypeStruct((T, D_MODEL), F32),
        scratch_shapes=[pltpu.VMEM((tm, FFN_HIDDEN), BF16)],
        compiler_params=_cparams(("parallel",)),
        name="ffn",
    )(*args)


MLA_CQ0, MLA_CKV0, MLA_KR0, MLA_KRS0, MLA_IN_W = 0, 384, 640, 768, 896
MLA_HCHUNK = 4


def _store_value_tiles(out_ref, vt, n_tiles):
    d, tm = vt.shape[0] // n_tiles, vt.shape[1]
    pad = jnp.where(lax.broadcasted_iota(jnp.int32, (LANE - d, tm), 0) == 0, 1.0, 0.0).astype(out_ref.dtype)
    for i in range(n_tiles):
        out_ref[i * LANE:i * LANE + LANE - d, :] = pad
        out_ref[i * LANE + LANE - d:(i + 1) * LANE, :] = vt[i * d:(i + 1) * d].astype(out_ref.dtype)


def _mla_proj_body(h_ref, g_ref, win_ref, qn_ref, kvn_ref, wuqt_ref, wuk_ref, wuvt_ref,
                   cos_ref, sin_ref, cost_ref, sinat_ref, sinbt_ref, qt_out, k_out, vt_out):
    xn = _rms(h_ref[...], g_ref[...]).astype(BF16)
    proj = jnp.dot(xn, win_ref[...], preferred_element_type=F32)
    cq = _rms(proj[:, MLA_CQ0:MLA_CKV0], qn_ref[...]).astype(BF16)
    ckv = _rms(proj[:, MLA_CKV0:MLA_KR0], kvn_ref[...]).astype(BF16)
    cos = cos_ref[...]
    sin = sin_ref[...]
    kr = proj[:, MLA_KR0:MLA_KRS0] * cos + proj[:, MLA_KRS0:MLA_IN_W] * sin
    cos_t = cost_ref[...]
    sina_t = sinat_ref[...]
    sinb_t = sinbt_ref[...]
    half = MLA_ROPE // 2
    scale = (MLA_NOPE + MLA_ROPE) ** -0.5 * LOG2E
    vt = lax.dot_general(wuvt_ref[...], ckv, _NT, preferred_element_type=F32)
    _store_value_tiles(vt_out, vt, MLA_HEADS)
    cw = MLA_HCHUNK * LANE
    for c in range(MLA_HEADS // MLA_HCHUNK):
        sl = slice(c * cw, (c + 1) * cw)
        qt = lax.dot_general(wuqt_ref[sl, :], cq, _NT, preferred_element_type=F32)
        kn = jnp.dot(ckv, wuk_ref[:, sl], preferred_element_type=F32)
        for hh in range(MLA_HCHUNK):
            hs = slice(hh * LANE, (hh + 1) * LANE)
            os_ = slice(c * cw + hh * LANE, c * cw + (hh + 1) * LANE)
            qh = qt[hs]
            roped = (qh * cos_t + pltpu.roll(qh, LANE - half, 0) * sina_t + pltpu.roll(qh, half, 0) * sinb_t)
            qt_out[os_, :] = (roped * scale).astype(BF16)
            k_out[:, os_] = (kn[:, hs] + kr).astype(BF16)


def _mla_proj(h, g, w, cos128, sin128, S):
    T = h.shape[0]
    tm = PROJ_TM
    ns = S // tm
    row = lambda i: (i, 0)
    col = lambda i: (0, i)
    pos = lambda i: (i % ns, 0)
    pos_t = lambda i: (0, i % ns)
    HL = MLA_HEADS * LANE
    HV = MLA_HEADS * LANE
    lane = np.arange(LANE)[None, :]
    x1 = (lane >= MLA_NOPE) & (lane < MLA_NOPE + MLA_ROPE // 2)
    x2 = (lane >= MLA_NOPE + MLA_ROPE // 2) & (lane < MLA_NOPE + MLA_ROPE)
    sign_a, sign_b = -x1.astype(np.float32), x2.astype(np.float32)
    return pl.pallas_call(
        _mla_proj_body,
        grid=(T // tm,),
        in_specs=[pl.BlockSpec((tm, D_MODEL), row), _const_spec((1, D_MODEL)),
                  _const_spec(w["w_in"].shape), _const_spec((1, MLA_Q_LORA)), _const_spec((1, MLA_KV_LORA)),
                  _const_spec(w["w_uqt"].shape),
                  _const_spec(w["w_uk"].shape), _const_spec(w["w_uvt"].shape),
                  pl.BlockSpec((tm, LANE), pos), pl.BlockSpec((tm, LANE), pos),
                  pl.BlockSpec((LANE, tm), pos_t), pl.BlockSpec((LANE, tm), pos_t), pl.BlockSpec((LANE, tm), pos_t)],
        out_specs=[pl.BlockSpec((HL, tm), col), pl.BlockSpec((tm, HL), row), pl.BlockSpec((HV, tm), col)],
        out_shape=[jax.ShapeDtypeStruct((HL, T), BF16), jax.ShapeDtypeStruct((T, HL), BF16),
                   jax.ShapeDtypeStruct((HV, T), BF16)],
        compiler_params=_cparams(("parallel",)),
        name="mla_proj",
    )(h, g.reshape(1, D_MODEL), w["w_in"], w["q_norm"], w["kv_norm"], w["w_uqt"],
      w["w_uk"], w["w_uvt"], cos128, sin128, cos128.T, (sin128 * sign_a).T, (sin128 * sign_b).T)


_NT = (((1,), (1,)), ((), ()))
SCORE_CHUNK = 2 * LANE


def _chunks(n, lead=0):
    out = [(0, lead)] if lead else []
    return out + [(o, min(SCORE_CHUNK, n - o)) for o in range(lead, n, SCORE_CHUNK)]


SUBLANE = 8


def _fold(x, op):
    w, cols = x.shape
    return op(x.reshape(w // SUBLANE, SUBLANE, cols), axis=0)


class _SoftmaxStreamT:
    def __init__(self, score_fn, chunks, values_fn, s_ref):
        self.score_fn, self.chunks, self.values_fn, self.s_ref = score_fn, chunks, values_fn, s_ref

    def pass1(self):
        mp = None
        for off, w in self.chunks:
            s = self.score_fn(off, w)
            self.s_ref[off:off + w, :] = s
            part = _fold(s, jnp.max)
            mp = part if mp is None else jnp.maximum(mp, part)
            yield
        self.m = jnp.max(mp, axis=0, keepdims=True)

    def pass2(self):
        acc = None
        for off, w in self.chunks:
            p = jnp.exp2(self.s_ref[off:off + w, :] - self.m)
            pv = jnp.dot(self.values_fn(off, w), p.astype(BF16), preferred_element_type=F32)
            acc = pv if acc is None else acc + pv
            yield
        self.out = acc / acc[0:1, :]


def _trace_pipelined(streams):
    for _ in streams[0].pass1():
        pass
    for i, st in enumerate(streams):
        gens = [st.pass2()] + ([streams[i + 1].pass1()] if i + 1 < len(streams) else [])
        while gens:
            gens = [g for g in gens if next(g, StopIteration) is not StopIteration]


MLA_HPS = 8


def _mla_attn_body(qt_ref, k_ref, vt_ref, o_ref, *s_refs, nq):
    t = ATT_T
    qi = pl.program_id(2)

    def branch(nt):
        n = nt * t
        causal = lax.broadcasted_iota(jnp.int32, (t, t), 0) <= lax.broadcasted_iota(jnp.int32, (t, t), 1)
        streams = []
        for hh in range(MLA_HPS):
            hs = slice(hh * LANE, (hh + 1) * LANE)
            qt = qt_ref[hs, :]

            def score(off, w, qt=qt, hs=hs):
                s = jnp.dot(k_ref[off:off + w, hs], qt, preferred_element_type=F32)
                return jnp.where(causal, s, NEG) if off == n - t else s

            values = lambda off, w, hs=hs: vt_ref[hs, off:off + w]
            streams.append(_SoftmaxStreamT(score, _chunks(n), values, s_refs[hh]))
        _trace_pipelined(streams)
        outs = [st.out[LANE - MLA_V:] for st in streams]
        o_ref[...] = jnp.concatenate(outs, axis=0).T.astype(BF16)

    for nt in range(1, nq + 1):
        pl.when(qi == nt - 1)(functools.partial(branch, nt))


def _mla_attn(qt, k, vt, B, S):
    t = ATT_T
    nq = S // t
    T = B * S
    hps = MLA_HPS
    return pl.pallas_call(
        functools.partial(_mla_attn_body, nq=nq),
        grid=(B, MLA_HEADS // hps, nq),
        in_specs=[pl.BlockSpec((hps * LANE, t), lambda b, p, i: (p, b * nq + i)),
                  pl.BlockSpec((S, hps * LANE), lambda b, p, i: (b, p)),
                  pl.BlockSpec((hps * LANE, S), lambda b, p, i: (p, b))],
        out_specs=pl.BlockSpec((t, hps * MLA_V), lambda b, p, i: (b * nq + i, p)),
        out_shape=jax.ShapeDtypeStruct((T, MLA_HEADS * MLA_V), BF16),
        scratch_shapes=[pltpu.VMEM((S, t), F32)] * hps,
        compiler_params=_cparams(("parallel", "parallel", "arbitrary")),
        name="mla_attn",
    )(qt, k, vt)


NSA_QW = NSA_HEADS * LANE
NSA_KVW = NSA_GROUPS * LANE
NSA_IN_W = NSA_QW + 3 * NSA_KVW + LANE
GATES_PER_GROUP = 3 * NSA_R
RANK_STEPS_PER_STAGE = 8
LOG2E = math.log2(math.e)


def _nsa_proj_body(h_ref, g_ref, wqt_ref, wkv_ref, wkvt_ref, wgt_ref, blk_ref,
                   qt_out, kvc_out, kvs_out, kvw_out, kvst_out, kvwt_out, gate_out):
    xn = _rms(h_ref[...], g_ref[...]).astype(BF16)
    scale = NSA_QK ** -0.5 * LOG2E
    tm = xn.shape[0]
    hc = 4
    zeros = jnp.zeros((LANE - NSA_QK, tm), BF16)
    for c in range(NSA_HEADS // hc):
        qt = lax.dot_general(wqt_ref[c * hc * NSA_QK:(c + 1) * hc * NSA_QK, :], xn, _NT,
                             preferred_element_type=F32)
        for hh in range(hc):
            r0 = (c * hc + hh) * LANE
            qt_out[r0:r0 + NSA_QK, :] = (qt[hh * NSA_QK:(hh + 1) * NSA_QK] * scale).astype(BF16)
            qt_out[r0 + NSA_QK:r0 + LANE, :] = zeros
    for n, out in enumerate((kvc_out, kvs_out, kvw_out)):
        sl = slice(n * NSA_KVW, (n + 1) * NSA_KVW)
        kv = jnp.dot(xn, wkv_ref[:, sl], preferred_element_type=F32)
        if out is kvs_out:
            kv = kv + jnp.concatenate([blk_ref[...]] * NSA_GROUPS, axis=1)
        out[...] = kv.astype(BF16)
    gv = NSA_GROUPS * NSA_V
    for n, out in enumerate((kvst_out, kvwt_out)):
        vt = lax.dot_general(wkvt_ref[n * gv:(n + 1) * gv, :], xn, _NT, preferred_element_type=F32)
        _store_value_tiles(out, vt, NSA_GROUPS)
    gl = lax.dot_general(wgt_ref[...], xn, _NT, preferred_element_type=F32)
    gate_out[...] = jax.nn.sigmoid(gl)


def _nsa_proj(h, g, w, blk_onehot):
    T = h.shape[0]
    tm = PROJ_TM
    ns = blk_onehot.shape[0] // tm
    row = lambda i: (i, 0)
    col = lambda i: (0, i)
    return pl.pallas_call(
        _nsa_proj_body,
        grid=(T // tm,),
        in_specs=[pl.BlockSpec((tm, D_MODEL), row), _const_spec((1, D_MODEL)), _const_spec(w["w_qt"].shape),
                  _const_spec(w["w_kv"].shape), _const_spec(w["w_kvt"].shape), _const_spec(w["w_gt"].shape),
                  pl.BlockSpec((tm, LANE), lambda i: (i % ns, 0))],
        out_specs=[pl.BlockSpec((NSA_QW, tm), col), pl.BlockSpec((tm, NSA_KVW), row),
                   pl.BlockSpec((tm, NSA_KVW), row), pl.BlockSpec((tm, NSA_KVW), row),
                   pl.BlockSpec((NSA_KVW, tm), col), pl.BlockSpec((NSA_KVW, tm), col),
                   pl.BlockSpec((LANE, tm), col)],
        out_shape=[jax.ShapeDtypeStruct((NSA_QW, T), BF16), jax.ShapeDtypeStruct((T, NSA_KVW), BF16),
                   jax.ShapeDtypeStruct((T, NSA_KVW), BF16), jax.ShapeDtypeStruct((T, NSA_KVW), BF16),
                   jax.ShapeDtypeStruct((NSA_KVW, T), BF16), jax.ShapeDtypeStruct((NSA_KVW, T), BF16),
                   jax.ShapeDtypeStruct((LANE, T), F32)],
        compiler_params=_cparams(("parallel",)),
        name="nsa_proj",
    )(h, g.reshape(1, D_MODEL), w["w_qt"], w["w_kv"], w["w_kvt"], w["w_gt"], blk_onehot)


def _compress_body(x_ref, pos_ref, w1_ref, w2_ref, out_ref, out_t_ref):
    n_chunk = x_ref.shape[0]
    x = x_ref[...].astype(F32)
    xa = (x + pos_ref[0]).astype(BF16)
    xb = (x + pos_ref[1]).astype(BF16)
    a = jnp.dot(xa, w1_ref[0], preferred_element_type=F32)
    b = jnp.dot(xb, w1_ref[1], preferred_element_type=F32)
    pre = a + pltpu.roll(b, n_chunk - 1, 0)
    hid = jax.nn.gelu(pre, approximate=True).astype(BF16)
    kv = jnp.dot(hid, w2_ref[...], preferred_element_type=F32)
    out_ref[...] = kv.astype(BF16)
    out_t_ref[...] = kv.T.astype(BF16)


def _compress(x, pos, w1, w2):
    B, G, n_chunk, width = x.shape
    return pl.pallas_call(
        _compress_body,
        grid=(B, G),
        in_specs=[pl.BlockSpec((None, None, n_chunk, width), lambda b, g: (b, g, 0, 0)),
                  _const_spec(pos.shape), _const_spec(w1.shape), _const_spec(w2.shape)],
        out_specs=[pl.BlockSpec((None, None, n_chunk, LANE), lambda b, g: (b, g, 0, 0)),
                   pl.BlockSpec((None, None, LANE, n_chunk), lambda b, g: (b, g, 0, 0))],
        out_shape=[jax.ShapeDtypeStruct((B, G, n_chunk, LANE), BF16),
                   jax.ShapeDtypeStruct((B, G, LANE, n_chunk), BF16)],
        compiler_params=_cparams(("parallel", "parallel")),
        name="nsa_compress",
    )(x, pos, w1, w2)


def _nsa_attn_body(*refs, S):
    step = pl.program_id(2)
    for qi in range(S // NSA_T):
        pl.when(step == qi)(functools.partial(_nsa_tile, qi, *refs, S=S))


def _nsa_tile(qi, qt_ref, qtn_ref, kvc_ref, kvct_ref, kvs_ref, kvst_ref, kvw_ref, kvwt_ref, gate_ref, cbt_ref,
              dt_ref, ovt_ref, o_ref, *scratch, S):
    t = NSA_T
    R = NSA_R
    n_sel = S // SEL_BLOCK
    t0 = qi * t
    s_sel, s_win = scratch[0:2], scratch[2:4]
    ocmp_scr, sel_scr = scratch[4:6]
    heads_on_lanes = lambda ref: jnp.concatenate([ref[r * LANE:(r + 1) * LANE, :] for r in range(R)], axis=1)
    qt = heads_on_lanes(qt_ref)
    tile_r = lambda x: jnp.concatenate([x] * R, axis=1)
    half_w = 2 * t
    tile_h = lambda x: jnp.concatenate([x] * (half_w // t), axis=1)

    def compressed_and_selection(q_tile, tile, out):
        kvc = kvc_ref[...]
        n_cp = kvc.shape[0]
        sc = jnp.dot(kvc, q_tile, preferred_element_type=F32)
        yield
        cend = lax.broadcasted_iota(jnp.int32, (n_cp, t), 0) * CMP_STRIDE + (CMP_BLOCK - 1)
        valid = tile_r(tile * t + lax.broadcasted_iota(jnp.int32, (n_cp, t), 1) >= cend)
        shift = (tile * (t // CMP_STRIDE) + CB_CENTER) % n_cp
        cb = pltpu.roll(cbt_ref[...], shift, 0) if shift else cbt_ref[...]
        sc = jnp.where(valid, sc + cb, NEG)
        e = jnp.exp2(sc - jnp.max(sc, axis=0, keepdims=True))
        yield
        p = jnp.where(valid, e / jnp.sum(e, axis=0, keepdims=True), 0.0)
        out["o_cmp"] = jnp.dot(kvct_ref[...], p.astype(BF16), preferred_element_type=F32)
        yield

        psum = p[:, 0:t]
        for r in range(1, R):
            psum = psum + p[:, r * t:(r + 1) * t]
        imp = jnp.dot(ovt_ref[...], psum, preferred_element_type=F32, precision=lax.Precision.HIGHEST)
        yield
        jj = lax.broadcasted_iota(jnp.int32, (n_sel, t), 0)
        blk_t = (tile * t + lax.broadcasted_iota(jnp.int32, (n_sel, t), 1)) >> SEL_SHIFT
        forced = (jj == 0) | (jj == blk_t) | (jj == blk_t - 1)
        score = jnp.where(forced, 1e6, jnp.where(jj <= blk_t, imp, -1e6))
        cnt = jnp.zeros((n_sel, t), jnp.int32)
        for jp in range(n_sel):
            row = score[jp:jp + 1, :]
            beats = (row > score) | ((row == score) & (jj > jp))
            cnt = cnt + beats.astype(jnp.int32)
            if jp % RANK_STEPS_PER_STAGE == RANK_STEPS_PER_STAGE - 1:
                yield
        out["sel"] = jnp.where(cnt < min(SEL_TOP_N, n_sel), 0.0, NEG).astype(BF16)

    if qi == 0:
        cur = {}
        for _ in compressed_and_selection(qt, 0, cur):
            pass
        o_cmp, sel = cur["o_cmp"], cur["sel"]
    else:
        o_cmp, sel = ocmp_scr[...], sel_scr[...]
    nxt, side = {}, []
    if (qi + 1) * t < S:
        side = [compressed_and_selection(heads_on_lanes(qtn_ref), qi + 1, nxt)]

    kw = min(qi, WINDOW // t)
    nw = (kw + 1) * t
    w0 = (qi - kw) * t
    n = (qi + 1) * t

    def win_add(off, w):
        lo_d, hi_d = kw * t - off - (w - 1), kw * t - off + (t - 1)
        if lo_d >= 0 and hi_d < WINDOW:
            return None
        dist = (kw * t - off + lax.broadcasted_iota(jnp.int32, (w, t), 1)
                - lax.broadcasted_iota(jnp.int32, (w, t), 0))
        return jnp.where((dist >= 0) & (dist < WINDOW), 0.0, NEG)

    q_sel = jnp.concatenate([qt[0:NSA_QK], tile_r(sel), jnp.zeros((LANE - NSA_QK - n_sel, R * t), BF16)], axis=0)

    def sel_add(off, w):
        if off + w <= n - t:
            return None
        kpos = off + lax.broadcasted_iota(jnp.int32, (w, t), 0)
        return jnp.where(kpos <= t0 + lax.broadcasted_iota(jnp.int32, (w, t), 1), 0.0, NEG)

    def stream(half, q_all, k_ref, vt_ref, k0, nk, add_fn, s_ref):
        cs = slice(half * half_w, (half + 1) * half_w)
        qh = q_all[:, cs]

        def score(off, w):
            s = jnp.dot(k_ref[k0 + off:k0 + off + w, :], qh, preferred_element_type=F32)
            add = add_fn(off, w)
            if add is not None:
                s = s + tile_h(add)
            d0 = off - (nk - 2 * t)
            if d0 + w > 0:
                assert d0 >= 0
                s = s + dt_ref[d0:d0 + w, cs]
            return s

        values = lambda off, w: vt_ref[:, k0 + off:k0 + off + w]
        return _SoftmaxStreamT(score, _chunks(nk, lead=nk % SCORE_CHUNK), values, s_ref)

    halves = range(R * t // half_w)
    wins = [stream(h, qt, kvw_ref, kvwt_ref, w0, nw, win_add, s_win[h]) for h in halves]
    sels = [stream(h, q_sel, kvs_ref, kvst_ref, 0, n, sel_add, s_sel[h]) for h in halves]
    _trace_pipelined(wins[:1] + sels + wins[1:])
    for gen in side:
        for _ in gen:
            pass

    g = pl.program_id(0)
    heads = []
    for r in range(R):
        h, hc = divmod(r * t, half_w)
        branches = (o_cmp[NSA_QK:, r * t:(r + 1) * t], sels[h].out[NSA_QK:, hc:hc + t], wins[h].out[NSA_QK:, hc:hc + t])
        o = None
        for br, ob in enumerate(branches):
            gate = gate_ref[pl.ds(g * GATES_PER_GROUP + br * R + r, 1), :]
            o = gate * ob if o is None else o + gate * ob
        heads.append(o)
    o_ref[...] = jnp.concatenate(heads, axis=0).T.astype(BF16)
    if nxt:
        ocmp_scr[...] = nxt["o_cmp"]
        sel_scr[...] = nxt["sel"]


def _nsa_attn(qt, kvc, kvct, kvs, kvst, kvw, kvwt, gates, cbt, dtt, ovt, B, S):
    t = NSA_T
    nq = S // t
    T = B * S
    G, R = NSA_GROUPS, NSA_R
    n_cp = kvc.shape[2]
    half_w = 2 * t
    nh = R * t // half_w
    kv_spec = pl.BlockSpec((S, LANE), lambda g, b, i: (b, g))
    kvt_spec = pl.BlockSpec((LANE, S), lambda g, b, i: (g, b))
    return pl.pallas_call(
        functools.partial(_nsa_attn_body, S=S),
        grid=(G, B, nq),
        in_specs=[pl.BlockSpec((R * LANE, t), lambda g, b, i: (g, b * nq + i)),
                  pl.BlockSpec((R * LANE, t), lambda g, b, i: (g, b * nq + jnp.minimum(i + 1, nq - 1))),
                  pl.BlockSpec((None, None, n_cp, LANE), lambda g, b, i: (b, g, 0, 0)),
                  pl.BlockSpec((None, None, LANE, n_cp), lambda g, b, i: (b, g, 0, 0)),
                  kv_spec, kvt_spec, kv_spec, kvt_spec,
                  pl.BlockSpec((LANE, t), lambda g, b, i: (0, b * nq + i)),
                  pl.BlockSpec((None, n_cp, R * t), lambda g, b, i: (g, 0, 0)),
                  pl.BlockSpec((None, 2 * t, R * t), lambda g, b, i: (g, 0, 0)),
                  _const_spec(ovt.shape)],
        out_specs=pl.BlockSpec((t, R * NSA_V), lambda g, b, i: (b * nq + i, g)),
        out_shape=jax.ShapeDtypeStruct((T, NSA_HEADS * NSA_V), BF16),
        scratch_shapes=([pltpu.VMEM((S, half_w), F32)] * nh + [pltpu.VMEM((WINDOW + t, half_w), F32)] * nh
                        + [pltpu.VMEM((LANE, R * t), F32), pltpu.VMEM((S // SEL_BLOCK, t), BF16)]),
        compiler_params=_cparams(("arbitrary", "arbitrary", "arbitrary")),
        name="nsa_attn",
    )(qt, qt, kvc, kvct, kvs, kvst, kvw, kvwt, gates, cbt, dtt, ovt)


def _t5_bucket(dist):
    n = jnp.maximum(dist, 0)
    max_exact = REL_BUCKETS // 2
    nf = jnp.maximum(n, 1).astype(F32)
    large = max_exact + (jnp.log(nf / max_exact) / math.log(REL_MAX_DIST / max_exact)
                         * (REL_BUCKETS - max_exact)).astype(jnp.int32)
    large = jnp.minimum(large, REL_BUCKETS - 1)
    return jnp.where(n < max_exact, n, large)


def _np_bucket(n):
    n = np.maximum(np.asarray(n), 0)
    max_exact = REL_BUCKETS // 2
    large = max_exact + (np.log(np.maximum(n, 1) / max_exact) / math.log(REL_MAX_DIST / max_exact)
                         * (REL_BUCKETS - max_exact)).astype(np.int64)
    return np.where(n < max_exact, n, np.minimum(large, REL_BUCKETS - 1))


def _bias_tables(rel_bias, S):
    t = NSA_T
    G, R = NSA_GROUPS, NSA_R
    n_cp = S // CMP_STRIDE
    a = np.arange(t)[:, None]
    dist_d = a - np.arange(2 * t)[None, :] + t
    dist_c = a - CMP_STRIDE * (np.arange(n_cp)[None, :] - CB_CENTER) - (CMP_BLOCK - 1)
    uncovered = min(t + 1, CMP_STRIDE * (CB_CENTER + 1) - (CMP_BLOCK - 1))
    assert (_np_bucket(np.arange(uncovered, 2 * S)) == REL_BUCKETS - 1).all()
    wrap_from = n_cp - max((S // t - 1) * (t // CMP_STRIDE) - CB_CENTER, 0)
    assert (dist_c[:, wrap_from:] < 0).all() and (dist_c[:, -1] < 0).all()

    def lookup(dist):
        oh = jax.nn.one_hot(_t5_bucket(jnp.asarray(dist)), REL_BUCKETS, dtype=F32)
        oh = oh - jax.nn.one_hot(REL_BUCKETS - 1, REL_BUCKETS, dtype=F32)
        val = jnp.einsum("acb,bh->hac", oh, rel_bias, precision=lax.Precision.HIGHEST)
        return val.reshape(G, R, *dist.shape)

    dtab = lookup(dist_d)
    cbias = jnp.where(jnp.asarray(dist_c >= 0), lookup(dist_c), 0.0)
    keys_first = lambda x: x.transpose(0, 3, 1, 2).reshape(G, x.shape[3], R * t)
    return keys_first(dtab) * LOG2E, keys_first(cbias) * LOG2E


def _selection_tables(S):
    n_cp = S // CMP_STRIDE
    n_cmp = (S - CMP_BLOCK) // CMP_STRIDE + 1
    n_sel = S // SEL_BLOCK
    cs = np.arange(n_cp) * CMP_STRIDE
    ce = cs + CMP_BLOCK
    ss = np.arange(n_sel) * SEL_BLOCK
    se = ss + SEL_BLOCK
    ov = np.minimum(ce[:, None], se[None, :]) - np.maximum(cs[:, None], ss[None, :])
    ov = (np.clip(ov, 0, None) / CMP_BLOCK).astype(np.float32)
    ov[n_cmp:] = 0.0
    assert n_sel <= LANE - NSA_QK
    blk_onehot = (np.arange(S)[:, None] // SEL_BLOCK == np.arange(LANE)[None, :] - NSA_QK).astype(np.float32)
    return jnp.asarray(ov.T), jnp.asarray(blk_onehot)


def _rope_tables(S):
    half = MLA_ROPE // 2
    inv = ROPE_THETA ** (-jnp.arange(half, dtype=F32) * 2.0 / MLA_ROPE)
    ang = jnp.arange(S, dtype=F32)[:, None] * inv[None, :]
    cos, sin = jnp.cos(ang), jnp.sin(ang)
    ones = jnp.ones((S, MLA_NOPE), F32)
    pad1 = jnp.ones((S, LANE - MLA_NOPE - MLA_ROPE), F32)
    cos128 = jnp.concatenate([ones, cos, cos, pad1], axis=1)
    sin128 = jnp.concatenate([0 * ones, sin, sin, 0 * pad1], axis=1)
    return cos128, sin128


def _mla_weights(w_in, q_norm, kv_norm, w_uq, w_ukv, w_o):
    H = MLA_HEADS
    half = MLA_ROPE // 2
    pad = LANE - MLA_NOPE - MLA_ROPE
    kr = w_in[:, MLA_Q_LORA + MLA_KV_LORA:]
    kr_sw = jnp.concatenate([-kr[:, half:], kr[:, :half]], axis=1)
    z = lambda n: jnp.zeros((D_MODEL, n), F32)
    w_in_ext = jnp.concatenate([w_in[:, :MLA_Q_LORA + MLA_KV_LORA],
                                z(MLA_NOPE), kr, z(pad), z(MLA_NOPE), kr_sw, z(pad)], axis=1)
    uq = w_uq.reshape(MLA_Q_LORA, H, MLA_NOPE + MLA_ROPE)
    qn, qr = uq[..., :MLA_NOPE], uq[..., MLA_NOPE:]
    zq = jnp.zeros((MLA_Q_LORA, H, pad), F32)
    w_uq_p = jnp.concatenate([qn, qr, zq], axis=-1).reshape(MLA_Q_LORA, H * LANE)
    ukv = w_ukv.reshape(MLA_KV_LORA, H, MLA_NOPE + MLA_V)
    zk = jnp.zeros((MLA_KV_LORA, H, LANE - MLA_NOPE), F32)
    w_uk = jnp.concatenate([ukv[..., :MLA_NOPE], zk], axis=-1).reshape(MLA_KV_LORA, H * LANE)
    w_uv = ukv[..., MLA_NOPE:].reshape(MLA_KV_LORA, H * MLA_V)
    return dict(w_in=w_in_ext.astype(BF16), q_norm=q_norm.reshape(1, -1), kv_norm=kv_norm.reshape(1, -1),
                w_uqt=w_uq_p.T.astype(BF16), w_uk=w_uk.astype(BF16),
                w_uvt=w_uv.T.astype(BF16), w_o=w_o.astype(BF16))


def _nsa_weights(w_in, pos_k, w1_k, w2_k, pos_v, w1_v, w2_v, w_o):
    H, G = NSA_HEADS, NSA_GROUPS
    gw = G * NSA_QK
    q0 = H * NSA_QK
    wq = w_in[:, :q0]
    cols, cols_t = [], []
    for n in range(3):
        k = w_in[:, q0 + (2 * n) * gw: q0 + (2 * n + 1) * gw].reshape(D_MODEL, G, NSA_QK)
        v = w_in[:, q0 + (2 * n + 1) * gw: q0 + (2 * n + 2) * gw].reshape(D_MODEL, G, NSA_V)
        right = v if n == 0 else jnp.zeros_like(v)
        cols.append(jnp.concatenate([k, right], axis=-1).reshape(D_MODEL, G * LANE))
        if n > 0:
            cols_t.append(v.reshape(D_MODEL, G * NSA_V))
    w_kv = jnp.concatenate(cols, axis=1)
    w_vt = jnp.concatenate(cols_t, axis=1).T
    wg = w_in[:, q0 + 6 * gw:].reshape(D_MODEL, G, NSA_R, 3).transpose(0, 1, 3, 2).reshape(D_MODEL, 3 * H)
    wg = jnp.concatenate([wg, jnp.zeros((D_MODEL, LANE - 3 * H), F32)], axis=1)
    eye = jnp.eye(2, dtype=F32)
    cw = CMP_STRIDE * 2 * NSA_QK
    pos = jnp.stack([pos_k, pos_v]).reshape(2, 2, CMP_STRIDE, NSA_QK)
    pos = pos.transpose(1, 2, 0, 3).reshape(2, 1, cw)
    w1 = jnp.stack([w1_k, w1_v]).reshape(2, 2, CMP_STRIDE, NSA_QK, CMP_HIDDEN)
    w1 = jnp.einsum("khldj,kq->hlkdqj", w1, eye).reshape(2, cw, 2 * CMP_HIDDEN).astype(BF16)
    w2 = jnp.einsum("kjd,kq->kjqd", jnp.stack([w2_k, w2_v]), eye).reshape(2 * CMP_HIDDEN, 2 * NSA_QK).astype(BF16)
    return dict(w_qt=wq.T.astype(BF16), w_kv=w_kv.astype(BF16), w_kvt=w_vt.astype(BF16),
                w_gt=wg.T.astype(BF16), pos=pos, w1=w1, w2=w2, w_o=w_o.astype(BF16))


def kernel(x, ffn_norm_a, ffn_a_w_gate, ffn_a_w_up, ffn_a_w_down, mix_norm, ffn_norm_b, ffn_b_w_gate, ffn_b_w_up, ffn_b_w_down, final_norm, rel_bias, mla_w_in, mla_q_norm, mla_kv_norm, mla_w_uq, mla_w_ukv, mla_w_o, nsa_w_in, nsa_cmp_pos_k, nsa_cmp_w1_k, nsa_cmp_w2_k, nsa_cmp_pos_v, nsa_cmp_w1_v, nsa_cmp_w2_v, nsa_w_o):
    B, S, D = x.shape
    assert D == D_MODEL and S % ATT_T == 0 and S % NSA_T == 0 and (B * S) % FFN_TM == 0
    T = B * S
    G = NSA_GROUPS
    n_chunk = S // CMP_STRIDE
    cos128, sin128 = _rope_tables(S)
    dtab, cbias = _bias_tables(rel_bias, S)
    ovt, blk_onehot = _selection_tables(S)

    h = x.reshape(T, D)
    for i in range(DEPTH):
        h = _ffn(h, ffn_norm_a[i], ffn_a_w_gate, ffn_a_w_up, ffn_a_w_down, i)
        j = i // N_MIXERS
        if i % N_MIXERS == 0:
            w = _mla_weights(mla_w_in[j], mla_q_norm[j], mla_kv_norm[j], mla_w_uq[j], mla_w_ukv[j], mla_w_o[j])
            q, k, v = _mla_proj(h, mix_norm[i], w, cos128, sin128, S)
            o = _mla_attn(q, k, v, B, S)
        else:
            w = _nsa_weights(nsa_w_in[j], nsa_cmp_pos_k[j], nsa_cmp_w1_k[j], nsa_cmp_w2_k[j],
                             nsa_cmp_pos_v[j], nsa_cmp_w1_v[j], nsa_cmp_w2_v[j], nsa_w_o[j])
            qt, kvc_in, kvs, kvw, kvst, kvwt, gates = _nsa_proj(h, mix_norm[i], w, blk_onehot)
            xc = kvc_in.reshape(B, n_chunk, CMP_STRIDE, G, LANE).transpose(0, 3, 1, 2, 4)
            xc = xc.reshape(B, G, n_chunk, CMP_STRIDE * LANE)
            kvc, kvct = _compress(xc, w["pos"], w["w1"], w["w2"])
            o = _nsa_attn(qt, kvc, kvct, kvs, kvst, kvw, kvwt, gates, cbias, dtab, ovt, B, S)
        w_o = mla_w_o if i % N_MIXERS == 0 else nsa_w_o
        h = _ffn(h, ffn_norm_b[i], ffn_b_w_gate, ffn_b_w_up, ffn_b_w_down, i,
                 proj=(o, w_o, j), final_g=final_norm if i == DEPTH - 1 else None)
    return h.reshape(B, S, D)
```

```python
import functools
import math

import numpy as np
import jax
import jax.numpy as jnp
from jax import lax
from jax.experimental import pallas as pl
from jax.experimental.pallas import tpu as pltpu

F32 = jnp.float32
BF16 = jnp.bfloat16

D_MODEL = 1024
DEPTH = 4
N_MIXERS = 2
RMS_EPS = 1e-6
FFN_HIDDEN = 2816
NEG = -1e30
MLA_HEADS = 16
MLA_Q_LORA = 384
MLA_KV_LORA = 256
MLA_NOPE = 64
MLA_ROPE = 32
MLA_V = 64
ROPE_THETA = 10000.0
NSA_HEADS = 16
NSA_GROUPS = 4
NSA_R = NSA_HEADS // NSA_GROUPS
NSA_QK = 64
NSA_V = 64
CMP_BLOCK = 32
CMP_STRIDE = 16
CMP_HIDDEN = 128
SEL_BLOCK = 64
SEL_SHIFT = SEL_BLOCK.bit_length() - 1
assert 1 << SEL_SHIFT == SEL_BLOCK
SEL_TOP_N = 16
WINDOW = 512
REL_BUCKETS = 32
REL_MAX_DIST = 128

LANE = 128
VMEM_LIMIT = 56 * 1024 * 1024

FFN_TM = 512
FFN_TF = 256
PROJ_TM = 512
ATT_T = 256
NSA_T = 256
CB_CENTER = 64


def _cparams(sem):
    return pltpu.CompilerParams(dimension_semantics=sem, vmem_limit_bytes=VMEM_LIMIT)


def _rms(x, g):
    ms = jnp.mean(x * x, axis=-1, keepdims=True)
    return x * lax.rsqrt(ms + RMS_EPS) * g


def _const_spec(shape):
    nd = len(shape)
    return pl.BlockSpec(shape, lambda *_: (0,) * nd)


def _wdot(a, w):
    return lax.dot_general(a, w, (((1,), (0,)), ((), ())), preferred_element_type=F32)


def _ffn_body(*refs, has_proj, has_final):
    it = iter(refs)
    h_ref = next(it)
    if has_proj:
        o_in_ref = next(it)
        wo_ref = next(it)
    g_ref = next(it)
    wg_ref = next(it)
    wu_ref = next(it)
    wd_ref = next(it)
    if has_final:
        gf_ref = next(it)
    out_ref = next(it)
    a_ref = next(it)

    x = h_ref[...]
    if has_proj:
        x = x + _wdot(o_in_ref[...], wo_ref[...])
    xn = _rms(x, g_ref[...]).astype(BF16)
    for c in range(FFN_HIDDEN // FFN_TF):
        sl = slice(c * FFN_TF, (c + 1) * FFN_TF)
        gt = _wdot(xn, wg_ref[:, sl])
        up = _wdot(xn, wu_ref[:, sl])
        a_ref[:, sl] = (gt * jax.nn.sigmoid(gt) * up).astype(BF16)
    y = x + 0.5 * _wdot(a_ref[...], wd_ref[...])
    if has_final:
        y = _rms(y, gf_ref[...])
    out_ref[...] = y


def _layer_spec(stacked, layer):
    return pl.BlockSpec((None,) + stacked.shape[1:], lambda i: (layer, 0, 0), pipeline_mode=pl.Buffered(1))


def _ffn(h, g, wg, wu, wd, layer, proj=None, final_g=None):
    T = h.shape[0]
    tm = FFN_TM
    row = lambda i: (i, 0)
    in_specs = [pl.BlockSpec((tm, D_MODEL), row)]
    args = [h]
    if proj is not None:
        o_in, wo, wo_idx = proj
        in_specs += [pl.BlockSpec((tm, o_in.shape[1]), row), _layer_spec(wo, wo_idx)]
        args += [o_in, wo]
    in_specs += [_const_spec((1, D_MODEL)), _layer_spec(wg, layer), _layer_spec(wu, layer), _layer_spec(wd, layer)]
    args += [g.reshape(1, D_MODEL), wg, wu, wd]
    if final_g is not None:
        in_specs.append(_const_spec((1, D_MODEL)))
        args.append(final_g.reshape(1, D_MODEL))
    return pl.pallas_call(
        functools.partial(_ffn_body, has_proj=proj is not None, has_final=final_g is not None),
        grid=(T // tm,),
        in_specs=in_specs,
        out_specs=pl.BlockSpec((tm, D_MODEL), row),
        out_shape=jax.ShapeDtypeStruct((T, D_MODEL), F32),
        scratch_shapes=[pltpu.VMEM((tm, FFN_HIDDEN), BF16)],
        compiler_params=_cparams(("parallel",)),
        name="ffn",
    )(*args)


MLA_CQ0, MLA_CKV0, MLA_KR0, MLA_KRS0, MLA_IN_W = 0, 384, 640, 768, 896
MLA_HCHUNK = 4


def _store_value_tiles(out_ref, vt, n_tiles):
    d, tm = vt.shape[0] // n_tiles, vt.shape[1]
    pad = jnp.where(lax.broadcasted_iota(jnp.int32, (LANE - d, tm), 0) == 0, 1.0, 0.0).astype(out_ref.dtype)
    for i in range(n_tiles):
        out_ref[i * LANE:i * LANE + LANE - d, :] = pad
        out_ref[i * LANE + LANE - d:(i + 1) * LANE, :] = vt[i * d:(i + 1) * d].astype(out_ref.dtype)


def _mla_proj_body(h_ref, g_ref, win_ref, qn_ref, kvn_ref, wuqt_ref, wuk_ref, wuvt_ref,
                   cos_ref, sin_ref, cost_ref, sinat_ref, sinbt_ref, qt_out, k_out, vt_out):
    xn = _rms(h_ref[...], g_ref[...]).astype(BF16)
    proj = jnp.dot(xn, win_ref[...], preferred_element_type=F32)
    cq = _rms(proj[:, MLA_CQ0:MLA_CKV0], qn_ref[...]).astype(BF16)
    ckv = _rms(proj[:, MLA_CKV0:MLA_KR0], kvn_ref[...]).astype(BF16)
    cos = cos_ref[...]
    sin = sin_ref[...]
    kr = proj[:, MLA_KR0:MLA_KRS0] * cos + proj[:, MLA_KRS0:MLA_IN_W] * sin
    cos_t = cost_ref[...]
    sina_t = sinat_ref[...]
    sinb_t = sinbt_ref[...]
    half = MLA_ROPE // 2
    scale = (MLA_NOPE + MLA_ROPE) ** -0.5 * LOG2E
    vt = lax.dot_general(wuvt_ref[...], ckv, _NT, preferred_element_type=F32)
    _store_value_tiles(vt_out, vt, MLA_HEADS)
    cw = MLA_HCHUNK * LANE
    for c in range(MLA_HEADS // MLA_HCHUNK):
        sl = slice(c * cw, (c + 1) * cw)
        qt = lax.dot_general(wuqt_ref[sl, :], cq, _NT, preferred_element_type=F32)
        kn = jnp.dot(ckv, wuk_ref[:, sl], preferred_element_type=F32)
        for hh in range(MLA_HCHUNK):
            hs = slice(hh * LANE, (hh + 1) * LANE)
            os_ = slice(c * cw + hh * LANE, c * cw + (hh + 1) * LANE)
            qh = qt[hs]
            roped = (qh * cos_t + pltpu.roll(qh, LANE - half, 0) * sina_t + pltpu.roll(qh, half, 0) * sinb_t)
            qt_out[os_, :] = (roped * scale).astype(BF16)
            k_out[:, os_] = (kn[:, hs] + kr).astype(BF16)


def _mla_proj(h, g, w, cos128, sin128, S):
    T = h.shape[0]
    tm = PROJ_TM
    ns = S // tm
    row = lambda i: (i, 0)
    col = lambda i: (0, i)
    pos = lambda i: (i % ns, 0)
    pos_t = lambda i: (0, i % ns)
    HL = MLA_HEADS * LANE
    HV = MLA_HEADS * LANE
    lane = np.arange(LANE)[None, :]
    x1 = (lane >= MLA_NOPE) & (lane < MLA_NOPE + MLA_ROPE // 2)
    x2 = (lane >= MLA_NOPE + MLA_ROPE // 2) & (lane < MLA_NOPE + MLA_ROPE)
    sign_a, sign_b = -x1.astype(np.float32), x2.astype(np.float32)
    return pl.pallas_call(
        _mla_proj_body,
        grid=(T // tm,),
        in_specs=[pl.BlockSpec((tm, D_MODEL), row), _const_spec((1, D_MODEL)),
                  _const_spec(w["w_in"].shape), _const_spec((1, MLA_Q_LORA)), _const_spec((1, MLA_KV_LORA)),
                  _const_spec(w["w_uqt"].shape),
                  _const_spec(w["w_uk"].shape), _const_spec(w["w_uvt"].shape),
                  pl.BlockSpec((tm, LANE), pos), pl.BlockSpec((tm, LANE), pos),
                  pl.BlockSpec((LANE, tm), pos_t), pl.BlockSpec((LANE, tm), pos_t), pl.BlockSpec((LANE, tm), pos_t)],
        out_specs=[pl.BlockSpec((HL, tm), col), pl.BlockSpec((tm, HL), row), pl.BlockSpec((HV, tm), col)],
        out_shape=[jax.ShapeDtypeStruct((HL, T), BF16), jax.ShapeDtypeStruct((T, HL), BF16),
                   jax.ShapeDtypeStruct((HV, T), BF16)],
        compiler_params=_cparams(("parallel",)),
        name="mla_proj",
    )(h, g.reshape(1, D_MODEL), w["w_in"], w["q_norm"], w["kv_norm"], w["w_uqt"],
      w["w_uk"], w["w_uvt"], cos128, sin128, cos128.T, (sin128 * sign_a).T, (sin128 * sign_b).T)


_NT = (((1,), (1,)), ((), ()))
SCORE_CHUNK = 2 * LANE


def _chunks(n, lead=0):
    out = [(0, lead)] if lead else []
    return out + [(o, min(SCORE_CHUNK, n - o)) for o in range(lead, n, SCORE_CHUNK)]


SUBLANE = 8


def _fold(x, op):
    w, cols = x.shape
    return op(x.reshape(w // SUBLANE, SUBLANE, cols), axis=0)


class _SoftmaxStreamT:
    def __init__(self, score_fn, chunks, values_fn, s_ref):
        self.score_fn, self.chunks, self.values_fn, self.s_ref = score_fn, chunks, values_fn, s_ref

    def pass1(self):
        mp = None
        for off, w in self.chunks:
            s = self.score_fn(off, w)
            self.s_ref[off:off + w, :] = s
            part = _fold(s, jnp.max)
            mp = part if mp is None else jnp.maximum(mp, part)
            yield
        self.m = jnp.max(mp, axis=0, keepdims=True)

    def pass2(self):
        acc = None
        for off, w in self.chunks:
            p = jnp.exp2(self.s_ref[off:off + w, :] - self.m)
            pv = jnp.dot(self.values_fn(off, w), p.astype(BF16), preferred_element_type=F32)
            acc = pv if acc is None else acc + pv
            yield
        self.out = acc / acc[0:1, :]


def _trace_pipelined(streams):
    for _ in streams[0].pass1():
        pass
    for i, st in enumerate(streams):
        gens = [st.pass2()] + ([streams[i + 1].pass1()] if i + 1 < len(streams) else [])
        while gens:
            gens = [g for g in gens if next(g, StopIteration) is not StopIteration]


MLA_HPS = 8


def _mla_attn_body(qt_ref, k_ref, vt_ref, o_ref, *s_refs, nq):
    t = ATT_T
    qi = pl.program_id(2)

    def branch(nt):
        n = nt * t
        causal = lax.broadcasted_iota(jnp.int32, (t, t), 0) <= lax.broadcasted_iota(jnp.int32, (t, t), 1)
        streams = []
        for hh in range(MLA_HPS):
            hs = slice(hh * LANE, (hh + 1) * LANE)
            qt = qt_ref[hs, :]

            def score(off, w, qt=qt, hs=hs):
                s = jnp.dot(k_ref[off:off + w, hs], qt, preferred_element_type=F32)
                return jnp.where(causal, s, NEG) if off == n - t else s

            values = lambda off, w, hs=hs: vt_ref[hs, off:off + w]
            streams.append(_SoftmaxStreamT(score, _chunks(n), values, s_refs[hh]))
        _trace_pipelined(streams)
        outs = [st.out[LANE - MLA_V:] for st in streams]
        o_ref[...] = jnp.concatenate(outs, axis=0).T.astype(BF16)

    for nt in range(1, nq + 1):
        pl.when(qi == nt - 1)(functools.partial(branch, nt))


def _mla_attn(qt, k, vt, B, S):
    t = ATT_T
    nq = S // t
    T = B * S
    hps = MLA_HPS
    return pl.pallas_call(
        functools.partial(_mla_attn_body, nq=nq),
        grid=(B, MLA_HEADS // hps, nq),
        in_specs=[pl.BlockSpec((hps * LANE, t), lambda b, p, i: (p, b * nq + i)),
                  pl.BlockSpec((S, hps * LANE), lambda b, p, i: (b, p)),
                  pl.BlockSpec((hps * LANE, S), lambda b, p, i: (p, b))],
        out_specs=pl.BlockSpec((t, hps * MLA_V), lambda b, p, i: (b * nq + i, p)),
        out_shape=jax.ShapeDtypeStruct((T, MLA_HEADS * MLA_V), BF16),
        scratch_shapes=[pltpu.VMEM((S, t), F32)] * hps,
        compiler_params=_cparams(("parallel", "parallel", "arbitrary")),
        name="mla_attn",
    )(qt, k, vt)


NSA_QW = NSA_HEADS * LANE
NSA_KVW = NSA_GROUPS * LANE
NSA_IN_W = NSA_QW + 3 * NSA_KVW + LANE
GATES_PER_GROUP = 3 * NSA_R
RANK_STEPS_PER_STAGE = 8
LOG2E = math.log2(math.e)


def _nsa_proj_body(h_ref, g_ref, wqt_ref, wkv_ref, wkvt_ref, wgt_ref, blk_ref,
                   qt_out, xc_out, kvs_out, kvw_out, kvst_out, kvwt_out, gate_out, kvc_scr):
    xn = _rms(h_ref[...], g_ref[...]).astype(BF16)
    scale = NSA_QK ** -0.5 * LOG2E
    tm = xn.shape[0]
    hc = 4
    zeros = jnp.zeros((LANE - NSA_QK, tm), BF16)
    for c in range(NSA_HEADS // hc):
        qt = lax.dot_general(wqt_ref[c * hc * NSA_QK:(c + 1) * hc * NSA_QK, :], xn, _NT,
                             preferred_element_type=F32)
        for hh in range(hc):
            r0 = (c * hc + hh) * LANE
            qt_out[r0:r0 + NSA_QK, :] = (qt[hh * NSA_QK:(hh + 1) * NSA_QK] * scale).astype(BF16)
            qt_out[r0 + NSA_QK:r0 + LANE, :] = zeros
    for n, out in enumerate((xc_out, kvs_out, kvw_out)):
        sl = slice(n * NSA_KVW, (n + 1) * NSA_KVW)
        kv = jnp.dot(xn, wkv_ref[:, sl], preferred_element_type=F32)
        if out is xc_out:
            for gi in range(NSA_GROUPS):
                kvc_scr[gi] = kv[:, gi * LANE:(gi + 1) * LANE]
                for l in range(CMP_STRIDE):
                    rows = kvc_scr[gi, pl.ds(l, tm // CMP_STRIDE, stride=CMP_STRIDE), :]
                    xc_out[gi, :, l * LANE:(l + 1) * LANE] = rows.astype(BF16)
            continue
        if out is kvs_out:
            kv = kv + jnp.concatenate([blk_ref[...]] * NSA_GROUPS, axis=1)
        out[...] = kv.astype(BF16)
    gv = NSA_GROUPS * NSA_V
    for n, out in enumerate((kvst_out, kvwt_out)):
        vt = lax.dot_general(wkvt_ref[n * gv:(n + 1) * gv, :], xn, _NT, preferred_element_type=F32)
        _store_value_tiles(out, vt, NSA_GROUPS)
    gl = lax.dot_general(wgt_ref[...], xn, _NT, preferred_element_type=F32)
    gate_out[...] = jax.nn.sigmoid(gl)


def _nsa_proj(h, g, w, blk_onehot):
    T = h.shape[0]
    tm = PROJ_TM
    ns = blk_onehot.shape[0] // tm
    row = lambda i: (i, 0)
    col = lambda i: (0, i)
    return pl.pallas_call(
        _nsa_proj_body,
        grid=(T // tm,),
        in_specs=[pl.BlockSpec((tm, D_MODEL), row), _const_spec((1, D_MODEL)), _const_spec(w["w_qt"].shape),
                  _const_spec(w["w_kv"].shape), _const_spec(w["w_kvt"].shape), _const_spec(w["w_gt"].shape),
                  pl.BlockSpec((tm, LANE), lambda i: (i % ns, 0))],
        out_specs=[pl.BlockSpec((NSA_QW, tm), col),
                   pl.BlockSpec((NSA_GROUPS, tm // CMP_STRIDE, CMP_STRIDE * LANE), lambda i: (0, i, 0)),
                   pl.BlockSpec((tm, NSA_KVW), row), pl.BlockSpec((tm, NSA_KVW), row),
                   pl.BlockSpec((NSA_KVW, tm), col), pl.BlockSpec((NSA_KVW, tm), col),
                   pl.BlockSpec((LANE, tm), col)],
        out_shape=[jax.ShapeDtypeStruct((NSA_QW, T), BF16),
                   jax.ShapeDtypeStruct((NSA_GROUPS, T // CMP_STRIDE, CMP_STRIDE * LANE), BF16),
                   jax.ShapeDtypeStruct((T, NSA_KVW), BF16), jax.ShapeDtypeStruct((T, NSA_KVW), BF16),
                   jax.ShapeDtypeStruct((NSA_KVW, T), BF16), jax.ShapeDtypeStruct((NSA_KVW, T), BF16),
                   jax.ShapeDtypeStruct((LANE, T), F32)],
        scratch_shapes=[pltpu.VMEM((NSA_GROUPS, tm, LANE), F32)],
        compiler_params=_cparams(("parallel",)),
        name="nsa_proj",
    )(h, g.reshape(1, D_MODEL), w["w_qt"], w["w_kv"], w["w_kvt"], w["w_gt"], blk_onehot)


def _compress_body(x_ref, pos_ref, w1_ref, w2_ref, out_ref, out_t_ref):
    n_chunk = x_ref.shape[0]
    x = x_ref[...].astype(F32)
    xa = (x + pos_ref[0]).astype(BF16)
    xb = (x + pos_ref[1]).astype(BF16)
    a = jnp.dot(xa, w1_ref[0], preferred_element_type=F32)
    b = jnp.dot(xb, w1_ref[1], preferred_element_type=F32)
    pre = a + pltpu.roll(b, n_chunk - 1, 0)
    hid = jax.nn.gelu(pre, approximate=True).astype(BF16)
    kv = jnp.dot(hid, w2_ref[...], preferred_element_type=F32)
    out_ref[...] = kv.astype(BF16)
    out_t_ref[...] = kv.T.astype(BF16)


def _compress(x, pos, w1, w2, B):
    G, rows, width = x.shape
    n_chunk = rows // B
    return pl.pallas_call(
        _compress_body,
        grid=(B, G),
        in_specs=[pl.BlockSpec((None, n_chunk, width), lambda b, g: (g, b, 0)),
                  _const_spec(pos.shape), _const_spec(w1.shape), _const_spec(w2.shape)],
        out_specs=[pl.BlockSpec((None, None, n_chunk, LANE), lambda b, g: (b, g, 0, 0)),
                   pl.BlockSpec((None, None, LANE, n_chunk), lambda b, g: (b, g, 0, 0))],
        out_shape=[jax.ShapeDtypeStruct((B, G, n_chunk, LANE), BF16),
                   jax.ShapeDtypeStruct((B, G, LANE, n_chunk), BF16)],
        compiler_params=_cparams(("parallel", "parallel")),
        name="nsa_compress",
    )(x, pos, w1, w2)


def _nsa_attn_body(*refs, S):
    step = pl.program_id(2)
    for qi in range(S // NSA_T):
        pl.when(step == qi)(functools.partial(_nsa_tile, qi, *refs, S=S))


def _nsa_tile(qi, qt_ref, qtn_ref, kvc_ref, kvct_ref, kvs_ref, kvst_ref, kvw_ref, kvwt_ref, gate_ref, cbt_ref,
              dt_ref, ovt_ref, o_ref, *scratch, S):
    t = NSA_T
    R = NSA_R
    n_sel = S // SEL_BLOCK
    t0 = qi * t
    s_sel, s_win = scratch[0:2], scratch[2:4]
    ocmp_scr, sel_scr = scratch[4:6]
    heads_on_lanes = lambda ref: jnp.concatenate([ref[r * LANE:(r + 1) * LANE, :] for r in range(R)], axis=1)
    qt = heads_on_lanes(qt_ref)
    tile_r = lambda x: jnp.concatenate([x] * R, axis=1)
    half_w = 2 * t
    tile_h = lambda x: jnp.concatenate([x] * (half_w // t), axis=1)

    def compressed_and_selection(q_tile, tile, out):
        kvc = kvc_ref[...]
        n_cp = kvc.shape[0]
        sc = jnp.dot(kvc, q_tile, preferred_element_type=F32)
        yield
        cend = lax.broadcasted_iota(jnp.int32, (n_cp, t), 0) * CMP_STRIDE + (CMP_BLOCK - 1)
        valid = tile_r(tile * t + lax.broadcasted_iota(jnp.int32, (n_cp, t), 1) >= cend)
        shift = (tile * (t // CMP_STRIDE) + CB_CENTER) % n_cp
        cb = pltpu.roll(cbt_ref[...], shift, 0) if shift else cbt_ref[...]
        sc = jnp.where(valid, sc + cb, NEG)
        e = jnp.exp2(sc - jnp.max(sc, axis=0, keepdims=True))
        yield
        p = jnp.where(valid, e / jnp.sum(e, axis=0, keepdims=True), 0.0)
        out["o_cmp"] = jnp.dot(kvct_ref[...], p.astype(BF16), preferred_element_type=F32)
        yield

        psum = p[:, 0:t]
        for r in range(1, R):
            psum = psum + p[:, r * t:(r + 1) * t]
        imp = jnp.dot(ovt_ref[...], psum, preferred_element_type=F32, precision=lax.Precision.HIGHEST)
        yield
        jj = lax.broadcasted_iota(jnp.int32, (n_sel, t), 0)
        blk_t = (tile * t + lax.broadcasted_iota(jnp.int32, (n_sel, t), 1)) >> SEL_SHIFT
        forced = (jj == 0) | (jj == blk_t) | (jj == blk_t - 1)
        score = jnp.where(forced, 1e6, jnp.where(jj <= blk_t, imp, -1e6))
        cnt = jnp.zeros((n_sel, t), jnp.int32)
        for jp in range(n_sel):
            row = score[jp:jp + 1, :]
            beats = (row > score) | ((row == score) & (jj > jp))
            cnt = cnt + beats.astype(jnp.int32)
            if jp % RANK_STEPS_PER_STAGE == RANK_STEPS_PER_STAGE - 1:
                yield
        out["sel"] = jnp.where(cnt < min(SEL_TOP_N, n_sel), 0.0, NEG).astype(BF16)

    if qi == 0:
        cur = {}
        for _ in compressed_and_selection(qt, 0, cur):
            pass
        o_cmp, sel = cur["o_cmp"], cur["sel"]
    else:
        o_cmp, sel = ocmp_scr[...], sel_scr[...]
    nxt, side = {}, []
    if (qi + 1) * t < S:
        side = [compressed_and_selection(heads_on_lanes(qtn_ref), qi + 1, nxt)]

    kw = min(qi, WINDOW // t)
    nw = (kw + 1) * t
    w0 = (qi - kw) * t
    n = (qi + 1) * t

    def win_add(off, w):
        lo_d, hi_d = kw * t - off - (w - 1), kw * t - off + (t - 1)
        if lo_d >= 0 and hi_d < WINDOW:
            return None
        dist = (kw * t - off + lax.broadcasted_iota(jnp.int32, (w, t), 1)
                - lax.broadcasted_iota(jnp.int32, (w, t), 0))
        return jnp.where((dist >= 0) & (dist < WINDOW), 0.0, NEG)

    q_sel = jnp.concatenate([qt[0:NSA_QK], tile_r(sel), jnp.zeros((LANE - NSA_QK - n_sel, R * t), BF16)], axis=0)

    def sel_add(off, w):
        if off + w <= n - t:
            return None
        kpos = off + lax.broadcasted_iota(jnp.int32, (w, t), 0)
        return jnp.where(kpos <= t0 + lax.broadcasted_iota(jnp.int32, (w, t), 1), 0.0, NEG)

    def stream(half, q_all, k_ref, vt_ref, k0, nk, add_fn, s_ref):
        cs = slice(half * half_w, (half + 1) * half_w)
        qh = q_all[:, cs]

        def score(off, w):
            s = jnp.dot(k_ref[k0 + off:k0 + off + w, :], qh, preferred_element_type=F32)
            add = add_fn(off, w)
            if add is not None:
                s = s + tile_h(add)
            d0 = off - (nk - 2 * t)
            if d0 + w > 0:
                assert d0 >= 0
                s = s + dt_ref[d0:d0 + w, cs]
            return s

        values = lambda off, w: vt_ref[:, k0 + off:k0 + off + w]
        return _SoftmaxStreamT(score, _chunks(nk, lead=nk % SCORE_CHUNK), values, s_ref)

    halves = range(R * t // half_w)
    wins = [stream(h, qt, kvw_ref, kvwt_ref, w0, nw, win_add, s_win[h]) for h in halves]
    sels = [stream(h, q_sel, kvs_ref, kvst_ref, 0, n, sel_add, s_sel[h]) for h in halves]
    _trace_pipelined(wins[:1] + sels + wins[1:])
    for gen in side:
        for _ in gen:
            pass

    g = pl.program_id(0)
    heads = []
    for r in range(R):
        h, hc = divmod(r * t, half_w)
        branches = (o_cmp[NSA_QK:, r * t:(r + 1) * t], sels[h].out[NSA_QK:, hc:hc + t], wins[h].out[NSA_QK:, hc:hc + t])
        o = None
        for br, ob in enumerate(branches):
            gate = gate_ref[pl.ds(g * GATES_PER_GROUP + br * R + r, 1), :]
            o = gate * ob if o is None else o + gate * ob
        heads.append(o)
    o_ref[...] = jnp.concatenate(heads, axis=0).T.astype(BF16)
    if nxt:
        ocmp_scr[...] = nxt["o_cmp"]
        sel_scr[...] = nxt["sel"]


def _nsa_attn(qt, kvc, kvct, kvs, kvst, kvw, kvwt, gates, cbt, dtt, ovt, B, S):
    t = NSA_T
    nq = S // t
    T = B * S
    G, R = NSA_GROUPS, NSA_R
    n_cp = kvc.shape[2]
    half_w = 2 * t
    nh = R * t // half_w
    kv_spec = pl.BlockSpec((S, LANE), lambda g, b, i: (b, g))
    kvt_spec = pl.BlockSpec((LANE, S), lambda g, b, i: (g, b))
    return pl.pallas_call(
        functools.partial(_nsa_attn_body, S=S),
        grid=(G, B, nq),
        in_specs=[pl.BlockSpec((R * LANE, t), lambda g, b, i: (g, b * nq + i)),
                  pl.BlockSpec((R * LANE, t), lambda g, b, i: (g, b * nq + jnp.minimum(i + 1, nq - 1))),
                  pl.BlockSpec((None, None, n_cp, LANE), lambda g, b, i: (b, g, 0, 0)),
                  pl.BlockSpec((None, None, LANE, n_cp), lambda g, b, i: (b, g, 0, 0)),
                  kv_spec, kvt_spec, kv_spec, kvt_spec,
                  pl.BlockSpec((LANE, t), lambda g, b, i: (0, b * nq + i)),
                  pl.BlockSpec((None, n_cp, R * t), lambda g, b, i: (g, 0, 0)),
                  pl.BlockSpec((None, 2 * t, R * t), lambda g, b, i: (g, 0, 0)),
                  _const_spec(ovt.shape)],
        out_specs=pl.BlockSpec((t, R * NSA_V), lambda g, b, i: (b * nq + i, g)),
        out_shape=jax.ShapeDtypeStruct((T, NSA_HEADS * NSA_V), BF16),
        scratch_shapes=([pltpu.VMEM((S, half_w), F32)] * nh + [pltpu.VMEM((WINDOW + t, half_w), F32)] * nh
                        + [pltpu.VMEM((LANE, R * t), F32), pltpu.VMEM((S // SEL_BLOCK, t), BF16)]),
        compiler_params=_cparams(("arbitrary", "arbitrary", "arbitrary")),
        name="nsa_attn",
    )(qt, qt, kvc, kvct, kvs, kvst, kvw, kvwt, gates, cbt, dtt, ovt)


def _t5_bucket(dist):
    n = jnp.maximum(dist, 0)
    max_exact = REL_BUCKETS // 2
    nf = jnp.maximum(n, 1).astype(F32)
    large = max_exact + (jnp.log(nf / max_exact) / math.log(REL_MAX_DIST / max_exact)
                         * (REL_BUCKETS - max_exact)).astype(jnp.int32)
    large = jnp.minimum(large, REL_BUCKETS - 1)
    return jnp.where(n < max_exact, n, large)


def _np_bucket(n):
    n = np.maximum(np.asarray(n), 0)
    max_exact = REL_BUCKETS // 2
    large = max_exact + (np.log(np.maximum(n, 1) / max_exact) / math.log(REL_MAX_DIST / max_exact)
                         * (REL_BUCKETS - max_exact)).astype(np.int64)
    return np.where(n < max_exact, n, np.minimum(large, REL_BUCKETS - 1))


def _bias_tables(rel_bias, S):
    t = NSA_T
    G, R = NSA_GROUPS, NSA_R
    n_cp = S // CMP_STRIDE
    a = np.arange(t)[:, None]
    dist_d = a - np.arange(2 * t)[None, :] + t
    dist_c = a - CMP_STRIDE * (np.arange(n_cp)[None, :] - CB_CENTER) - (CMP_BLOCK - 1)
    uncovered = min(t + 1, CMP_STRIDE * (CB_CENTER + 1) - (CMP_BLOCK - 1))
    assert (_np_bucket(np.arange(uncovered, 2 * S)) == REL_BUCKETS - 1).all()
    wrap_from = n_cp - max((S // t - 1) * (t // CMP_STRIDE) - CB_CENTER, 0)
    assert (dist_c[:, wrap_from:] < 0).all() and (dist_c[:, -1] < 0).all()

    def lookup(dist):
        oh = jax.nn.one_hot(_t5_bucket(jnp.asarray(dist)), REL_BUCKETS, dtype=F32)
        oh = oh - jax.nn.one_hot(REL_BUCKETS - 1, REL_BUCKETS, dtype=F32)
        val = jnp.einsum("acb,bh->hac", oh, rel_bias, precision=lax.Precision.HIGHEST)
        return val.reshape(G, R, *dist.shape)

    dtab = lookup(dist_d)
    cbias = jnp.where(jnp.asarray(dist_c >= 0), lookup(dist_c), 0.0)
    keys_first = lambda x: x.transpose(0, 3, 1, 2).reshape(G, x.shape[3], R * t)
    return keys_first(dtab) * LOG2E, keys_first(cbias) * LOG2E


def _selection_tables(S):
    n_cp = S // CMP_STRIDE
    n_cmp = (S - CMP_BLOCK) // CMP_STRIDE + 1
    n_sel = S // SEL_BLOCK
    cs = np.arange(n_cp) * CMP_STRIDE
    ce = cs + CMP_BLOCK
    ss = np.arange(n_sel) * SEL_BLOCK
    se = ss + SEL_BLOCK
    ov = np.minimum(ce[:, None], se[None, :]) - np.maximum(cs[:, None], ss[None, :])
    ov = (np.clip(ov, 0, None) / CMP_BLOCK).astype(np.float32)
    ov[n_cmp:] = 0.0
    assert n_sel <= LANE - NSA_QK
    blk_onehot = (np.arange(S)[:, None] // SEL_BLOCK == np.arange(LANE)[None, :] - NSA_QK).astype(np.float32)
    return jnp.asarray(ov.T), jnp.asarray(blk_onehot)


def _rope_tables(S):
    half = MLA_ROPE // 2
    inv = ROPE_THETA ** (-jnp.arange(half, dtype=F32) * 2.0 / MLA_ROPE)
    ang = jnp.arange(S, dtype=F32)[:, None] * inv[None, :]
    cos, sin = jnp.cos(ang), jnp.sin(ang)
    ones = jnp.ones((S, MLA_NOPE), F32)
    pad1 = jnp.ones((S, LANE - MLA_NOPE - MLA_ROPE), F32)
    cos128 = jnp.concatenate([ones, cos, cos, pad1], axis=1)
    sin128 = jnp.concatenate([0 * ones, sin, sin, 0 * pad1], axis=1)
    return cos128, sin128


def _mla_weights(w_in, q_norm, kv_norm, w_uq, w_ukv, w_o):
    H = MLA_HEADS
    half = MLA_ROPE // 2
    pad = LANE - MLA_NOPE - MLA_ROPE
    kr = w_in[:, MLA_Q_LORA + MLA_KV_LORA:]
    kr_sw = jnp.concatenate([-kr[:, half:], kr[:, :half]], axis=1)
    z = lambda n: jnp.zeros((D_MODEL, n), F32)
    w_in_ext = jnp.concatenate([w_in[:, :MLA_Q_LORA + MLA_KV_LORA],
                                z(MLA_NOPE), kr, z(pad), z(MLA_NOPE), kr_sw, z(pad)], axis=1)
    uq = w_uq.reshape(MLA_Q_LORA, H, MLA_NOPE + MLA_ROPE)
    qn, qr = uq[..., :MLA_NOPE], uq[..., MLA_NOPE:]
    zq = jnp.zeros((MLA_Q_LORA, H, pad), F32)
    w_uq_p = jnp.concatenate([qn, qr, zq], axis=-1).reshape(MLA_Q_LORA, H * LANE)
    ukv = w_ukv.reshape(MLA_KV_LORA, H, MLA_NOPE + MLA_V)
    zk = jnp.zeros((MLA_KV_LORA, H, LANE - MLA_NOPE), F32)
    w_uk = jnp.concatenate([ukv[..., :MLA_NOPE], zk], axis=-1).reshape(MLA_KV_LORA, H * LANE)
    w_uv = ukv[..., MLA_NOPE:].reshape(MLA_KV_LORA, H * MLA_V)
    return dict(w_in=w_in_ext.astype(BF16), q_norm=q_norm.reshape(1, -1), kv_norm=kv_norm.reshape(1, -1),
                w_uqt=w_uq_p.T.astype(BF16), w_uk=w_uk.astype(BF16),
                w_uvt=w_uv.T.astype(BF16), w_o=w_o.astype(BF16))


def _nsa_weights(w_in, pos_k, w1_k, w2_k, pos_v, w1_v, w2_v, w_o):
    H, G = NSA_HEADS, NSA_GROUPS
    gw = G * NSA_QK
    q0 = H * NSA_QK
    wq = w_in[:, :q0]
    cols, cols_t = [], []
    for n in range(3):
        k = w_in[:, q0 + (2 * n) * gw: q0 + (2 * n + 1) * gw].reshape(D_MODEL, G, NSA_QK)
        v = w_in[:, q0 + (2 * n + 1) * gw: q0 + (2 * n + 2) * gw].reshape(D_MODEL, G, NSA_V)
        right = v if n == 0 else jnp.zeros_like(v)
        cols.append(jnp.concatenate([k, right], axis=-1).reshape(D_MODEL, G * LANE))
        if n > 0:
            cols_t.append(v.reshape(D_MODEL, G * NSA_V))
    w_kv = jnp.concatenate(cols, axis=1)
    w_vt = jnp.concatenate(cols_t, axis=1).T
    wg = w_in[:, q0 + 6 * gw:].reshape(D_MODEL, G, NSA_R, 3).transpose(0, 1, 3, 2).reshape(D_MODEL, 3 * H)
    wg = jnp.concatenate([wg, jnp.zeros((D_MODEL, LANE - 3 * H), F32)], axis=1)
    eye = jnp.eye(2, dtype=F32)
    cw = CMP_STRIDE * 2 * NSA_QK
    pos = jnp.stack([pos_k, pos_v]).reshape(2, 2, CMP_STRIDE, NSA_QK)
    pos = pos.transpose(1, 2, 0, 3).reshape(2, 1, cw)
    w1 = jnp.stack([w1_k, w1_v]).reshape(2, 2, CMP_STRIDE, NSA_QK, CMP_HIDDEN)
    w1 = jnp.einsum("khldj,kq->hlkdqj", w1, eye).reshape(2, cw, 2 * CMP_HIDDEN).astype(BF16)
    w2 = jnp.einsum("kjd,kq->kjqd", jnp.stack([w2_k, w2_v]), eye).reshape(2 * CMP_HIDDEN, 2 * NSA_QK).astype(BF16)
    return dict(w_qt=wq.T.astype(BF16), w_kv=w_kv.astype(BF16), w_kvt=w_vt.astype(BF16),
                w_gt=wg.T.astype(BF16), pos=pos, w1=w1, w2=w2, w_o=w_o.astype(BF16))


def kernel(x, ffn_norm_a, ffn_a_w_gate, ffn_a_w_up, ffn_a_w_down, mix_norm, ffn_norm_b, ffn_b_w_gate, ffn_b_w_up, ffn_b_w_down, final_norm, rel_bias, mla_w_in, mla_q_norm, mla_kv_norm, mla_w_uq, mla_w_ukv, mla_w_o, nsa_w_in, nsa_cmp_pos_k, nsa_cmp_w1_k, nsa_cmp_w2_k, nsa_cmp_pos_v, nsa_cmp_w1_v, nsa_cmp_w2_v, nsa_w_o):
    B, S, D = x.shape
    assert D == D_MODEL and S % ATT_T == 0 and S % NSA_T == 0 and (B * S) % FFN_TM == 0
    T = B * S
    G = NSA_GROUPS
    n_chunk = S // CMP_STRIDE
    cos128, sin128 = _rope_tables(S)
    dtab, cbias = _bias_tables(rel_bias, S)
    ovt, blk_onehot = _selection_tables(S)

    h = x.reshape(T, D)
    for i in range(DEPTH):
        h = _ffn(h, ffn_norm_a[i], ffn_a_w_gate, ffn_a_w_up, ffn_a_w_down, i)
        j = i // N_MIXERS
        if i % N_MIXERS == 0:
            w = _mla_weights(mla_w_in[j], mla_q_norm[j], mla_kv_norm[j], mla_w_uq[j], mla_w_ukv[j], mla_w_o[j])
            q, k, v = _mla_proj(h, mix_norm[i], w, cos128, sin128, S)
            o = _mla_attn(q, k, v, B, S)
        else:
            w = _nsa_weights(nsa_w_in[j], nsa_cmp_pos_k[j], nsa_cmp_w1_k[j], nsa_cmp_w2_k[j],
                             nsa_cmp_pos_v[j], nsa_cmp_w1_v[j], nsa_cmp_w2_v[j], nsa_w_o[j])
            qt, xc, kvs, kvw, kvst, kvwt, gates = _nsa_proj(h, mix_norm[i], w, blk_onehot)
            kvc, kvct = _compress(xc, w["pos"], w["w1"], w["w2"], B)
            o = _nsa_attn(qt, kvc, kvct, kvs, kvst, kvw, kvwt, gates, cbias, dtab, ovt, B, S)
        w_o = mla_w_o if i % N_MIXERS == 0 else nsa_w_o
        h = _ffn(h, ffn_norm_b[i], ffn_b_w_gate, ffn_b_w_up, ffn_b_w_down, i,
                 proj=(o, w_o, j), final_g=final_norm if i == DEPTH - 1 else None)
    return h.reshape(B, S, D)
```

```python
import functools
import math

import numpy as np
import jax
import jax.numpy as jnp
from jax import lax
from jax.experimental import pallas as pl
from jax.experimental.pallas import tpu as pltpu

F32 = jnp.float32
BF16 = jnp.bfloat16

D_MODEL = 1024
DEPTH = 4
N_MIXERS = 2
RMS_EPS = 1e-6
FFN_HIDDEN = 2816
NEG = -1e30
MLA_HEADS = 16
MLA_Q_LORA = 384
MLA_KV_LORA = 256
MLA_NOPE = 64
MLA_ROPE = 32
MLA_V = 64
ROPE_THETA = 10000.0
NSA_HEADS = 16
NSA_GROUPS = 4
NSA_R = NSA_HEADS // NSA_GROUPS
NSA_QK = 64
NSA_V = 64
CMP_BLOCK = 32
CMP_STRIDE = 16
CMP_HIDDEN = 128
SEL_BLOCK = 64
SEL_SHIFT = SEL_BLOCK.bit_length() - 1
assert 1 << SEL_SHIFT == SEL_BLOCK
SEL_TOP_N = 16
WINDOW = 512
REL_BUCKETS = 32
REL_MAX_DIST = 128

LANE = 128
VMEM_LIMIT = 56 * 1024 * 1024

FFN_TM = 512
FFN_TF = 256
PROJ_TM = 512
ATT_T = 256
NSA_T = 256
CB_CENTER = 64


def _cparams(sem):
    return pltpu.CompilerParams(dimension_semantics=sem, vmem_limit_bytes=VMEM_LIMIT)


def _rms(x, g):
    ms = jnp.mean(x * x, axis=-1, keepdims=True)
    return x * lax.rsqrt(ms + RMS_EPS) * g


def _const_spec(shape):
    nd = len(shape)
    return pl.BlockSpec(shape, lambda *_: (0,) * nd)


def _wdot(a, w):
    return lax.dot_general(a, w, (((1,), (0,)), ((), ())), preferred_element_type=F32)


def _ffn_body(*refs, has_proj, has_final):
    it = iter(refs)
    h_ref = next(it)
    if has_proj:
        o_in_ref = next(it)
        wo_ref = next(it)
    g_ref = next(it)
    wg_ref = next(it)
    wu_ref = next(it)
    wd_ref = next(it)
    if has_final:
        gf_ref = next(it)
    out_ref = next(it)
    a_ref = next(it)

    x = h_ref[...]
    if has_proj:
        x = x + _wdot(o_in_ref[...], wo_ref[...])
    xn = _rms(x, g_ref[...]).astype(BF16)
    for c in range(FFN_HIDDEN // FFN_TF):
        sl = slice(c * FFN_TF, (c + 1) * FFN_TF)
        gt = _wdot(xn, wg_ref[:, sl])
        up = _wdot(xn, wu_ref[:, sl])
        a_ref[:, sl] = (gt * jax.nn.sigmoid(gt) * up).astype(BF16)
    y = x + 0.5 * _wdot(a_ref[...], wd_ref[...])
    if has_final:
        y = _rms(y, gf_ref[...])
    out_ref[...] = y


def _layer_spec(stacked, layer):
    return pl.BlockSpec((None,) + stacked.shape[1:], lambda i: (layer, 0, 0), pipeline_mode=pl.Buffered(1))


def _ffn(h, g, wg, wu, wd, layer, proj=None, final_g=None):
    T = h.shape[0]
    tm = FFN_TM
    row = lambda i: (i, 0)
    in_specs = [pl.BlockSpec((tm, D_MODEL), row)]
    args = [h]
    if proj is not None:
        o_in, wo, wo_idx = proj
        in_specs += [pl.BlockSpec((tm, o_in.shape[1]), row), _layer_spec(wo, wo_idx)]
        args += [o_in, wo]
    in_specs += [_const_spec((1, D_MODEL)), _layer_spec(wg, layer), _layer_spec(wu, layer), _layer_spec(wd, layer)]
    args += [g.reshape(1, D_MODEL), wg, wu, wd]
    if final_g is not None:
        in_specs.append(_const_spec((1, D_MODEL)))
        args.append(final_g.reshape(1, D_MODEL))
    return pl.pallas_call(
        functools.partial(_ffn_body, has_proj=proj is not None, has_final=final_g is not None),
        grid=(T // tm,),
        in_specs=in_specs,
        out_specs=pl.BlockSpec((tm, D_MODEL), row),
        out_shape=jax.ShapeDtypeStruct((T, D_MODEL), F32),
        scratch_shapes=[pltpu.VMEM((tm, FFN_HIDDEN), BF16)],
        compiler_params=_cparams(("parallel",)),
        name="ffn",
    )(*args)


MLA_CQ0, MLA_CKV0, MLA_KR0, MLA_KRS0, MLA_IN_W = 0, 384, 640, 768, 896
MLA_HCHUNK = 4


def _store_value_tiles(out_ref, vt, n_tiles):
    d, tm = vt.shape[0] // n_tiles, vt.shape[1]
    pad = jnp.where(lax.broadcasted_iota(jnp.int32, (LANE - d, tm), 0) == 0, 1.0, 0.0).astype(out_ref.dtype)
    for i in range(n_tiles):
        out_ref[i * LANE:i * LANE + LANE - d, :] = pad
        out_ref[i * LANE + LANE - d:(i + 1) * LANE, :] = vt[i * d:(i + 1) * d].astype(out_ref.dtype)


def _mla_proj_body(h_ref, g_ref, win_ref, qn_ref, kvn_ref, wuqt_ref, wuk_ref, wuvt_ref,
                   cos_ref, sin_ref, cost_ref, sinat_ref, sinbt_ref, qt_out, k_out, vt_out):
    xn = _rms(h_ref[...], g_ref[...]).astype(BF16)
    proj = jnp.dot(xn, win_ref[...], preferred_element_type=F32)
    cq = _rms(proj[:, MLA_CQ0:MLA_CKV0], qn_ref[...]).astype(BF16)
    ckv = _rms(proj[:, MLA_CKV0:MLA_KR0], kvn_ref[...]).astype(BF16)
    cos = cos_ref[...]
    sin = sin_ref[...]
    kr = proj[:, MLA_KR0:MLA_KRS0] * cos + proj[:, MLA_KRS0:MLA_IN_W] * sin
    cos_t = cost_ref[...]
    sina_t = sinat_ref[...]
    sinb_t = sinbt_ref[...]
    half = MLA_ROPE // 2
    scale = (MLA_NOPE + MLA_ROPE) ** -0.5 * LOG2E
    vt = lax.dot_general(wuvt_ref[...], ckv, _NT, preferred_element_type=F32)
    _store_value_tiles(vt_out, vt, MLA_HEADS)
    cw = MLA_HCHUNK * LANE
    for c in range(MLA_HEADS // MLA_HCHUNK):
        sl = slice(c * cw, (c + 1) * cw)
        qt = lax.dot_general(wuqt_ref[sl, :], cq, _NT, preferred_element_type=F32)
        kn = jnp.dot(ckv, wuk_ref[:, sl], preferred_element_type=F32)
        for hh in range(MLA_HCHUNK):
            hs = slice(hh * LANE, (hh + 1) * LANE)
            os_ = slice(c * cw + hh * LANE, c * cw + (hh + 1) * LANE)
            qh = qt[hs]
            roped = (qh * cos_t + pltpu.roll(qh, LANE - half, 0) * sina_t + pltpu.roll(qh, half, 0) * sinb_t)
            qt_out[os_, :] = (roped * scale).astype(BF16)
            k_out[:, os_] = (kn[:, hs] + kr).astype(BF16)


def _mla_proj(h, g, w, cos128, sin128, S):
    T = h.shape[0]
    tm = PROJ_TM
    ns = S // tm
    row = lambda i: (i, 0)
    col = lambda i: (0, i)
    pos = lambda i: (i % ns, 0)
    pos_t = lambda i: (0, i % ns)
    HL = MLA_HEADS * LANE
    HV = MLA_HEADS * LANE
    lane = np.arange(LANE)[None, :]
    x1 = (lane >= MLA_NOPE) & (lane < MLA_NOPE + MLA_ROPE // 2)
    x2 = (lane >= MLA_NOPE + MLA_ROPE // 2) & (lane < MLA_NOPE + MLA_ROPE)
    sign_a, sign_b = -x1.astype(np.float32), x2.astype(np.float32)
    return pl.pallas_call(
        _mla_proj_body,
        grid=(T // tm,),
        in_specs=[pl.BlockSpec((tm, D_MODEL), row), _const_spec((1, D_MODEL)),
                  _const_spec(w["w_in"].shape), _const_spec((1, MLA_Q_LORA)), _const_spec((1, MLA_KV_LORA)),
                  _const_spec(w["w_uqt"].shape),
                  _const_spec(w["w_uk"].shape), _const_spec(w["w_uvt"].shape),
                  pl.BlockSpec((tm, LANE), pos), pl.BlockSpec((tm, LANE), pos),
                  pl.BlockSpec((LANE, tm), pos_t), pl.BlockSpec((LANE, tm), pos_t), pl.BlockSpec((LANE, tm), pos_t)],
        out_specs=[pl.BlockSpec((HL, tm), col), pl.BlockSpec((tm, HL), row), pl.BlockSpec((HV, tm), col)],
        out_shape=[jax.ShapeDtypeStruct((HL, T), BF16), jax.ShapeDtypeStruct((T, HL), BF16),
                   jax.ShapeDtypeStruct((HV, T), BF16)],
        compiler_params=_cparams(("parallel",)),
        name="mla_proj",
    )(h, g.reshape(1, D_MODEL), w["w_in"], w["q_norm"], w["kv_norm"], w["w_uqt"],
      w["w_uk"], w["w_uvt"], cos128, sin128, cos128.T, (sin128 * sign_a).T, (sin128 * sign_b).T)


_NT = (((1,), (1,)), ((), ()))
SCORE_CHUNK = 2 * LANE


def _chunks(n, lead=0):
    out = [(0, lead)] if lead else []
    return out + [(o, min(SCORE_CHUNK, n - o)) for o in range(lead, n, SCORE_CHUNK)]


SUBLANE = 8


def _fold(x, op):
    w, cols = x.shape
    return op(x.reshape(w // SUBLANE, SUBLANE, cols), axis=0)


class _SoftmaxStreamT:
    def __init__(self, score_fn, chunks, values_fn, s_ref):
        self.score_fn, self.chunks, self.values_fn, self.s_ref = score_fn, chunks, values_fn, s_ref

    def pass1(self):
        mp = None
        for off, w in self.chunks:
            s = self.score_fn(off, w)
            self.s_ref[off:off + w, :] = s
            part = _fold(s, jnp.max)
            mp = part if mp is None else jnp.maximum(mp, part)
            yield
        self.m = jnp.max(mp, axis=0, keepdims=True)

    def pass2(self):
        acc = None
        for off, w in self.chunks:
            p = jnp.exp2(self.s_ref[off:off + w, :] - self.m)
            pv = jnp.dot(self.values_fn(off, w), p.astype(BF16), preferred_element_type=F32)
            acc = pv if acc is None else acc + pv
            yield
        self.out = acc / acc[0:1, :]


def _trace_pipelined(streams):
    for _ in streams[0].pass1():
        pass
    for i, st in enumerate(streams):
        gens = [st.pass2()] + ([streams[i + 1].pass1()] if i + 1 < len(streams) else [])
        while gens:
            gens = [g for g in gens if next(g, StopIteration) is not StopIteration]


MLA_HPS = 8


def _mla_attn_body(qt_ref, k_ref, vt_ref, o_ref, *s_refs, nq):
    t = ATT_T
    qi = pl.program_id(2)

    def branch(nt):
        n = nt * t
        streams = []
        for hh in range(MLA_HPS):
            hs = slice(hh * LANE, (hh + 1) * LANE)
            qt = qt_ref[hs, :]

            def score(off, w, qt=qt, hs=hs):
                s = jnp.dot(k_ref[off:off + w, hs], qt, preferred_element_type=F32)
                if off + w <= n - t:
                    return s
                kpos = off + lax.broadcasted_iota(jnp.int32, (w, t), 0)
                return jnp.where(kpos <= n - t + lax.broadcasted_iota(jnp.int32, (w, t), 1), s, NEG)

            values = lambda off, w, hs=hs: vt_ref[hs, off:off + w]
            streams.append(_SoftmaxStreamT(score, _chunks(n, lead=n % SCORE_CHUNK), values, s_refs[hh]))
        _trace_pipelined(streams)
        outs = [st.out[LANE - MLA_V:] for st in streams]
        o_ref[...] = jnp.concatenate(outs, axis=0).T.astype(BF16)

    for nt in range(1, nq + 1):
        pl.when(qi == nt - 1)(functools.partial(branch, nt))


def _mla_attn(qt, k, vt, B, S):
    t = ATT_T
    nq = S // t
    T = B * S
    hps = MLA_HPS
    return pl.pallas_call(
        functools.partial(_mla_attn_body, nq=nq),
        grid=(B, MLA_HEADS // hps, nq),
        in_specs=[pl.BlockSpec((hps * LANE, t), lambda b, p, i: (p, b * nq + i)),
                  pl.BlockSpec((S, hps * LANE), lambda b, p, i: (b, p)),
                  pl.BlockSpec((hps * LANE, S), lambda b, p, i: (p, b))],
        out_specs=pl.BlockSpec((t, hps * MLA_V), lambda b, p, i: (b * nq + i, p)),
        out_shape=jax.ShapeDtypeStruct((T, MLA_HEADS * MLA_V), BF16),
        scratch_shapes=[pltpu.VMEM((S, t), F32)] * hps,
        compiler_params=_cparams(("parallel", "parallel", "arbitrary")),
        name="mla_attn",
    )(qt, k, vt)


NSA_QW = NSA_HEADS * LANE
NSA_KVW = NSA_GROUPS * LANE
GATES_PER_GROUP = 3 * NSA_R
NSA_STREAM_HEADS = 2
RANK_STEPS_PER_STAGE = 8
LOG2E = math.log2(math.e)


def _nsa_proj_body(h_ref, g_ref, wqt_ref, wkv_ref, wkvt_ref, wgt_ref, blk_ref,
                   qt_out, xc_out, kvs_out, kvw_out, kvst_out, kvwt_out, gate_out, kvc_scr):
    xn = _rms(h_ref[...], g_ref[...]).astype(BF16)
    scale = NSA_QK ** -0.5 * LOG2E
    tm = xn.shape[0]
    hc = 4
    zeros = jnp.zeros((LANE - NSA_QK, tm), BF16)
    for c in range(NSA_HEADS // hc):
        qt = lax.dot_general(wqt_ref[c * hc * NSA_QK:(c + 1) * hc * NSA_QK, :], xn, _NT,
                             preferred_element_type=F32)
        for hh in range(hc):
            r0 = (c * hc + hh) * LANE
            qt_out[r0:r0 + NSA_QK, :] = (qt[hh * NSA_QK:(hh + 1) * NSA_QK] * scale).astype(BF16)
            qt_out[r0 + NSA_QK:r0 + LANE, :] = zeros
    for n, out in enumerate((xc_out, kvs_out, kvw_out)):
        sl = slice(n * NSA_KVW, (n + 1) * NSA_KVW)
        kv = jnp.dot(xn, wkv_ref[:, sl], preferred_element_type=F32)
        if out is xc_out:
            for gi in range(NSA_GROUPS):
                kvc_scr[gi] = kv[:, gi * LANE:(gi + 1) * LANE]
                for l in range(CMP_STRIDE):
                    rows = kvc_scr[gi, pl.ds(l, tm // CMP_STRIDE, stride=CMP_STRIDE), :]
                    xc_out[gi, :, l * LANE:(l + 1) * LANE] = rows.astype(BF16)
            continue
        if out is kvs_out:
            kv = kv + jnp.concatenate([blk_ref[...]] * NSA_GROUPS, axis=1)
        out[...] = kv.astype(BF16)
    gv = NSA_GROUPS * NSA_V
    for n, out in enumerate((kvst_out, kvwt_out)):
        vt = lax.dot_general(wkvt_ref[n * gv:(n + 1) * gv, :], xn, _NT, preferred_element_type=F32)
        _store_value_tiles(out, vt, NSA_GROUPS)
    gl = lax.dot_general(wgt_ref[...], xn, _NT, preferred_element_type=F32)
    gate_out[...] = jax.nn.sigmoid(gl)


def _nsa_proj(h, g, w, blk_onehot):
    T = h.shape[0]
    tm = PROJ_TM
    ns = blk_onehot.shape[0] // tm
    row = lambda i: (i, 0)
    col = lambda i: (0, i)
    return pl.pallas_call(
        _nsa_proj_body,
        grid=(T // tm,),
        in_specs=[pl.BlockSpec((tm, D_MODEL), row), _const_spec((1, D_MODEL)), _const_spec(w["w_qt"].shape),
                  _const_spec(w["w_kv"].shape), _const_spec(w["w_kvt"].shape), _const_spec(w["w_gt"].shape),
                  pl.BlockSpec((tm, LANE), lambda i: (i % ns, 0))],
        out_specs=[pl.BlockSpec((NSA_QW, tm), col),
                   pl.BlockSpec((NSA_GROUPS, tm // CMP_STRIDE, CMP_STRIDE * LANE), lambda i: (0, i, 0)),
                   pl.BlockSpec((tm, NSA_KVW), row), pl.BlockSpec((tm, NSA_KVW), row),
                   pl.BlockSpec((NSA_KVW, tm), col), pl.BlockSpec((NSA_KVW, tm), col),
                   pl.BlockSpec((LANE, tm), col)],
        out_shape=[jax.ShapeDtypeStruct((NSA_QW, T), BF16),
                   jax.ShapeDtypeStruct((NSA_GROUPS, T // CMP_STRIDE, CMP_STRIDE * LANE), BF16),
                   jax.ShapeDtypeStruct((T, NSA_KVW), BF16), jax.ShapeDtypeStruct((T, NSA_KVW), BF16),
                   jax.ShapeDtypeStruct((NSA_KVW, T), BF16), jax.ShapeDtypeStruct((NSA_KVW, T), BF16),
                   jax.ShapeDtypeStruct((LANE, T), F32)],
        scratch_shapes=[pltpu.VMEM((NSA_GROUPS, tm, LANE), F32)],
        compiler_params=_cparams(("parallel",)),
        name="nsa_proj",
    )(h, g.reshape(1, D_MODEL), w["w_qt"], w["w_kv"], w["w_kvt"], w["w_gt"], blk_onehot)


def _compress_body(x_ref, pos_ref, w1_ref, w2_ref, out_ref, out_t_ref):
    n_chunk = x_ref.shape[0]
    x = x_ref[...].astype(F32)
    xa = (x + pos_ref[0]).astype(BF16)
    xb = (x + pos_ref[1]).astype(BF16)
    a = jnp.dot(xa, w1_ref[0], preferred_element_type=F32)
    b = jnp.dot(xb, w1_ref[1], preferred_element_type=F32)
    pre = a + pltpu.roll(b, n_chunk - 1, 0)
    hid = jax.nn.gelu(pre, approximate=True).astype(BF16)
    kv = jnp.dot(hid, w2_ref[...], preferred_element_type=F32)
    out_ref[...] = kv.astype(BF16)
    out_t_ref[...] = kv.T.astype(BF16)


def _compress(x, pos, w1, w2, B):
    G, rows, width = x.shape
    n_chunk = rows // B
    return pl.pallas_call(
        _compress_body,
        grid=(B, G),
        in_specs=[pl.BlockSpec((None, n_chunk, width), lambda b, g: (g, b, 0)),
                  _const_spec(pos.shape), _const_spec(w1.shape), _const_spec(w2.shape)],
        out_specs=[pl.BlockSpec((None, None, n_chunk, LANE), lambda b, g: (b, g, 0, 0)),
                   pl.BlockSpec((None, None, LANE, n_chunk), lambda b, g: (b, g, 0, 0))],
        out_shape=[jax.ShapeDtypeStruct((B, G, n_chunk, LANE), BF16),
                   jax.ShapeDtypeStruct((B, G, LANE, n_chunk), BF16)],
        compiler_params=_cparams(("parallel", "parallel")),
        name="nsa_compress",
    )(x, pos, w1, w2)


def _nsa_attn_body(*refs, S):
    step = pl.program_id(2)
    for qi in range(S // NSA_T):
        pl.when(step == qi)(functools.partial(_nsa_tile, qi, *refs, S=S))


def _nsa_tile(qi, qt_ref, qtn_ref, kvc_ref, kvct_ref, kvs_ref, kvst_ref, kvw_ref, kvwt_ref, gate_ref, cbt_ref,
              dt_ref, ovt_ref, o_ref, *scratch, S):
    t = NSA_T
    R = NSA_R
    n_sel = S // SEL_BLOCK
    t0 = qi * t
    ns = R // NSA_STREAM_HEADS
    s_sel, s_win = scratch[0:ns], scratch[ns:2 * ns]
    ocmp_scr, sel_scr = scratch[2 * ns:2 * ns + 2]
    heads_on_lanes = lambda ref: jnp.concatenate([ref[r * LANE:(r + 1) * LANE, :] for r in range(R)], axis=1)
    qt = heads_on_lanes(qt_ref)
    tile_r = lambda x: jnp.concatenate([x] * R, axis=1)
    half_w = NSA_STREAM_HEADS * t
    tile_h = lambda x: jnp.concatenate([x] * (half_w // t), axis=1)

    def compressed_and_selection(q_tile, tile, out):
        kvc = kvc_ref[...]
        n_cp = kvc.shape[0]
        sc = jnp.dot(kvc, q_tile, preferred_element_type=F32)
        yield
        cend = lax.broadcasted_iota(jnp.int32, (n_cp, t), 0) * CMP_STRIDE + (CMP_BLOCK - 1)
        valid = tile_r(tile * t + lax.broadcasted_iota(jnp.int32, (n_cp, t), 1) >= cend)
        shift = (tile * (t // CMP_STRIDE) + CB_CENTER) % n_cp
        cb = pltpu.roll(cbt_ref[...], shift, 0) if shift else cbt_ref[...]
        sc = jnp.where(valid, sc + cb, NEG)
        e = jnp.exp2(sc - jnp.max(sc, axis=0, keepdims=True))
        yield
        p = jnp.where(valid, e / jnp.sum(e, axis=0, keepdims=True), 0.0)
        out["o_cmp"] = jnp.dot(kvct_ref[...], p.astype(BF16), preferred_element_type=F32)
        yield

        psum = p[:, 0:t]
        for r in range(1, R):
            psum = psum + p[:, r * t:(r + 1) * t]
        imp = jnp.dot(ovt_ref[...], psum, preferred_element_type=F32, precision=lax.Precision.HIGHEST)
        yield
        jj = lax.broadcasted_iota(jnp.int32, (n_sel, t), 0)
        blk_t = (tile * t + lax.broadcasted_iota(jnp.int32, (n_sel, t), 1)) >> SEL_SHIFT
        forced = (jj == 0) | (jj == blk_t) | (jj == blk_t - 1)
        score = jnp.where(forced, 1e6, jnp.where(jj <= blk_t, imp, -1e6))
        cnt = jnp.zeros((n_sel, t), jnp.int32)
        for jp in range(n_sel):
            row = score[jp:jp + 1, :]
            beats = (row > score) | ((row == score) & (jj > jp))
            cnt = cnt + beats.astype(jnp.int32)
            if jp % RANK_STEPS_PER_STAGE == RANK_STEPS_PER_STAGE - 1:
                yield
        out["sel"] = jnp.where(cnt < min(SEL_TOP_N, n_sel), 0.0, NEG).astype(BF16)

    if qi == 0:
        cur = {}
        for _ in compressed_and_selection(qt, 0, cur):
            pass
        o_cmp, sel = cur["o_cmp"], cur["sel"]
    else:
        o_cmp, sel = ocmp_scr[...], sel_scr[...]
    nxt, side = {}, []
    if (qi + 1) * t < S:
        side = [compressed_and_selection(heads_on_lanes(qtn_ref), qi + 1, nxt)]

    kw = min(qi, WINDOW // t)
    nw = (kw + 1) * t
    w0 = (qi - kw) * t
    n = (qi + 1) * t

    def win_add(off, w):
        lo_d, hi_d = kw * t - off - (w - 1), kw * t - off + (t - 1)
        if lo_d >= 0 and hi_d < WINDOW:
            return None
        dist = (kw * t - off + lax.broadcasted_iota(jnp.int32, (w, t), 1)
                - lax.broadcasted_iota(jnp.int32, (w, t), 0))
        return jnp.where((dist >= 0) & (dist < WINDOW), 0.0, NEG)

    q_sel = jnp.concatenate([qt[0:NSA_QK], tile_r(sel), jnp.zeros((LANE - NSA_QK - n_sel, R * t), BF16)], axis=0)

    def sel_add(off, w):
        if off + w <= n - t:
            return None
        kpos = off + lax.broadcasted_iota(jnp.int32, (w, t), 0)
        return jnp.where(kpos <= t0 + lax.broadcasted_iota(jnp.int32, (w, t), 1), 0.0, NEG)

    def stream(half, q_all, k_ref, vt_ref, k0, nk, add_fn, s_ref):
        cs = slice(half * half_w, (half + 1) * half_w)
        qh = q_all[:, cs]

        def score(off, w):
            s = jnp.dot(k_ref[k0 + off:k0 + off + w, :], qh, preferred_element_type=F32)
            add = add_fn(off, w)
            if add is not None:
                s = s + tile_h(add)
            d0 = off - (nk - 2 * t)
            if d0 + w > 0:
                assert d0 >= 0
                s = s + dt_ref[d0:d0 + w, cs]
            return s

        values = lambda off, w: vt_ref[:, k0 + off:k0 + off + w]
        return _SoftmaxStreamT(score, _chunks(nk, lead=nk % SCORE_CHUNK), values, s_ref)

    halves = range(R * t // half_w)
    wins = [stream(h, qt, kvw_ref, kvwt_ref, w0, nw, win_add, s_win[h]) for h in halves]
    sels = [stream(h, q_sel, kvs_ref, kvst_ref, 0, n, sel_add, s_sel[h]) for h in halves]
    _trace_pipelined(wins[:1] + sels + wins[1:])
    for gen in side:
        for _ in gen:
            pass

    g = pl.program_id(0)
    heads = []
    for r in range(R):
        h, hc = divmod(r * t, half_w)
        branches = (o_cmp[NSA_QK:, r * t:(r + 1) * t], sels[h].out[NSA_QK:, hc:hc + t], wins[h].out[NSA_QK:, hc:hc + t])
        o = None
        for br, ob in enumerate(branches):
            gate = gate_ref[pl.ds(g * GATES_PER_GROUP + br * R + r, 1), :]
            o = gate * ob if o is None else o + gate * ob
        heads.append(o)
    o_ref[...] = jnp.concatenate(heads, axis=0).T.astype(BF16)
    if nxt:
        ocmp_scr[...] = nxt["o_cmp"]
        sel_scr[...] = nxt["sel"]


def _nsa_attn(qt, kvc, kvct, kvs, kvst, kvw, kvwt, gates, cbt, dtt, ovt, B, S):
    t = NSA_T
    nq = S // t
    T = B * S
    G, R = NSA_GROUPS, NSA_R
    n_cp = kvc.shape[2]
    half_w = NSA_STREAM_HEADS * t
    nh = R // NSA_STREAM_HEADS
    kv_spec = pl.BlockSpec((S, LANE), lambda g, b, i: (b, g))
    kvt_spec = pl.BlockSpec((LANE, S), lambda g, b, i: (g, b))
    return pl.pallas_call(
        functools.partial(_nsa_attn_body, S=S),
        grid=(G, B, nq),
        in_specs=[pl.BlockSpec((R * LANE, t), lambda g, b, i: (g, b * nq + i)),
                  pl.BlockSpec((R * LANE, t), lambda g, b, i: (g, b * nq + jnp.minimum(i + 1, nq - 1))),
                  pl.BlockSpec((None, None, n_cp, LANE), lambda g, b, i: (b, g, 0, 0)),
                  pl.BlockSpec((None, None, LANE, n_cp), lambda g, b, i: (b, g, 0, 0)),
                  kv_spec, kvt_spec, kv_spec, kvt_spec,
                  pl.BlockSpec((LANE, t), lambda g, b, i: (0, b * nq + i)),
                  pl.BlockSpec((None, n_cp, R * t), lambda g, b, i: (g, 0, 0)),
                  pl.BlockSpec((None, 2 * t, R * t), lambda g, b, i: (g, 0, 0)),
                  _const_spec(ovt.shape)],
        out_specs=pl.BlockSpec((t, R * NSA_V), lambda g, b, i: (b * nq + i, g)),
        out_shape=jax.ShapeDtypeStruct((T, NSA_HEADS * NSA_V), BF16),
        scratch_shapes=([pltpu.VMEM((S, half_w), F32)] * nh + [pltpu.VMEM((WINDOW + t, half_w), F32)] * nh
                        + [pltpu.VMEM((LANE, R * t), F32), pltpu.VMEM((S // SEL_BLOCK, t), BF16)]),
        compiler_params=_cparams(("arbitrary", "arbitrary", "arbitrary")),
        name="nsa_attn",
    )(qt, qt, kvc, kvct, kvs, kvst, kvw, kvwt, gates, cbt, dtt, ovt)


def _t5_bucket(dist):
    n = jnp.maximum(dist, 0)
    max_exact = REL_BUCKETS // 2
    nf = jnp.maximum(n, 1).astype(F32)
    large = max_exact + (jnp.log(nf / max_exact) / math.log(REL_MAX_DIST / max_exact)
                         * (REL_BUCKETS - max_exact)).astype(jnp.int32)
    large = jnp.minimum(large, REL_BUCKETS - 1)
    return jnp.where(n < max_exact, n, large)


def _np_bucket(n):
    n = np.maximum(np.asarray(n), 0)
    max_exact = REL_BUCKETS // 2
    large = max_exact + (np.log(np.maximum(n, 1) / max_exact) / math.log(REL_MAX_DIST / max_exact)
                         * (REL_BUCKETS - max_exact)).astype(np.int64)
    return np.where(n < max_exact, n, np.minimum(large, REL_BUCKETS - 1))


def _bias_tables(rel_bias, S):
    t = NSA_T
    G, R = NSA_GROUPS, NSA_R
    n_cp = S // CMP_STRIDE
    a = np.arange(t)[:, None]
    dist_d = a - np.arange(2 * t)[None, :] + t
    dist_c = a - CMP_STRIDE * (np.arange(n_cp)[None, :] - CB_CENTER) - (CMP_BLOCK - 1)
    uncovered = min(t + 1, CMP_STRIDE * (CB_CENTER + 1) - (CMP_BLOCK - 1))
    assert (_np_bucket(np.arange(uncovered, 2 * S)) == REL_BUCKETS - 1).all()
    wrap_from = n_cp - max((S // t - 1) * (t // CMP_STRIDE) - CB_CENTER, 0)
    assert (dist_c[:, wrap_from:] < 0).all() and (dist_c[:, -1] < 0).all()

    def lookup(dist):
        oh = jax.nn.one_hot(_t5_bucket(jnp.asarray(dist)), REL_BUCKETS, dtype=F32)
        oh = oh - jax.nn.one_hot(REL_BUCKETS - 1, REL_BUCKETS, dtype=F32)
        val = jnp.einsum("acb,bh->hac", oh, rel_bias, precision=lax.Precision.HIGHEST)
        return val.reshape(G, R, *dist.shape)

    dtab = lookup(dist_d)
    cbias = jnp.where(jnp.asarray(dist_c >= 0), lookup(dist_c), 0.0)
    keys_first = lambda x: x.transpose(0, 3, 1, 2).reshape(G, x.shape[3], R * t)
    return keys_first(dtab) * LOG2E, keys_first(cbias) * LOG2E


def _selection_tables(S):
    n_cp = S // CMP_STRIDE
    n_cmp = (S - CMP_BLOCK) // CMP_STRIDE + 1
    n_sel = S // SEL_BLOCK
    cs = np.arange(n_cp) * CMP_STRIDE
    ce = cs + CMP_BLOCK
    ss = np.arange(n_sel) * SEL_BLOCK
    se = ss + SEL_BLOCK
    ov = np.minimum(ce[:, None], se[None, :]) - np.maximum(cs[:, None], ss[None, :])
    ov = (np.clip(ov, 0, None) / CMP_BLOCK).astype(np.float32)
    ov[n_cmp:] = 0.0
    assert n_sel <= LANE - NSA_QK
    blk_onehot = (np.arange(S)[:, None] // SEL_BLOCK == np.arange(LANE)[None, :] - NSA_QK).astype(np.float32)
    return jnp.asarray(ov.T), jnp.asarray(blk_onehot)


def _rope_tables(S):
    half = MLA_ROPE // 2
    inv = ROPE_THETA ** (-jnp.arange(half, dtype=F32) * 2.0 / MLA_ROPE)
    ang = jnp.arange(S, dtype=F32)[:, None] * inv[None, :]
    cos, sin = jnp.cos(ang), jnp.sin(ang)
    ones = jnp.ones((S, MLA_NOPE), F32)
    pad1 = jnp.ones((S, LANE - MLA_NOPE - MLA_ROPE), F32)
    cos128 = jnp.concatenate([ones, cos, cos, pad1], axis=1)
    sin128 = jnp.concatenate([0 * ones, sin, sin, 0 * pad1], axis=1)
    return cos128, sin128


def _mla_weights(w_in, q_norm, kv_norm, w_uq, w_ukv, w_o):
    H = MLA_HEADS
    half = MLA_ROPE // 2
    pad = LANE - MLA_NOPE - MLA_ROPE
    kr = w_in[:, MLA_Q_LORA + MLA_KV_LORA:]
    kr_sw = jnp.concatenate([-kr[:, half:], kr[:, :half]], axis=1)
    z = lambda n: jnp.zeros((D_MODEL, n), F32)
    w_in_ext = jnp.concatenate([w_in[:, :MLA_Q_LORA + MLA_KV_LORA],
                                z(MLA_NOPE), kr, z(pad), z(MLA_NOPE), kr_sw, z(pad)], axis=1)
    uq = w_uq.reshape(MLA_Q_LORA, H, MLA_NOPE + MLA_ROPE)
    qn, qr = uq[..., :MLA_NOPE], uq[..., MLA_NOPE:]
    zq = jnp.zeros((MLA_Q_LORA, H, pad), F32)
    w_uq_p = jnp.concatenate([qn, qr, zq], axis=-1).reshape(MLA_Q_LORA, H * LANE)
    ukv = w_ukv.reshape(MLA_KV_LORA, H, MLA_NOPE + MLA_V)
    zk = jnp.zeros((MLA_KV_LORA, H, LANE - MLA_NOPE), F32)
    w_uk = jnp.concatenate([ukv[..., :MLA_NOPE], zk], axis=-1).reshape(MLA_KV_LORA, H * LANE)
    w_uv = ukv[..., MLA_NOPE:].reshape(MLA_KV_LORA, H * MLA_V)
    return dict(w_in=w_in_ext.astype(BF16), q_norm=q_norm.reshape(1, -1), kv_norm=kv_norm.reshape(1, -1),
                w_uqt=w_uq_p.T.astype(BF16), w_uk=w_uk.astype(BF16),
                w_uvt=w_uv.T.astype(BF16), w_o=w_o.astype(BF16))


def _nsa_weights(w_in, pos_k, w1_k, w2_k, pos_v, w1_v, w2_v, w_o):
    H, G = NSA_HEADS, NSA_GROUPS
    gw = G * NSA_QK
    q0 = H * NSA_QK
    wq = w_in[:, :q0]
    cols, cols_t = [], []
    for n in range(3):
        k = w_in[:, q0 + (2 * n) * gw: q0 + (2 * n + 1) * gw].reshape(D_MODEL, G, NSA_QK)
        v = w_in[:, q0 + (2 * n + 1) * gw: q0 + (2 * n + 2) * gw].reshape(D_MODEL, G, NSA_V)
        right = v if n == 0 else jnp.zeros_like(v)
        cols.append(jnp.concatenate([k, right], axis=-1).reshape(D_MODEL, G * LANE))
        if n > 0:
            cols_t.append(v.reshape(D_MODEL, G * NSA_V))
    w_kv = jnp.concatenate(cols, axis=1)
    w_vt = jnp.concatenate(cols_t, axis=1).T
    wg = w_in[:, q0 + 6 * gw:].reshape(D_MODEL, G, NSA_R, 3).transpose(0, 1, 3, 2).reshape(D_MODEL, 3 * H)
    wg = jnp.concatenate([wg, jnp.zeros((D_MODEL, LANE - 3 * H), F32)], axis=1)
    eye = jnp.eye(2, dtype=F32)
    cw = CMP_STRIDE * 2 * NSA_QK
    pos = jnp.stack([pos_k, pos_v]).reshape(2, 2, CMP_STRIDE, NSA_QK)
    pos = pos.transpose(1, 2, 0, 3).reshape(2, 1, cw)
    w1 = jnp.stack([w1_k, w1_v]).reshape(2, 2, CMP_STRIDE, NSA_QK, CMP_HIDDEN)
    w1 = jnp.einsum("khldj,kq->hlkdqj", w1, eye).reshape(2, cw, 2 * CMP_HIDDEN).astype(BF16)
    w2 = jnp.einsum("kjd,kq->kjqd", jnp.stack([w2_k, w2_v]), eye).reshape(2 * CMP_HIDDEN, 2 * NSA_QK).astype(BF16)
    return dict(w_qt=wq.T.astype(BF16), w_kv=w_kv.astype(BF16), w_kvt=w_vt.astype(BF16),
                w_gt=wg.T.astype(BF16), pos=pos, w1=w1, w2=w2, w_o=w_o.astype(BF16))


def kernel(x, ffn_norm_a, ffn_a_w_gate, ffn_a_w_up, ffn_a_w_down, mix_norm, ffn_norm_b, ffn_b_w_gate, ffn_b_w_up, ffn_b_w_down, final_norm, rel_bias, mla_w_in, mla_q_norm, mla_kv_norm, mla_w_uq, mla_w_ukv, mla_w_o, nsa_w_in, nsa_cmp_pos_k, nsa_cmp_w1_k, nsa_cmp_w2_k, nsa_cmp_pos_v, nsa_cmp_w1_v, nsa_cmp_w2_v, nsa_w_o):
    B, S, D = x.shape
    assert D == D_MODEL and S % ATT_T == 0 and S % NSA_T == 0 and (B * S) % FFN_TM == 0
    T = B * S
    cos128, sin128 = _rope_tables(S)
    dtab, cbias = _bias_tables(rel_bias, S)
    ovt, blk_onehot = _selection_tables(S)

    h = x.reshape(T, D)
    for i in range(DEPTH):
        h = _ffn(h, ffn_norm_a[i], ffn_a_w_gate, ffn_a_w_up, ffn_a_w_down, i)
        j = i // N_MIXERS
        if i % N_MIXERS == 0:
            w = _mla_weights(mla_w_in[j], mla_q_norm[j], mla_kv_norm[j], mla_w_uq[j], mla_w_ukv[j], mla_w_o[j])
            qt, k, vt = _mla_proj(h, mix_norm[i], w, cos128, sin128, S)
            o = _mla_attn(qt, k, vt, B, S)
        else:
            w = _nsa_weights(nsa_w_in[j], nsa_cmp_pos_k[j], nsa_cmp_w1_k[j], nsa_cmp_w2_k[j],
                             nsa_cmp_pos_v[j], nsa_cmp_w1_v[j], nsa_cmp_w2_v[j], nsa_w_o[j])
            qt, xc, kvs, kvw, kvst, kvwt, gates = _nsa_proj(h, mix_norm[i], w, blk_onehot)
            kvc, kvct = _compress(xc, w["pos"], w["w1"], w["w2"], B)
            o = _nsa_attn(qt, kvc, kvct, kvs, kvst, kvw, kvwt, gates, cbias, dtab, ovt, B, S)
        w_o = mla_w_o if i % N_MIXERS == 0 else nsa_w_o
        h = _ffn(h, ffn_norm_b[i], ffn_b_w_gate, ffn_b_w_up, ffn_b_w_down, i,
                 proj=(o, w_o, j), final_g=final_norm if i == DEPTH - 1 else None)
    return h.reshape(B, S, D)
```

```python
import functools
import math

import numpy as np
import jax
import jax.numpy as jnp
from jax import lax
from jax.experimental import pallas as pl
from jax.experimental.pallas import tpu as pltpu

F32 = jnp.float32
BF16 = jnp.bfloat16

D_MODEL = 1024
DEPTH = 4
N_MIXERS = 2
RMS_EPS = 1e-6
FFN_HIDDEN = 2816
NEG = -1e30
MLA_HEADS = 16
MLA_Q_LORA = 384
MLA_KV_LORA = 256
MLA_NOPE = 64
MLA_ROPE = 32
MLA_V = 64
ROPE_THETA = 10000.0
NSA_HEADS = 16
NSA_GROUPS = 4
NSA_R = NSA_HEADS // NSA_GROUPS
NSA_QK = 64
NSA_V = 64
CMP_BLOCK = 32
CMP_STRIDE = 16
CMP_HIDDEN = 128
SEL_BLOCK = 64
SEL_SHIFT = SEL_BLOCK.bit_length() - 1
assert 1 << SEL_SHIFT == SEL_BLOCK
SEL_TOP_N = 16
WINDOW = 512
REL_BUCKETS = 32
REL_MAX_DIST = 128

LANE = 128
VMEM_LIMIT = 56 * 1024 * 1024

FFN_TM = 512
FFN_TF = 256
PROJ_TM = 512
ATT_T = 256
NSA_T = 256
CB_CENTER = 64


def _cparams(sem):
    return pltpu.CompilerParams(dimension_semantics=sem, vmem_limit_bytes=VMEM_LIMIT)


def _rms(x, g):
    ms = jnp.mean(x * x, axis=-1, keepdims=True)
    return x * lax.rsqrt(ms + RMS_EPS) * g


def _const_spec(shape):
    nd = len(shape)
    return pl.BlockSpec(shape, lambda *_: (0,) * nd)


def _wdot(a, w):
    return lax.dot_general(a, w, (((1,), (0,)), ((), ())), preferred_element_type=F32)


def _ffn_body(*refs, has_proj, has_final):
    it = iter(refs)
    h_ref = next(it)
    if has_proj:
        o_in_ref = next(it)
        wo_ref = next(it)
    g_ref = next(it)
    wg_ref = next(it)
    wu_ref = next(it)
    wd_ref = next(it)
    if has_final:
        gf_ref = next(it)
    out_ref = next(it)
    a_ref = next(it)

    x = h_ref[...]
    if has_proj:
        x = x + _wdot(o_in_ref[...], wo_ref[...])
    xn = _rms(x, g_ref[...]).astype(BF16)
    for c in range(FFN_HIDDEN // FFN_TF):
        sl = slice(c * FFN_TF, (c + 1) * FFN_TF)
        gt = _wdot(xn, wg_ref[:, sl])
        up = _wdot(xn, wu_ref[:, sl])
        a_ref[:, sl] = (gt * jax.nn.sigmoid(gt) * up).astype(BF16)
    y = x + 0.5 * _wdot(a_ref[...], wd_ref[...])
    if has_final:
        y = _rms(y, gf_ref[...])
    out_ref[...] = y


def _layer_spec(stacked, layer):
    return pl.BlockSpec((None,) + stacked.shape[1:], lambda i: (layer, 0, 0), pipeline_mode=pl.Buffered(1))


def _ffn(h, g, wg, wu, wd, layer, proj=None, final_g=None):
    T = h.shape[0]
    tm = FFN_TM
    row = lambda i: (i, 0)
    in_specs = [pl.BlockSpec((tm, D_MODEL), row)]
    args = [h]
    if proj is not None:
        o_in, wo, wo_idx = proj
        in_specs += [pl.BlockSpec((tm, o_in.shape[1]), row), _layer_spec(wo, wo_idx)]
        args += [o_in, wo]
    in_specs += [_const_spec((1, D_MODEL)), _layer_spec(wg, layer), _layer_spec(wu, layer), _layer_spec(wd, layer)]
    args += [g.reshape(1, D_MODEL), wg, wu, wd]
    if final_g is not None:
        in_specs.append(_const_spec((1, D_MODEL)))
        args.append(final_g.reshape(1, D_MODEL))
    return pl.pallas_call(
        functools.partial(_ffn_body, has_proj=proj is not None, has_final=final_g is not None),
        grid=(T // tm,),
        in_specs=in_specs,
        out_specs=pl.BlockSpec((tm, D_MODEL), row),
        out_shape=jax.ShapeDtypeStruct((T, D_MODEL), F32),
        scratch_shapes=[pltpu.VMEM((tm, FFN_HIDDEN), BF16)],
        compiler_params=_cparams(("parallel",)),
        name="ffn",
    )(*args)


MLA_CQ0 = 0
MLA_CKV0 = MLA_CQ0 + MLA_Q_LORA
MLA_KR0 = MLA_CKV0 + MLA_KV_LORA
MLA_KRS0 = MLA_KR0 + LANE
MLA_IN_W = MLA_KRS0 + LANE
MLA_HCHUNK = 4


def _store_value_tiles(out_ref, vt, n_tiles):
    d, tm = vt.shape[0] // n_tiles, vt.shape[1]
    pad = jnp.where(lax.broadcasted_iota(jnp.int32, (LANE - d, tm), 0) == 0, 1.0, 0.0).astype(out_ref.dtype)
    for i in range(n_tiles):
        out_ref[i * LANE:i * LANE + LANE - d, :] = pad
        out_ref[i * LANE + LANE - d:(i + 1) * LANE, :] = vt[i * d:(i + 1) * d].astype(out_ref.dtype)


def _mla_proj_body(h_ref, g_ref, win_ref, qn_ref, kvn_ref, wuqt_ref, wuk_ref, wuvt_ref,
                   cos_ref, sin_ref, cost_ref, sinat_ref, sinbt_ref, qt_out, k_out, vt_out):
    xn = _rms(h_ref[...], g_ref[...]).astype(BF16)
    proj = jnp.dot(xn, win_ref[...], preferred_element_type=F32)
    cq = _rms(proj[:, MLA_CQ0:MLA_CKV0], qn_ref[...]).astype(BF16)
    ckv = _rms(proj[:, MLA_CKV0:MLA_KR0], kvn_ref[...]).astype(BF16)
    cos = cos_ref[...]
    sin = sin_ref[...]
    kr = proj[:, MLA_KR0:MLA_KRS0] * cos + proj[:, MLA_KRS0:MLA_IN_W] * sin
    cos_t = cost_ref[...]
    sina_t = sinat_ref[...]
    sinb_t = sinbt_ref[...]
    half = MLA_ROPE // 2
    scale = (MLA_NOPE + MLA_ROPE) ** -0.5 * LOG2E
    vt = lax.dot_general(wuvt_ref[...], ckv, _NT, preferred_element_type=F32)
    _store_value_tiles(vt_out, vt, MLA_HEADS)
    cw = MLA_HCHUNK * LANE
    for c in range(MLA_HEADS // MLA_HCHUNK):
        sl = slice(c * cw, (c + 1) * cw)
        qt = lax.dot_general(wuqt_ref[sl, :], cq, _NT, preferred_element_type=F32)
        kn = jnp.dot(ckv, wuk_ref[:, sl], preferred_element_type=F32)
        for hh in range(MLA_HCHUNK):
            hs = slice(hh * LANE, (hh + 1) * LANE)
            os_ = slice(c * cw + hh * LANE, c * cw + (hh + 1) * LANE)
            qh = qt[hs]
            roped = (qh * cos_t + pltpu.roll(qh, LANE - half, 0) * sina_t + pltpu.roll(qh, half, 0) * sinb_t)
            qt_out[os_, :] = (roped * scale).astype(BF16)
            k_out[:, os_] = (kn[:, hs] + kr).astype(BF16)


def _mla_proj(h, g, w, cos128, sin128, S):
    T = h.shape[0]
    tm = PROJ_TM
    ns = S // tm
    row = lambda i: (i, 0)
    col = lambda i: (0, i)
    pos = lambda i: (i % ns, 0)
    pos_t = lambda i: (0, i % ns)
    HL = MLA_HEADS * LANE
    HV = MLA_HEADS * LANE
    lane = np.arange(LANE)[None, :]
    x1 = (lane >= MLA_NOPE) & (lane < MLA_NOPE + MLA_ROPE // 2)
    x2 = (lane >= MLA_NOPE + MLA_ROPE // 2) & (lane < MLA_NOPE + MLA_ROPE)
    sign_a, sign_b = -x1.astype(np.float32), x2.astype(np.float32)
    return pl.pallas_call(
        _mla_proj_body,
        grid=(T // tm,),
        in_specs=[pl.BlockSpec((tm, D_MODEL), row), _const_spec((1, D_MODEL)),
                  _const_spec(w["w_in"].shape), _const_spec((1, MLA_Q_LORA)), _const_spec((1, MLA_KV_LORA)),
                  _const_spec(w["w_uqt"].shape),
                  _const_spec(w["w_uk"].shape), _const_spec(w["w_uvt"].shape),
                  pl.BlockSpec((tm, LANE), pos), pl.BlockSpec((tm, LANE), pos),
                  pl.BlockSpec((LANE, tm), pos_t), pl.BlockSpec((LANE, tm), pos_t), pl.BlockSpec((LANE, tm), pos_t)],
        out_specs=[pl.BlockSpec((HL, tm), col), pl.BlockSpec((tm, HL), row), pl.BlockSpec((HV, tm), col)],
        out_shape=[jax.ShapeDtypeStruct((HL, T), BF16), jax.ShapeDtypeStruct((T, HL), BF16),
                   jax.ShapeDtypeStruct((HV, T), BF16)],
        compiler_params=_cparams(("parallel",)),
        name="mla_proj",
    )(h, g.reshape(1, D_MODEL), w["w_in"], w["q_norm"], w["kv_norm"], w["w_uqt"],
      w["w_uk"], w["w_uvt"], cos128, sin128, cos128.T, (sin128 * sign_a).T, (sin128 * sign_b).T)


_NT = (((1,), (1,)), ((), ()))
SCORE_CHUNK = 2 * LANE


def _chunks(n, lead=0):
    out = [(0, lead)] if lead else []
    return out + [(o, min(SCORE_CHUNK, n - o)) for o in range(lead, n, SCORE_CHUNK)]


SUBLANE = 8
MLA_PV_CHUNKS = 4
NSA_PV_CHUNKS = 2


def _fold(x, op):
    w, cols = x.shape
    return op(x.reshape(w // SUBLANE, SUBLANE, cols), axis=0)


class _SoftmaxStreamT:
    def __init__(self, score_fn, chunks, values_fn, s_ref, pv_chunks):
        self.score_fn, self.chunks, self.values_fn, self.s_ref = score_fn, chunks, values_fn, s_ref
        self.pv_chunks = pv_chunks

    def pass1(self):
        mp = None
        for off, w in self.chunks:
            s = self.score_fn(off, w)
            self.s_ref[off:off + w, :] = s
            part = _fold(s, jnp.max)
            mp = part if mp is None else jnp.maximum(mp, part)
            yield
        self.m = jnp.max(mp, axis=0, keepdims=True)

    def pass2(self):
        acc = None
        pending = []
        for i, (off, w) in enumerate(self.chunks):
            pending.append((off, w, jnp.exp2(self.s_ref[off:off + w, :] - self.m).astype(BF16)))
            if len(pending) == self.pv_chunks or i == len(self.chunks) - 1:
                off0, wsum = pending[0][0], sum(c[1] for c in pending)
                p = pending[0][2] if len(pending) == 1 else jnp.concatenate([c[2] for c in pending], axis=0)
                pv = jnp.dot(self.values_fn(off0, wsum), p, preferred_element_type=F32)
                acc = pv if acc is None else acc + pv
                pending = []
            yield
        self.out = acc / acc[0:1, :]


def _trace_pipelined(streams):
    for _ in streams[0].pass1():
        pass
    for i, st in enumerate(streams):
        gens = [st.pass2()] + ([streams[i + 1].pass1()] if i + 1 < len(streams) else [])
        while gens:
            gens = [g for g in gens if next(g, StopIteration) is not StopIteration]


MLA_HPS = 8


def _mla_attn_body(qt_ref, k_ref, vt_ref, o_ref, *s_refs, nq):
    t = ATT_T
    qi = pl.program_id(2)

    def branch(nt):
        n = nt * t
        streams = []
        for hh in range(MLA_HPS):
            hs = slice(hh * LANE, (hh + 1) * LANE)
            qt = qt_ref[hs, :]

            def score(off, w, qt=qt, hs=hs):
                s = jnp.dot(k_ref[off:off + w, hs], qt, preferred_element_type=F32)
                if off + w <= n - t:
                    return s
                kpos = off + lax.broadcasted_iota(jnp.int32, (w, t), 0)
                return jnp.where(kpos <= n - t + lax.broadcasted_iota(jnp.int32, (w, t), 1), s, NEG)

            values = lambda off, w, hs=hs: vt_ref[hs, off:off + w]
            streams.append(_SoftmaxStreamT(score, _chunks(n, lead=n % SCORE_CHUNK), values, s_refs[hh], MLA_PV_CHUNKS))
        _trace_pipelined(streams)
        outs = [st.out[LANE - MLA_V:] for st in streams]
        o_ref[...] = jnp.concatenate(outs, axis=0).T.astype(BF16)

    for nt in range(1, nq + 1):
        pl.when(qi == nt - 1)(functools.partial(branch, nt))


def _mla_attn(qt, k, vt, B, S):
    t = ATT_T
    nq = S // t
    T = B * S
    hps = MLA_HPS
    return pl.pallas_call(
        functools.partial(_mla_attn_body, nq=nq),
        grid=(B, MLA_HEADS // hps, nq),
        in_specs=[pl.BlockSpec((hps * LANE, t), lambda b, p, i: (p, b * nq + i)),
                  pl.BlockSpec((S, hps * LANE), lambda b, p, i: (b, p)),
                  pl.BlockSpec((hps * LANE, S), lambda b, p, i: (p, b))],
        out_specs=pl.BlockSpec((t, hps * MLA_V), lambda b, p, i: (b * nq + i, p)),
        out_shape=jax.ShapeDtypeStruct((T, MLA_HEADS * MLA_V), BF16),
        scratch_shapes=[pltpu.VMEM((S, t), F32)] * hps,
        compiler_params=_cparams(("parallel", "parallel", "arbitrary")),
        name="mla_attn",
    )(qt, k, vt)


NSA_QW = NSA_HEADS * LANE
NSA_KVW = NSA_GROUPS * LANE
GATES_PER_GROUP = 3 * NSA_R
NSA_STREAM_HEADS = 2
RANK_STEPS_PER_STAGE = 8
LOG2E = math.log2(math.e)


def _nsa_proj_body(h_ref, g_ref, wqt_ref, wkv_ref, wkvt_ref, wgt_ref, blk_ref,
                   qt_out, xc_out, kvs_out, kvw_out, kvst_out, kvwt_out, gate_out, kvc_scr):
    xn = _rms(h_ref[...], g_ref[...]).astype(BF16)
    scale = NSA_QK ** -0.5 * LOG2E
    tm = xn.shape[0]
    hc = 4
    zeros = jnp.zeros((LANE - NSA_QK, tm), BF16)
    for c in range(NSA_HEADS // hc):
        qt = lax.dot_general(wqt_ref[c * hc * NSA_QK:(c + 1) * hc * NSA_QK, :], xn, _NT,
                             preferred_element_type=F32)
        for hh in range(hc):
            r0 = (c * hc + hh) * LANE
            qt_out[r0:r0 + NSA_QK, :] = (qt[hh * NSA_QK:(hh + 1) * NSA_QK] * scale).astype(BF16)
            qt_out[r0 + NSA_QK:r0 + LANE, :] = zeros
    for n, out in enumerate((xc_out, kvs_out, kvw_out)):
        sl = slice(n * NSA_KVW, (n + 1) * NSA_KVW)
        kv = jnp.dot(xn, wkv_ref[:, sl], preferred_element_type=F32)
        if out is xc_out:
            for gi in range(NSA_GROUPS):
                kvc_scr[gi] = kv[:, gi * LANE:(gi + 1) * LANE]
                for l in range(CMP_STRIDE):
                    rows = kvc_scr[gi, pl.ds(l, tm // CMP_STRIDE, stride=CMP_STRIDE), :]
                    xc_out[gi, :, l * LANE:(l + 1) * LANE] = rows.astype(BF16)
            continue
        if out is kvs_out:
            kv = kv + jnp.concatenate([blk_ref[...]] * NSA_GROUPS, axis=1)
        out[...] = kv.astype(BF16)
    gv = NSA_GROUPS * NSA_V
    for n, out in enumerate((kvst_out, kvwt_out)):
        vt = lax.dot_general(wkvt_ref[n * gv:(n + 1) * gv, :], xn, _NT, preferred_element_type=F32)
        _store_value_tiles(out, vt, NSA_GROUPS)
    gl = lax.dot_general(wgt_ref[...], xn, _NT, preferred_element_type=F32)
    gate_out[...] = jax.nn.sigmoid(gl)


def _nsa_proj(h, g, w, blk_onehot):
    T = h.shape[0]
    tm = PROJ_TM
    ns = blk_onehot.shape[0] // tm
    row = lambda i: (i, 0)
    col = lambda i: (0, i)
    return pl.pallas_call(
        _nsa_proj_body,
        grid=(T // tm,),
        in_specs=[pl.BlockSpec((tm, D_MODEL), row), _const_spec((1, D_MODEL)), _const_spec(w["w_qt"].shape),
                  _const_spec(w["w_kv"].shape), _const_spec(w["w_kvt"].shape), _const_spec(w["w_gt"].shape),
                  pl.BlockSpec((tm, LANE), lambda i: (i % ns, 0))],
        out_specs=[pl.BlockSpec((NSA_QW, tm), col),
                   pl.BlockSpec((NSA_GROUPS, tm // CMP_STRIDE, CMP_STRIDE * LANE), lambda i: (0, i, 0)),
                   pl.BlockSpec((tm, NSA_KVW), row), pl.BlockSpec((tm, NSA_KVW), row),
                   pl.BlockSpec((NSA_KVW, tm), col), pl.BlockSpec((NSA_KVW, tm), col),
                   pl.BlockSpec((LANE, tm), col)],
        out_shape=[jax.ShapeDtypeStruct((NSA_QW, T), BF16),
                   jax.ShapeDtypeStruct((NSA_GROUPS, T // CMP_STRIDE, CMP_STRIDE * LANE), BF16),
                   jax.ShapeDtypeStruct((T, NSA_KVW), BF16), jax.ShapeDtypeStruct((T, NSA_KVW), BF16),
                   jax.ShapeDtypeStruct((NSA_KVW, T), BF16), jax.ShapeDtypeStruct((NSA_KVW, T), BF16),
                   jax.ShapeDtypeStruct((LANE, T), F32)],
        scratch_shapes=[pltpu.VMEM((NSA_GROUPS, tm, LANE), F32)],
        compiler_params=_cparams(("parallel",)),
        name="nsa_proj",
    )(h, g.reshape(1, D_MODEL), w["w_qt"], w["w_kv"], w["w_kvt"], w["w_gt"], blk_onehot)


def _compress_body(x_ref, pos_ref, w1_ref, w2_ref, out_ref, out_t_ref):
    n_chunk = x_ref.shape[0]
    x = x_ref[...].astype(F32)
    xa = (x + pos_ref[0]).astype(BF16)
    xb = (x + pos_ref[1]).astype(BF16)
    a = jnp.dot(xa, w1_ref[0], preferred_element_type=F32)
    b = jnp.dot(xb, w1_ref[1], preferred_element_type=F32)
    pre = a + pltpu.roll(b, n_chunk - 1, 0)
    hid = jax.nn.gelu(pre, approximate=True).astype(BF16)
    kv = jnp.dot(hid, w2_ref[...], preferred_element_type=F32)
    out_ref[...] = kv.astype(BF16)
    out_t_ref[...] = kv.T.astype(BF16)


def _compress(x, pos, w1, w2, B):
    G, rows, width = x.shape
    n_chunk = rows // B
    return pl.pallas_call(
        _compress_body,
        grid=(B, G),
        in_specs=[pl.BlockSpec((None, n_chunk, width), lambda b, g: (g, b, 0)),
                  _const_spec(pos.shape), _const_spec(w1.shape), _const_spec(w2.shape)],
        out_specs=[pl.BlockSpec((None, None, n_chunk, LANE), lambda b, g: (b, g, 0, 0)),
                   pl.BlockSpec((None, None, LANE, n_chunk), lambda b, g: (b, g, 0, 0))],
        out_shape=[jax.ShapeDtypeStruct((B, G, n_chunk, LANE), BF16),
                   jax.ShapeDtypeStruct((B, G, LANE, n_chunk), BF16)],
        compiler_params=_cparams(("parallel", "parallel")),
        name="nsa_compress",
    )(x, pos, w1, w2)


def _nsa_attn_body(*refs, S):
    step = pl.program_id(2)
    for qi in range(S // NSA_T):
        pl.when(step == qi)(functools.partial(_nsa_tile, qi, *refs, S=S))


def _nsa_tile(qi, qt_ref, qtn_ref, kvc_ref, kvct_ref, kvs_ref, kvst_ref, kvw_ref, kvwt_ref, gate_ref, cbt_ref,
              dt_ref, ovt_ref, o_ref, *scratch, S):
    t = NSA_T
    R = NSA_R
    n_sel = S // SEL_BLOCK
    t0 = qi * t
    ns = R // NSA_STREAM_HEADS
    s_sel, s_win = scratch[0:ns], scratch[ns:2 * ns]
    ocmp_scr, sel_scr = scratch[2 * ns:2 * ns + 2]
    heads_on_lanes = lambda ref: jnp.concatenate([ref[r * LANE:(r + 1) * LANE, :] for r in range(R)], axis=1)
    qt = heads_on_lanes(qt_ref)
    tile_r = lambda x: jnp.concatenate([x] * R, axis=1)
    half_w = NSA_STREAM_HEADS * t
    tile_h = lambda x: jnp.concatenate([x] * (half_w // t), axis=1)

    def compressed_and_selection(q_tile, tile, out):
        kvc = kvc_ref[...]
        n_cp = kvc.shape[0]
        sc = jnp.dot(kvc, q_tile, preferred_element_type=F32)
        yield
        cend = lax.broadcasted_iota(jnp.int32, (n_cp, t), 0) * CMP_STRIDE + (CMP_BLOCK - 1)
        valid = tile_r(tile * t + lax.broadcasted_iota(jnp.int32, (n_cp, t), 1) >= cend)
        shift = (tile * (t // CMP_STRIDE) + CB_CENTER) % n_cp
        cb = pltpu.roll(cbt_ref[...], shift, 0) if shift else cbt_ref[...]
        sc = jnp.where(valid, sc + cb, NEG)
        e = jnp.exp2(sc - jnp.max(sc, axis=0, keepdims=True))
        yield
        p = jnp.where(valid, e / jnp.sum(e, axis=0, keepdims=True), 0.0)
        out["o_cmp"] = jnp.dot(kvct_ref[...], p.astype(BF16), preferred_element_type=F32)
        yield

        psum = p[:, 0:t]
        for r in range(1, R):
            psum = psum + p[:, r * t:(r + 1) * t]
        imp = jnp.dot(ovt_ref[...], psum, preferred_element_type=F32, precision=lax.Precision.HIGHEST)
        yield
        jj = lax.broadcasted_iota(jnp.int32, (n_sel, t), 0)
        blk_t = (tile * t + lax.broadcasted_iota(jnp.int32, (n_sel, t), 1)) >> SEL_SHIFT
        forced = (jj == 0) | (jj == blk_t) | (jj == blk_t - 1)
        score = jnp.where(forced, 1e6, jnp.where(jj <= blk_t, imp, -1e6))
        cnt = jnp.zeros((n_sel, t), jnp.int32)
        for jp in range(n_sel):
            row = score[jp:jp + 1, :]
            beats = (row > score) | ((row == score) & (jj > jp))
            cnt = cnt + beats.astype(jnp.int32)
            if jp % RANK_STEPS_PER_STAGE == RANK_STEPS_PER_STAGE - 1:
                yield
        out["sel"] = jnp.where(cnt < min(SEL_TOP_N, n_sel), 0.0, NEG).astype(BF16)

    if qi == 0:
        cur = {}
        for _ in compressed_and_selection(qt, 0, cur):
            pass
        o_cmp, sel = cur["o_cmp"], cur["sel"]
    else:
        o_cmp, sel = ocmp_scr[...], sel_scr[...]
    nxt, side = {}, []
    if (qi + 1) * t < S:
        side = [compressed_and_selection(heads_on_lanes(qtn_ref), qi + 1, nxt)]

    kw = min(qi, WINDOW // t)
    nw = (kw + 1) * t
    w0 = (qi - kw) * t
    n = (qi + 1) * t

    def win_add(off, w):
        lo_d, hi_d = kw * t - off - (w - 1), kw * t - off + (t - 1)
        if lo_d >= 0 and hi_d < WINDOW:
            return None
        dist = (kw * t - off + lax.broadcasted_iota(jnp.int32, (w, t), 1)
                - lax.broadcasted_iota(jnp.int32, (w, t), 0))
        return jnp.where((dist >= 0) & (dist < WINDOW), 0.0, NEG)

    q_sel = jnp.concatenate([qt[0:NSA_QK], tile_r(sel), jnp.zeros((LANE - NSA_QK - n_sel, R * t), BF16)], axis=0)

    def sel_add(off, w):
        if off + w <= n - t:
            return None
        kpos = off + lax.broadcasted_iota(jnp.int32, (w, t), 0)
        return jnp.where(kpos <= t0 + lax.broadcasted_iota(jnp.int32, (w, t), 1), 0.0, NEG)

    def stream(half, q_all, k_ref, vt_ref, k0, nk, add_fn, s_ref):
        cs = slice(half * half_w, (half + 1) * half_w)
        qh = q_all[:, cs]

        def score(off, w):
            s = jnp.dot(k_ref[k0 + off:k0 + off + w, :], qh, preferred_element_type=F32)
            add = add_fn(off, w)
            if add is not None:
                s = s + tile_h(add)
            d0 = off - (nk - 2 * t)
            if d0 + w > 0:
                assert d0 >= 0
                s = s + dt_ref[d0:d0 + w, cs]
            return s

        values = lambda off, w: vt_ref[:, k0 + off:k0 + off + w]
        return _SoftmaxStreamT(score, _chunks(nk, lead=nk % SCORE_CHUNK), values, s_ref, NSA_PV_CHUNKS)

    halves = range(R * t // half_w)
    wins = [stream(h, qt, kvw_ref, kvwt_ref, w0, nw, win_add, s_win[h]) for h in halves]
    sels = [stream(h, q_sel, kvs_ref, kvst_ref, 0, n, sel_add, s_sel[h]) for h in halves]
    _trace_pipelined(wins[:1] + sels + wins[1:])
    for gen in side:
        for _ in gen:
            pass

    g = pl.program_id(0)
    heads = []
    for r in range(R):
        h, hc = divmod(r * t, half_w)
        branches = (o_cmp[NSA_QK:, r * t:(r + 1) * t], sels[h].out[NSA_QK:, hc:hc + t], wins[h].out[NSA_QK:, hc:hc + t])
        o = None
        for br, ob in enumerate(branches):
            gate = gate_ref[pl.ds(g * GATES_PER_GROUP + br * R + r, 1), :]
            o = gate * ob if o is None else o + gate * ob
        heads.append(o)
    o_ref[...] = jnp.concatenate(heads, axis=0).T.astype(BF16)
    if nxt:
        ocmp_scr[...] = nxt["o_cmp"]
        sel_scr[...] = nxt["sel"]


def _nsa_attn(qt, kvc, kvct, kvs, kvst, kvw, kvwt, gates, cbt, dtt, ovt, B, S):
    t = NSA_T
    nq = S // t
    T = B * S
    G, R = NSA_GROUPS, NSA_R
    n_cp = kvc.shape[2]
    half_w = NSA_STREAM_HEADS * t
    nh = R // NSA_STREAM_HEADS
    kv_spec = pl.BlockSpec((S, LANE), lambda g, b, i: (b, g))
    kvt_spec = pl.BlockSpec((LANE, S), lambda g, b, i: (g, b))
    return pl.pallas_call(
        functools.partial(_nsa_attn_body, S=S),
        grid=(G, B, nq),
        in_specs=[pl.BlockSpec((R * LANE, t), lambda g, b, i: (g, b * nq + i)),
                  pl.BlockSpec((R * LANE, t), lambda g, b, i: (g, b * nq + jnp.minimum(i + 1, nq - 1))),
                  pl.BlockSpec((None, None, n_cp, LANE), lambda g, b, i: (b, g, 0, 0)),
                  pl.BlockSpec((None, None, LANE, n_cp), lambda g, b, i: (b, g, 0, 0)),
                  kv_spec, kvt_spec, kv_spec, kvt_spec,
                  pl.BlockSpec((LANE, t), lambda g, b, i: (0, b * nq + i)),
                  pl.BlockSpec((None, n_cp, R * t), lambda g, b, i: (g, 0, 0)),
                  pl.BlockSpec((None, 2 * t, R * t), lambda g, b, i: (g, 0, 0)),
                  _const_spec(ovt.shape)],
        out_specs=pl.BlockSpec((t, R * NSA_V), lambda g, b, i: (b * nq + i, g)),
        out_shape=jax.ShapeDtypeStruct((T, NSA_HEADS * NSA_V), BF16),
        scratch_shapes=([pltpu.VMEM((S, half_w), F32)] * nh + [pltpu.VMEM((WINDOW + t, half_w), F32)] * nh
                        + [pltpu.VMEM((LANE, R * t), F32), pltpu.VMEM((S // SEL_BLOCK, t), BF16)]),
        compiler_params=_cparams(("arbitrary", "arbitrary", "arbitrary")),
        name="nsa_attn",
    )(qt, qt, kvc, kvct, kvs, kvst, kvw, kvwt, gates, cbt, dtt, ovt)


def _t5_bucket(dist):
    n = jnp.maximum(dist, 0)
    max_exact = REL_BUCKETS // 2
    nf = jnp.maximum(n, 1).astype(F32)
    large = max_exact + (jnp.log(nf / max_exact) / math.log(REL_MAX_DIST / max_exact)
                         * (REL_BUCKETS - max_exact)).astype(jnp.int32)
    large = jnp.minimum(large, REL_BUCKETS - 1)
    return jnp.where(n < max_exact, n, large)


def _np_bucket(n):
    n = np.maximum(np.asarray(n), 0)
    max_exact = REL_BUCKETS // 2
    large = max_exact + (np.log(np.maximum(n, 1) / max_exact) / math.log(REL_MAX_DIST / max_exact)
                         * (REL_BUCKETS - max_exact)).astype(np.int64)
    return np.where(n < max_exact, n, np.minimum(large, REL_BUCKETS - 1))


def _bias_tables(rel_bias, S):
    t = NSA_T
    G, R = NSA_GROUPS, NSA_R
    n_cp = S // CMP_STRIDE
    a = np.arange(t)[:, None]
    dist_d = a - np.arange(2 * t)[None, :] + t
    dist_c = a - CMP_STRIDE * (np.arange(n_cp)[None, :] - CB_CENTER) - (CMP_BLOCK - 1)
    uncovered = min(t + 1, CMP_STRIDE * (CB_CENTER + 1) - (CMP_BLOCK - 1))
    assert (_np_bucket(np.arange(uncovered, 2 * S)) == REL_BUCKETS - 1).all()
    wrap_from = n_cp - max((S // t - 1) * (t // CMP_STRIDE) - CB_CENTER, 0)
    assert (dist_c[:, wrap_from:] < 0).all() and (dist_c[:, -1] < 0).all()

    def lookup(dist):
        oh = jax.nn.one_hot(_t5_bucket(jnp.asarray(dist)), REL_BUCKETS, dtype=F32)
        oh = oh - jax.nn.one_hot(REL_BUCKETS - 1, REL_BUCKETS, dtype=F32)
        val = jnp.einsum("acb,bh->hac", oh, rel_bias, precision=lax.Precision.HIGHEST)
        return val.reshape(G, R, *dist.shape)

    dtab = lookup(dist_d)
    cbias = jnp.where(jnp.asarray(dist_c >= 0), lookup(dist_c), 0.0)
    keys_first = lambda x: x.transpose(0, 3, 1, 2).reshape(G, x.shape[3], R * t)
    return keys_first(dtab) * LOG2E, keys_first(cbias) * LOG2E


def _selection_tables(S):
    n_cp = S // CMP_STRIDE
    n_cmp = (S - CMP_BLOCK) // CMP_STRIDE + 1
    n_sel = S // SEL_BLOCK
    cs = np.arange(n_cp) * CMP_STRIDE
    ce = cs + CMP_BLOCK
    ss = np.arange(n_sel) * SEL_BLOCK
    se = ss + SEL_BLOCK
    ov = np.minimum(ce[:, None], se[None, :]) - np.maximum(cs[:, None], ss[None, :])
    ov = (np.clip(ov, 0, None) / CMP_BLOCK).astype(np.float32)
    ov[n_cmp:] = 0.0
    assert n_sel <= LANE - NSA_QK
    blk_onehot = (np.arange(S)[:, None] // SEL_BLOCK == np.arange(LANE)[None, :] - NSA_QK).astype(np.float32)
    return jnp.asarray(ov.T), jnp.asarray(blk_onehot)


def _rope_tables(S):
    half = MLA_ROPE // 2
    inv = ROPE_THETA ** (-jnp.arange(half, dtype=F32) * 2.0 / MLA_ROPE)
    ang = jnp.arange(S, dtype=F32)[:, None] * inv[None, :]
    cos, sin = jnp.cos(ang), jnp.sin(ang)
    ones = jnp.ones((S, MLA_NOPE), F32)
    pad1 = jnp.ones((S, LANE - MLA_NOPE - MLA_ROPE), F32)
    cos128 = jnp.concatenate([ones, cos, cos, pad1], axis=1)
    sin128 = jnp.concatenate([0 * ones, sin, sin, 0 * pad1], axis=1)
    return cos128, sin128


def _mla_weights(w_in, q_norm, kv_norm, w_uq, w_ukv, w_o):
    H = MLA_HEADS
    half = MLA_ROPE // 2
    pad = LANE - MLA_NOPE - MLA_ROPE
    kr = w_in[:, MLA_Q_LORA + MLA_KV_LORA:]
    kr_sw = jnp.concatenate([-kr[:, half:], kr[:, :half]], axis=1)
    z = lambda n: jnp.zeros((D_MODEL, n), F32)
    w_in_ext = jnp.concatenate([w_in[:, :MLA_Q_LORA + MLA_KV_LORA],
                                z(MLA_NOPE), kr, z(pad), z(MLA_NOPE), kr_sw, z(pad)], axis=1)
    uq = w_uq.reshape(MLA_Q_LORA, H, MLA_NOPE + MLA_ROPE)
    qn, qr = uq[..., :MLA_NOPE], uq[..., MLA_NOPE:]
    zq = jnp.zeros((MLA_Q_LORA, H, pad), F32)
    w_uq_p = jnp.concatenate([qn, qr, zq], axis=-1).reshape(MLA_Q_LORA, H * LANE)
    ukv = w_ukv.reshape(MLA_KV_LORA, H, MLA_NOPE + MLA_V)
    zk = jnp.zeros((MLA_KV_LORA, H, LANE - MLA_NOPE), F32)
    w_uk = jnp.concatenate([ukv[..., :MLA_NOPE], zk], axis=-1).reshape(MLA_KV_LORA, H * LANE)
    w_uv = ukv[..., MLA_NOPE:].reshape(MLA_KV_LORA, H * MLA_V)
    return dict(w_in=w_in_ext.astype(BF16), q_norm=q_norm.reshape(1, -1), kv_norm=kv_norm.reshape(1, -1),
                w_uqt=w_uq_p.T.astype(BF16), w_uk=w_uk.astype(BF16),
                w_uvt=w_uv.T.astype(BF16), w_o=w_o.astype(BF16))


def _nsa_weights(w_in, pos_k, w1_k, w2_k, pos_v, w1_v, w2_v, w_o):
    H, G = NSA_HEADS, NSA_GROUPS
    gw = G * NSA_QK
    q0 = H * NSA_QK
    wq = w_in[:, :q0]
    cols, cols_t = [], []
    for n in range(3):
        k = w_in[:, q0 + (2 * n) * gw: q0 + (2 * n + 1) * gw].reshape(D_MODEL, G, NSA_QK)
        v = w_in[:, q0 + (2 * n + 1) * gw: q0 + (2 * n + 2) * gw].reshape(D_MODEL, G, NSA_V)
        right = v if n == 0 else jnp.zeros_like(v)
        cols.append(jnp.concatenate([k, right], axis=-1).reshape(D_MODEL, G * LANE))
        if n > 0:
            cols_t.append(v.reshape(D_MODEL, G * NSA_V))
    w_kv = jnp.concatenate(cols, axis=1)
    w_vt = jnp.concatenate(cols_t, axis=1).T
    wg = w_in[:, q0 + 6 * gw:].reshape(D_MODEL, G, NSA_R, 3).transpose(0, 1, 3, 2).reshape(D_MODEL, 3 * H)
    wg = jnp.concatenate([wg, jnp.zeros((D_MODEL, LANE - 3 * H), F32)], axis=1)
    eye = jnp.eye(2, dtype=F32)
    cw = CMP_STRIDE * 2 * NSA_QK
    pos = jnp.stack([pos_k, pos_v]).reshape(2, 2, CMP_STRIDE, NSA_QK)
    pos = pos.transpose(1, 2, 0, 3).reshape(2, 1, cw)
    w1 = jnp.stack([w1_k, w1_v]).reshape(2, 2, CMP_STRIDE, NSA_QK, CMP_HIDDEN)
    w1 = jnp.einsum("khldj,kq->hlkdqj", w1, eye).reshape(2, cw, 2 * CMP_HIDDEN).astype(BF16)
    w2 = jnp.einsum("kjd,kq->kjqd", jnp.stack([w2_k, w2_v]), eye).reshape(2 * CMP_HIDDEN, 2 * NSA_QK).astype(BF16)
    return dict(w_qt=wq.T.astype(BF16), w_kv=w_kv.astype(BF16), w_kvt=w_vt.astype(BF16),
                w_gt=wg.T.astype(BF16), pos=pos, w1=w1, w2=w2, w_o=w_o.astype(BF16))


def kernel(x, ffn_norm_a, ffn_a_w_gate, ffn_a_w_up, ffn_a_w_down, mix_norm, ffn_norm_b, ffn_b_w_gate, ffn_b_w_up, ffn_b_w_down, final_norm, rel_bias, mla_w_in, mla_q_norm, mla_kv_norm, mla_w_uq, mla_w_ukv, mla_w_o, nsa_w_in, nsa_cmp_pos_k, nsa_cmp_w1_k, nsa_cmp_w2_k, nsa_cmp_pos_v, nsa_cmp_w1_v, nsa_cmp_w2_v, nsa_w_o):
    B, S, D = x.shape
    assert D == D_MODEL and S % ATT_T == 0 and S % NSA_T == 0 and (B * S) % FFN_TM == 0
    T = B * S
    cos128, sin128 = _rope_tables(S)
    dtab, cbias = _bias_tables(rel_bias, S)
    ovt, blk_onehot = _selection_tables(S)

    h = x.reshape(T, D)
    for i in range(DEPTH):
        h = _ffn(h, ffn_norm_a[i], ffn_a_w_gate, ffn_a_w_up, ffn_a_w_down, i)
        j = i // N_MIXERS
        if i % N_MIXERS == 0:
            w = _mla_weights(mla_w_in[j], mla_q_norm[j], mla_kv_norm[j], mla_w_uq[j], mla_w_ukv[j], mla_w_o[j])
            qt, k, vt = _mla_proj(h, mix_norm[i], w, cos128, sin128, S)
            o = _mla_attn(qt, k, vt, B, S)
        else:
            w = _nsa_weights(nsa_w_in[j], nsa_cmp_pos_k[j], nsa_cmp_w1_k[j], nsa_cmp_w2_k[j],
                             nsa_cmp_pos_v[j], nsa_cmp_w1_v[j], nsa_cmp_w2_v[j], nsa_w_o[j])
            qt, xc, kvs, kvw, kvst, kvwt, gates = _nsa_proj(h, mix_norm[i], w, blk_onehot)
            kvc, kvct = _compress(xc, w["pos"], w["w1"], w["w2"], B)
            o = _nsa_attn(qt, kvc, kvct, kvs, kvst, kvw, kvwt, gates, cbias, dtab, ovt, B, S)
        w_o = mla_w_o if i % N_MIXERS == 0 else nsa_w_o
        h = _ffn(h, ffn_norm_b[i], ffn_b_w_gate, ffn_b_w_up, ffn_b_w_down, i,
                 proj=(o, w_o, j), final_g=final_norm if i == DEPTH - 1 else None)
    return h.reshape(B, S, D)
```

```python
import functools
import math

import numpy as np
import jax
import jax.numpy as jnp
from jax import lax
from jax.experimental import pallas as pl
from jax.experimental.pallas import tpu as pltpu

F32 = jnp.float32
BF16 = jnp.bfloat16

D_MODEL = 1024
DEPTH = 4
N_MIXERS = 2
RMS_EPS = 1e-6
FFN_HIDDEN = 2816
NEG = -1e30
MLA_HEADS = 16
MLA_Q_LORA = 384
MLA_KV_LORA = 256
MLA_NOPE = 64
MLA_ROPE = 32
MLA_V = 64
ROPE_THETA = 10000.0
NSA_HEADS = 16
NSA_GROUPS = 4
NSA_R = NSA_HEADS // NSA_GROUPS
NSA_QK = 64
NSA_V = 64
CMP_BLOCK = 32
CMP_STRIDE = 16
CMP_HIDDEN = 128
SEL_BLOCK = 64
SEL_SHIFT = SEL_BLOCK.bit_length() - 1
assert 1 << SEL_SHIFT == SEL_BLOCK
SEL_TOP_N = 16
WINDOW = 512
REL_BUCKETS = 32
REL_MAX_DIST = 128

LANE = 128
VMEM_LIMIT = 56 * 1024 * 1024

FFN_TM = 512
FFN_TF = 256
PROJ_TM = 512
ATT_T = 256
NSA_T = 256
CB_CENTER = 64


def _cparams(sem):
    return pltpu.CompilerParams(dimension_semantics=sem, vmem_limit_bytes=VMEM_LIMIT)


def _rms(x, g):
    ms = jnp.mean(x * x, axis=-1, keepdims=True)
    return x * lax.rsqrt(ms + RMS_EPS) * g


def _const_spec(shape):
    nd = len(shape)
    return pl.BlockSpec(shape, lambda *_: (0,) * nd)


def _wdot(a, w):
    return lax.dot_general(a, w, (((1,), (0,)), ((), ())), preferred_element_type=F32)


def _ffn_body(*refs, has_proj, has_final):
    it = iter(refs)
    h_ref = next(it)
    if has_proj:
        o_in_ref = next(it)
        wo_ref = next(it)
    g_ref = next(it)
    wg_ref = next(it)
    wu_ref = next(it)
    wd_ref = next(it)
    if has_final:
        gf_ref = next(it)
    out_ref = next(it)
    a_ref = next(it)

    x = h_ref[...]
    if has_proj:
        x = x + _wdot(o_in_ref[...], wo_ref[...])
    xn = _rms(x, g_ref[...]).astype(BF16)
    for c in range(FFN_HIDDEN // FFN_TF):
        sl = slice(c * FFN_TF, (c + 1) * FFN_TF)
        gt = _wdot(xn, wg_ref[:, sl])
        up = _wdot(xn, wu_ref[:, sl])
        a_ref[:, sl] = (gt * jax.nn.sigmoid(gt) * up).astype(BF16)
    y = x + 0.5 * _wdot(a_ref[...], wd_ref[...])
    if has_final:
        y = _rms(y, gf_ref[...])
    out_ref[...] = y


def _layer_spec(stacked, layer):
    return pl.BlockSpec((None,) + stacked.shape[1:], lambda i: (layer, 0, 0), pipeline_mode=pl.Buffered(1))


def _ffn(h, g, wg, wu, wd, layer, proj=None, final_g=None):
    T = h.shape[0]
    tm = FFN_TM
    row = lambda i: (i, 0)
    in_specs = [pl.BlockSpec((tm, D_MODEL), row)]
    args = [h]
    if proj is not None:
        o_in, wo, wo_idx = proj
        in_specs += [pl.BlockSpec((tm, o_in.shape[1]), row), _layer_spec(wo, wo_idx)]
        args += [o_in, wo]
    in_specs += [_const_spec((1, D_MODEL)), _layer_spec(wg, layer), _layer_spec(wu, layer), _layer_spec(wd, layer)]
    args += [g.reshape(1, D_MODEL), wg, wu, wd]
    if final_g is not None:
        in_specs.append(_const_spec((1, D_MODEL)))
        args.append(final_g.reshape(1, D_MODEL))
    return pl.pallas_call(
        functools.partial(_ffn_body, has_proj=proj is not None, has_final=final_g is not None),
        grid=(T // tm,),
        in_specs=in_specs,
        out_specs=pl.BlockSpec((tm, D_MODEL), row),
        out_shape=jax.ShapeDtypeStruct((T, D_MODEL), F32),
        scratch_shapes=[pltpu.VMEM((tm, FFN_HIDDEN), BF16)],
        compiler_params=_cparams(("parallel",)),
        name="ffn",
    )(*args)


MLA_CQ0 = 0
MLA_CKV0 = MLA_CQ0 + MLA_Q_LORA
MLA_KR0 = MLA_CKV0 + MLA_KV_LORA
MLA_KRS0 = MLA_KR0 + LANE
MLA_IN_W = MLA_KRS0 + LANE
MLA_HCHUNK = 4


def _store_value_tiles(out_ref, vt, n_tiles):
    d, tm = vt.shape[0] // n_tiles, vt.shape[1]
    pad = jnp.where(lax.broadcasted_iota(jnp.int32, (LANE - d, tm), 0) == 0, 1.0, 0.0).astype(out_ref.dtype)
    for i in range(n_tiles):
        out_ref[i * LANE:i * LANE + LANE - d, :] = pad
        out_ref[i * LANE + LANE - d:(i + 1) * LANE, :] = vt[i * d:(i + 1) * d].astype(out_ref.dtype)


def _mla_proj_body(h_ref, g_ref, win_ref, qn_ref, kvn_ref, wuqt_ref, wuk_ref, wuvt_ref,
                   cos_ref, sin_ref, cost_ref, sinat_ref, sinbt_ref, qt_out, k_out, vt_out):
    xn = _rms(h_ref[...], g_ref[...]).astype(BF16)
    proj = jnp.dot(xn, win_ref[...], preferred_element_type=F32)
    cq = _rms(proj[:, MLA_CQ0:MLA_CKV0], qn_ref[...]).astype(BF16)
    ckv = _rms(proj[:, MLA_CKV0:MLA_KR0], kvn_ref[...]).astype(BF16)
    cos = cos_ref[...]
    sin = sin_ref[...]
    kr = proj[:, MLA_KR0:MLA_KRS0] * cos + proj[:, MLA_KRS0:MLA_IN_W] * sin
    cos_t = cost_ref[...]
    sina_t = sinat_ref[...]
    sinb_t = sinbt_ref[...]
    half = MLA_ROPE // 2
    scale = (MLA_NOPE + MLA_ROPE) ** -0.5 * LOG2E
    vt = lax.dot_general(wuvt_ref[...], ckv, _NT, preferred_element_type=F32)
    _store_value_tiles(vt_out, vt, MLA_HEADS)
    cw = MLA_HCHUNK * LANE
    for c in range(MLA_HEADS // MLA_HCHUNK):
        sl = slice(c * cw, (c + 1) * cw)
        qt = lax.dot_general(wuqt_ref[sl, :], cq, _NT, preferred_element_type=F32)
        kn = jnp.dot(ckv, wuk_ref[:, sl], preferred_element_type=F32)
        for hh in range(MLA_HCHUNK):
            hs = slice(hh * LANE, (hh + 1) * LANE)
            os_ = slice(c * cw + hh * LANE, c * cw + (hh + 1) * LANE)
            qh = qt[hs]
            roped = (qh * cos_t + pltpu.roll(qh, LANE - half, 0) * sina_t + pltpu.roll(qh, half, 0) * sinb_t)
            qt_out[os_, :] = (roped * scale).astype(BF16)
            k_out[:, os_] = (kn[:, hs] + kr).astype(BF16)


def _mla_proj(h, g, w, cos128, sin128, S):
    T = h.shape[0]
    tm = PROJ_TM
    ns = S // tm
    row = lambda i: (i, 0)
    col = lambda i: (0, i)
    pos = lambda i: (i % ns, 0)
    pos_t = lambda i: (0, i % ns)
    HL = MLA_HEADS * LANE
    HV = MLA_HEADS * LANE
    lane = np.arange(LANE)[None, :]
    x1 = (lane >= MLA_NOPE) & (lane < MLA_NOPE + MLA_ROPE // 2)
    x2 = (lane >= MLA_NOPE + MLA_ROPE // 2) & (lane < MLA_NOPE + MLA_ROPE)
    sign_a, sign_b = -x1.astype(np.float32), x2.astype(np.float32)
    return pl.pallas_call(
        _mla_proj_body,
        grid=(T // tm,),
        in_specs=[pl.BlockSpec((tm, D_MODEL), row), _const_spec((1, D_MODEL)),
                  _const_spec(w["w_in"].shape), _const_spec((1, MLA_Q_LORA)), _const_spec((1, MLA_KV_LORA)),
                  _const_spec(w["w_uqt"].shape),
                  _const_spec(w["w_uk"].shape), _const_spec(w["w_uvt"].shape),
                  pl.BlockSpec((tm, LANE), pos), pl.BlockSpec((tm, LANE), pos),
                  pl.BlockSpec((LANE, tm), pos_t), pl.BlockSpec((LANE, tm), pos_t), pl.BlockSpec((LANE, tm), pos_t)],
        out_specs=[pl.BlockSpec((HL, tm), col), pl.BlockSpec((tm, HL), row), pl.BlockSpec((HV, tm), col)],
        out_shape=[jax.ShapeDtypeStruct((HL, T), BF16), jax.ShapeDtypeStruct((T, HL), BF16),
                   jax.ShapeDtypeStruct((HV, T), BF16)],
        compiler_params=_cparams(("parallel",)),
        name="mla_proj",
    )(h, g.reshape(1, D_MODEL), w["w_in"], w["q_norm"], w["kv_norm"], w["w_uqt"],
      w["w_uk"], w["w_uvt"], cos128, sin128, cos128.T, (sin128 * sign_a).T, (sin128 * sign_b).T)


_NT = (((1,), (1,)), ((), ()))
SCORE_CHUNK = 2 * LANE


def _chunks(n, lead=0):
    out = [(0, lead)] if lead else []
    return out + [(o, min(SCORE_CHUNK, n - o)) for o in range(lead, n, SCORE_CHUNK)]


SUBLANE = 8
MLA_PV_CHUNKS = 4
NSA_PV_CHUNKS = 2


def _fold(x, op):
    w, cols = x.shape
    return op(x.reshape(w // SUBLANE, SUBLANE, cols), axis=0)


class _SoftmaxStreamT:
    def __init__(self, score_fn, chunks, values_fn, s_ref, pv_chunks):
        self.score_fn, self.chunks, self.values_fn, self.s_ref = score_fn, chunks, values_fn, s_ref
        self.pv_chunks = pv_chunks

    def pass1(self):
        mp = None
        for off, w in self.chunks:
            s = self.score_fn(off, w)
            self.s_ref[off:off + w, :] = s
            part = _fold(s, jnp.max)
            mp = part if mp is None else jnp.maximum(mp, part)
            yield
        self.m = jnp.max(mp, axis=0, keepdims=True)

    def pass2(self):
        acc = None
        pending = []
        for i, (off, w) in enumerate(self.chunks):
            pending.append((off, w, jnp.exp2(self.s_ref[off:off + w, :] - self.m).astype(BF16)))
            if len(pending) == self.pv_chunks or i == len(self.chunks) - 1:
                off0, wsum = pending[0][0], sum(c[1] for c in pending)
                p = pending[0][2] if len(pending) == 1 else jnp.concatenate([c[2] for c in pending], axis=0)
                pv = jnp.dot(self.values_fn(off0, wsum), p, preferred_element_type=F32)
                acc = pv if acc is None else acc + pv
                pending = []
            yield
        self.out = acc / acc[0:1, :]


def _trace_pipelined(streams):
    for _ in streams[0].pass1():
        pass
    for i, st in enumerate(streams):
        gens = [st.pass2()] + ([streams[i + 1].pass1()] if i + 1 < len(streams) else [])
        while gens:
            gens = [g for g in gens if next(g, StopIteration) is not StopIteration]


MLA_HPS = 8


def _mla_attn_body(qt_ref, k_ref, vt_ref, o_ref, *s_refs, nq):
    t = ATT_T
    qi = pl.program_id(2)

    def branch(nt):
        n = nt * t
        streams = []
        for hh in range(MLA_HPS):
            hs = slice(hh * LANE, (hh + 1) * LANE)
            qt = qt_ref[hs, :]

            def score(off, w, qt=qt, hs=hs):
                s = jnp.dot(k_ref[off:off + w, hs], qt, preferred_element_type=F32)
                if off + w <= n - t:
                    return s
                kpos = off + lax.broadcasted_iota(jnp.int32, (w, t), 0)
                return jnp.where(kpos <= n - t + lax.broadcasted_iota(jnp.int32, (w, t), 1), s, NEG)

            values = lambda off, w, hs=hs: vt_ref[hs, off:off + w]
            streams.append(_SoftmaxStreamT(score, _chunks(n, lead=n % SCORE_CHUNK), values, s_refs[hh], MLA_PV_CHUNKS))
        _trace_pipelined(streams)
        outs = [st.out[LANE - MLA_V:] for st in streams]
        o_ref[...] = jnp.concatenate(outs, axis=0).T.astype(BF16)

    for nt in range(1, nq + 1):
        pl.when(qi == nt - 1)(functools.partial(branch, nt))


def _mla_attn(qt, k, vt, B, S):
    t = ATT_T
    nq = S // t
    T = B * S
    hps = MLA_HPS
    return pl.pallas_call(
        functools.partial(_mla_attn_body, nq=nq),
        grid=(B, MLA_HEADS // hps, nq),
        in_specs=[pl.BlockSpec((hps * LANE, t), lambda b, p, i: (p, b * nq + i)),
                  pl.BlockSpec((S, hps * LANE), lambda b, p, i: (b, p)),
                  pl.BlockSpec((hps * LANE, S), lambda b, p, i: (p, b))],
        out_specs=pl.BlockSpec((t, hps * MLA_V), lambda b, p, i: (b * nq + i, p)),
        out_shape=jax.ShapeDtypeStruct((T, MLA_HEADS * MLA_V), BF16),
        scratch_shapes=[pltpu.VMEM((S, t), F32)] * hps,
        compiler_params=_cparams(("parallel", "parallel", "arbitrary")),
        name="mla_attn",
    )(qt, k, vt)


NSA_QW = NSA_HEADS * LANE
NSA_KVW = NSA_GROUPS * LANE
GATES_PER_GROUP = 3 * NSA_R
NSA_STREAM_HEADS = 2
RANK_STEPS_PER_STAGE = 8
LOG2E = math.log2(math.e)


def _nsa_proj_body(h_ref, g_ref, wt_ref, wkv_ref, blk_ref,
                   qt_out, xc_out, kvs_out, kvw_out, kvst_out, kvwt_out, gate_out, kvc_scr):
    xn = _rms(h_ref[...], g_ref[...]).astype(BF16)
    scale = NSA_QK ** -0.5 * LOG2E
    tm = xn.shape[0]
    nt = lax.dot_general(wt_ref[...], xn, _NT, preferred_element_type=F32)
    zeros = jnp.zeros((LANE - NSA_QK, tm), BF16)
    for hh in range(NSA_HEADS):
        qt_out[hh * LANE:hh * LANE + NSA_QK, :] = (nt[hh * NSA_QK:(hh + 1) * NSA_QK] * scale).astype(BF16)
        qt_out[hh * LANE + NSA_QK:(hh + 1) * LANE, :] = zeros
    for n, out in enumerate((xc_out, kvs_out, kvw_out)):
        sl = slice(n * NSA_KVW, (n + 1) * NSA_KVW)
        kv = jnp.dot(xn, wkv_ref[:, sl], preferred_element_type=F32)
        if out is xc_out:
            for gi in range(NSA_GROUPS):
                kvc_scr[gi] = kv[:, gi * LANE:(gi + 1) * LANE]
                for l in range(CMP_STRIDE):
                    rows = kvc_scr[gi, pl.ds(l, tm // CMP_STRIDE, stride=CMP_STRIDE), :]
                    xc_out[gi, :, l * LANE:(l + 1) * LANE] = rows.astype(BF16)
            continue
        if out is kvs_out:
            kv = kv + jnp.concatenate([blk_ref[...]] * NSA_GROUPS, axis=1)
        out[...] = kv.astype(BF16)
    q_rows, gv = NSA_HEADS * NSA_QK, NSA_GROUPS * NSA_V
    for n, out in enumerate((kvst_out, kvwt_out)):
        _store_value_tiles(out, nt[q_rows + n * gv:q_rows + (n + 1) * gv], NSA_GROUPS)
    gate_out[...] = jax.nn.sigmoid(nt[q_rows + 2 * gv:])


def _nsa_proj(h, g, w, blk_onehot):
    T = h.shape[0]
    tm = PROJ_TM
    ns = blk_onehot.shape[0] // tm
    row = lambda i: (i, 0)
    col = lambda i: (0, i)
    return pl.pallas_call(
        _nsa_proj_body,
        grid=(T // tm,),
        in_specs=[pl.BlockSpec((tm, D_MODEL), row), _const_spec((1, D_MODEL)), _const_spec(w["w_t"].shape),
                  _const_spec(w["w_kv"].shape),
                  pl.BlockSpec((tm, LANE), lambda i: (i % ns, 0))],
        out_specs=[pl.BlockSpec((NSA_QW, tm), col),
                   pl.BlockSpec((NSA_GROUPS, tm // CMP_STRIDE, CMP_STRIDE * LANE), lambda i: (0, i, 0)),
                   pl.BlockSpec((tm, NSA_KVW), row), pl.BlockSpec((tm, NSA_KVW), row),
                   pl.BlockSpec((NSA_KVW, tm), col), pl.BlockSpec((NSA_KVW, tm), col),
                   pl.BlockSpec((LANE, tm), col)],
        out_shape=[jax.ShapeDtypeStruct((NSA_QW, T), BF16),
                   jax.ShapeDtypeStruct((NSA_GROUPS, T // CMP_STRIDE, CMP_STRIDE * LANE), BF16),
                   jax.ShapeDtypeStruct((T, NSA_KVW), BF16), jax.ShapeDtypeStruct((T, NSA_KVW), BF16),
                   jax.ShapeDtypeStruct((NSA_KVW, T), BF16), jax.ShapeDtypeStruct((NSA_KVW, T), BF16),
                   jax.ShapeDtypeStruct((LANE, T), F32)],
        scratch_shapes=[pltpu.VMEM((NSA_GROUPS, tm, LANE), F32)],
        compiler_params=_cparams(("parallel",)),
        name="nsa_proj",
    )(h, g.reshape(1, D_MODEL), w["w_t"], w["w_kv"], blk_onehot)


def _compress_body(x_ref, pos_ref, w1_ref, w2_ref, out_ref, out_t_ref):
    n_chunk = x_ref.shape[0]
    x = x_ref[...].astype(F32)
    xa = (x + pos_ref[0]).astype(BF16)
    xb = (x + pos_ref[1]).astype(BF16)
    a = jnp.dot(xa, w1_ref[0], preferred_element_type=F32)
    b = jnp.dot(xb, w1_ref[1], preferred_element_type=F32)
    pre = a + pltpu.roll(b, n_chunk - 1, 0)
    hid = jax.nn.gelu(pre, approximate=True).astype(BF16)
    kv = jnp.dot(hid, w2_ref[...], preferred_element_type=F32)
    out_ref[...] = kv.astype(BF16)
    out_t_ref[...] = kv.T.astype(BF16)


def _compress(x, pos, w1, w2, B):
    G, rows, width = x.shape
    n_chunk = rows // B
    return pl.pallas_call(
        _compress_body,
        grid=(B, G),
        in_specs=[pl.BlockSpec((None, n_chunk, width), lambda b, g: (g, b, 0)),
                  _const_spec(pos.shape), _const_spec(w1.shape), _const_spec(w2.shape)],
        out_specs=[pl.BlockSpec((None, None, n_chunk, LANE), lambda b, g: (b, g, 0, 0)),
                   pl.BlockSpec((None, None, LANE, n_chunk), lambda b, g: (b, g, 0, 0))],
        out_shape=[jax.ShapeDtypeStruct((B, G, n_chunk, LANE), BF16),
                   jax.ShapeDtypeStruct((B, G, LANE, n_chunk), BF16)],
        compiler_params=_cparams(("parallel", "parallel")),
        name="nsa_compress",
    )(x, pos, w1, w2)


def _nsa_attn_body(*refs, S):
    step = pl.program_id(2)
    for qi in range(S // NSA_T):
        pl.when(step == qi)(functools.partial(_nsa_tile, qi, *refs, S=S))


def _nsa_tile(qi, qt_ref, qtn_ref, kvc_ref, kvct_ref, kvs_ref, kvst_ref, kvw_ref, kvwt_ref, gate_ref, cbt_ref,
              dt_ref, ovt_ref, o_ref, *scratch, S):
    t = NSA_T
    R = NSA_R
    n_sel = S // SEL_BLOCK
    t0 = qi * t
    ns = R // NSA_STREAM_HEADS
    s_sel, s_win = scratch[0:ns], scratch[ns:2 * ns]
    ocmp_scr, sel_scr = scratch[2 * ns:2 * ns + 2]
    heads_on_lanes = lambda ref: jnp.concatenate([ref[r * LANE:(r + 1) * LANE, :] for r in range(R)], axis=1)
    qt = heads_on_lanes(qt_ref)
    tile_r = lambda x: jnp.concatenate([x] * R, axis=1)
    half_w = NSA_STREAM_HEADS * t
    tile_h = lambda x: jnp.concatenate([x] * (half_w // t), axis=1)

    def compressed_and_selection(q_tile, tile, out):
        kvc = kvc_ref[...]
        n_cp = kvc.shape[0]
        sc = jnp.dot(kvc, q_tile, preferred_element_type=F32)
        yield
        cend = lax.broadcasted_iota(jnp.int32, (n_cp, t), 0) * CMP_STRIDE + (CMP_BLOCK - 1)
        valid = tile_r(tile * t + lax.broadcasted_iota(jnp.int32, (n_cp, t), 1) >= cend)
        shift = (tile * (t // CMP_STRIDE) + CB_CENTER) % n_cp
        cb = pltpu.roll(cbt_ref[...], shift, 0) if shift else cbt_ref[...]
        sc = jnp.where(valid, sc + cb, NEG)
        e = jnp.exp2(sc - jnp.max(sc, axis=0, keepdims=True))
        yield
        p = jnp.where(valid, e / jnp.sum(e, axis=0, keepdims=True), 0.0)
        out["o_cmp"] = jnp.dot(kvct_ref[...], p.astype(BF16), preferred_element_type=F32)
        yield

        psum = p[:, 0:t]
        for r in range(1, R):
            psum = psum + p[:, r * t:(r + 1) * t]
        imp = jnp.dot(ovt_ref[...], psum, preferred_element_type=F32, precision=lax.Precision.HIGHEST)
        yield
        jj = lax.broadcasted_iota(jnp.int32, (n_sel, t), 0)
        blk_t = (tile * t + lax.broadcasted_iota(jnp.int32, (n_sel, t), 1)) >> SEL_SHIFT
        forced = (jj == 0) | (jj == blk_t) | (jj == blk_t - 1)
        score = jnp.where(forced, 1e6, jnp.where(jj <= blk_t, imp, -1e6))
        cnt = jnp.zeros((n_sel, t), jnp.int32)
        for jp in range(n_sel):
            row = score[jp:jp + 1, :]
            beats = (row > score) | ((row == score) & (jj > jp))
            cnt = cnt + beats.astype(jnp.int32)
            if jp % RANK_STEPS_PER_STAGE == RANK_STEPS_PER_STAGE - 1:
                yield
        out["sel"] = jnp.where(cnt < min(SEL_TOP_N, n_sel), 0.0, NEG).astype(BF16)

    if qi == 0:
        cur = {}
        for _ in compressed_and_selection(qt, 0, cur):
            pass
        o_cmp, sel = cur["o_cmp"], cur["sel"]
    else:
        o_cmp, sel = ocmp_scr[...], sel_scr[...]
    nxt, side = {}, []
    if (qi + 1) * t < S:
        side = [compressed_and_selection(heads_on_lanes(qtn_ref), qi + 1, nxt)]

    kw = min(qi, WINDOW // t)
    nw = (kw + 1) * t
    w0 = (qi - kw) * t
    n = (qi + 1) * t

    def win_add(off, w):
        lo_d, hi_d = kw * t - off - (w - 1), kw * t - off + (t - 1)
        if lo_d >= 0 and hi_d < WINDOW:
            return None
        dist = (kw * t - off + lax.broadcasted_iota(jnp.int32, (w, t), 1)
                - lax.broadcasted_iota(jnp.int32, (w, t), 0))
        return jnp.where((dist >= 0) & (dist < WINDOW), 0.0, NEG)

    q_sel = jnp.concatenate([qt[0:NSA_QK], tile_r(sel), jnp.zeros((LANE - NSA_QK - n_sel, R * t), BF16)], axis=0)

    def sel_add(off, w):
        if off + w <= n - t:
            return None
        kpos = off + lax.broadcasted_iota(jnp.int32, (w, t), 0)
        return jnp.where(kpos <= t0 + lax.broadcasted_iota(jnp.int32, (w, t), 1), 0.0, NEG)

    def stream(half, q_all, k_ref, vt_ref, k0, nk, add_fn, s_ref):
        cs = slice(half * half_w, (half + 1) * half_w)
        qh = q_all[:, cs]

        def score(off, w):
            s = jnp.dot(k_ref[k0 + off:k0 + off + w, :], qh, preferred_element_type=F32)
            add = add_fn(off, w)
            if add is not None:
                s = s + tile_h(add)
            d0 = off - (nk - 2 * t)
            if d0 + w > 0:
                assert d0 >= 0
                s = s + dt_ref[d0:d0 + w, cs]
            return s

        values = lambda off, w: vt_ref[:, k0 + off:k0 + off + w]
        return _SoftmaxStreamT(score, _chunks(nk, lead=nk % SCORE_CHUNK), values, s_ref, NSA_PV_CHUNKS)

    halves = range(R * t // half_w)
    wins = [stream(h, qt, kvw_ref, kvwt_ref, w0, nw, win_add, s_win[h]) for h in halves]
    sels = [stream(h, q_sel, kvs_ref, kvst_ref, 0, n, sel_add, s_sel[h]) for h in halves]
    _trace_pipelined(wins[:1] + sels + wins[1:])
    for gen in side:
        for _ in gen:
            pass

    g = pl.program_id(0)
    heads = []
    for r in range(R):
        h, hc = divmod(r * t, half_w)
        branches = (o_cmp[NSA_QK:, r * t:(r + 1) * t], sels[h].out[NSA_QK:, hc:hc + t], wins[h].out[NSA_QK:, hc:hc + t])
        o = None
        for br, ob in enumerate(branches):
            gate = gate_ref[pl.ds(g * GATES_PER_GROUP + br * R + r, 1), :]
            o = gate * ob if o is None else o + gate * ob
        heads.append(o)
    o_ref[...] = jnp.concatenate(heads, axis=0).T.astype(BF16)
    if nxt:
        ocmp_scr[...] = nxt["o_cmp"]
        sel_scr[...] = nxt["sel"]


def _nsa_attn(qt, kvc, kvct, kvs, kvst, kvw, kvwt, gates, cbt, dtt, ovt, B, S):
    t = NSA_T
    nq = S // t
    T = B * S
    G, R = NSA_GROUPS, NSA_R
    n_cp = kvc.shape[2]
    half_w = NSA_STREAM_HEADS * t
    nh = R // NSA_STREAM_HEADS
    kv_spec = pl.BlockSpec((S, LANE), lambda g, b, i: (b, g))
    kvt_spec = pl.BlockSpec((LANE, S), lambda g, b, i: (g, b))
    return pl.pallas_call(
        functools.partial(_nsa_attn_body, S=S),
        grid=(G, B, nq),
        in_specs=[pl.BlockSpec((R * LANE, t), lambda g, b, i: (g, b * nq + i)),
                  pl.BlockSpec((R * LANE, t), lambda g, b, i: (g, b * nq + jnp.minimum(i + 1, nq - 1))),
                  pl.BlockSpec((None, None, n_cp, LANE), lambda g, b, i: (b, g, 0, 0)),
                  pl.BlockSpec((None, None, LANE, n_cp), lambda g, b, i: (b, g, 0, 0)),
                  kv_spec, kvt_spec, kv_spec, kvt_spec,
                  pl.BlockSpec((LANE, t), lambda g, b, i: (0, b * nq + i)),
                  pl.BlockSpec((None, n_cp, R * t), lambda g, b, i: (g, 0, 0)),
                  pl.BlockSpec((None, 2 * t, R * t), lambda g, b, i: (g, 0, 0)),
                  _const_spec(ovt.shape)],
        out_specs=pl.BlockSpec((t, R * NSA_V), lambda g, b, i: (b * nq + i, g)),
        out_shape=jax.ShapeDtypeStruct((T, NSA_HEADS * NSA_V), BF16),
        scratch_shapes=([pltpu.VMEM((S, half_w), F32)] * nh + [pltpu.VMEM((WINDOW + t, half_w), F32)] * nh
                        + [pltpu.VMEM((LANE, R * t), F32), pltpu.VMEM((S // SEL_BLOCK, t), BF16)]),
        compiler_params=_cparams(("arbitrary", "arbitrary", "arbitrary")),
        name="nsa_attn",
    )(qt, qt, kvc, kvct, kvs, kvst, kvw, kvwt, gates, cbt, dtt, ovt)


def _t5_bucket(dist):
    n = jnp.maximum(dist, 0)
    max_exact = REL_BUCKETS // 2
    nf = jnp.maximum(n, 1).astype(F32)
    large = max_exact + (jnp.log(nf / max_exact) / math.log(REL_MAX_DIST / max_exact)
                         * (REL_BUCKETS - max_exact)).astype(jnp.int32)
    large = jnp.minimum(large, REL_BUCKETS - 1)
    return jnp.where(n < max_exact, n, large)


def _np_bucket(n):
    n = np.maximum(np.asarray(n), 0)
    max_exact = REL_BUCKETS // 2
    large = max_exact + (np.log(np.maximum(n, 1) / max_exact) / math.log(REL_MAX_DIST / max_exact)
                         * (REL_BUCKETS - max_exact)).astype(np.int64)
    return np.where(n < max_exact, n, np.minimum(large, REL_BUCKETS - 1))


def _bias_tables(rel_bias, S):
    t = NSA_T
    G, R = NSA_GROUPS, NSA_R
    n_cp = S // CMP_STRIDE
    a = np.arange(t)[:, None]
    dist_d = a - np.arange(2 * t)[None, :] + t
    dist_c = a - CMP_STRIDE * (np.arange(n_cp)[None, :] - CB_CENTER) - (CMP_BLOCK - 1)
    uncovered = min(t + 1, CMP_STRIDE * (CB_CENTER + 1) - (CMP_BLOCK - 1))
    assert (_np_bucket(np.arange(uncovered, 2 * S)) == REL_BUCKETS - 1).all()
    wrap_from = n_cp - max((S // t - 1) * (t // CMP_STRIDE) - CB_CENTER, 0)
    assert (dist_c[:, wrap_from:] < 0).all() and (dist_c[:, -1] < 0).all()

    def lookup(dist):
        oh = jax.nn.one_hot(_t5_bucket(jnp.asarray(dist)), REL_BUCKETS, dtype=F32)
        oh = oh - jax.nn.one_hot(REL_BUCKETS - 1, REL_BUCKETS, dtype=F32)
        val = jnp.einsum("acb,bh->hac", oh, rel_bias, precision=lax.Precision.HIGHEST)
        return val.reshape(G, R, *dist.shape)

    dtab = lookup(dist_d)
    cbias = jnp.where(jnp.asarray(dist_c >= 0), lookup(dist_c), 0.0)
    keys_first = lambda x: x.transpose(0, 3, 1, 2).reshape(G, x.shape[3], R * t)
    return keys_first(dtab) * LOG2E, keys_first(cbias) * LOG2E


def _selection_tables(S):
    n_cp = S // CMP_STRIDE
    n_cmp = (S - CMP_BLOCK) // CMP_STRIDE + 1
    n_sel = S // SEL_BLOCK
    cs = np.arange(n_cp) * CMP_STRIDE
    ce = cs + CMP_BLOCK
    ss = np.arange(n_sel) * SEL_BLOCK
    se = ss + SEL_BLOCK
    ov = np.minimum(ce[:, None], se[None, :]) - np.maximum(cs[:, None], ss[None, :])
    ov = (np.clip(ov, 0, None) / CMP_BLOCK).astype(np.float32)
    ov[n_cmp:] = 0.0
    assert n_sel <= LANE - NSA_QK
    blk_onehot = (np.arange(S)[:, None] // SEL_BLOCK == np.arange(LANE)[None, :] - NSA_QK).astype(np.float32)
    return jnp.asarray(ov.T), jnp.asarray(blk_onehot)


def _rope_tables(S):
    half = MLA_ROPE // 2
    inv = ROPE_THETA ** (-jnp.arange(half, dtype=F32) * 2.0 / MLA_ROPE)
    ang = jnp.arange(S, dtype=F32)[:, None] * inv[None, :]
    cos, sin = jnp.cos(ang), jnp.sin(ang)
    ones = jnp.ones((S, MLA_NOPE), F32)
    pad1 = jnp.ones((S, LANE - MLA_NOPE - MLA_ROPE), F32)
    cos128 = jnp.concatenate([ones, cos, cos, pad1], axis=1)
    sin128 = jnp.concatenate([0 * ones, sin, sin, 0 * pad1], axis=1)
    return cos128, sin128


def _mla_weights(w_in, q_norm, kv_norm, w_uq, w_ukv, w_o):
    H = MLA_HEADS
    half = MLA_ROPE // 2
    pad = LANE - MLA_NOPE - MLA_ROPE
    kr = w_in[:, MLA_Q_LORA + MLA_KV_LORA:]
    kr_sw = jnp.concatenate([-kr[:, half:], kr[:, :half]], axis=1)
    z = lambda n: jnp.zeros((D_MODEL, n), F32)
    w_in_ext = jnp.concatenate([w_in[:, :MLA_Q_LORA + MLA_KV_LORA],
                                z(MLA_NOPE), kr, z(pad), z(MLA_NOPE), kr_sw, z(pad)], axis=1)
    uq = w_uq.reshape(MLA_Q_LORA, H, MLA_NOPE + MLA_ROPE)
    qn, qr = uq[..., :MLA_NOPE], uq[..., MLA_NOPE:]
    zq = jnp.zeros((MLA_Q_LORA, H, pad), F32)
    w_uq_p = jnp.concatenate([qn, qr, zq], axis=-1).reshape(MLA_Q_LORA, H * LANE)
    ukv = w_ukv.reshape(MLA_KV_LORA, H, MLA_NOPE + MLA_V)
    zk = jnp.zeros((MLA_KV_LORA, H, LANE - MLA_NOPE), F32)
    w_uk = jnp.concatenate([ukv[..., :MLA_NOPE], zk], axis=-1).reshape(MLA_KV_LORA, H * LANE)
    w_uv = ukv[..., MLA_NOPE:].reshape(MLA_KV_LORA, H * MLA_V)
    return dict(w_in=w_in_ext.astype(BF16), q_norm=q_norm.reshape(1, -1), kv_norm=kv_norm.reshape(1, -1),
                w_uqt=w_uq_p.T.astype(BF16), w_uk=w_uk.astype(BF16),
                w_uvt=w_uv.T.astype(BF16), w_o=w_o.astype(BF16))


def _nsa_weights(w_in, pos_k, w1_k, w2_k, pos_v, w1_v, w2_v, w_o):
    H, G = NSA_HEADS, NSA_GROUPS
    gw = G * NSA_QK
    q0 = H * NSA_QK
    wq = w_in[:, :q0]
    cols, cols_t = [], []
    for n in range(3):
        k = w_in[:, q0 + (2 * n) * gw: q0 + (2 * n + 1) * gw].reshape(D_MODEL, G, NSA_QK)
        v = w_in[:, q0 + (2 * n + 1) * gw: q0 + (2 * n + 2) * gw].reshape(D_MODEL, G, NSA_V)
        right = v if n == 0 else jnp.zeros_like(v)
        cols.append(jnp.concatenate([k, right], axis=-1).reshape(D_MODEL, G * LANE))
        if n > 0:
            cols_t.append(v.reshape(D_MODEL, G * NSA_V))
    w_kv = jnp.concatenate(cols, axis=1)
    w_vt = jnp.concatenate(cols_t, axis=1).T
    wg = w_in[:, q0 + 6 * gw:].reshape(D_MODEL, G, NSA_R, 3).transpose(0, 1, 3, 2).reshape(D_MODEL, 3 * H)
    wg = jnp.concatenate([wg, jnp.zeros((D_MODEL, LANE - 3 * H), F32)], axis=1)
    eye = jnp.eye(2, dtype=F32)
    cw = CMP_STRIDE * 2 * NSA_QK
    pos = jnp.stack([pos_k, pos_v]).reshape(2, 2, CMP_STRIDE, NSA_QK)
    pos = pos.transpose(1, 2, 0, 3).reshape(2, 1, cw)
    w1 = jnp.stack([w1_k, w1_v]).reshape(2, 2, CMP_STRIDE, NSA_QK, CMP_HIDDEN)
    w1 = jnp.einsum("khldj,kq->hlkdqj", w1, eye).reshape(2, cw, 2 * CMP_HIDDEN).astype(BF16)
    w2 = jnp.einsum("kjd,kq->kjqd", jnp.stack([w2_k, w2_v]), eye).reshape(2 * CMP_HIDDEN, 2 * NSA_QK).astype(BF16)
    w_t = jnp.concatenate([wq.T, w_vt, wg.T], axis=0)
    return dict(w_t=w_t.astype(BF16), w_kv=w_kv.astype(BF16), pos=pos, w1=w1, w2=w2)


def kernel(x, ffn_norm_a, ffn_a_w_gate, ffn_a_w_up, ffn_a_w_down, mix_norm, ffn_norm_b, ffn_b_w_gate, ffn_b_w_up, ffn_b_w_down, final_norm, rel_bias, mla_w_in, mla_q_norm, mla_kv_norm, mla_w_uq, mla_w_ukv, mla_w_o, nsa_w_in, nsa_cmp_pos_k, nsa_cmp_w1_k, nsa_cmp_w2_k, nsa_cmp_pos_v, nsa_cmp_w1_v, nsa_cmp_w2_v, nsa_w_o):
    B, S, D = x.shape
    assert D == D_MODEL and S % ATT_T == 0 and S % NSA_T == 0 and (B * S) % FFN_TM == 0
    T = B * S
    cos128, sin128 = _rope_tables(S)
    dtab, cbias = _bias_tables(rel_bias, S)
    ovt, blk_onehot = _selection_tables(S)

    h = x.reshape(T, D)
    for i in range(DEPTH):
        h = _ffn(h, ffn_norm_a[i], ffn_a_w_gate, ffn_a_w_up, ffn_a_w_down, i)
        j = i // N_MIXERS
        if i % N_MIXERS == 0:
            w = _mla_weights(mla_w_in[j], mla_q_norm[j], mla_kv_norm[j], mla_w_uq[j], mla_w_ukv[j], mla_w_o[j])
            qt, k, vt = _mla_proj(h, mix_norm[i], w, cos128, sin128, S)
            o = _mla_attn(qt, k, vt, B, S)
        else:
            w = _nsa_weights(nsa_w_in[j], nsa_cmp_pos_k[j], nsa_cmp_w1_k[j], nsa_cmp_w2_k[j],
                             nsa_cmp_pos_v[j], nsa_cmp_w1_v[j], nsa_cmp_w2_v[j], nsa_w_o[j])
            qt, xc, kvs, kvw, kvst, kvwt, gates = _nsa_proj(h, mix_norm[i], w, blk_onehot)
            kvc, kvct = _compress(xc, w["pos"], w["w1"], w["w2"], B)
            o = _nsa_attn(qt, kvc, kvct, kvs, kvst, kvw, kvwt, gates, cbias, dtab, ovt, B, S)
        w_o = mla_w_o if i % N_MIXERS == 0 else nsa_w_o
        h = _ffn(h, ffn_norm_b[i], ffn_b_w_gate, ffn_b_w_up, ffn_b_w_down, i,
                 proj=(o, w_o, j), final_g=final_norm if i == DEPTH - 1 else None)
    return h.reshape(B, S, D)
```

```python
import functools
import math

import numpy as np
import jax
import jax.numpy as jnp
from jax import lax
from jax.experimental import pallas as pl
from jax.experimental.pallas import tpu as pltpu

F32 = jnp.float32
BF16 = jnp.bfloat16

D_MODEL = 1024
DEPTH = 4
N_MIXERS = 2
RMS_EPS = 1e-6
FFN_HIDDEN = 2816
NEG = -1e30
MLA_HEADS = 16
MLA_Q_LORA = 384
MLA_KV_LORA = 256
MLA_NOPE = 64
MLA_ROPE = 32
MLA_V = 64
ROPE_THETA = 10000.0
NSA_HEADS = 16
NSA_GROUPS = 4
NSA_R = NSA_HEADS // NSA_GROUPS
NSA_QK = 64
NSA_V = 64
CMP_BLOCK = 32
CMP_STRIDE = 16
CMP_HIDDEN = 128
SEL_BLOCK = 64
SEL_SHIFT = SEL_BLOCK.bit_length() - 1
assert 1 << SEL_SHIFT == SEL_BLOCK
SEL_TOP_N = 16
WINDOW = 512
REL_BUCKETS = 32
REL_MAX_DIST = 128

LANE = 128
VMEM_LIMIT = 56 * 1024 * 1024

FFN_TM = 512
FFN_TF = 256
PROJ_TM = 1024
ATT_T = 256
NSA_T = 256
CB_CENTER = 64


def _cparams(sem):
    return pltpu.CompilerParams(dimension_semantics=sem, vmem_limit_bytes=VMEM_LIMIT)


def _rms(x, g):
    ms = jnp.mean(x * x, axis=-1, keepdims=True)
    return x * lax.rsqrt(ms + RMS_EPS) * g


def _const_spec(shape):
    nd = len(shape)
    return pl.BlockSpec(shape, lambda *_: (0,) * nd)


def _wdot(a, w):
    return lax.dot_general(a, w, (((1,), (0,)), ((), ())), preferred_element_type=F32)


def _ffn_body(*refs, has_proj, has_final):
    it = iter(refs)
    h_ref = next(it)
    if has_proj:
        o_in_ref = next(it)
        wo_ref = next(it)
    g_ref = next(it)
    wg_ref = next(it)
    wu_ref = next(it)
    wd_ref = next(it)
    if has_final:
        gf_ref = next(it)
    out_ref = next(it)
    a_ref = next(it)

    x = h_ref[...]
    if has_proj:
        x = x + _wdot(o_in_ref[...], wo_ref[...])
    xn = _rms(x, g_ref[...]).astype(BF16)
    for c in range(FFN_HIDDEN // FFN_TF):
        sl = slice(c * FFN_TF, (c + 1) * FFN_TF)
        gt = _wdot(xn, wg_ref[:, sl])
        up = _wdot(xn, wu_ref[:, sl])
        a_ref[:, sl] = (gt * jax.nn.sigmoid(gt) * up).astype(BF16)
    y = x + 0.5 * _wdot(a_ref[...], wd_ref[...])
    if has_final:
        y = _rms(y, gf_ref[...])
    out_ref[...] = y


def _layer_spec(stacked, layer):
    return pl.BlockSpec((None,) + stacked.shape[1:], lambda i: (layer, 0, 0), pipeline_mode=pl.Buffered(1))


def _ffn(h, g, wg, wu, wd, layer, proj=None, final_g=None):
    T = h.shape[0]
    tm = FFN_TM
    row = lambda i: (i, 0)
    in_specs = [pl.BlockSpec((tm, D_MODEL), row)]
    args = [h]
    if proj is not None:
        o_in, wo, wo_idx = proj
        in_specs += [pl.BlockSpec((tm, o_in.shape[1]), row), _layer_spec(wo, wo_idx)]
        args += [o_in, wo]
    in_specs += [_const_spec((1, D_MODEL)), _layer_spec(wg, layer), _layer_spec(wu, layer), _layer_spec(wd, layer)]
    args += [g.reshape(1, D_MODEL), wg, wu, wd]
    if final_g is not None:
        in_specs.append(_const_spec((1, D_MODEL)))
        args.append(final_g.reshape(1, D_MODEL))
    return pl.pallas_call(
        functools.partial(_ffn_body, has_proj=proj is not None, has_final=final_g is not None),
        grid=(T // tm,),
        in_specs=in_specs,
        out_specs=pl.BlockSpec((tm, D_MODEL), row),
        out_shape=jax.ShapeDtypeStruct((T, D_MODEL), F32),
        scratch_shapes=[pltpu.VMEM((tm, FFN_HIDDEN), BF16)],
        compiler_params=_cparams(("parallel",)),
        name="ffn",
    )(*args)


MLA_CQ0 = 0
MLA_CKV0 = MLA_CQ0 + MLA_Q_LORA
MLA_KR0 = MLA_CKV0 + MLA_KV_LORA
MLA_KRS0 = MLA_KR0 + LANE
MLA_IN_W = MLA_KRS0 + LANE
MLA_HCHUNK = 2


def _store_value_tiles(out_ref, vt, n_tiles):
    d, tm = vt.shape[0] // n_tiles, vt.shape[1]
    pad = jnp.where(lax.broadcasted_iota(jnp.int32, (LANE - d, tm), 0) == 0, 1.0, 0.0).astype(out_ref.dtype)
    for i in range(n_tiles):
        out_ref[i * LANE:i * LANE + LANE - d, :] = pad
        out_ref[i * LANE + LANE - d:(i + 1) * LANE, :] = vt[i * d:(i + 1) * d].astype(out_ref.dtype)


def _mla_proj_body(h_ref, g_ref, win_ref, qn_ref, kvn_ref, wuqt_ref, wuk_ref, wuvt_ref,
                   cos_ref, sin_ref, cost_ref, sinat_ref, sinbt_ref, qt_out, k_out, vt_out):
    xn = _rms(h_ref[...], g_ref[...]).astype(BF16)
    proj = jnp.dot(xn, win_ref[...], preferred_element_type=F32)
    cq = _rms(proj[:, MLA_CQ0:MLA_CKV0], qn_ref[...]).astype(BF16)
    ckv = _rms(proj[:, MLA_CKV0:MLA_KR0], kvn_ref[...]).astype(BF16)
    cos = cos_ref[...]
    sin = sin_ref[...]
    kr = proj[:, MLA_KR0:MLA_KRS0] * cos + proj[:, MLA_KRS0:MLA_IN_W] * sin
    cos_t = cost_ref[...]
    sina_t = sinat_ref[...]
    sinb_t = sinbt_ref[...]
    half = MLA_ROPE // 2
    scale = (MLA_NOPE + MLA_ROPE) ** -0.5 * LOG2E
    vt = lax.dot_general(wuvt_ref[...], ckv, _NT, preferred_element_type=F32)
    _store_value_tiles(vt_out, vt, MLA_HEADS)
    cw = MLA_HCHUNK * LANE
    for c in range(MLA_HEADS // MLA_HCHUNK):
        sl = slice(c * cw, (c + 1) * cw)
        qt = lax.dot_general(wuqt_ref[sl, :], cq, _NT, preferred_element_type=F32)
        kn = jnp.dot(ckv, wuk_ref[:, sl], preferred_element_type=F32)
        for hh in range(MLA_HCHUNK):
            hs = slice(hh * LANE, (hh + 1) * LANE)
            os_ = slice(c * cw + hh * LANE, c * cw + (hh + 1) * LANE)
            qh = qt[hs]
            roped = (qh * cos_t + pltpu.roll(qh, LANE - half, 0) * sina_t + pltpu.roll(qh, half, 0) * sinb_t)
            qt_out[os_, :] = (roped * scale).astype(BF16)
            k_out[:, os_] = (kn[:, hs] + kr).astype(BF16)


def _mla_proj(h, g, w, cos128, sin128, S):
    T = h.shape[0]
    tm = PROJ_TM
    ns = S // tm
    row = lambda i: (i, 0)
    col = lambda i: (0, i)
    pos = lambda i: (i % ns, 0)
    pos_t = lambda i: (0, i % ns)
    HL = MLA_HEADS * LANE
    HV = MLA_HEADS * LANE
    lane = np.arange(LANE)[None, :]
    x1 = (lane >= MLA_NOPE) & (lane < MLA_NOPE + MLA_ROPE // 2)
    x2 = (lane >= MLA_NOPE + MLA_ROPE // 2) & (lane < MLA_NOPE + MLA_ROPE)
    sign_a, sign_b = -x1.astype(np.float32), x2.astype(np.float32)
    return pl.pallas_call(
        _mla_proj_body,
        grid=(T // tm,),
        in_specs=[pl.BlockSpec((tm, D_MODEL), row), _const_spec((1, D_MODEL)),
                  _const_spec(w["w_in"].shape), _const_spec((1, MLA_Q_LORA)), _const_spec((1, MLA_KV_LORA)),
                  _const_spec(w["w_uqt"].shape),
                  _const_spec(w["w_uk"].shape), _const_spec(w["w_uvt"].shape),
                  pl.BlockSpec((tm, LANE), pos), pl.BlockSpec((tm, LANE), pos),
                  pl.BlockSpec((LANE, tm), pos_t), pl.BlockSpec((LANE, tm), pos_t), pl.BlockSpec((LANE, tm), pos_t)],
        out_specs=[pl.BlockSpec((HL, tm), col), pl.BlockSpec((tm, HL), row), pl.BlockSpec((HV, tm), col)],
        out_shape=[jax.ShapeDtypeStruct((HL, T), BF16), jax.ShapeDtypeStruct((T, HL), BF16),
                   jax.ShapeDtypeStruct((HV, T), BF16)],
        compiler_params=_cparams(("parallel",)),
        name="mla_proj",
    )(h, g.reshape(1, D_MODEL), w["w_in"], w["q_norm"], w["kv_norm"], w["w_uqt"],
      w["w_uk"], w["w_uvt"], cos128, sin128, cos128.T, (sin128 * sign_a).T, (sin128 * sign_b).T)


_NT = (((1,), (1,)), ((), ()))
SCORE_CHUNK = 2 * LANE


def _chunks(n, lead=0):
    out = [(0, lead)] if lead else []
    return out + [(o, min(SCORE_CHUNK, n - o)) for o in range(lead, n, SCORE_CHUNK)]


SUBLANE = 8
MLA_PV_CHUNKS = 4
NSA_PV_CHUNKS = 2


def _fold(x, op):
    w, cols = x.shape
    return op(x.reshape(w // SUBLANE, SUBLANE, cols), axis=0)


class _SoftmaxStreamT:
    def __init__(self, score_fn, chunks, values_fn, s_ref, pv_chunks):
        self.score_fn, self.chunks, self.values_fn, self.s_ref = score_fn, chunks, values_fn, s_ref
        self.pv_chunks = pv_chunks

    def pass1(self):
        mp = None
        for off, w in self.chunks:
            s = self.score_fn(off, w)
            self.s_ref[off:off + w, :] = s
            part = _fold(s, jnp.max)
            mp = part if mp is None else jnp.maximum(mp, part)
            yield
        self.m = jnp.max(mp, axis=0, keepdims=True)

    def pass2(self):
        acc = None
        pending = []
        for i, (off, w) in enumerate(self.chunks):
            pending.append((off, w, jnp.exp2(self.s_ref[off:off + w, :] - self.m).astype(BF16)))
            if len(pending) == self.pv_chunks or i == len(self.chunks) - 1:
                off0, wsum = pending[0][0], sum(c[1] for c in pending)
                p = pending[0][2] if len(pending) == 1 else jnp.concatenate([c[2] for c in pending], axis=0)
                pv = jnp.dot(self.values_fn(off0, wsum), p, preferred_element_type=F32)
                acc = pv if acc is None else acc + pv
                pending = []
            yield
        self.out = acc / acc[0:1, :]


def _trace_pipelined(streams):
    for _ in streams[0].pass1():
        pass
    for i, st in enumerate(streams):
        gens = [st.pass2()] + ([streams[i + 1].pass1()] if i + 1 < len(streams) else [])
        while gens:
            gens = [g for g in gens if next(g, StopIteration) is not StopIteration]


MLA_HPS = 8


def _mla_attn_body(qt_ref, k_ref, vt_ref, o_ref, *s_refs, nq):
    t = ATT_T
    qi = pl.program_id(2)

    def branch(nt):
        n = nt * t
        streams = []
        for hh in range(MLA_HPS):
            hs = slice(hh * LANE, (hh + 1) * LANE)
            qt = qt_ref[hs, :]

            def score(off, w, qt=qt, hs=hs):
                s = jnp.dot(k_ref[off:off + w, hs], qt, preferred_element_type=F32)
                if off + w <= n - t:
                    return s
                kpos = off + lax.broadcasted_iota(jnp.int32, (w, t), 0)
                return jnp.where(kpos <= n - t + lax.broadcasted_iota(jnp.int32, (w, t), 1), s, NEG)

            values = lambda off, w, hs=hs: vt_ref[hs, off:off + w]
            streams.append(_SoftmaxStreamT(score, _chunks(n, lead=n % SCORE_CHUNK), values, s_refs[hh], MLA_PV_CHUNKS))
        _trace_pipelined(streams)
        outs = [st.out[LANE - MLA_V:] for st in streams]
        o_ref[...] = jnp.concatenate(outs, axis=0).T.astype(BF16)

    for nt in range(1, nq + 1):
        pl.when(qi == nt - 1)(functools.partial(branch, nt))


def _mla_attn(qt, k, vt, B, S):
    t = ATT_T
    nq = S // t
    T = B * S
    hps = MLA_HPS
    return pl.pallas_call(
        functools.partial(_mla_attn_body, nq=nq),
        grid=(B, MLA_HEADS // hps, nq),
        in_specs=[pl.BlockSpec((hps * LANE, t), lambda b, p, i: (p, b * nq + i)),
                  pl.BlockSpec((S, hps * LANE), lambda b, p, i: (b, p)),
                  pl.BlockSpec((hps * LANE, S), lambda b, p, i: (p, b))],
        out_specs=pl.BlockSpec((t, hps * MLA_V), lambda b, p, i: (b * nq + i, p)),
        out_shape=jax.ShapeDtypeStruct((T, MLA_HEADS * MLA_V), BF16),
        scratch_shapes=[pltpu.VMEM((S, t), F32)] * hps,
        compiler_params=_cparams(("parallel", "parallel", "arbitrary")),
        name="mla_attn",
    )(qt, k, vt)


NSA_QW = NSA_HEADS * LANE
NSA_KVW = NSA_GROUPS * LANE
GATES_PER_GROUP = 3 * NSA_R
NSA_STREAM_HEADS = 2
RANK_STEPS_PER_STAGE = 8
LOG2E = math.log2(math.e)


def _nsa_proj_body(h_ref, g_ref, wt_ref, wkv_ref, blk_ref,
                   qt_out, xc_out, kvs_out, kvw_out, kvst_out, kvwt_out, gate_out, kvc_scr):
    xn = _rms(h_ref[...], g_ref[...]).astype(BF16)
    scale = NSA_QK ** -0.5 * LOG2E
    tm = xn.shape[0]
    nt = lax.dot_general(wt_ref[...], xn, _NT, preferred_element_type=F32)
    zeros = jnp.zeros((LANE - NSA_QK, tm), BF16)
    for hh in range(NSA_HEADS):
        qt_out[hh * LANE:hh * LANE + NSA_QK, :] = (nt[hh * NSA_QK:(hh + 1) * NSA_QK] * scale).astype(BF16)
        qt_out[hh * LANE + NSA_QK:(hh + 1) * LANE, :] = zeros
    for n, out in enumerate((xc_out, kvs_out, kvw_out)):
        sl = slice(n * NSA_KVW, (n + 1) * NSA_KVW)
        kv = jnp.dot(xn, wkv_ref[:, sl], preferred_element_type=F32)
        if out is xc_out:
            for gi in range(NSA_GROUPS):
                kvc_scr[gi] = kv[:, gi * LANE:(gi + 1) * LANE]
                for l in range(CMP_STRIDE):
                    rows = kvc_scr[gi, pl.ds(l, tm // CMP_STRIDE, stride=CMP_STRIDE), :]
                    xc_out[gi, :, l * LANE:(l + 1) * LANE] = rows.astype(BF16)
            continue
        if out is kvs_out:
            kv = kv + jnp.concatenate([blk_ref[...]] * NSA_GROUPS, axis=1)
        out[...] = kv.astype(BF16)
    q_rows, gv = NSA_HEADS * NSA_QK, NSA_GROUPS * NSA_V
    for n, out in enumerate((kvst_out, kvwt_out)):
        _store_value_tiles(out, nt[q_rows + n * gv:q_rows + (n + 1) * gv], NSA_GROUPS)
    gate_out[...] = jax.nn.sigmoid(nt[q_rows + 2 * gv:])


def _nsa_proj(h, g, w, blk_onehot):
    T = h.shape[0]
    tm = PROJ_TM
    ns = blk_onehot.shape[0] // tm
    row = lambda i: (i, 0)
    col = lambda i: (0, i)
    return pl.pallas_call(
        _nsa_proj_body,
        grid=(T // tm,),
        in_specs=[pl.BlockSpec((tm, D_MODEL), row), _const_spec((1, D_MODEL)), _const_spec(w["w_t"].shape),
                  _const_spec(w["w_kv"].shape),
                  pl.BlockSpec((tm, LANE), lambda i: (i % ns, 0))],
        out_specs=[pl.BlockSpec((NSA_QW, tm), col),
                   pl.BlockSpec((NSA_GROUPS, tm // CMP_STRIDE, CMP_STRIDE * LANE), lambda i: (0, i, 0)),
                   pl.BlockSpec((tm, NSA_KVW), row), pl.BlockSpec((tm, NSA_KVW), row),
                   pl.BlockSpec((NSA_KVW, tm), col), pl.BlockSpec((NSA_KVW, tm), col),
                   pl.BlockSpec((LANE, tm), col)],
        out_shape=[jax.ShapeDtypeStruct((NSA_QW, T), BF16),
                   jax.ShapeDtypeStruct((NSA_GROUPS, T // CMP_STRIDE, CMP_STRIDE * LANE), BF16),
                   jax.ShapeDtypeStruct((T, NSA_KVW), BF16), jax.ShapeDtypeStruct((T, NSA_KVW), BF16),
                   jax.ShapeDtypeStruct((NSA_KVW, T), BF16), jax.ShapeDtypeStruct((NSA_KVW, T), BF16),
                   jax.ShapeDtypeStruct((LANE, T), F32)],
        scratch_shapes=[pltpu.VMEM((NSA_GROUPS, tm, LANE), F32)],
        compiler_params=_cparams(("parallel",)),
        name="nsa_proj",
    )(h, g.reshape(1, D_MODEL), w["w_t"], w["w_kv"], blk_onehot)


def _compress_body(x_ref, pos_ref, w1_ref, w2_ref, out_ref, out_t_ref):
    n_chunk = x_ref.shape[0]
    x = x_ref[...].astype(F32)
    xa = (x + pos_ref[0]).astype(BF16)
    xb = (x + pos_ref[1]).astype(BF16)
    a = jnp.dot(xa, w1_ref[0], preferred_element_type=F32)
    b = jnp.dot(xb, w1_ref[1], preferred_element_type=F32)
    pre = a + pltpu.roll(b, n_chunk - 1, 0)
    hid = jax.nn.gelu(pre, approximate=True).astype(BF16)
    kv = jnp.dot(hid, w2_ref[...], preferred_element_type=F32)
    out_ref[...] = kv.astype(BF16)
    out_t_ref[...] = kv.T.astype(BF16)


def _compress(x, pos, w1, w2, B):
    G, rows, width = x.shape
    n_chunk = rows // B
    return pl.pallas_call(
        _compress_body,
        grid=(B, G),
        in_specs=[pl.BlockSpec((None, n_chunk, width), lambda b, g: (g, b, 0)),
                  _const_spec(pos.shape), _const_spec(w1.shape), _const_spec(w2.shape)],
        out_specs=[pl.BlockSpec((None, None, n_chunk, LANE), lambda b, g: (b, g, 0, 0)),
                   pl.BlockSpec((None, None, LANE, n_chunk), lambda b, g: (b, g, 0, 0))],
        out_shape=[jax.ShapeDtypeStruct((B, G, n_chunk, LANE), BF16),
                   jax.ShapeDtypeStruct((B, G, LANE, n_chunk), BF16)],
        compiler_params=_cparams(("parallel", "parallel")),
        name="nsa_compress",
    )(x, pos, w1, w2)


def _nsa_attn_body(*refs, S):
    step = pl.program_id(2)
    for qi in range(S // NSA_T):
        pl.when(step == qi)(functools.partial(_nsa_tile, qi, *refs, S=S))


def _nsa_tile(qi, qt_ref, qtn_ref, kvc_ref, kvct_ref, kvs_ref, kvst_ref, kvw_ref, kvwt_ref, gate_ref, cbt_ref,
              dt_ref, ovt_ref, o_ref, *scratch, S):
    t = NSA_T
    R = NSA_R
    n_sel = S // SEL_BLOCK
    t0 = qi * t
    ns = R // NSA_STREAM_HEADS
    s_sel, s_win = scratch[0:ns], scratch[ns:2 * ns]
    ocmp_scr, sel_scr = scratch[2 * ns:2 * ns + 2]
    heads_on_lanes = lambda ref: jnp.concatenate([ref[r * LANE:(r + 1) * LANE, :] for r in range(R)], axis=1)
    qt = heads_on_lanes(qt_ref)
    tile_r = lambda x: jnp.concatenate([x] * R, axis=1)
    half_w = NSA_STREAM_HEADS * t
    tile_h = lambda x: jnp.concatenate([x] * (half_w // t), axis=1)

    def compressed_and_selection(q_tile, tile, out):
        kvc = kvc_ref[...]
        n_cp = kvc.shape[0]
        sc = jnp.dot(kvc, q_tile, preferred_element_type=F32)
        yield
        cend = lax.broadcasted_iota(jnp.int32, (n_cp, t), 0) * CMP_STRIDE + (CMP_BLOCK - 1)
        valid = tile_r(tile * t + lax.broadcasted_iota(jnp.int32, (n_cp, t), 1) >= cend)
        shift = (tile * (t // CMP_STRIDE) + CB_CENTER) % n_cp
        cb = pltpu.roll(cbt_ref[...], shift, 0) if shift else cbt_ref[...]
        sc = jnp.where(valid, sc + cb, NEG)
        e = jnp.exp2(sc - jnp.max(sc, axis=0, keepdims=True))
        yield
        p = jnp.where(valid, e / jnp.sum(e, axis=0, keepdims=True), 0.0)
        out["o_cmp"] = jnp.dot(kvct_ref[...], p.astype(BF16), preferred_element_type=F32)
        yield

        psum = p[:, 0:t]
        for r in range(1, R):
            psum = psum + p[:, r * t:(r + 1) * t]
        imp = jnp.dot(ovt_ref[...], psum, preferred_element_type=F32, precision=lax.Precision.HIGHEST)
        yield
        jj = lax.broadcasted_iota(jnp.int32, (n_sel, t), 0)
        blk_t = (tile * t + lax.broadcasted_iota(jnp.int32, (n_sel, t), 1)) >> SEL_SHIFT
        forced = (jj == 0) | (jj == blk_t) | (jj == blk_t - 1)
        score = jnp.where(forced, 1e6, jnp.where(jj <= blk_t, imp, -1e6))
        cnt = jnp.zeros((n_sel, t), jnp.int32)
        for jp in range(n_sel):
            row = score[jp:jp + 1, :]
            beats = (row > score) | ((row == score) & (jj > jp))
            cnt = cnt + beats.astype(jnp.int32)
            if jp % RANK_STEPS_PER_STAGE == RANK_STEPS_PER_STAGE - 1:
                yield
        out["sel"] = jnp.where(cnt < min(SEL_TOP_N, n_sel), 0.0, NEG).astype(BF16)

    if qi == 0:
        cur = {}
        for _ in compressed_and_selection(qt, 0, cur):
            pass
        o_cmp, sel = cur["o_cmp"], cur["sel"]
    else:
        o_cmp, sel = ocmp_scr[...], sel_scr[...]
    nxt, side = {}, []
    if (qi + 1) * t < S:
        side = [compressed_and_selection(heads_on_lanes(qtn_ref), qi + 1, nxt)]

    kw = min(qi, WINDOW // t)
    nw = (kw + 1) * t
    w0 = (qi - kw) * t
    n = (qi + 1) * t

    def win_add(off, w):
        lo_d, hi_d = kw * t - off - (w - 1), kw * t - off + (t - 1)
        if lo_d >= 0 and hi_d < WINDOW:
            return None
        dist = (kw * t - off + lax.broadcasted_iota(jnp.int32, (w, t), 1)
                - lax.broadcasted_iota(jnp.int32, (w, t), 0))
        return jnp.where((dist >= 0) & (dist < WINDOW), 0.0, NEG)

    q_sel = jnp.concatenate([qt[0:NSA_QK], tile_r(sel), jnp.zeros((LANE - NSA_QK - n_sel, R * t), BF16)], axis=0)

    def sel_add(off, w):
        if off + w <= n - t:
            return None
        kpos = off + lax.broadcasted_iota(jnp.int32, (w, t), 0)
        return jnp.where(kpos <= t0 + lax.broadcasted_iota(jnp.int32, (w, t), 1), 0.0, NEG)

    def stream(half, q_all, k_ref, vt_ref, k0, nk, add_fn, s_ref):
        cs = slice(half * half_w, (half + 1) * half_w)
        qh = q_all[:, cs]

        def score(off, w):
            s = jnp.dot(k_ref[k0 + off:k0 + off + w, :], qh, preferred_element_type=F32)
            add = add_fn(off, w)
            if add is not None:
                s = s + tile_h(add)
            d0 = off - (nk - 2 * t)
            if d0 + w > 0:
                assert d0 >= 0
                s = s + dt_ref[d0:d0 + w, cs]
            return s

        values = lambda off, w: vt_ref[:, k0 + off:k0 + off + w]
        return _SoftmaxStreamT(score, _chunks(nk, lead=nk % SCORE_CHUNK), values, s_ref, NSA_PV_CHUNKS)

    halves = range(R * t // half_w)
    wins = [stream(h, qt, kvw_ref, kvwt_ref, w0, nw, win_add, s_win[h]) for h in halves]
    sels = [stream(h, q_sel, kvs_ref, kvst_ref, 0, n, sel_add, s_sel[h]) for h in halves]
    _trace_pipelined(wins[:1] + sels + wins[1:])
    for gen in side:
        for _ in gen:
            pass

    g = pl.program_id(0)
    heads = []
    for r in range(R):
        h, hc = divmod(r * t, half_w)
        branches = (o_cmp[NSA_QK:, r * t:(r + 1) * t], sels[h].out[NSA_QK:, hc:hc + t], wins[h].out[NSA_QK:, hc:hc + t])
        o = None
        for br, ob in enumerate(branches):
            gate = gate_ref[pl.ds(g * GATES_PER_GROUP + br * R + r, 1), :]
            o = gate * ob if o is None else o + gate * ob
        heads.append(o)
    o_ref[...] = jnp.concatenate(heads, axis=0).T.astype(BF16)
    if nxt:
        ocmp_scr[...] = nxt["o_cmp"]
        sel_scr[...] = nxt["sel"]


def _nsa_attn(qt, kvc, kvct, kvs, kvst, kvw, kvwt, gates, cbt, dtt, ovt, B, S):
    t = NSA_T
    nq = S // t
    T = B * S
    G, R = NSA_GROUPS, NSA_R
    n_cp = kvc.shape[2]
    half_w = NSA_STREAM_HEADS * t
    nh = R // NSA_STREAM_HEADS
    kv_spec = pl.BlockSpec((S, LANE), lambda g, b, i: (b, g))
    kvt_spec = pl.BlockSpec((LANE, S), lambda g, b, i: (g, b))
    return pl.pallas_call(
        functools.partial(_nsa_attn_body, S=S),
        grid=(G, B, nq),
        in_specs=[pl.BlockSpec((R * LANE, t), lambda g, b, i: (g, b * nq + i)),
                  pl.BlockSpec((R * LANE, t), lambda g, b, i: (g, b * nq + jnp.minimum(i + 1, nq - 1))),
                  pl.BlockSpec((None, None, n_cp, LANE), lambda g, b, i: (b, g, 0, 0)),
                  pl.BlockSpec((None, None, LANE, n_cp), lambda g, b, i: (b, g, 0, 0)),
                  kv_spec, kvt_spec, kv_spec, kvt_spec,
                  pl.BlockSpec((LANE, t), lambda g, b, i: (0, b * nq + i)),
                  pl.BlockSpec((None, n_cp, R * t), lambda g, b, i: (g, 0, 0)),
                  pl.BlockSpec((None, 2 * t, R * t), lambda g, b, i: (g, 0, 0)),
                  _const_spec(ovt.shape)],
        out_specs=pl.BlockSpec((t, R * NSA_V), lambda g, b, i: (b * nq + i, g)),
        out_shape=jax.ShapeDtypeStruct((T, NSA_HEADS * NSA_V), BF16),
        scratch_shapes=([pltpu.VMEM((S, half_w), F32)] * nh + [pltpu.VMEM((WINDOW + t, half_w), F32)] * nh
                        + [pltpu.VMEM((LANE, R * t), F32), pltpu.VMEM((S // SEL_BLOCK, t), BF16)]),
        compiler_params=_cparams(("arbitrary", "arbitrary", "arbitrary")),
        name="nsa_attn",
    )(qt, qt, kvc, kvct, kvs, kvst, kvw, kvwt, gates, cbt, dtt, ovt)


def _t5_bucket(dist):
    n = jnp.maximum(dist, 0)
    max_exact = REL_BUCKETS // 2
    nf = jnp.maximum(n, 1).astype(F32)
    large = max_exact + (jnp.log(nf / max_exact) / math.log(REL_MAX_DIST / max_exact)
                         * (REL_BUCKETS - max_exact)).astype(jnp.int32)
    large = jnp.minimum(large, REL_BUCKETS - 1)
    return jnp.where(n < max_exact, n, large)


def _np_bucket(n):
    n = np.maximum(np.asarray(n), 0)
    max_exact = REL_BUCKETS // 2
    large = max_exact + (np.log(np.maximum(n, 1) / max_exact) / math.log(REL_MAX_DIST / max_exact)
                         * (REL_BUCKETS - max_exact)).astype(np.int64)
    return np.where(n < max_exact, n, np.minimum(large, REL_BUCKETS - 1))


def _bias_tables(rel_bias, S):
    t = NSA_T
    G, R = NSA_GROUPS, NSA_R
    n_cp = S // CMP_STRIDE
    a = np.arange(t)[:, None]
    dist_d = a - np.arange(2 * t)[None, :] + t
    dist_c = a - CMP_STRIDE * (np.arange(n_cp)[None, :] - CB_CENTER) - (CMP_BLOCK - 1)
    uncovered = min(t + 1, CMP_STRIDE * (CB_CENTER + 1) - (CMP_BLOCK - 1))
    assert (_np_bucket(np.arange(uncovered, 2 * S)) == REL_BUCKETS - 1).all()
    wrap_from = n_cp - max((S // t - 1) * (t // CMP_STRIDE) - CB_CENTER, 0)
    assert (dist_c[:, wrap_from:] < 0).all() and (dist_c[:, -1] < 0).all()

    def lookup(dist):
        oh = jax.nn.one_hot(_t5_bucket(jnp.asarray(dist)), REL_BUCKETS, dtype=F32)
        oh = oh - jax.nn.one_hot(REL_BUCKETS - 1, REL_BUCKETS, dtype=F32)
        val = jnp.einsum("acb,bh->hac", oh, rel_bias, precision=lax.Precision.HIGHEST)
        return val.reshape(G, R, *dist.shape)

    dtab = lookup(dist_d)
    cbias = jnp.where(jnp.asarray(dist_c >= 0), lookup(dist_c), 0.0)
    keys_first = lambda x: x.transpose(0, 3, 1, 2).reshape(G, x.shape[3], R * t)
    return keys_first(dtab) * LOG2E, keys_first(cbias) * LOG2E


def _selection_tables(S):
    n_cp = S // CMP_STRIDE
    n_cmp = (S - CMP_BLOCK) // CMP_STRIDE + 1
    n_sel = S // SEL_BLOCK
    cs = np.arange(n_cp) * CMP_STRIDE
    ce = cs + CMP_BLOCK
    ss = np.arange(n_sel) * SEL_BLOCK
    se = ss + SEL_BLOCK
    ov = np.minimum(ce[:, None], se[None, :]) - np.maximum(cs[:, None], ss[None, :])
    ov = (np.clip(ov, 0, None) / CMP_BLOCK).astype(np.float32)
    ov[n_cmp:] = 0.0
    assert n_sel <= LANE - NSA_QK
    blk_onehot = (np.arange(S)[:, None] // SEL_BLOCK == np.arange(LANE)[None, :] - NSA_QK).astype(np.float32)
    return jnp.asarray(ov.T), jnp.asarray(blk_onehot)


def _rope_tables(S):
    half = MLA_ROPE // 2
    inv = ROPE_THETA ** (-jnp.arange(half, dtype=F32) * 2.0 / MLA_ROPE)
    ang = jnp.arange(S, dtype=F32)[:, None] * inv[None, :]
    cos, sin = jnp.cos(ang), jnp.sin(ang)
    ones = jnp.ones((S, MLA_NOPE), F32)
    pad1 = jnp.ones((S, LANE - MLA_NOPE - MLA_ROPE), F32)
    cos128 = jnp.concatenate([ones, cos, cos, pad1], axis=1)
    sin128 = jnp.concatenate([0 * ones, sin, sin, 0 * pad1], axis=1)
    return cos128, sin128


def _mla_weights(w_in, q_norm, kv_norm, w_uq, w_ukv, w_o):
    H = MLA_HEADS
    half = MLA_ROPE // 2
    pad = LANE - MLA_NOPE - MLA_ROPE
    kr = w_in[:, MLA_Q_LORA + MLA_KV_LORA:]
    kr_sw = jnp.concatenate([-kr[:, half:], kr[:, :half]], axis=1)
    z = lambda n: jnp.zeros((D_MODEL, n), F32)
    w_in_ext = jnp.concatenate([w_in[:, :MLA_Q_LORA + MLA_KV_LORA],
                                z(MLA_NOPE), kr, z(pad), z(MLA_NOPE), kr_sw, z(pad)], axis=1)
    uq = w_uq.reshape(MLA_Q_LORA, H, MLA_NOPE + MLA_ROPE)
    qn, qr = uq[..., :MLA_NOPE], uq[..., MLA_NOPE:]
    zq = jnp.zeros((MLA_Q_LORA, H, pad), F32)
    w_uq_p = jnp.concatenate([qn, qr, zq], axis=-1).reshape(MLA_Q_LORA, H * LANE)
    ukv = w_ukv.reshape(MLA_KV_LORA, H, MLA_NOPE + MLA_V)
    zk = jnp.zeros((MLA_KV_LORA, H, LANE - MLA_NOPE), F32)
    w_uk = jnp.concatenate([ukv[..., :MLA_NOPE], zk], axis=-1).reshape(MLA_KV_LORA, H * LANE)
    w_uv = ukv[..., MLA_NOPE:].reshape(MLA_KV_LORA, H * MLA_V)
    return dict(w_in=w_in_ext.astype(BF16), q_norm=q_norm.reshape(1, -1), kv_norm=kv_norm.reshape(1, -1),
                w_uqt=w_uq_p.T.astype(BF16), w_uk=w_uk.astype(BF16),
                w_uvt=w_uv.T.astype(BF16), w_o=w_o.astype(BF16))


def _nsa_weights(w_in, pos_k, w1_k, w2_k, pos_v, w1_v, w2_v, w_o):
    H, G = NSA_HEADS, NSA_GROUPS
    gw = G * NSA_QK
    q0 = H * NSA_QK
    wq = w_in[:, :q0]
    cols, cols_t = [], []
    for n in range(3):
        k = w_in[:, q0 + (2 * n) * gw: q0 + (2 * n + 1) * gw].reshape(D_MODEL, G, NSA_QK)
        v = w_in[:, q0 + (2 * n + 1) * gw: q0 + (2 * n + 2) * gw].reshape(D_MODEL, G, NSA_V)
        right = v if n == 0 else jnp.zeros_like(v)
        cols.append(jnp.concatenate([k, right], axis=-1).reshape(D_MODEL, G * LANE))
        if n > 0:
            cols_t.append(v.reshape(D_MODEL, G * NSA_V))
    w_kv = jnp.concatenate(cols, axis=1)
    w_vt = jnp.concatenate(cols_t, axis=1).T
    wg = w_in[:, q0 + 6 * gw:].reshape(D_MODEL, G, NSA_R, 3).transpose(0, 1, 3, 2).reshape(D_MODEL, 3 * H)
    wg = jnp.concatenate([wg, jnp.zeros((D_MODEL, LANE - 3 * H), F32)], axis=1)
    eye = jnp.eye(2, dtype=F32)
    cw = CMP_STRIDE * 2 * NSA_QK
    pos = jnp.stack([pos_k, pos_v]).reshape(2, 2, CMP_STRIDE, NSA_QK)
    pos = pos.transpose(1, 2, 0, 3).reshape(2, 1, cw)
    w1 = jnp.stack([w1_k, w1_v]).reshape(2, 2, CMP_STRIDE, NSA_QK, CMP_HIDDEN)
    w1 = jnp.einsum("khldj,kq->hlkdqj", w1, eye).reshape(2, cw, 2 * CMP_HIDDEN).astype(BF16)
    w2 = jnp.einsum("kjd,kq->kjqd", jnp.stack([w2_k, w2_v]), eye).reshape(2 * CMP_HIDDEN, 2 * NSA_QK).astype(BF16)
    w_t = jnp.concatenate([wq.T, w_vt, wg.T], axis=0)
    return dict(w_t=w_t.astype(BF16), w_kv=w_kv.astype(BF16), pos=pos, w1=w1, w2=w2)


def kernel(x, ffn_norm_a, ffn_a_w_gate, ffn_a_w_up, ffn_a_w_down, mix_norm, ffn_norm_b, ffn_b_w_gate, ffn_b_w_up, ffn_b_w_down, final_norm, rel_bias, mla_w_in, mla_q_norm, mla_kv_norm, mla_w_uq, mla_w_ukv, mla_w_o, nsa_w_in, nsa_cmp_pos_k, nsa_cmp_w1_k, nsa_cmp_w2_k, nsa_cmp_pos_v, nsa_cmp_w1_v, nsa_cmp_w2_v, nsa_w_o):
    B, S, D = x.shape
    assert D == D_MODEL and S % ATT_T == 0 and S % NSA_T == 0 and (B * S) % FFN_TM == 0
    T = B * S
    cos128, sin128 = _rope_tables(S)
    dtab, cbias = _bias_tables(rel_bias, S)
    ovt, blk_onehot = _selection_tables(S)

    h = x.reshape(T, D)
    for i in range(DEPTH):
        h = _ffn(h, ffn_norm_a[i], ffn_a_w_gate, ffn_a_w_up, ffn_a_w_down, i)
        j = i // N_MIXERS
        if i % N_MIXERS == 0:
            w = _mla_weights(mla_w_in[j], mla_q_norm[j], mla_kv_norm[j], mla_w_uq[j], mla_w_ukv[j], mla_w_o[j])
            qt, k, vt = _mla_proj(h, mix_norm[i], w, cos128, sin128, S)
            o = _mla_attn(qt, k, vt, B, S)
        else:
            w = _nsa_weights(nsa_w_in[j], nsa_cmp_pos_k[j], nsa_cmp_w1_k[j], nsa_cmp_w2_k[j],
                             nsa_cmp_pos_v[j], nsa_cmp_w1_v[j], nsa_cmp_w2_v[j], nsa_w_o[j])
            qt, xc, kvs, kvw, kvst, kvwt, gates = _nsa_proj(h, mix_norm[i], w, blk_onehot)
            kvc, kvct = _compress(xc, w["pos"], w["w1"], w["w2"], B)
            o = _nsa_attn(qt, kvc, kvct, kvs, kvst, kvw, kvwt, gates, cbias, dtab, ovt, B, S)
        w_o = mla_w_o if i % N_MIXERS == 0 else nsa_w_o
        h = _ffn(h, ffn_norm_b[i], ffn_b_w_gate, ffn_b_w_up, ffn_b_w_down, i,
                 proj=(o, w_o, j), final_g=final_norm if i == DEPTH - 1 else None)
    return h.reshape(B, S, D)
```
